```python
import math
import jax, jax.numpy as jnp
from jax import lax
import numpy as np

D_MODEL = 1024
BATCH = 8
SEQ = 2048
DEPTH = 2

D_CONV = 512
CONV_GROUPS = 8
CONV_WIDTH = 3
N_HEADS = 8
QK_NOPE = 64
QK_ROPE = 32
V_HEAD = 64
Q_LORA = 384
KV_LORA = 256
ROPE_THETA = 10000.0
Q_BLOCK = 128
D_FF = 2816
N_EXPERTS = 8
TOP_K = 2
D_FF_EXPERT = 1408
N_DENSE = (DEPTH + 1) // 2
N_MOE = DEPTH // 2
EPS = 1e-6
IN_WIDTHS = (D_CONV, D_CONV, D_CONV, Q_LORA, KV_LORA, QK_ROPE, D_MODEL, D_MODEL)
D_IN = sum(IN_WIDTHS)

kernel_name = "hybrid_gated_conv_mla_moe_trunk"


def rmsnorm(x, g):
    xf = x.astype(jnp.float32)
    y = xf * lax.rsqrt(jnp.mean(xf * xf, axis=-1, keepdims=True) + EPS)
    return (y * g.astype(jnp.float32)).astype(x.dtype)


def rope_tables(positions, dtype):
    inv_freq = ROPE_THETA ** (-jnp.arange(0, QK_ROPE, 2, dtype=jnp.float32) / QK_ROPE)
    ang = positions.astype(jnp.float32)[..., None] * inv_freq
    return jnp.cos(ang).astype(dtype), jnp.sin(ang).astype(dtype)


def apply_rope(x, cos, sin):
    half = x.shape[-1] // 2
    x1, x2 = x[..., :half], x[..., half:]
    return jnp.concatenate([x1 * cos - x2 * sin, x2 * cos + x1 * sin], axis=-1)


def causal_short_conv(u, w):
    s = u.shape[1]
    up = jnp.pad(u, ((0, 0), (CONV_WIDTH - 1, 0), (0, 0)))
    return sum(up[:, k:k + s] * w[k] for k in range(CONV_WIDTH))


def causal_block_attention(q, k, v):
    b, s, h, dqk = q.shape
    nb = s // Q_BLOCK
    scale = 1.0 / math.sqrt(dqk)
    qb = q.reshape(b, nb, Q_BLOCK, h, dqk).transpose(1, 0, 2, 3, 4)
    key_pos = jnp.arange(s)

    def one_block(args):
        q_blk, i = args
        sc = jnp.einsum('bqhd,bkhd->bhqk', q_blk, k).astype(jnp.float32) * scale
        q_pos = i * Q_BLOCK + jnp.arange(Q_BLOCK)
        mask = key_pos[None, :] <= q_pos[:, None]
        sc = jnp.where(mask[None, None], sc, -1e30)
        p = jax.nn.softmax(sc, axis=-1).astype(v.dtype)
        return jnp.einsum('bhqk,bkhd->bqhd', p, v)

    out = lax.map(one_block, (qb, jnp.arange(nb)))
    return out.transpose(1, 0, 2, 3, 4).reshape(b, s, h, v.shape[-1])


def token_mixer(xn, cos, sin, w_in, conv_w, w_conv_out, q_norm, w_uq, kv_norm, w_ukv, w_mla_out, w_o):
    b, s, _ = xn.shape
    offsets = list(np.cumsum(IN_WIDTHS)[:-1])
    b_g, c_g, u, c_q, c_kv, k_r, g_conv, g_mla = jnp.split(xn @ w_in, offsets, axis=-1)

    y_conv = (b_g * causal_short_conv(c_g * u, conv_w)) @ w_conv_out

    q = (rmsnorm(c_q, q_norm) @ w_uq).reshape(b, s, N_HEADS, QK_NOPE + QK_ROPE)
    q_nope, q_pe = q[..., :QK_NOPE], q[..., QK_NOPE:]
    q_pe = apply_rope(q_pe, cos[:, :, None, :], sin[:, :, None, :])
    kv = (rmsnorm(c_kv, kv_norm) @ w_ukv).reshape(b, s, N_HEADS, QK_NOPE + V_HEAD)
    k_nope, v = kv[..., :QK_NOPE], kv[..., QK_NOPE:]
    k_pe = apply_rope(k_r, cos, sin)
    k_pe = jnp.broadcast_to(k_pe[:, :, None, :], (b, s, N_HEADS, QK_ROPE))
    q_full = jnp.concatenate([q_nope, q_pe], axis=-1)
    k_full = jnp.concatenate([k_nope, k_pe], axis=-1)
    attn = causal_block_attention(q_full, k_full, v).reshape(b, s, N_HEADS * V_HEAD)
    y_mla = attn @ w_mla_out

    merged = jax.nn.sigmoid(g_conv) * y_conv + jax.nn.sigmoid(g_mla) * y_mla
    return merged @ w_o


def swiglu(x, wg, wu, wd):
    return (jax.nn.silu(x @ wg) * (x @ wu)) @ wd


def moe_swiglu(x, router, wg, wu, wd):
    b, s, d = x.shape
    xt = x.reshape(b * s, d)
    logits = (xt @ router).astype(jnp.float32)
    top_vals, top_idx = lax.top_k(logits, TOP_K)
    top_w = jax.nn.softmax(top_vals, axis=-1)
    combine = jnp.sum(jax.nn.one_hot(top_idx, N_EXPERTS, dtype=jnp.float32) * top_w[..., None], axis=1)
    combine = combine.astype(x.dtype)
    y = jnp.zeros_like(xt)
    for e in range(N_EXPERTS):
        y = y + combine[:, e:e + 1] * swiglu(xt, wg[e], wu[e], wd[e])
    return y.reshape(b, s, d)


def setup_inputs(seed: int = 0) -> dict:
    key = jax.random.key(seed)
    ks = jax.random.split(key, 24)
    f32 = jnp.float32

    def nrm(k, shape, fan_in):
        return jax.random.normal(k, shape, f32) * (fan_in ** -0.5)

    def gain(k, shape):
        return 1.0 + 0.02 * jax.random.normal(k, shape, f32)

    x = jax.random.normal(ks[0], (BATCH, SEQ, D_MODEL), f32)
    offset = jax.random.randint(ks[1], (BATCH, 1), 0, 1024, dtype=jnp.int32)
    positions = offset + jnp.arange(SEQ, dtype=jnp.int32)[None, :]
    return {
        "x": x,
        "positions": positions,
        "attn_norm": gain(ks[2], (DEPTH, D_MODEL)),
        "w_in": nrm(ks[3], (DEPTH, D_MODEL, D_IN), D_MODEL),
        "conv_w": nrm(ks[4], (DEPTH, CONV_WIDTH, D_CONV), CONV_WIDTH),
        "w_conv_out": nrm(ks[5], (DEPTH, D_CONV, D_MODEL), D_CONV),
        "q_norm": gain(ks[6], (DEPTH, Q_LORA)),
        "w_uq": nrm(ks[7], (DEPTH, Q_LORA, N_HEADS * (QK_NOPE + QK_ROPE)), Q_LORA),
        "kv_norm": gain(ks[8], (DEPTH, KV_LORA)),
        "w_ukv": nrm(ks[9], (DEPTH, KV_LORA, N_HEADS * (QK_NOPE + V_HEAD)), KV_LORA),
        "w_mla_out": nrm(ks[10], (DEPTH, N_HEADS * V_HEAD, D_MODEL), N_HEADS * V_HEAD),
        "w_o": nrm(ks[11], (DEPTH, D_MODEL, D_MODEL), D_MODEL),
        "ffn_norm": gain(ks[12], (DEPTH, D_MODEL)),
        "w_gate": nrm(ks[13], (N_DENSE, D_MODEL, D_FF), D_MODEL),
        "w_up": nrm(ks[14], (N_DENSE, D_MODEL, D_FF), D_MODEL),
        "w_down": nrm(ks[15], (N_DENSE, D_FF, D_MODEL), D_FF),
        "router": nrm(ks[16], (N_MOE, D_MODEL, N_EXPERTS), D_MODEL),
        "w_gate_e": nrm(ks[17], (N_MOE, N_EXPERTS, D_MODEL, D_FF_EXPERT), D_MODEL),
        "w_up_e": nrm(ks[18], (N_MOE, N_EXPERTS, D_MODEL, D_FF_EXPERT), D_MODEL),
        "w_down_e": nrm(ks[19], (N_MOE, N_EXPERTS, D_FF_EXPERT, D_MODEL), D_FF_EXPERT),
        "final_norm": gain(ks[20], (D_MODEL,)),
    }


def reference(x, positions, attn_norm, w_in, conv_w, w_conv_out, q_norm, w_uq, kv_norm, w_ukv,
              w_mla_out, w_o, ffn_norm, w_gate, w_up, w_down, router, w_gate_e, w_up_e, w_down_e,
              final_norm):
    cos, sin = rope_tables(positions, x.dtype)
    for l in range(DEPTH):
        xn = rmsnorm(x, attn_norm[l])
        x = x + token_mixer(xn, cos, sin, w_in[l], conv_w[l], w_conv_out[l], q_norm[l], w_uq[l],
                            kv_norm[l], w_ukv[l], w_mla_out[l], w_o[l])
        hn = rmsnorm(x, ffn_norm[l])
        if l % 2 == 0:
            i = l // 2
            x = x + swiglu(hn, w_gate[i], w_up[i], w_down[i])
        else:
            i = l // 2
            x = x + moe_swiglu(hn, router[i], w_gate_e[i], w_up_e[i], w_down_e[i])
    return rmsnorm(x, final_norm)
```

```python
import functools
import math

import jax
import jax.numpy as jnp
from jax import lax
from jax.experimental import pallas as pl
from jax.experimental.pallas import tpu as pltpu

F32 = jnp.float32
BF16 = jnp.bfloat16

D_MODEL = 1024
BATCH = 8
SEQ = 2048
TOKENS = BATCH * SEQ
DEPTH = 2
D_CONV = 512
CONV_WIDTH = 3
N_HEADS = 8
QK_NOPE = 64
QK_ROPE = 32
HALF_ROPE = QK_ROPE // 2
V_HEAD = 64
Q_LORA = 384
KV_LORA = 256
ROPE_THETA = 10000.0
D_FF = 2816
N_EXPERTS = 8
TOP_K = 2
D_FF_EXPERT = 1408
EPS = 1e-6

LANES = 128
HEAD_PAD = LANES
QK_DIM = QK_NOPE + QK_ROPE
Q_SCALE = (1.0 / math.sqrt(QK_DIM)) * math.log2(math.e)
NEG_BIG = -1e30

OFF_BCU = 0
OFF_CQ = 3 * D_CONV
OFF_CKV = OFF_CQ + Q_LORA
OFF_GATE = OFF_CKV + KV_LORA
OFF_KR = OFF_GATE + 2 * D_MODEL
OFF_KRR = OFF_KR + LANES
D_IN_PAD = OFF_KRR + LANES

TM_FRONT = 512
TQ = 256
TM_MOE = 256
TM_COMB = 256
FF_CHUNK = 256
N_PAIRS = TOKENS * TOP_K
N_MOE_TILES = N_PAIRS // TM_MOE + N_EXPERTS
P_PAD = N_MOE_TILES * TM_MOE
VMEM_LIMIT = 56 * 1024 * 1024


def _rms(x, g):
    return x * lax.rsqrt(jnp.mean(x * x, axis=-1, keepdims=True) + EPS) * g


def _const_spec(shape):
    nd = len(shape)
    return pl.BlockSpec(shape, lambda *_: (0,) * nd)


def _trig_kernel(ang_ref, cos_ref, sin_ref):
    a = ang_ref[...]
    cos_ref[...] = jnp.cos(a)
    sin_ref[...] = jnp.sin(a)


def _rope_tiles(positions):
    inv_freq = ROPE_THETA ** (-jnp.arange(0, QK_ROPE, 2, dtype=F32) / QK_ROPE)
    ang = positions.astype(F32).reshape(TOKENS, 1) * inv_freq
    ang = ang.reshape(TOKENS * HALF_ROPE // LANES, LANES)
    cos, sin = pl.pallas_call(
        _trig_kernel,
        out_shape=(jax.ShapeDtypeStruct(ang.shape, F32),) * 2,
        name="rope_trig",
    )(ang)
    cos = cos.reshape(TOKENS, HALF_ROPE)
    sin = sin.reshape(TOKENS, HALF_ROPE)
    z_nope = jnp.zeros((TOKENS, QK_NOPE), F32)
    z_pad = jnp.zeros((TOKENS, HEAD_PAD - QK_DIM), F32)
    ck = jnp.concatenate([z_nope, cos, cos, z_pad], axis=1)
    sk = jnp.concatenate([z_nope, -sin, sin, z_pad], axis=1)
    return ck, sk


def _front_kernel(x_ref, an_ref, win_ref, cw_ref, wco_ref, qn_ref, wqa_ref, wqr_ref, kvn_ref, wuk_ref, wuv_ref,
                  ck_ref, sk_ref, q_out, k_out, v_out, gc_out, sg_out, cu_scr):
    tm = x_ref.shape[0]
    i = pl.program_id(0)
    xn = _rms(x_ref[...], an_ref[...]).astype(BF16)

    def proj(a, b):
        return jnp.dot(xn, win_ref[:, a:b], preferred_element_type=F32)

    bcu = proj(OFF_BCU, OFF_CQ)
    b_g = bcu[:, :D_CONV]
    cu = bcu[:, D_CONV:2 * D_CONV] * bcu[:, 2 * D_CONV:]

    @pl.when(i % (SEQ // tm) == 0)
    def _():
        cu_scr[0:8, :] = jnp.zeros((8, D_CONV), F32)

    cu_scr[8:8 + tm, :] = cu
    prev1 = cu_scr[7:7 + tm, :]
    prev2 = cu_scr[6:6 + tm, :]
    cw = cw_ref[...]
    conv = prev2 * cw[0:1, :] + prev1 * cw[1:2, :] + cu * cw[2:3, :]
    cu_scr[0:8, :] = cu[tm - 8:, :]
    y_conv = jnp.dot((b_g * conv).astype(BF16), wco_ref[...], preferred_element_type=F32)

    gates = proj(OFF_GATE, OFF_KR)
    gc_out[...] = (jax.nn.sigmoid(gates[:, :D_MODEL]) * y_conv).astype(BF16)
    sg_out[...] = jax.nn.sigmoid(gates[:, D_MODEL:]).astype(BF16)

    ck = ck_ref[...]
    sk = sk_ref[...]
    lane = lax.broadcasted_iota(jnp.int32, (1, HEAD_PAD), 1)
    nope_mask = (lane < QK_NOPE).astype(F32)
    cq_t = (ck + nope_mask) * Q_SCALE
    sq_t = sk * Q_SCALE

    cqn = _rms(proj(OFF_CQ, OFF_CKV), qn_ref[...]).astype(BF16)
    qa = jnp.dot(cqn, wqa_ref[...], preferred_element_type=F32)
    qr = jnp.dot(cqn, wqr_ref[...], preferred_element_type=F32)
    for h in range(N_HEADS):
        blk = slice(h * HEAD_PAD, (h + 1) * HEAD_PAD)
        q_out[:, blk] = (qa[:, blk] * cq_t + qr[:, blk] * sq_t).astype(BF16)

    ckvn = _rms(proj(OFF_CKV, OFF_GATE), kvn_ref[...]).astype(BF16)
    kn = jnp.dot(ckvn, wuk_ref[...], preferred_element_type=F32)
    kpe = proj(OFF_KR, OFF_KRR) * ck + proj(OFF_KRR, D_IN_PAD) * sk
    for h in range(N_HEADS):
        blk = slice(h * HEAD_PAD, (h + 1) * HEAD_PAD)
        k_out[:, blk] = (kn[:, blk] + kpe).astype(BF16)
    v_out[...] = jnp.dot(ckvn, wuv_ref[...], preferred_element_type=F32).astype(BF16)


def _front(x, an, win, cw, wco, qn, wqa, wqr, kvn, wuk, wuv, ck, sk):
    tm = TM_FRONT
    row = lambda n: pl.BlockSpec((tm, n), lambda i: (i, 0))
    return pl.pallas_call(
        _front_kernel,
        grid=(TOKENS // tm,),
        in_specs=[row(D_MODEL), _const_spec((1, D_MODEL)), _const_spec((D_MODEL, D_IN_PAD)),
                  _const_spec((CONV_WIDTH, D_CONV)), _const_spec((D_CONV, D_MODEL)),
                  _const_spec((1, Q_LORA)), _const_spec((Q_LORA, N_HEADS * HEAD_PAD)),
                  _const_spec((Q_LORA, N_HEADS * HEAD_PAD)),
                  _const_spec((1, KV_LORA)), _const_spec((KV_LORA, N_HEADS * HEAD_PAD)),
                  _const_spec((KV_LORA, N_HEADS * V_HEAD)),
                  row(HEAD_PAD), row(HEAD_PAD)],
        out_specs=[row(N_HEADS * HEAD_PAD), row(N_HEADS * HEAD_PAD), row(N_HEADS * V_HEAD),
                   row(D_MODEL), row(D_MODEL)],
        out_shape=[jax.ShapeDtypeStruct((TOKENS, N_HEADS * HEAD_PAD), BF16),
                   jax.ShapeDtypeStruct((TOKENS, N_HEADS * HEAD_PAD), BF16),
                   jax.ShapeDtypeStruct((TOKENS, N_HEADS * V_HEAD), BF16),
                   jax.ShapeDtypeStruct((TOKENS, D_MODEL), BF16),
                   jax.ShapeDtypeStruct((TOKENS, D_MODEL), BF16)],
        scratch_shapes=[pltpu.VMEM((tm + 8, D_CONV), F32)],
        compiler_params=pltpu.CompilerParams(dimension_semantics=("arbitrary",), vmem_limit_bytes=VMEM_LIMIT),
        name="front",
    )(x, an, win, cw, wco, qn, wqa, wqr, kvn, wuk, wuv, ck, sk)


def _attn_kernel(q_ref, k_ref, v_ref, o_ref):
    i = pl.program_id(2)
    row = lax.broadcasted_iota(jnp.int32, (TQ, TQ), 0)
    col = lax.broadcasted_iota(jnp.int32, (TQ, TQ), 1)
    causal = col <= row
    outs = []
    for h in range(2):
        hs = slice(h * HEAD_PAD, (h + 1) * HEAD_PAD)
        q = q_ref[:, hs]

        def step(j, carry, masked):
            m, l, acc = carry
            start = pl.multiple_of(j * TQ, TQ)
            k = k_ref[pl.ds(start, TQ), hs]
            v = v_ref[pl.ds(start, TQ), :]
            s = lax.dot_general(q, k, (((1,), (1,)), ((), ())), preferred_element_type=F32)
            if masked:
                s = jnp.where(causal, s, NEG_BIG)
            m_new = jnp.maximum(m, jnp.max(s, axis=-1, keepdims=True))
            alpha = jnp.exp2(m - m_new)
            p = jnp.exp2(s - m_new)
            l = alpha * l + jnp.sum(p, axis=-1, keepdims=True)
            acc = alpha * acc + jnp.dot(p.astype(BF16), v, preferred_element_type=F32)
            return m_new, l, acc

        init = (jnp.full((TQ, 1), NEG_BIG, F32), jnp.zeros((TQ, 1), F32), jnp.zeros((TQ, 2 * V_HEAD), F32))
        carry = lax.fori_loop(0, i, functools.partial(step, masked=False), init)
        m, l, acc = step(i, carry, True)
        outs.append(acc / l)
    lane = lax.broadcasted_iota(jnp.int32, (TQ, 2 * V_HEAD), 1)
    o_ref[...] = jnp.where(lane < V_HEAD, outs[0], outs[1]).astype(BF16)


def _attention(q, k, v):
    nq = SEQ // TQ
    return pl.pallas_call(
        _attn_kernel,
        grid=(BATCH, N_HEADS // 2, nq),
        in_specs=[pl.BlockSpec((TQ, 2 * HEAD_PAD), lambda b, hp, i: (b * nq + i, hp)),
                  pl.BlockSpec((SEQ, 2 * HEAD_PAD), lambda b, hp, i: (b, hp)),
                  pl.BlockSpec((SEQ, 2 * V_HEAD), lambda b, hp, i: (b, hp))],
        out_specs=pl.BlockSpec((TQ, 2 * V_HEAD), lambda b, hp, i: (b * nq + i, hp)),
        out_shape=jax.ShapeDtypeStruct((TOKENS, N_HEADS * V_HEAD), BF16),
        compiler_params=pltpu.CompilerParams(dimension_semantics=("arbitrary",) * 3, vmem_limit_bytes=VMEM_LIMIT),
        name="attention",
    )(q, k, v)


def _merge_core(attn_ref, gc_ref, sg_ref, x_ref, wmo_ref, wo_ref, fn_ref):
    y_mla = jnp.dot(attn_ref[...], wmo_ref[...], preferred_element_type=F32)
    merged = gc_ref[...].astype(F32) + sg_ref[...].astype(F32) * y_mla
    x2 = x_ref[...] + jnp.dot(merged.astype(BF16), wo_ref[...], preferred_element_type=F32)
    return x2, _rms(x2, fn_ref[...])


def _merge_dense_kernel(attn_ref, gc_ref, sg_ref, x_ref, wmo_ref, wo_ref, fn_ref, x2_out, hn_out):
    x2, hn = _merge_core(attn_ref, gc_ref, sg_ref, x_ref, wmo_ref, wo_ref, fn_ref)
    x2_out[...] = x2
    hn_out[...] = hn.astype(BF16)


def _merge_route_kernel(attn_ref, gc_ref, sg_ref, x_ref, wmo_ref, wo_ref, fn_ref, router_ref,
                        x2_out, hn_out, route_out):
    x2, hn = _merge_core(attn_ref, gc_ref, sg_ref, x_ref, wmo_ref, wo_ref, fn_ref)
    x2_out[...] = x2
    hn_out[...] = hn
    logits = jnp.dot(hn, router_ref[...], preferred_element_type=F32, precision=lax.Precision.HIGHEST)
    lane = lax.broadcasted_iota(jnp.int32, logits.shape, 1)
    logits = jnp.where(lane < N_EXPERTS, logits, -jnp.inf)
    m1 = jnp.max(logits, axis=-1, keepdims=True)
    i1 = jnp.min(jnp.where(logits == m1, lane, LANES), axis=-1, keepdims=True)
    rest = jnp.where(lane == i1, -jnp.inf, logits)
    m2 = jnp.max(rest, axis=-1, keepdims=True)
    i2 = jnp.min(jnp.where(rest == m2, lane, LANES), axis=-1, keepdims=True)
    t = jnp.exp(m2 - m1)
    w1 = 1.0 / (1.0 + t)
    w2 = t * w1
    route_out[...] = jnp.where(lane == 0, i1.astype(F32),
                               jnp.where(lane == 1, i2.astype(F32),
                                         jnp.where(lane == 2, w1, jnp.where(lane == 3, w2, 0.0))))


def _merge(attn, gc, sg, x, wmo, wo, fn, router=None):
    tm = TM_FRONT
    row = lambda n: pl.BlockSpec((tm, n), lambda i: (i, 0))
    in_specs = [row(N_HEADS * V_HEAD), row(D_MODEL), row(D_MODEL), row(D_MODEL),
                _const_spec((N_HEADS * V_HEAD, D_MODEL)), _const_spec((D_MODEL, D_MODEL)), _const_spec((1, D_MODEL))]
    args = [attn, gc, sg, x, wmo, wo, fn]
    if router is None:
        kern = _merge_dense_kernel
        out_specs = [row(D_MODEL), row(D_MODEL)]
        out_shape = [jax.ShapeDtypeStruct((TOKENS, D_MODEL), F32), jax.ShapeDtypeStruct((TOKENS, D_MODEL), BF16)]
        name = "merge_dense"
    else:
        kern = _merge_route_kernel
        in_specs.append(_const_spec((D_MODEL, LANES)))
        args.append(router)
        out_specs = [row(D_MODEL), row(D_MODEL), row(LANES)]
        out_shape = [jax.ShapeDtypeStruct((TOKENS, D_MODEL), F32), jax.ShapeDtypeStruct((TOKENS, D_MODEL), F32),
                     jax.ShapeDtypeStruct((TOKENS, LANES), F32)]
        name = "merge_route"
    return pl.pallas_call(
        kern, grid=(TOKENS // tm,), in_specs=in_specs, out_specs=out_specs, out_shape=out_shape,
        compiler_params=pltpu.CompilerParams(dimension_semantics=("arbitrary",), vmem_limit_bytes=VMEM_LIMIT),
        name=name,
    )(*args)


def _ffn_kernel(hn_ref, x_ref, wg_ref, wu_ref, wd_ref, o_ref):
    hn = hn_ref[...]
    acc = x_ref[...]
    for c in range(D_FF // FF_CHUNK):
        cs = slice(c * FF_CHUNK, (c + 1) * FF_CHUNK)
        g = jnp.dot(hn, wg_ref[:, cs], preferred_element_type=F32)
        u = jnp.dot(hn, wu_ref[:, cs], preferred_element_type=F32)
        a = (jax.nn.silu(g) * u).astype(BF16)
        acc = acc + jnp.dot(a, wd_ref[cs, :], preferred_element_type=F32)
    o_ref[...] = acc


def _ffn(hn, x2, wg, wu, wd):
    tm = TM_FRONT
    row = lambda n: pl.BlockSpec((tm, n), lambda i: (i, 0))
    return pl.pallas_call(
        _ffn_kernel,
        grid=(TOKENS // tm,),
        in_specs=[row(D_MODEL), row(D_MODEL), _const_spec((D_MODEL, D_FF)), _const_spec((D_MODEL, D_FF)),
                  _const_spec((D_FF, D_MODEL))],
        out_specs=row(D_MODEL),
        out_shape=jax.ShapeDtypeStruct((TOKENS, D_MODEL), F32),
        compiler_params=pltpu.CompilerParams(dimension_semantics=("arbitrary",), vmem_limit_bytes=VMEM_LIMIT),
        name="ffn_dense",
    )(hn, x2, wg, wu, wd)


def _row_gather_copy(src_hbm, row, dst_vmem, slot, r, sem):
    return pltpu.make_async_copy(src_hbm.at[pl.ds(row, 1), :], dst_vmem.at[slot, pl.ds(r, 1), :], sem.at[slot])


def _moe_kernel(te_ref, nv_ref, rt_ref, hn_hbm, rw_ref, wg_ref, wu_ref, wd_ref, y_ref, xbuf, sem):
    i = pl.program_id(0)
    nv = nv_ref[0]

    def issue(tile, slot):
        base = tile * TM_MOE

        def body(r, c):
            _row_gather_copy(hn_hbm, rt_ref[base + r], xbuf, slot, r, sem).start()
            return c

        lax.fori_loop(0, TM_MOE, body, 0, unroll=8)

    @pl.when(i == 0)
    def _():
        issue(0, 0)

    @pl.when(i + 1 < nv)
    def _():
        issue(i + 1, (i + 1) % 2)

    @pl.when(i < nv)
    def _():
        slot = i % 2
        pltpu.make_async_copy(hn_hbm.at[pl.ds(0, TM_MOE), :], xbuf.at[slot], sem.at[slot]).wait()
        x = xbuf[slot].astype(BF16)
        g = jnp.dot(x, wg_ref[0], preferred_element_type=F32)
        u = jnp.dot(x, wu_ref[0], preferred_element_type=F32)
        a = (jax.nn.silu(g) * u).astype(BF16)
        y_ref[...] = jnp.dot(a, wd_ref[0], preferred_element_type=F32) * rw_ref[...]

    @pl.when(i >= nv)
    def _():
        y_ref[...] = jnp.zeros_like(y_ref)


def _moe(tile_expert, n_valid, row_token, hn, row_w, wg, wu, wd):
    grid_spec = pltpu.PrefetchScalarGridSpec(
        num_scalar_prefetch=3,
        grid=(N_MOE_TILES,),
        in_specs=[pl.BlockSpec(memory_space=pl.ANY),
                  pl.BlockSpec((TM_MOE, 1), lambda i, te, nv, rt: (i, 0)),
                  pl.BlockSpec((1, D_MODEL, D_FF_EXPERT), lambda i, te, nv, rt: (te[i], 0, 0)),
                  pl.BlockSpec((1, D_MODEL, D_FF_EXPERT), lambda i, te, nv, rt: (te[i], 0, 0)),
                  pl.BlockSpec((1, D_FF_EXPERT, D_MODEL), lambda i, te, nv, rt: (te[i], 0, 0))],
        out_specs=pl.BlockSpec((TM_MOE, D_MODEL), lambda i, te, nv, rt: (i, 0)),
        scratch_shapes=[pltpu.VMEM((2, TM_MOE, D_MODEL), F32), pltpu.SemaphoreType.DMA((2,))],
    )
    return pl.pallas_call(
        _moe_kernel,
        grid_spec=grid_spec,
        out_shape=jax.ShapeDtypeStruct((P_PAD, D_MODEL), F32),
        compiler_params=pltpu.CompilerParams(dimension_semantics=("arbitrary",), vmem_limit_bytes=VMEM_LIMIT),
        name="moe_experts",
    )(tile_expert, n_valid, row_token, hn, row_w, wg, wu, wd)


def _combine_kernel(pos_ref, x_ref, y_hbm, fn_ref, o_ref, ybuf, sem):
    i = pl.program_id(0)
    n = pl.num_programs(0)

    def issue(tile, slot):
        base = tile * (TM_COMB * TOP_K)

        def body(r, c):
            for k in range(TOP_K):
                _row_gather_copy(y_hbm, pos_ref[base + TOP_K * r + k], ybuf, slot, k * TM_COMB + r, sem).start()
            return c

        lax.fori_loop(0, TM_COMB, body, 0, unroll=4)

    @pl.when(i == 0)
    def _():
        issue(0, 0)

    @pl.when(i + 1 < n)
    def _():
        issue(i + 1, (i + 1) % 2)

    slot = i % 2
    pltpu.make_async_copy(y_hbm.at[pl.ds(0, TOP_K * TM_COMB), :], ybuf.at[slot], sem.at[slot]).wait()
    y = ybuf[slot, 0:TM_COMB, :] + ybuf[slot, TM_COMB:, :]
    o_ref[...] = _rms(x_ref[...] + y, fn_ref[...])


def _combine(pos, x2, y_sorted, fn):
    grid_spec = pltpu.PrefetchScalarGridSpec(
        num_scalar_prefetch=1,
        grid=(TOKENS // TM_COMB,),
        in_specs=[pl.BlockSpec((TM_COMB, D_MODEL), lambda i, pos: (i, 0)),
                  pl.BlockSpec(memory_space=pl.ANY),
                  pl.BlockSpec((1, D_MODEL), lambda i, pos: (0, 0))],
        out_specs=pl.BlockSpec((TM_COMB, D_MODEL), lambda i, pos: (i, 0)),
        scratch_shapes=[pltpu.VMEM((2, TOP_K * TM_COMB, D_MODEL), F32), pltpu.SemaphoreType.DMA((2,))],
    )
    return pl.pallas_call(
        _combine_kernel,
        grid_spec=grid_spec,
        out_shape=jax.ShapeDtypeStruct((TOKENS, D_MODEL), F32),
        compiler_params=pltpu.CompilerParams(dimension_semantics=("arbitrary",), vmem_limit_bytes=VMEM_LIMIT),
        name="moe_combine",
    )(pos, x2, y_sorted, fn)


def _routing_tables(route):
    experts = route[:, :TOP_K].astype(jnp.int32).reshape(N_PAIRS)
    weights = route[:, TOP_K:2 * TOP_K].reshape(N_PAIRS)
    onehot = (experts[:, None] == jnp.arange(N_EXPERTS, dtype=jnp.int32)[None, :]).astype(jnp.int32)
    csum = jnp.cumsum(onehot, axis=0)
    counts = csum[-1]
    rank = jnp.sum(csum * onehot, axis=1) - 1
    padded = ((counts + TM_MOE - 1) // TM_MOE) * TM_MOE
    group_end = jnp.cumsum(padded)
    group_start = group_end - padded
    pos = jnp.sum(onehot * group_start[None, :], axis=1) + rank
    row_token = jnp.zeros((P_PAD,), jnp.int32).at[pos].set(jnp.arange(N_PAIRS, dtype=jnp.int32) // TOP_K)
    row_w = jnp.zeros((P_PAD,), F32).at[pos].set(weights).reshape(P_PAD, 1)
    tile_start = jnp.arange(N_MOE_TILES, dtype=jnp.int32) * TM_MOE
    tile_expert = jnp.minimum(jnp.sum((tile_start[:, None] >= group_end[None, :]).astype(jnp.int32), axis=1),
                              N_EXPERTS - 1)
    n_valid = (group_end[-1] // TM_MOE).reshape(1)
    return pos, row_token, row_w, tile_expert, n_valid


def _layer_weights(w_in, w_uq, w_ukv):
    kr = w_in[:, OFF_GATE:OFF_GATE + QK_ROPE]
    z_nope = jnp.zeros((D_MODEL, QK_NOPE), F32)
    z_pad = jnp.zeros((D_MODEL, HEAD_PAD - QK_DIM), F32)
    kr_tile = jnp.concatenate([z_nope, kr[:, :HALF_ROPE], kr[:, HALF_ROPE:], z_pad], axis=1)
    kr_rot = jnp.concatenate([z_nope, kr[:, HALF_ROPE:], kr[:, :HALF_ROPE], z_pad], axis=1)
    win = jnp.concatenate([w_in[:, :OFF_GATE], w_in[:, OFF_GATE + QK_ROPE:], kr_tile, kr_rot], axis=1).astype(BF16)

    wq = w_uq.reshape(Q_LORA, N_HEADS, QK_DIM)
    nope, x1, x2 = wq[:, :, :QK_NOPE], wq[:, :, QK_NOPE:QK_NOPE + HALF_ROPE], wq[:, :, QK_NOPE + HALF_ROPE:]
    zq_pad = jnp.zeros((Q_LORA, N_HEADS, HEAD_PAD - QK_DIM), F32)
    wqa = jnp.concatenate([nope, x1, x2, zq_pad], axis=2).reshape(Q_LORA, N_HEADS * HEAD_PAD).astype(BF16)
    wqr = jnp.concatenate([jnp.zeros_like(nope), x2, x1, zq_pad], axis=2)
    wqr = wqr.reshape(Q_LORA, N_HEADS * HEAD_PAD).astype(BF16)

    wkv = w_ukv.reshape(KV_LORA, N_HEADS, QK_NOPE + V_HEAD)
    wuk = jnp.concatenate([wkv[:, :, :QK_NOPE], jnp.zeros((KV_LORA, N_HEADS, HEAD_PAD - QK_NOPE), F32)], axis=2)
    wuk = wuk.reshape(KV_LORA, N_HEADS * HEAD_PAD).astype(BF16)
    wuv = wkv[:, :, QK_NOPE:].reshape(KV_LORA, N_HEADS * V_HEAD).astype(BF16)
    return win, wqa, wqr, wuk, wuv


def kernel(x, positions, attn_norm, w_in, conv_w, w_conv_out, q_norm, w_uq, kv_norm, w_ukv, w_mla_out, w_o, ffn_norm,
           w_gate, w_up, w_down, router, w_gate_e, w_up_e, w_down_e, final_norm):
    assert x.shape == (BATCH, SEQ, D_MODEL) and positions.shape == (BATCH, SEQ)
    ck, sk = _rope_tiles(positions)
    xt = x.reshape(TOKENS, D_MODEL)
    out = None
    for l in range(DEPTH):
        win, wqa, wqr, wuk, wuv = _layer_weights(w_in[l], w_uq[l], w_ukv[l])
        q, k, v, gc, sg = _front(xt, attn_norm[l].reshape(1, D_MODEL), win, conv_w[l], w_conv_out[l].astype(BF16),
                                 q_norm[l].reshape(1, Q_LORA), wqa, wqr, kv_norm[l].reshape(1, KV_LORA), wuk, wuv,
                                 ck, sk)
        attn = _attention(q, k, v)
        wmo = w_mla_out[l].astype(BF16)
        wo = w_o[l].astype(BF16)
        fn = ffn_norm[l].reshape(1, D_MODEL)
        if l % 2 == 0:
            j = l // 2
            x2, hn = _merge(attn, gc, sg, xt, wmo, wo, fn)
            xt = _ffn(hn, x2, w_gate[j].astype(BF16), w_up[j].astype(BF16), w_down[j].astype(BF16))
        else:
            j = l // 2
            router_pad = jnp.pad(router[j], ((0, 0), (0, LANES - N_EXPERTS)))
            x2, hn, route = _merge(attn, gc, sg, xt, wmo, wo, fn, router_pad)
            pos, row_token, row_w, tile_expert, n_valid = _routing_tables(route)
            y_sorted = _moe(tile_expert, n_valid, row_token, hn, row_w, w_gate_e[j].astype(BF16),
                            w_up_e[j].astype(BF16), w_down_e[j].astype(BF16))
            out = _combine(pos, x2, y_sorted, final_norm.reshape(1, D_MODEL))
    return out.reshape(BATCH, SEQ, D_MODEL)
```

```python
import functools
import math

import jax
import jax.numpy as jnp
from jax import lax
from jax.experimental import pallas as pl
from jax.experimental.pallas import tpu as pltpu

F32 = jnp.float32
BF16 = jnp.bfloat16

D_MODEL = 1024
BATCH = 8
SEQ = 2048
TOKENS = BATCH * SEQ
DEPTH = 2
D_CONV = 512
CONV_WIDTH = 3
N_HEADS = 8
QK_NOPE = 64
QK_ROPE = 32
HALF_ROPE = QK_ROPE // 2
V_HEAD = 64
Q_LORA = 384
KV_LORA = 256
ROPE_THETA = 10000.0
D_FF = 2816
N_EXPERTS = 8
TOP_K = 2
D_FF_EXPERT = 1408
EPS = 1e-6

LANES = 128
HEAD_PAD = LANES
QK_DIM = QK_NOPE + QK_ROPE
Q_SCALE = (1.0 / math.sqrt(QK_DIM)) * math.log2(math.e)
NEG_BIG = -1e30

OFF_BCU = 0
OFF_CQ = 3 * D_CONV
OFF_CKV = OFF_CQ + Q_LORA
OFF_GATE = OFF_CKV + KV_LORA
OFF_KR = OFF_GATE + 2 * D_MODEL
OFF_KRR = OFF_KR + LANES
D_IN_PAD = OFF_KRR + LANES

TM_FRONT = 512
TQ = 256
TK = 512
TM_MOE = 256
TM_COMB = 256
FF_CHUNK = 256
N_PAIRS = TOKENS * TOP_K
N_MOE_TILES = N_PAIRS // TM_MOE + N_EXPERTS
P_PAD = N_MOE_TILES * TM_MOE
VMEM_LIMIT = 56 * 1024 * 1024


def _rms(x, g):
    return x * lax.rsqrt(jnp.mean(x * x, axis=-1, keepdims=True) + EPS) * g


def _const_spec(shape):
    nd = len(shape)
    return pl.BlockSpec(shape, lambda *_: (0,) * nd)


def _trig_kernel(ang_ref, cos_ref, sin_ref):
    a = ang_ref[...]
    cos_ref[...] = jnp.cos(a)
    sin_ref[...] = jnp.sin(a)


def _rope_tiles(positions):
    inv_freq = ROPE_THETA ** (-jnp.arange(0, QK_ROPE, 2, dtype=F32) / QK_ROPE)
    ang = positions.astype(F32).reshape(TOKENS, 1) * inv_freq
    ang = ang.reshape(TOKENS * HALF_ROPE // LANES, LANES)
    cos, sin = pl.pallas_call(
        _trig_kernel,
        out_shape=(jax.ShapeDtypeStruct(ang.shape, F32),) * 2,
        name="rope_trig",
    )(ang)
    cos = cos.reshape(TOKENS, HALF_ROPE)
    sin = sin.reshape(TOKENS, HALF_ROPE)
    z_nope = jnp.zeros((TOKENS, QK_NOPE), F32)
    z_pad = jnp.zeros((TOKENS, HEAD_PAD - QK_DIM), F32)
    ck = jnp.concatenate([z_nope, cos, cos, z_pad], axis=1)
    sk = jnp.concatenate([z_nope, -sin, sin, z_pad], axis=1)
    return ck, sk, ck.T, sk.T


def _dot_nt(a, b):
    return lax.dot_general(a, b, (((1,), (1,)), ((), ())), preferred_element_type=F32)


def _front_kernel(x_ref, an_ref, win_ref, cw_ref, wco_ref, qn_ref, wqa_ref, wqr_ref, kvn_ref, wuk_ref, wuv_ref,
                  ck_ref, sk_ref, ckt_ref, skt_ref, q_out, k_out, v_out, gc_out, sg_out, cu_scr):
    tm = x_ref.shape[0]
    i = pl.program_id(0)
    xn = _rms(x_ref[...], an_ref[...]).astype(BF16)

    def proj(a, b):
        return jnp.dot(xn, win_ref[:, a:b], preferred_element_type=F32)

    bcu = proj(OFF_BCU, OFF_CQ)
    b_g = bcu[:, :D_CONV]
    cu = bcu[:, D_CONV:2 * D_CONV] * bcu[:, 2 * D_CONV:]

    @pl.when(i % (SEQ // tm) == 0)
    def _():
        cu_scr[0:8, :] = jnp.zeros((8, D_CONV), F32)

    cu_scr[8:8 + tm, :] = cu
    prev1 = cu_scr[7:7 + tm, :]
    prev2 = cu_scr[6:6 + tm, :]
    cw = cw_ref[...]
    conv = prev2 * cw[0:1, :] + prev1 * cw[1:2, :] + cu * cw[2:3, :]
    cu_scr[0:8, :] = cu[tm - 8:, :]
    y_conv = jnp.dot((b_g * conv).astype(BF16), wco_ref[...], preferred_element_type=F32)

    gates = proj(OFF_GATE, OFF_KR)
    gc_out[...] = (jax.nn.sigmoid(gates[:, :D_MODEL]) * y_conv).astype(BF16)
    sg_out[...] = jax.nn.sigmoid(gates[:, D_MODEL:]).astype(BF16)

    feat = lax.broadcasted_iota(jnp.int32, (HEAD_PAD, 1), 0)
    nope_mask = (feat < QK_NOPE).astype(F32)
    cq_t = (ckt_ref[...] + nope_mask) * Q_SCALE
    sq_t = skt_ref[...] * Q_SCALE
    cqn = _rms(proj(OFF_CQ, OFF_CKV), qn_ref[...]).astype(BF16)
    qa = _dot_nt(wqa_ref[...], cqn)
    qr = _dot_nt(wqr_ref[...], cqn)
    for h in range(N_HEADS):
        blk = slice(h * HEAD_PAD, (h + 1) * HEAD_PAD)
        q_out[blk, :] = (qa[blk, :] * cq_t + qr[blk, :] * sq_t).astype(BF16)

    ck = ck_ref[...]
    sk = sk_ref[...]
    ckvn = _rms(proj(OFF_CKV, OFF_GATE), kvn_ref[...]).astype(BF16)
    kn = jnp.dot(ckvn, wuk_ref[...], preferred_element_type=F32)
    kpe = proj(OFF_KR, OFF_KRR) * ck + proj(OFF_KRR, D_IN_PAD) * sk
    for h in range(N_HEADS):
        blk = slice(h * HEAD_PAD, (h + 1) * HEAD_PAD)
        k_out[:, blk] = (kn[:, blk] + kpe).astype(BF16)
    vt = _dot_nt(wuv_ref[...], ckvn).astype(BF16)
    for c in range(tm // TK):
        v_out[c] = vt[:, c * TK:(c + 1) * TK]


def _front(x, an, win, cw, wco, qn, wqa, wqr, kvn, wuk, wuv, ck, sk, ckt, skt):
    tm = TM_FRONT
    row = lambda n: pl.BlockSpec((tm, n), lambda i: (i, 0))
    col = lambda n: pl.BlockSpec((n, tm), lambda i: (0, i))
    return pl.pallas_call(
        _front_kernel,
        grid=(TOKENS // tm,),
        in_specs=[row(D_MODEL), _const_spec((1, D_MODEL)), _const_spec((D_MODEL, D_IN_PAD)),
                  _const_spec((CONV_WIDTH, D_CONV)), _const_spec((D_CONV, D_MODEL)),
                  _const_spec((1, Q_LORA)), _const_spec((N_HEADS * HEAD_PAD, Q_LORA)),
                  _const_spec((N_HEADS * HEAD_PAD, Q_LORA)),
                  _const_spec((1, KV_LORA)), _const_spec((KV_LORA, N_HEADS * HEAD_PAD)),
                  _const_spec((N_HEADS * V_HEAD, KV_LORA)),
                  row(HEAD_PAD), row(HEAD_PAD), col(HEAD_PAD), col(HEAD_PAD)],
        out_specs=[col(N_HEADS * HEAD_PAD), row(N_HEADS * HEAD_PAD),
                   pl.BlockSpec((tm // TK, N_HEADS * V_HEAD, TK), lambda i: (i, 0, 0)),
                   row(D_MODEL), row(D_MODEL)],
        out_shape=[jax.ShapeDtypeStruct((N_HEADS * HEAD_PAD, TOKENS), BF16),
                   jax.ShapeDtypeStruct((TOKENS, N_HEADS * HEAD_PAD), BF16),
                   jax.ShapeDtypeStruct((TOKENS // TK, N_HEADS * V_HEAD, TK), BF16),
                   jax.ShapeDtypeStruct((TOKENS, D_MODEL), BF16),
                   jax.ShapeDtypeStruct((TOKENS, D_MODEL), BF16)],
        scratch_shapes=[pltpu.VMEM((tm + 8, D_CONV), F32)],
        compiler_params=pltpu.CompilerParams(dimension_semantics=("arbitrary",), vmem_limit_bytes=VMEM_LIMIT),
        name="front",
    )(x, an, win, cw, wco, qn, wqa, wqr, kvn, wuk, wuv, ck, sk, ckt, skt)


def _attn_kernel(qt_ref, k_ref, vt_ref, o_ref):
    i = pl.program_id(2)
    n_full = (i * TQ) // TK
    qts = [qt_ref[h * HEAD_PAD:(h + 1) * HEAD_PAD, :] for h in range(2)]

    def step(j, carry, masked):
        start = pl.multiple_of(j * TK, TK)
        if masked:
            kv_pos = start + lax.broadcasted_iota(jnp.int32, (TK, TQ), 0)
            q_pos = i * TQ + lax.broadcasted_iota(jnp.int32, (TK, TQ), 1)
            causal = kv_pos <= q_pos
        sts = []
        for h in range(2):
            k = k_ref[pl.ds(start, TK), h * HEAD_PAD:(h + 1) * HEAD_PAD]
            st = jnp.dot(k, qts[h], preferred_element_type=F32)
            if masked:
                st = jnp.where(causal, st, NEG_BIG)
            sts.append(st)
        stats = []
        for h in range(2):
            m, l, _ = carry[h]
            m_new = jnp.maximum(m, jnp.max(sts[h], axis=0, keepdims=True))
            alpha = jnp.exp2(m - m_new)
            p = jnp.exp2(sts[h] - m_new)
            l = alpha * l + jnp.sum(p, axis=0, keepdims=True)
            stats.append((m_new, l, alpha, p.astype(BF16)))
        new = []
        for h in range(2):
            m_new, l, alpha, p = stats[h]
            vt = vt_ref[j, h * V_HEAD:(h + 1) * V_HEAD, :]
            acc = alpha * carry[h][2] + jnp.dot(vt, p, preferred_element_type=F32)
            new.append((m_new, l, acc))
        return tuple(new)

    init = tuple((jnp.full((1, TQ), NEG_BIG, F32), jnp.zeros((1, TQ), F32), jnp.zeros((V_HEAD, TQ), F32))
                 for _ in range(2))
    carry = lax.fori_loop(0, n_full, functools.partial(step, masked=False), init)
    for d in range(max(1, TQ // TK)):
        carry = step(n_full + d, carry, True)
    out_t = jnp.concatenate([acc / l for (_, l, acc) in carry], axis=0)
    o_ref[...] = out_t.T.astype(BF16)


def _attention(qt, k, vt):
    nq = SEQ // TQ
    return pl.pallas_call(
        _attn_kernel,
        grid=(BATCH, N_HEADS // 2, nq),
        in_specs=[pl.BlockSpec((2 * HEAD_PAD, TQ), lambda b, hp, i: (hp, b * nq + i)),
                  pl.BlockSpec((SEQ, 2 * HEAD_PAD), lambda b, hp, i: (b, hp)),
                  pl.BlockSpec((SEQ // TK, 2 * V_HEAD, TK), lambda b, hp, i: (b, hp, 0))],
        out_specs=pl.BlockSpec((TQ, 2 * V_HEAD), lambda b, hp, i: (b * nq + i, hp)),
        out_shape=jax.ShapeDtypeStruct((TOKENS, N_HEADS * V_HEAD), BF16),
        compiler_params=pltpu.CompilerParams(dimension_semantics=("arbitrary",) * 3, vmem_limit_bytes=VMEM_LIMIT),
        name="attention",
    )(qt, k, vt)


def _merge_core(attn_ref, gc_ref, sg_ref, x_ref, wmo_ref, wo_ref, fn_ref):
    y_mla = jnp.dot(attn_ref[...], wmo_ref[...], preferred_element_type=F32)
    merged = gc_ref[...].astype(F32) + sg_ref[...].astype(F32) * y_mla
    x2 = x_ref[...] + jnp.dot(merged.astype(BF16), wo_ref[...], preferred_element_type=F32)
    return x2, _rms(x2, fn_ref[...])


def _merge_dense_kernel(attn_ref, gc_ref, sg_ref, x_ref, wmo_ref, wo_ref, fn_ref, x2_out, hn_out):
    x2, hn = _merge_core(attn_ref, gc_ref, sg_ref, x_ref, wmo_ref, wo_ref, fn_ref)
    x2_out[...] = x2
    hn_out[...] = hn.astype(BF16)


def _merge_route_kernel(attn_ref, gc_ref, sg_ref, x_ref, wmo_ref, wo_ref, fn_ref, router_ref,
                        x2_out, hn_out, route_out):
    x2, hn = _merge_core(attn_ref, gc_ref, sg_ref, x_ref, wmo_ref, wo_ref, fn_ref)
    x2_out[...] = x2
    hn_out[...] = hn
    logits = jnp.dot(hn, router_ref[...], preferred_element_type=F32, precision=lax.Precision.HIGHEST)
    lane = lax.broadcasted_iota(jnp.int32, logits.shape, 1)
    logits = jnp.where(lane < N_EXPERTS, logits, -jnp.inf)
    m1 = jnp.max(logits, axis=-1, keepdims=True)
    i1 = jnp.min(jnp.where(logits == m1, lane, LANES), axis=-1, keepdims=True)
    rest = jnp.where(lane == i1, -jnp.inf, logits)
    m2 = jnp.max(rest, axis=-1, keepdims=True)
    i2 = jnp.min(jnp.where(rest == m2, lane, LANES), axis=-1, keepdims=True)
    t = jnp.exp(m2 - m1)
    w1 = 1.0 / (1.0 + t)
    w2 = t * w1
    route_out[...] = jnp.where(lane == 0, i1.astype(F32),
                               jnp.where(lane == 1, i2.astype(F32),
                                         jnp.where(lane == 2, w1, jnp.where(lane == 3, w2, 0.0))))


def _merge(attn, gc, sg, x, wmo, wo, fn, router=None):
    tm = TM_FRONT
    row = lambda n: pl.BlockSpec((tm, n), lambda i: (i, 0))
    in_specs = [row(N_HEADS * V_HEAD), row(D_MODEL), row(D_MODEL), row(D_MODEL),
                _const_spec((N_HEADS * V_HEAD, D_MODEL)), _const_spec((D_MODEL, D_MODEL)), _const_spec((1, D_MODEL))]
    args = [attn, gc, sg, x, wmo, wo, fn]
    if router is None:
        kern = _merge_dense_kernel
        out_specs = [row(D_MODEL), row(D_MODEL)]
        out_shape = [jax.ShapeDtypeStruct((TOKENS, D_MODEL), F32), jax.ShapeDtypeStruct((TOKENS, D_MODEL), BF16)]
        name = "merge_dense"
    else:
        kern = _merge_route_kernel
        in_specs.append(_const_spec((D_MODEL, LANES)))
        args.append(router)
        out_specs = [row(D_MODEL), row(D_MODEL), row(LANES)]
        out_shape = [jax.ShapeDtypeStruct((TOKENS, D_MODEL), F32), jax.ShapeDtypeStruct((TOKENS, D_MODEL), F32),
                     jax.ShapeDtypeStruct((TOKENS, LANES), F32)]
        name = "merge_route"
    return pl.pallas_call(
        kern, grid=(TOKENS // tm,), in_specs=in_specs, out_specs=out_specs, out_shape=out_shape,
        compiler_params=pltpu.CompilerParams(dimension_semantics=("arbitrary",), vmem_limit_bytes=VMEM_LIMIT),
        name=name,
    )(*args)


def _ffn_kernel(hn_ref, x_ref, wg_ref, wu_ref, wd_ref, o_ref):
    hn = hn_ref[...]
    acc = x_ref[...]
    for c in range(D_FF // FF_CHUNK):
        cs = slice(c * FF_CHUNK, (c + 1) * FF_CHUNK)
        g = jnp.dot(hn, wg_ref[:, cs], preferred_element_type=F32)
        u = jnp.dot(hn, wu_ref[:, cs], preferred_element_type=F32)
        a = (jax.nn.silu(g) * u).astype(BF16)
        acc = acc + jnp.dot(a, wd_ref[cs, :], preferred_element_type=F32)
    o_ref[...] = acc


def _ffn(hn, x2, wg, wu, wd):
    tm = TM_FRONT
    row = lambda n: pl.BlockSpec((tm, n), lambda i: (i, 0))
    return pl.pallas_call(
        _ffn_kernel,
        grid=(TOKENS // tm,),
        in_specs=[row(D_MODEL), row(D_MODEL), _const_spec((D_MODEL, D_FF)), _const_spec((D_MODEL, D_FF)),
                  _const_spec((D_FF, D_MODEL))],
        out_specs=row(D_MODEL),
        out_shape=jax.ShapeDtypeStruct((TOKENS, D_MODEL), F32),
        compiler_params=pltpu.CompilerParams(dimension_semantics=("arbitrary",), vmem_limit_bytes=VMEM_LIMIT),
        name="ffn_dense",
    )(hn, x2, wg, wu, wd)


def _row_gather_copy(src_hbm, row, dst_vmem, slot, r, sem):
    return pltpu.make_async_copy(src_hbm.at[pl.ds(row, 1), :], dst_vmem.at[slot, pl.ds(r, 1), :], sem.at[slot])


def _moe_kernel(te_ref, nv_ref, rt_ref, hn_hbm, rw_ref, wg_ref, wu_ref, wd_ref, y_ref, xbuf, sem):
    i = pl.program_id(0)
    nv = nv_ref[0]

    def issue(tile, slot):
        base = tile * TM_MOE

        def body(r, c):
            _row_gather_copy(hn_hbm, rt_ref[base + r], xbuf, slot, r, sem).start()
            return c

        lax.fori_loop(0, TM_MOE, body, 0, unroll=8)

    @pl.when(i == 0)
    def _():
        issue(0, 0)

    @pl.when(i + 1 < nv)
    def _():
        issue(i + 1, (i + 1) % 2)

    @pl.when(i < nv)
    def _():
        slot = i % 2
        pltpu.make_async_copy(hn_hbm.at[pl.ds(0, TM_MOE), :], xbuf.at[slot], sem.at[slot]).wait()
        x = xbuf[slot].astype(BF16)
        g = jnp.dot(x, wg_ref[0], preferred_element_type=F32)
        u = jnp.dot(x, wu_ref[0], preferred_element_type=F32)
        a = (jax.nn.silu(g) * u).astype(BF16)
        y_ref[...] = jnp.dot(a, wd_ref[0], preferred_element_type=F32) * rw_ref[...]

    @pl.when(i >= nv)
    def _():
        y_ref[...] = jnp.zeros_like(y_ref)


def _moe(tile_expert, n_valid, row_token, hn, row_w, wg, wu, wd):
    grid_spec = pltpu.PrefetchScalarGridSpec(
        num_scalar_prefetch=3,
        grid=(N_MOE_TILES,),
        in_specs=[pl.BlockSpec(memory_space=pl.ANY),
                  pl.BlockSpec((TM_MOE, 1), lambda i, te, nv, rt: (i, 0)),
                  pl.BlockSpec((1, D_MODEL, D_FF_EXPERT), lambda i, te, nv, rt: (te[i], 0, 0)),
                  pl.BlockSpec((1, D_MODEL, D_FF_EXPERT), lambda i, te, nv, rt: (te[i], 0, 0)),
                  pl.BlockSpec((1, D_FF_EXPERT, D_MODEL), lambda i, te, nv, rt: (te[i], 0, 0))],
        out_specs=pl.BlockSpec((TM_MOE, D_MODEL), lambda i, te, nv, rt: (i, 0)),
        scratch_shapes=[pltpu.VMEM((2, TM_MOE, D_MODEL), F32), pltpu.SemaphoreType.DMA((2,))],
    )
    return pl.pallas_call(
        _moe_kernel,
        grid_spec=grid_spec,
        out_shape=jax.ShapeDtypeStruct((P_PAD, D_MODEL), F32),
        compiler_params=pltpu.CompilerParams(dimension_semantics=("arbitrary",), vmem_limit_bytes=VMEM_LIMIT),
        name="moe_experts",
    )(tile_expert, n_valid, row_token, hn, row_w, wg, wu, wd)


def _combine_kernel(pos_ref, x_ref, y_hbm, fn_ref, o_ref, ybuf, sem):
    i = pl.program_id(0)
    n = pl.num_programs(0)

    def issue(tile, slot):
        base = tile * (TM_COMB * TOP_K)

        def body(r, c):
            for k in range(TOP_K):
                _row_gather_copy(y_hbm, pos_ref[base + TOP_K * r + k], ybuf, slot, k * TM_COMB + r, sem).start()
            return c

        lax.fori_loop(0, TM_COMB, body, 0, unroll=4)

    @pl.when(i == 0)
    def _():
        issue(0, 0)

    @pl.when(i + 1 < n)
    def _():
        issue(i + 1, (i + 1) % 2)

    slot = i % 2
    pltpu.make_async_copy(y_hbm.at[pl.ds(0, TOP_K * TM_COMB), :], ybuf.at[slot], sem.at[slot]).wait()
    y = ybuf[slot, 0:TM_COMB, :] + ybuf[slot, TM_COMB:, :]
    o_ref[...] = _rms(x_ref[...] + y, fn_ref[...])


def _combine(pos, x2, y_sorted, fn):
    grid_spec = pltpu.PrefetchScalarGridSpec(
        num_scalar_prefetch=1,
        grid=(TOKENS // TM_COMB,),
        in_specs=[pl.BlockSpec((TM_COMB, D_MODEL), lambda i, pos: (i, 0)),
                  pl.BlockSpec(memory_space=pl.ANY),
                  pl.BlockSpec((1, D_MODEL), lambda i, pos: (0, 0))],
        out_specs=pl.BlockSpec((TM_COMB, D_MODEL), lambda i, pos: (i, 0)),
        scratch_shapes=[pltpu.VMEM((2, TOP_K * TM_COMB, D_MODEL), F32), pltpu.SemaphoreType.DMA((2,))],
    )
    return pl.pallas_call(
        _combine_kernel,
        grid_spec=grid_spec,
        out_shape=jax.ShapeDtypeStruct((TOKENS, D_MODEL), F32),
        compiler_params=pltpu.CompilerParams(dimension_semantics=("arbitrary",), vmem_limit_bytes=VMEM_LIMIT),
        name="moe_combine",
    )(pos, x2, y_sorted, fn)


def _routing_tables(route):
    experts = route[:, :TOP_K].astype(jnp.int32).reshape(N_PAIRS)
    weights = route[:, TOP_K:2 * TOP_K].reshape(N_PAIRS)
    onehot = (experts[:, None] == jnp.arange(N_EXPERTS, dtype=jnp.int32)[None, :]).astype(jnp.int32)
    csum = jnp.cumsum(onehot, axis=0)
    counts = csum[-1]
    rank = jnp.sum(csum * onehot, axis=1) - 1
    padded = ((counts + TM_MOE - 1) // TM_MOE) * TM_MOE
    group_end = jnp.cumsum(padded)
    group_start = group_end - padded
    pos = jnp.sum(onehot * group_start[None, :], axis=1) + rank
    row_token = jnp.zeros((P_PAD,), jnp.int32).at[pos].set(jnp.arange(N_PAIRS, dtype=jnp.int32) // TOP_K)
    row_w = jnp.zeros((P_PAD,), F32).at[pos].set(weights).reshape(P_PAD, 1)
    tile_start = jnp.arange(N_MOE_TILES, dtype=jnp.int32) * TM_MOE
    tile_expert = jnp.minimum(jnp.sum((tile_start[:, None] >= group_end[None, :]).astype(jnp.int32), axis=1),
                              N_EXPERTS - 1)
    n_valid = (group_end[-1] // TM_MOE).reshape(1)
    return pos, row_token, row_w, tile_expert, n_valid


def _layer_weights(w_in, w_uq, w_ukv):
    kr = w_in[:, OFF_GATE:OFF_GATE + QK_ROPE]
    z_nope = jnp.zeros((D_MODEL, QK_NOPE), F32)
    z_pad = jnp.zeros((D_MODEL, HEAD_PAD - QK_DIM), F32)
    kr_tile = jnp.concatenate([z_nope, kr[:, :HALF_ROPE], kr[:, HALF_ROPE:], z_pad], axis=1)
    kr_rot = jnp.concatenate([z_nope, kr[:, HALF_ROPE:], kr[:, :HALF_ROPE], z_pad], axis=1)
    win = jnp.concatenate([w_in[:, :OFF_GATE], w_in[:, OFF_GATE + QK_ROPE:], kr_tile, kr_rot], axis=1).astype(BF16)

    wq = w_uq.reshape(Q_LORA, N_HEADS, QK_DIM)
    nope, x1, x2 = wq[:, :, :QK_NOPE], wq[:, :, QK_NOPE:QK_NOPE + HALF_ROPE], wq[:, :, QK_NOPE + HALF_ROPE:]
    zq_pad = jnp.zeros((Q_LORA, N_HEADS, HEAD_PAD - QK_DIM), F32)
    wqa = jnp.concatenate([nope, x1, x2, zq_pad], axis=2).reshape(Q_LORA, N_HEADS * HEAD_PAD).T.astype(BF16)
    wqr = jnp.concatenate([jnp.zeros_like(nope), x2, x1, zq_pad], axis=2)
    wqr = wqr.reshape(Q_LORA, N_HEADS * HEAD_PAD).T.astype(BF16)

    wkv = w_ukv.reshape(KV_LORA, N_HEADS, QK_NOPE + V_HEAD)
    wuk = jnp.concatenate([wkv[:, :, :QK_NOPE], jnp.zeros((KV_LORA, N_HEADS, HEAD_PAD - QK_NOPE), F32)], axis=2)
    wuk = wuk.reshape(KV_LORA, N_HEADS * HEAD_PAD).astype(BF16)
    wuv = wkv[:, :, QK_NOPE:].reshape(KV_LORA, N_HEADS * V_HEAD).T.astype(BF16)
    return win, wqa, wqr, wuk, wuv


def kernel(x, positions, attn_norm, w_in, conv_w, w_conv_out, q_norm, w_uq, kv_norm, w_ukv, w_mla_out, w_o, ffn_norm,
           w_gate, w_up, w_down, router, w_gate_e, w_up_e, w_down_e, final_norm):
    assert x.shape == (BATCH, SEQ, D_MODEL) and positions.shape == (BATCH, SEQ)
    ck, sk, ckt, skt = _rope_tiles(positions)
    xt = x.reshape(TOKENS, D_MODEL)
    out = None
    for l in range(DEPTH):
        win, wqa, wqr, wuk, wuv = _layer_weights(w_in[l], w_uq[l], w_ukv[l])
        q, k, v, gc, sg = _front(xt, attn_norm[l].reshape(1, D_MODEL), win, conv_w[l], w_conv_out[l].astype(BF16),
                                 q_norm[l].reshape(1, Q_LORA), wqa, wqr, kv_norm[l].reshape(1, KV_LORA), wuk, wuv,
                                 ck, sk, ckt, skt)
        attn = _attention(q, k, v)
        wmo = w_mla_out[l].astype(BF16)
        wo = w_o[l].astype(BF16)
        fn = ffn_norm[l].reshape(1, D_MODEL)
        if l % 2 == 0:
            j = l // 2
            x2, hn = _merge(attn, gc, sg, xt, wmo, wo, fn)
            xt = _ffn(hn, x2, w_gate[j].astype(BF16), w_up[j].astype(BF16), w_down[j].astype(BF16))
        else:
            j = l // 2
            router_pad = jnp.pad(router[j], ((0, 0), (0, LANES - N_EXPERTS)))
            x2, hn, route = _merge(attn, gc, sg, xt, wmo, wo, fn, router_pad)
            pos, row_token, row_w, tile_expert, n_valid = _routing_tables(route)
            y_sorted = _moe(tile_expert, n_valid, row_token, hn, row_w, w_gate_e[j].astype(BF16),
                            w_up_e[j].astype(BF16), w_down_e[j].astype(BF16))
            out = _combine(pos, x2, y_sorted, final_norm.reshape(1, D_MODEL))
    return out.reshape(BATCH, SEQ, D_MODEL)
```

```python
import functools
import math

import jax
import jax.numpy as jnp
from jax import lax
from jax.experimental import pallas as pl
from jax.experimental.pallas import tpu as pltpu

F32 = jnp.float32
BF16 = jnp.bfloat16

D_MODEL = 1024
BATCH = 8
SEQ = 2048
TOKENS = BATCH * SEQ
DEPTH = 2
D_CONV = 512
CONV_WIDTH = 3
N_HEADS = 8
QK_NOPE = 64
QK_ROPE = 32
HALF_ROPE = QK_ROPE // 2
V_HEAD = 64
Q_LORA = 384
KV_LORA = 256
ROPE_THETA = 10000.0
D_FF = 2816
N_EXPERTS = 8
TOP_K = 2
D_FF_EXPERT = 1408
EPS = 1e-6

LANES = 128
HEAD_PAD = LANES
QK_DIM = QK_NOPE + QK_ROPE
Q_SCALE = (1.0 / math.sqrt(QK_DIM)) * math.log2(math.e)
NEG_BIG = -1e30

OFF_BCU = 0
OFF_CQ = 3 * D_CONV
OFF_CKV = OFF_CQ + Q_LORA
OFF_GATE = OFF_CKV + KV_LORA
OFF_KR = OFF_GATE + 2 * D_MODEL
OFF_KRR = OFF_KR + LANES
D_IN_PAD = OFF_KRR + LANES

TM_FRONT = 512
TQ = 256
TK = 512
TM_MOE = 256
TM_COMB = 256
FF_CHUNK = 256
N_PAIRS = TOKENS * TOP_K
N_MOE_TILES = N_PAIRS // TM_MOE + N_EXPERTS
P_PAD = N_MOE_TILES * TM_MOE
VMEM_LIMIT = 56 * 1024 * 1024


def _rms(x, g):
    return x * lax.rsqrt(jnp.mean(x * x, axis=-1, keepdims=True) + EPS) * g


def _const_spec(shape):
    nd = len(shape)
    return pl.BlockSpec(shape, lambda *_: (0,) * nd)


def _trig_kernel(ang_ref, cos_ref, sin_ref):
    a = ang_ref[...]
    cos_ref[...] = jnp.cos(a)
    sin_ref[...] = jnp.sin(a)


def _rope_tiles(positions):
    inv_freq = ROPE_THETA ** (-jnp.arange(0, QK_ROPE, 2, dtype=F32) / QK_ROPE)
    ang = positions.astype(F32).reshape(TOKENS, 1) * inv_freq
    ang = ang.reshape(TOKENS * HALF_ROPE // LANES, LANES)
    cos, sin = pl.pallas_call(
        _trig_kernel,
        out_shape=(jax.ShapeDtypeStruct(ang.shape, F32),) * 2,
        name="rope_trig",
    )(ang)
    cos = cos.reshape(TOKENS, HALF_ROPE)
    sin = sin.reshape(TOKENS, HALF_ROPE)
    z_nope = jnp.zeros((TOKENS, QK_NOPE), F32)
    z_pad = jnp.zeros((TOKENS, HEAD_PAD - QK_DIM), F32)
    ck = jnp.concatenate([z_nope, cos, cos, z_pad], axis=1)
    sk = jnp.concatenate([z_nope, -sin, sin, z_pad], axis=1)
    return ck, sk, ck.T, sk.T


def _dot_nt(a, b):
    return lax.dot_general(a, b, (((1,), (1,)), ((), ())), preferred_element_type=F32)


def _front_kernel(x_ref, an_ref, win_ref, cw_ref, wco_ref, qn_ref, wqa_ref, wqr_ref, kvn_ref, wuk_ref, wuv_ref,
                  ck_ref, sk_ref, ckt_ref, skt_ref, q_out, k_out, v_out, gc_out, sg_out, cu_scr):
    tm = x_ref.shape[0]
    i = pl.program_id(0)
    xn = _rms(x_ref[...], an_ref[...]).astype(BF16)

    def proj(a, b):
        return jnp.dot(xn, win_ref[:, a:b], preferred_element_type=F32)

    bcu = proj(OFF_BCU, OFF_CQ)
    b_g = bcu[:, :D_CONV]
    cu = bcu[:, D_CONV:2 * D_CONV] * bcu[:, 2 * D_CONV:]

    @pl.when(i % (SEQ // tm) == 0)
    def _():
        cu_scr[0:8, :] = jnp.zeros((8, D_CONV), F32)

    cu_scr[8:8 + tm, :] = cu
    prev1 = cu_scr[7:7 + tm, :]
    prev2 = cu_scr[6:6 + tm, :]
    cw = cw_ref[...]
    conv = prev2 * cw[0:1, :] + prev1 * cw[1:2, :] + cu * cw[2:3, :]
    cu_scr[0:8, :] = cu[tm - 8:, :]
    y_conv = jnp.dot((b_g * conv).astype(BF16), wco_ref[...], preferred_element_type=F32)

    gates = proj(OFF_GATE, OFF_KR)
    gc_out[...] = (jax.nn.sigmoid(gates[:, :D_MODEL]) * y_conv).astype(BF16)
    sg_out[...] = jax.nn.sigmoid(gates[:, D_MODEL:]).astype(BF16)

    feat = lax.broadcasted_iota(jnp.int32, (HEAD_PAD, 1), 0)
    nope_mask = (feat < QK_NOPE).astype(F32)
    cq_t = (ckt_ref[...] + nope_mask) * Q_SCALE
    sq_t = skt_ref[...] * Q_SCALE
    cqn = _rms(proj(OFF_CQ, OFF_CKV), qn_ref[...]).astype(BF16)
    qa = _dot_nt(wqa_ref[...], cqn)
    qr = _dot_nt(wqr_ref[...], cqn)
    for h in range(N_HEADS):
        blk = slice(h * HEAD_PAD, (h + 1) * HEAD_PAD)
        q_out[blk, :] = (qa[blk, :] * cq_t + qr[blk, :] * sq_t).astype(BF16)

    ck = ck_ref[...]
    sk = sk_ref[...]
    ckvn = _rms(proj(OFF_CKV, OFF_GATE), kvn_ref[...]).astype(BF16)
    kn = jnp.dot(ckvn, wuk_ref[...], preferred_element_type=F32)
    kpe = proj(OFF_KR, OFF_KRR) * ck + proj(OFF_KRR, D_IN_PAD) * sk
    for h in range(N_HEADS):
        blk = slice(h * HEAD_PAD, (h + 1) * HEAD_PAD)
        k_out[:, blk] = (kn[:, blk] + kpe).astype(BF16)
    vt = _dot_nt(wuv_ref[...], ckvn).astype(BF16)
    for c in range(tm // TK):
        v_out[c] = vt[:, c * TK:(c + 1) * TK]


def _front(x, an, win, cw, wco, qn, wqa, wqr, kvn, wuk, wuv, ck, sk, ckt, skt):
    tm = TM_FRONT
    row = lambda n: pl.BlockSpec((tm, n), lambda i: (i, 0))
    col = lambda n: pl.BlockSpec((n, tm), lambda i: (0, i))
    return pl.pallas_call(
        _front_kernel,
        grid=(TOKENS // tm,),
        in_specs=[row(D_MODEL), _const_spec((1, D_MODEL)), _const_spec((D_MODEL, D_IN_PAD)),
                  _const_spec((CONV_WIDTH, D_CONV)), _const_spec((D_CONV, D_MODEL)),
                  _const_spec((1, Q_LORA)), _const_spec((N_HEADS * HEAD_PAD, Q_LORA)),
                  _const_spec((N_HEADS * HEAD_PAD, Q_LORA)),
                  _const_spec((1, KV_LORA)), _const_spec((KV_LORA, N_HEADS * HEAD_PAD)),
                  _const_spec((N_HEADS * V_HEAD, KV_LORA)),
                  row(HEAD_PAD), row(HEAD_PAD), col(HEAD_PAD), col(HEAD_PAD)],
        out_specs=[col(N_HEADS * HEAD_PAD), row(N_HEADS * HEAD_PAD),
                   pl.BlockSpec((tm // TK, N_HEADS * V_HEAD, TK), lambda i: (i, 0, 0)),
                   row(D_MODEL), row(D_MODEL)],
        out_shape=[jax.ShapeDtypeStruct((N_HEADS * HEAD_PAD, TOKENS), BF16),
                   jax.ShapeDtypeStruct((TOKENS, N_HEADS * HEAD_PAD), BF16),
                   jax.ShapeDtypeStruct((TOKENS // TK, N_HEADS * V_HEAD, TK), BF16),
                   jax.ShapeDtypeStruct((TOKENS, D_MODEL), BF16),
                   jax.ShapeDtypeStruct((TOKENS, D_MODEL), BF16)],
        scratch_shapes=[pltpu.VMEM((tm + 8, D_CONV), F32)],
        compiler_params=pltpu.CompilerParams(dimension_semantics=("arbitrary",), vmem_limit_bytes=VMEM_LIMIT),
        name="front",
    )(x, an, win, cw, wco, qn, wqa, wqr, kvn, wuk, wuv, ck, sk, ckt, skt)


def _attn_items():
    items = []
    for i in range(SEQ // TQ):
        ks = 0
        while ks + TK <= i * TQ:
            items.append((i, ks, TK, False))
            ks += TK
        items.append((i, ks, (i + 1) * TQ - ks, True))
    return items


def _attn_kernel(qt_ref, k_ref, vt_ref, o_ref):
    row = lax.broadcasted_iota(jnp.int32, (TQ, TQ), 0)
    col = lax.broadcasted_iota(jnp.int32, (TQ, TQ), 1)
    causal = row <= col

    def scores(item):
        i, ks, n, diag = item
        sts = []
        for h in range(2):
            hs = slice(h * HEAD_PAD, (h + 1) * HEAD_PAD)
            st = jnp.dot(k_ref[ks:ks + n, hs], qt_ref[hs, i * TQ:(i + 1) * TQ], preferred_element_type=F32)
            if diag:
                tail = jnp.where(causal, st[n - TQ:, :], NEG_BIG)
                st = tail if n == TQ else jnp.concatenate([st[:n - TQ, :], tail], axis=0)
            sts.append(st)
        return sts

    def consume(item, sts, carry):
        i, ks, n, _ = item
        stats = []
        for h in range(2):
            m, l, _ = carry[h]
            m_new = jnp.maximum(m, jnp.max(sts[h], axis=0, keepdims=True))
            alpha = jnp.exp2(m - m_new)
            p = jnp.exp2(sts[h] - m_new)
            l = alpha * l + jnp.sum(p, axis=0, keepdims=True)
            stats.append((m_new, l, alpha, p.astype(BF16)))
        new = []
        for h in range(2):
            m_new, l, alpha, p = stats[h]
            vt = vt_ref[ks // TK, h * V_HEAD:(h + 1) * V_HEAD, ks % TK:ks % TK + n]
            acc = alpha * carry[h][2] + jnp.dot(vt, p, preferred_element_type=F32)
            new.append((m_new, l, acc))
        return new

    items = _attn_items()
    init = [(jnp.full((1, TQ), NEG_BIG, F32), jnp.zeros((1, TQ), F32), jnp.zeros((V_HEAD, TQ), F32))
            for _ in range(2)]
    cur = scores(items[0])
    carry = init
    for t, item in enumerate(items):
        nxt = scores(items[t + 1]) if t + 1 < len(items) else None
        carry = consume(item, cur, carry)
        if item[3]:
            i = item[0]
            out_t = jnp.concatenate([acc / l for (_, l, acc) in carry], axis=0)
            o_ref[i * TQ:(i + 1) * TQ, :] = out_t.T.astype(BF16)
            carry = init
        cur = nxt


def _attention(qt, k, vt):
    return pl.pallas_call(
        _attn_kernel,
        grid=(BATCH, N_HEADS // 2),
        in_specs=[pl.BlockSpec((2 * HEAD_PAD, SEQ), lambda b, hp: (hp, b)),
                  pl.BlockSpec((SEQ, 2 * HEAD_PAD), lambda b, hp: (b, hp)),
                  pl.BlockSpec((SEQ // TK, 2 * V_HEAD, TK), lambda b, hp: (b, hp, 0))],
        out_specs=pl.BlockSpec((SEQ, 2 * V_HEAD), lambda b, hp: (b, hp)),
        out_shape=jax.ShapeDtypeStruct((TOKENS, N_HEADS * V_HEAD), BF16),
        compiler_params=pltpu.CompilerParams(dimension_semantics=("arbitrary",) * 2, vmem_limit_bytes=VMEM_LIMIT),
        name="attention",
    )(qt, k, vt)


def _merge_core(attn_ref, gc_ref, sg_ref, x_ref, wmo_ref, wo_ref, fn_ref):
    y_mla = jnp.dot(attn_ref[...], wmo_ref[...], preferred_element_type=F32)
    merged = gc_ref[...].astype(F32) + sg_ref[...].astype(F32) * y_mla
    x2 = x_ref[...] + jnp.dot(merged.astype(BF16), wo_ref[...], preferred_element_type=F32)
    return x2, _rms(x2, fn_ref[...])


def _merge_dense_kernel(attn_ref, gc_ref, sg_ref, x_ref, wmo_ref, wo_ref, fn_ref, x2_out, hn_out):
    x2, hn = _merge_core(attn_ref, gc_ref, sg_ref, x_ref, wmo_ref, wo_ref, fn_ref)
    x2_out[...] = x2
    hn_out[...] = hn.astype(BF16)


def _merge_route_kernel(attn_ref, gc_ref, sg_ref, x_ref, wmo_ref, wo_ref, fn_ref, router_ref,
                        x2_out, hn_out, route_out):
    x2, hn = _merge_core(attn_ref, gc_ref, sg_ref, x_ref, wmo_ref, wo_ref, fn_ref)
    x2_out[...] = x2
    hn_out[...] = hn
    logits = jnp.dot(hn, router_ref[...], preferred_element_type=F32, precision=lax.Precision.HIGHEST)
    lane = lax.broadcasted_iota(jnp.int32, logits.shape, 1)
    logits = jnp.where(lane < N_EXPERTS, logits, -jnp.inf)
    m1 = jnp.max(logits, axis=-1, keepdims=True)
    i1 = jnp.min(jnp.where(logits == m1, lane, LANES), axis=-1, keepdims=True)
    rest = jnp.where(lane == i1, -jnp.inf, logits)
    m2 = jnp.max(rest, axis=-1, keepdims=True)
    i2 = jnp.min(jnp.where(rest == m2, lane, LANES), axis=-1, keepdims=True)
    t = jnp.exp(m2 - m1)
    w1 = 1.0 / (1.0 + t)
    w2 = t * w1
    route_out[...] = jnp.where(lane == 0, i1.astype(F32),
                               jnp.where(lane == 1, i2.astype(F32),
                                         jnp.where(lane == 2, w1, jnp.where(lane == 3, w2, 0.0))))


def _merge(attn, gc, sg, x, wmo, wo, fn, router=None):
    tm = TM_FRONT
    row = lambda n: pl.BlockSpec((tm, n), lambda i: (i, 0))
    in_specs = [row(N_HEADS * V_HEAD), row(D_MODEL), row(D_MODEL), row(D_MODEL),
                _const_spec((N_HEADS * V_HEAD, D_MODEL)), _const_spec((D_MODEL, D_MODEL)), _const_spec((1, D_MODEL))]
    args = [attn, gc, sg, x, wmo, wo, fn]
    if router is None:
        kern = _merge_dense_kernel
        out_specs = [row(D_MODEL), row(D_MODEL)]
        out_shape = [jax.ShapeDtypeStruct((TOKENS, D_MODEL), F32), jax.ShapeDtypeStruct((TOKENS, D_MODEL), BF16)]
        name = "merge_dense"
    else:
        kern = _merge_route_kernel
        in_specs.append(_const_spec((D_MODEL, LANES)))
        args.append(router)
        out_specs = [row(D_MODEL), row(D_MODEL), row(LANES)]
        out_shape = [jax.ShapeDtypeStruct((TOKENS, D_MODEL), F32), jax.ShapeDtypeStruct((TOKENS, D_MODEL), F32),
                     jax.ShapeDtypeStruct((TOKENS, LANES), F32)]
        name = "merge_route"
    return pl.pallas_call(
        kern, grid=(TOKENS // tm,), in_specs=in_specs, out_specs=out_specs, out_shape=out_shape,
        compiler_params=pltpu.CompilerParams(dimension_semantics=("arbitrary",), vmem_limit_bytes=VMEM_LIMIT),
        name=name,
    )(*args)


def _ffn_kernel(hn_ref, x_ref, wg_ref, wu_ref, wd_ref, o_ref):
    hn = hn_ref[...]
    acc = x_ref[...]
    for c in range(D_FF // FF_CHUNK):
        cs = slice(c * FF_CHUNK, (c + 1) * FF_CHUNK)
        g = jnp.dot(hn, wg_ref[:, cs], preferred_element_type=F32)
        u = jnp.dot(hn, wu_ref[:, cs], preferred_element_type=F32)
        a = (jax.nn.silu(g) * u).astype(BF16)
        acc = acc + jnp.dot(a, wd_ref[cs, :], preferred_element_type=F32)
    o_ref[...] = acc


def _ffn(hn, x2, wg, wu, wd):
    tm = TM_FRONT
    row = lambda n: pl.BlockSpec((tm, n), lambda i: (i, 0))
    return pl.pallas_call(
        _ffn_kernel,
        grid=(TOKENS // tm,),
        in_specs=[row(D_MODEL), row(D_MODEL), _const_spec((D_MODEL, D_FF)), _const_spec((D_MODEL, D_FF)),
                  _const_spec((D_FF, D_MODEL))],
        out_specs=row(D_MODEL),
        out_shape=jax.ShapeDtypeStruct((TOKENS, D_MODEL), F32),
        compiler_params=pltpu.CompilerParams(dimension_semantics=("arbitrary",), vmem_limit_bytes=VMEM_LIMIT),
        name="ffn_dense",
    )(hn, x2, wg, wu, wd)


def _row_gather_copy(src_hbm, row, dst_vmem, slot, r, sem):
    return pltpu.make_async_copy(src_hbm.at[pl.ds(row, 1), :], dst_vmem.at[slot, pl.ds(r, 1), :], sem.at[slot])


def _moe_kernel(te_ref, nv_ref, rt_ref, hn_hbm, rw_ref, wg_ref, wu_ref, wd_ref, y_ref, xbuf, sem):
    i = pl.program_id(0)
    nv = nv_ref[0]

    def issue(tile, slot):
        base = tile * TM_MOE

        def body(r, c):
            _row_gather_copy(hn_hbm, rt_ref[base + r], xbuf, slot, r, sem).start()
            return c

        lax.fori_loop(0, TM_MOE, body, 0, unroll=8)

    @pl.when(i == 0)
    def _():
        issue(0, 0)

    @pl.when(i + 1 < nv)
    def _():
        issue(i + 1, (i + 1) % 2)

    @pl.when(i < nv)
    def _():
        slot = i % 2
        pltpu.make_async_copy(hn_hbm.at[pl.ds(0, TM_MOE), :], xbuf.at[slot], sem.at[slot]).wait()
        x = xbuf[slot].astype(BF16)
        g = jnp.dot(x, wg_ref[0], preferred_element_type=F32)
        u = jnp.dot(x, wu_ref[0], preferred_element_type=F32)
        a = (jax.nn.silu(g) * u).astype(BF16)
        y_ref[...] = jnp.dot(a, wd_ref[0], preferred_element_type=F32) * rw_ref[...]

    @pl.when(i >= nv)
    def _():
        y_ref[...] = jnp.zeros_like(y_ref)


def _moe(tile_expert, n_valid, row_token, hn, row_w, wg, wu, wd):
    grid_spec = pltpu.PrefetchScalarGridSpec(
        num_scalar_prefetch=3,
        grid=(N_MOE_TILES,),
        in_specs=[pl.BlockSpec(memory_space=pl.ANY),
                  pl.BlockSpec((TM_MOE, 1), lambda i, te, nv, rt: (i, 0)),
                  pl.BlockSpec((1, D_MODEL, D_FF_EXPERT), lambda i, te, nv, rt: (te[i], 0, 0)),
                  pl.BlockSpec((1, D_MODEL, D_FF_EXPERT), lambda i, te, nv, rt: (te[i], 0, 0)),
                  pl.BlockSpec((1, D_FF_EXPERT, D_MODEL), lambda i, te, nv, rt: (te[i], 0, 0))],
        out_specs=pl.BlockSpec((TM_MOE, D_MODEL), lambda i, te, nv, rt: (i, 0)),
        scratch_shapes=[pltpu.VMEM((2, TM_MOE, D_MODEL), F32), pltpu.SemaphoreType.DMA((2,))],
    )
    return pl.pallas_call(
        _moe_kernel,
        grid_spec=grid_spec,
        out_shape=jax.ShapeDtypeStruct((P_PAD, D_MODEL), F32),
        compiler_params=pltpu.CompilerParams(dimension_semantics=("arbitrary",), vmem_limit_bytes=VMEM_LIMIT),
        name="moe_experts",
    )(tile_expert, n_valid, row_token, hn, row_w, wg, wu, wd)


def _combine_kernel(pos_ref, x_ref, y_hbm, fn_ref, o_ref, ybuf, sem):
    i = pl.program_id(0)
    n = pl.num_programs(0)

    def issue(tile, slot):
        base = tile * (TM_COMB * TOP_K)

        def body(r, c):
            for k in range(TOP_K):
                _row_gather_copy(y_hbm, pos_ref[base + TOP_K * r + k], ybuf, slot, k * TM_COMB + r, sem).start()
            return c

        lax.fori_loop(0, TM_COMB, body, 0, unroll=4)

    @pl.when(i == 0)
    def _():
        issue(0, 0)

    @pl.when(i + 1 < n)
    def _():
        issue(i + 1, (i + 1) % 2)

    slot = i % 2
    pltpu.make_async_copy(y_hbm.at[pl.ds(0, TOP_K * TM_COMB), :], ybuf.at[slot], sem.at[slot]).wait()
    y = ybuf[slot, 0:TM_COMB, :] + ybuf[slot, TM_COMB:, :]
    o_ref[...] = _rms(x_ref[...] + y, fn_ref[...])


def _combine(pos, x2, y_sorted, fn):
    grid_spec = pltpu.PrefetchScalarGridSpec(
        num_scalar_prefetch=1,
        grid=(TOKENS // TM_COMB,),
        in_specs=[pl.BlockSpec((TM_COMB, D_MODEL), lambda i, pos: (i, 0)),
                  pl.BlockSpec(memory_space=pl.ANY),
                  pl.BlockSpec((1, D_MODEL), lambda i, pos: (0, 0))],
        out_specs=pl.BlockSpec((TM_COMB, D_MODEL), lambda i, pos: (i, 0)),
        scratch_shapes=[pltpu.VMEM((2, TOP_K * TM_COMB, D_MODEL), F32), pltpu.SemaphoreType.DMA((2,))],
    )
    return pl.pallas_call(
        _combine_kernel,
        grid_spec=grid_spec,
        out_shape=jax.ShapeDtypeStruct((TOKENS, D_MODEL), F32),
        compiler_params=pltpu.CompilerParams(dimension_semantics=("arbitrary",), vmem_limit_bytes=VMEM_LIMIT),
        name="moe_combine",
    )(pos, x2, y_sorted, fn)


def _routing_tables(route):
    experts = route[:, :TOP_K].astype(jnp.int32).reshape(N_PAIRS)
    weights = route[:, TOP_K:2 * TOP_K].reshape(N_PAIRS)
    onehot = (experts[:, None] == jnp.arange(N_EXPERTS, dtype=jnp.int32)[None, :]).astype(jnp.int32)
    csum = jnp.cumsum(onehot, axis=0)
    counts = csum[-1]
    rank = jnp.sum(csum * onehot, axis=1) - 1
    padded = ((counts + TM_MOE - 1) // TM_MOE) * TM_MOE
    group_end = jnp.cumsum(padded)
    group_start = group_end - padded
    pos = jnp.sum(onehot * group_start[None, :], axis=1) + rank
    row_token = jnp.zeros((P_PAD,), jnp.int32).at[pos].set(jnp.arange(N_PAIRS, dtype=jnp.int32) // TOP_K)
    row_w = jnp.zeros((P_PAD,), F32).at[pos].set(weights).reshape(P_PAD, 1)
    tile_start = jnp.arange(N_MOE_TILES, dtype=jnp.int32) * TM_MOE
    tile_expert = jnp.minimum(jnp.sum((tile_start[:, None] >= group_end[None, :]).astype(jnp.int32), axis=1),
                              N_EXPERTS - 1)
    n_valid = (group_end[-1] // TM_MOE).reshape(1)
    return pos, row_token, row_w, tile_expert, n_valid


def _layer_weights(w_in, w_uq, w_ukv):
    kr = w_in[:, OFF_GATE:OFF_GATE + QK_ROPE]
    z_nope = jnp.zeros((D_MODEL, QK_NOPE), F32)
    z_pad = jnp.zeros((D_MODEL, HEAD_PAD - QK_DIM), F32)
    kr_tile = jnp.concatenate([z_nope, kr[:, :HALF_ROPE], kr[:, HALF_ROPE:], z_pad], axis=1)
    kr_rot = jnp.concatenate([z_nope, kr[:, HALF_ROPE:], kr[:, :HALF_ROPE], z_pad], axis=1)
    win = jnp.concatenate([w_in[:, :OFF_GATE], w_in[:, OFF_GATE + QK_ROPE:], kr_tile, kr_rot], axis=1).astype(BF16)

    wq = w_uq.reshape(Q_LORA, N_HEADS, QK_DIM)
    nope, x1, x2 = wq[:, :, :QK_NOPE], wq[:, :, QK_NOPE:QK_NOPE + HALF_ROPE], wq[:, :, QK_NOPE + HALF_ROPE:]
    zq_pad = jnp.zeros((Q_LORA, N_HEADS, HEAD_PAD - QK_DIM), F32)
    wqa = jnp.concatenate([nope, x1, x2, zq_pad], axis=2).reshape(Q_LORA, N_HEADS * HEAD_PAD).T.astype(BF16)
    wqr = jnp.concatenate([jnp.zeros_like(nope), x2, x1, zq_pad], axis=2)
    wqr = wqr.reshape(Q_LORA, N_HEADS * HEAD_PAD).T.astype(BF16)

    wkv = w_ukv.reshape(KV_LORA, N_HEADS, QK_NOPE + V_HEAD)
    wuk = jnp.concatenate([wkv[:, :, :QK_NOPE], jnp.zeros((KV_LORA, N_HEADS, HEAD_PAD - QK_NOPE), F32)], axis=2)
    wuk = wuk.reshape(KV_LORA, N_HEADS * HEAD_PAD).astype(BF16)
    wuv = wkv[:, :, QK_NOPE:].reshape(KV_LORA, N_HEADS * V_HEAD).T.astype(BF16)
    return win, wqa, wqr, wuk, wuv


def kernel(x, positions, attn_norm, w_in, conv_w, w_conv_out, q_norm, w_uq, kv_norm, w_ukv, w_mla_out, w_o, ffn_norm,
           w_gate, w_up, w_down, router, w_gate_e, w_up_e, w_down_e, final_norm):
    assert x.shape == (BATCH, SEQ, D_MODEL) and positions.shape == (BATCH, SEQ)
    ck, sk, ckt, skt = _rope_tiles(positions)
    xt = x.reshape(TOKENS, D_MODEL)
    out = None
    for l in range(DEPTH):
        win, wqa, wqr, wuk, wuv = _layer_weights(w_in[l], w_uq[l], w_ukv[l])
        q, k, v, gc, sg = _front(xt, attn_norm[l].reshape(1, D_MODEL), win, conv_w[l], w_conv_out[l].astype(BF16),
                                 q_norm[l].reshape(1, Q_LORA), wqa, wqr, kv_norm[l].reshape(1, KV_LORA), wuk, wuv,
                                 ck, sk, ckt, skt)
        attn = _attention(q, k, v)
        wmo = w_mla_out[l].astype(BF16)
        wo = w_o[l].astype(BF16)
        fn = ffn_norm[l].reshape(1, D_MODEL)
        if l % 2 == 0:
            j = l // 2
            x2, hn = _merge(attn, gc, sg, xt, wmo, wo, fn)
            xt = _ffn(hn, x2, w_gate[j].astype(BF16), w_up[j].astype(BF16), w_down[j].astype(BF16))
        else:
            j = l // 2
            router_pad = jnp.pad(router[j], ((0, 0), (0, LANES - N_EXPERTS)))
            x2, hn, route = _merge(attn, gc, sg, xt, wmo, wo, fn, router_pad)
            pos, row_token, row_w, tile_expert, n_valid = _routing_tables(route)
            y_sorted = _moe(tile_expert, n_valid, row_token, hn, row_w, w_gate_e[j].astype(BF16),
                            w_up_e[j].astype(BF16), w_down_e[j].astype(BF16))
            out = _combine(pos, x2, y_sorted, final_norm.reshape(1, D_MODEL))
    return out.reshape(BATCH, SEQ, D_MODEL)
```

```python
import math

import jax
import jax.numpy as jnp
from jax import lax
from jax.experimental import pallas as pl
from jax.experimental.pallas import tpu as pltpu

F32 = jnp.float32
BF16 = jnp.bfloat16

D_MODEL = 1024
BATCH = 8
SEQ = 2048
TOKENS = BATCH * SEQ
DEPTH = 2
D_CONV = 512
CONV_WIDTH = 3
N_HEADS = 8
QK_NOPE = 64
QK_ROPE = 32
HALF_ROPE = QK_ROPE // 2
V_HEAD = 64
Q_LORA = 384
KV_LORA = 256
ROPE_THETA = 10000.0
D_FF = 2816
N_EXPERTS = 8
TOP_K = 2
D_FF_EXPERT = 1408
EPS = 1e-6

LANES = 128
HEAD_PAD = LANES
QK_DIM = QK_NOPE + QK_ROPE
Q_SCALE = (1.0 / math.sqrt(QK_DIM)) * math.log2(math.e)
NEG_BIG = -1e30

OFF_BCU = 0
OFF_CQ = 3 * D_CONV
OFF_CKV = OFF_CQ + Q_LORA
OFF_GATE = OFF_CKV + KV_LORA
OFF_KR = OFF_GATE + 2 * D_MODEL
OFF_KRR = OFF_KR + LANES
D_IN_PAD = OFF_KRR + LANES

TM_FRONT = 512
TQ = 256
TK = 512
FF_CHUNK = 256
VMEM_LIMIT = 56 * 1024 * 1024

SUBLANES = 8
TM_MOE = 256
TT = 256
N_TT = TOKENS // TT
E_PAD = 16
PACKED = D_MODEL // 2
N_PAIRS = TOKENS * TOP_K
SLOTS = TT * TOP_K + N_EXPERTS * SUBLANES
_MAX_ROWS = N_PAIRS + N_TT * N_EXPERTS * (SUBLANES - 1) + N_EXPERTS * (TM_MOE - SUBLANES)
N_MOE_TILES = -(-_MAX_ROWS // TM_MOE)
P_PAD = N_MOE_TILES * TM_MOE
SLAB_PIECES = tuple(TT >> s for s in range(6))
WAIT_PIECES = (512,) + SLAB_PIECES
assert SLAB_PIECES[-1] == SUBLANES and SLOTS < 2 * WAIT_PIECES[0]


def _rms(x, g):
    return x * lax.rsqrt(jnp.mean(x * x, axis=-1, keepdims=True) + EPS) * g


def _const_spec(shape):
    nd = len(shape)
    return pl.BlockSpec(shape, lambda *_: (0,) * nd)


def _trig_kernel(ang_ref, cos_ref, sin_ref):
    a = ang_ref[...]
    cos_ref[...] = jnp.cos(a)
    sin_ref[...] = jnp.sin(a)


def _rope_tiles(positions):
    inv_freq = ROPE_THETA ** (-jnp.arange(0, QK_ROPE, 2, dtype=F32) / QK_ROPE)
    ang = positions.astype(F32).reshape(TOKENS, 1) * inv_freq
    ang = ang.reshape(TOKENS * HALF_ROPE // LANES, LANES)
    cos, sin = pl.pallas_call(
        _trig_kernel,
        out_shape=(jax.ShapeDtypeStruct(ang.shape, F32),) * 2,
        name="rope_trig",
    )(ang)
    cos = cos.reshape(TOKENS, HALF_ROPE)
    sin = sin.reshape(TOKENS, HALF_ROPE)
    z_nope = jnp.zeros((TOKENS, QK_NOPE), F32)
    z_pad = jnp.zeros((TOKENS, HEAD_PAD - QK_DIM), F32)
    ck = jnp.concatenate([z_nope, cos, cos, z_pad], axis=1)
    sk = jnp.concatenate([z_nope, -sin, sin, z_pad], axis=1)
    return ck, sk, ck.T, sk.T


def _dot_nt(a, b):
    return lax.dot_general(a, b, (((1,), (1,)), ((), ())), preferred_element_type=F32)


def _front_kernel(x_ref, an_ref, win_ref, cw_ref, wco_ref, qn_ref, wqa_ref, wqr_ref, kvn_ref, wuk_ref, wuv_ref,
                  ck_ref, sk_ref, ckt_ref, skt_ref, q_out, k_out, v_out, gc_out, sg_out, cu_scr):
    tm = x_ref.shape[0]
    i = pl.program_id(0)
    xn = _rms(x_ref[...], an_ref[...]).astype(BF16)

    def proj(a, b):
        return jnp.dot(xn, win_ref[:, a:b], preferred_element_type=F32)

    bcu = proj(OFF_BCU, OFF_CQ)
    b_g = bcu[:, :D_CONV]
    cu = bcu[:, D_CONV:2 * D_CONV] * bcu[:, 2 * D_CONV:]

    @pl.when(i % (SEQ // tm) == 0)
    def _():
        cu_scr[0:8, :] = jnp.zeros((8, D_CONV), F32)

    cu_scr[8:8 + tm, :] = cu
    prev1 = cu_scr[7:7 + tm, :]
    prev2 = cu_scr[6:6 + tm, :]
    cw = cw_ref[...]
    conv = prev2 * cw[0:1, :] + prev1 * cw[1:2, :] + cu * cw[2:3, :]
    cu_scr[0:8, :] = cu[tm - 8:, :]
    y_conv = jnp.dot((b_g * conv).astype(BF16), wco_ref[...], preferred_element_type=F32)

    gates = proj(OFF_GATE, OFF_KR)
    gc_out[...] = (jax.nn.sigmoid(gates[:, :D_MODEL]) * y_conv).astype(BF16)
    sg_out[...] = jax.nn.sigmoid(gates[:, D_MODEL:]).astype(BF16)

    feat = lax.broadcasted_iota(jnp.int32, (HEAD_PAD, 1), 0)
    nope_mask = (feat < QK_NOPE).astype(F32)
    cq_t = (ckt_ref[...] + nope_mask) * Q_SCALE
    sq_t = skt_ref[...] * Q_SCALE
    cqn = _rms(proj(OFF_CQ, OFF_CKV), qn_ref[...]).astype(BF16)
    qa = _dot_nt(wqa_ref[...], cqn)
    qr = _dot_nt(wqr_ref[...], cqn)
    for h in range(N_HEADS):
        blk = slice(h * HEAD_PAD, (h + 1) * HEAD_PAD)
        q_out[blk, :] = (qa[blk, :] * cq_t + qr[blk, :] * sq_t).astype(BF16)

    ck = ck_ref[...]
    sk = sk_ref[...]
    ckvn = _rms(proj(OFF_CKV, OFF_GATE), kvn_ref[...]).astype(BF16)
    kn = jnp.dot(ckvn, wuk_ref[...], preferred_element_type=F32)
    kpe = proj(OFF_KR, OFF_KRR) * ck + proj(OFF_KRR, D_IN_PAD) * sk
    for h in range(N_HEADS):
        blk = slice(h * HEAD_PAD, (h + 1) * HEAD_PAD)
        k_out[:, blk] = (kn[:, blk] + kpe).astype(BF16)
    vt = _dot_nt(wuv_ref[...], ckvn).astype(BF16)
    for c in range(tm // TK):
        v_out[c] = vt[:, c * TK:(c + 1) * TK]


def _front(x, an, win, cw, wco, qn, wqa, wqr, kvn, wuk, wuv, ck, sk, ckt, skt):
    tm = TM_FRONT
    row = lambda n: pl.BlockSpec((tm, n), lambda i: (i, 0))
    col = lambda n: pl.BlockSpec((n, tm), lambda i: (0, i))
    return pl.pallas_call(
        _front_kernel,
        grid=(TOKENS // tm,),
        in_specs=[row(D_MODEL), _const_spec((1, D_MODEL)), _const_spec((D_MODEL, D_IN_PAD)),
                  _const_spec((CONV_WIDTH, D_CONV)), _const_spec((D_CONV, D_MODEL)),
                  _const_spec((1, Q_LORA)), _const_spec((N_HEADS * HEAD_PAD, Q_LORA)),
                  _const_spec((N_HEADS * HEAD_PAD, Q_LORA)),
                  _const_spec((1, KV_LORA)), _const_spec((KV_LORA, N_HEADS * HEAD_PAD)),
                  _const_spec((N_HEADS * V_HEAD, KV_LORA)),
                  row(HEAD_PAD), row(HEAD_PAD), col(HEAD_PAD), col(HEAD_PAD)],
        out_specs=[col(N_HEADS * HEAD_PAD), row(N_HEADS * HEAD_PAD),
                   pl.BlockSpec((tm // TK, N_HEADS * V_HEAD, TK), lambda i: (i, 0, 0)),
                   row(D_MODEL), row(D_MODEL)],
        out_shape=[jax.ShapeDtypeStruct((N_HEADS * HEAD_PAD, TOKENS), BF16),
                   jax.ShapeDtypeStruct((TOKENS, N_HEADS * HEAD_PAD), BF16),
                   jax.ShapeDtypeStruct((TOKENS // TK, N_HEADS * V_HEAD, TK), BF16),
                   jax.ShapeDtypeStruct((TOKENS, D_MODEL), BF16),
                   jax.ShapeDtypeStruct((TOKENS, D_MODEL), BF16)],
        scratch_shapes=[pltpu.VMEM((tm + 8, D_CONV), F32)],
        compiler_params=pltpu.CompilerParams(dimension_semantics=("arbitrary",), vmem_limit_bytes=VMEM_LIMIT),
        name="front",
    )(x, an, win, cw, wco, qn, wqa, wqr, kvn, wuk, wuv, ck, sk, ckt, skt)


def _attn_items():
    items = []
    for i in range(SEQ // TQ):
        ks = 0
        while ks + TK <= i * TQ:
            items.append((i, ks, TK, False))
            ks += TK
        items.append((i, ks, (i + 1) * TQ - ks, True))
    return items


def _attn_kernel(qt_ref, k_ref, vt_ref, o_ref):
    row = lax.broadcasted_iota(jnp.int32, (TQ, TQ), 0)
    col = lax.broadcasted_iota(jnp.int32, (TQ, TQ), 1)
    causal = row <= col

    def scores(item):
        i, ks, n, diag = item
        sts = []
        for h in range(2):
            hs = slice(h * HEAD_PAD, (h + 1) * HEAD_PAD)
            st = jnp.dot(k_ref[ks:ks + n, hs], qt_ref[hs, i * TQ:(i + 1) * TQ], preferred_element_type=F32)
            if diag:
                tail = jnp.where(causal, st[n - TQ:, :], NEG_BIG)
                st = tail if n == TQ else jnp.concatenate([st[:n - TQ, :], tail], axis=0)
            sts.append(st)
        return sts

    def consume(item, sts, carry):
        i, ks, n, _ = item
        stats = []
        for h in range(2):
            m, l, _ = carry[h]
            m_new = jnp.maximum(m, jnp.max(sts[h], axis=0, keepdims=True))
            alpha = jnp.exp2(m - m_new)
            p = jnp.exp2(sts[h] - m_new)
            l = alpha * l + jnp.sum(p, axis=0, keepdims=True)
            stats.append((m_new, l, alpha, p.astype(BF16)))
        new = []
        for h in range(2):
            m_new, l, alpha, p = stats[h]
            vt = vt_ref[ks // TK, h * V_HEAD:(h + 1) * V_HEAD, ks % TK:ks % TK + n]
            acc = alpha * carry[h][2] + jnp.dot(vt, p, preferred_element_type=F32)
            new.append((m_new, l, acc))
        return new

    items = _attn_items()
    init = [(jnp.full((1, TQ), NEG_BIG, F32), jnp.zeros((1, TQ), F32), jnp.zeros((V_HEAD, TQ), F32))
            for _ in range(2)]
    cur = scores(items[0])
    carry = init
    for t, item in enumerate(items):
        nxt = scores(items[t + 1]) if t + 1 < len(items) else None
        carry = consume(item, cur, carry)
        if item[3]:
            i = item[0]
            out_t = jnp.concatenate([acc / l for (_, l, acc) in carry], axis=0)
            o_ref[i * TQ:(i + 1) * TQ, :] = out_t.T.astype(BF16)
            carry = init
        cur = nxt


def _attention(qt, k, vt):
    return pl.pallas_call(
        _attn_kernel,
        grid=(BATCH, N_HEADS // 2),
        in_specs=[pl.BlockSpec((2 * HEAD_PAD, SEQ), lambda b, hp: (hp, b)),
                  pl.BlockSpec((SEQ, 2 * HEAD_PAD), lambda b, hp: (b, hp)),
                  pl.BlockSpec((SEQ // TK, 2 * V_HEAD, TK), lambda b, hp: (b, hp, 0))],
        out_specs=pl.BlockSpec((SEQ, 2 * V_HEAD), lambda b, hp: (b, hp)),
        out_shape=jax.ShapeDtypeStruct((TOKENS, N_HEADS * V_HEAD), BF16),
        compiler_params=pltpu.CompilerParams(dimension_semantics=("arbitrary",) * 2, vmem_limit_bytes=VMEM_LIMIT),
        name="attention",
    )(qt, k, vt)


def _merge_core(attn_ref, gc_ref, sg_ref, x_ref, wmo_ref, wo_ref, fn_ref):
    y_mla = jnp.dot(attn_ref[...], wmo_ref[...], preferred_element_type=F32)
    merged = gc_ref[...].astype(F32) + sg_ref[...].astype(F32) * y_mla
    x2 = x_ref[...] + jnp.dot(merged.astype(BF16), wo_ref[...], preferred_element_type=F32)
    return x2, _rms(x2, fn_ref[...])


def _merge_dense_kernel(attn_ref, gc_ref, sg_ref, x_ref, wmo_ref, wo_ref, fn_ref, x2_out, hn_out):
    x2, hn = _merge_core(attn_ref, gc_ref, sg_ref, x_ref, wmo_ref, wo_ref, fn_ref)
    x2_out[...] = x2
    hn_out[...] = hn.astype(BF16)


def _merge_route_kernel(attn_ref, gc_ref, sg_ref, x_ref, wmo_ref, wo_ref, fn_ref, router_ref,
                        x2_out, hn_out, route_out, route_t_out):
    x2, hn = _merge_core(attn_ref, gc_ref, sg_ref, x_ref, wmo_ref, wo_ref, fn_ref)
    x2_out[...] = x2
    hb = hn.astype(BF16)
    hn_out[...] = hb
    hl = (hn - hb.astype(F32)).astype(BF16)
    r = router_ref[...]
    rh = r.astype(BF16)
    rl = (r - rh.astype(F32)).astype(BF16)
    logits = (jnp.dot(hb, rh, preferred_element_type=F32) + jnp.dot(hl, rh, preferred_element_type=F32)
              + jnp.dot(hb, rl, preferred_element_type=F32))
    lane = lax.broadcasted_iota(jnp.int32, logits.shape, 1)
    logits = jnp.where(lane < N_EXPERTS, logits, -jnp.inf)
    m1 = jnp.max(logits, axis=-1, keepdims=True)
    i1 = jnp.min(jnp.where(logits == m1, lane, LANES), axis=-1, keepdims=True)
    rest = jnp.where(lane == i1, -jnp.inf, logits)
    m2 = jnp.max(rest, axis=-1, keepdims=True)
    i2 = jnp.min(jnp.where(rest == m2, lane, LANES), axis=-1, keepdims=True)
    t = jnp.exp(m2 - m1)
    w1 = 1.0 / (1.0 + t)
    w2 = t * w1
    route = jnp.where(lane == 0, i1.astype(F32),
                      jnp.where(lane == 1, i2.astype(F32), jnp.where(lane == 2, w1, jnp.where(lane == 3, w2, 0.0))))
    route_out[...] = route
    route_t_out[...] = route.T[0:SUBLANES, :]


def _merge(attn, gc, sg, x, wmo, wo, fn, router=None):
    tm = TM_FRONT
    row = lambda n: pl.BlockSpec((tm, n), lambda i: (i, 0))
    in_specs = [row(N_HEADS * V_HEAD), row(D_MODEL), row(D_MODEL), row(D_MODEL),
                _const_spec((N_HEADS * V_HEAD, D_MODEL)), _const_spec((D_MODEL, D_MODEL)), _const_spec((1, D_MODEL))]
    args = [attn, gc, sg, x, wmo, wo, fn]
    if router is None:
        kern = _merge_dense_kernel
        out_specs = [row(D_MODEL), row(D_MODEL)]
        out_shape = [jax.ShapeDtypeStruct((TOKENS, D_MODEL), F32), jax.ShapeDtypeStruct((TOKENS, D_MODEL), BF16)]
        name = "merge_dense"
    else:
        kern = _merge_route_kernel
        in_specs.append(_const_spec((D_MODEL, LANES)))
        args.append(router)
        out_specs = [row(D_MODEL), row(D_MODEL), row(LANES), pl.BlockSpec((SUBLANES, tm), lambda i: (0, i))]
        out_shape = [jax.ShapeDtypeStruct((TOKENS, D_MODEL), F32), jax.ShapeDtypeStruct((TOKENS, D_MODEL), BF16),
                     jax.ShapeDtypeStruct((TOKENS, LANES), F32), jax.ShapeDtypeStruct((SUBLANES, TOKENS), F32)]
        name = "merge_route"
    return pl.pallas_call(
        kern, grid=(TOKENS // tm,), in_specs=in_specs, out_specs=out_specs, out_shape=out_shape,
        compiler_params=pltpu.CompilerParams(dimension_semantics=("arbitrary",), vmem_limit_bytes=VMEM_LIMIT),
        name=name,
    )(*args)


def _ffn_kernel(hn_ref, x_ref, wg_ref, wu_ref, wd_ref, o_ref):
    hn = hn_ref[...]
    acc = x_ref[...]
    for c in range(D_FF // FF_CHUNK):
        cs = slice(c * FF_CHUNK, (c + 1) * FF_CHUNK)
        g = jnp.dot(hn, wg_ref[:, cs], preferred_element_type=F32)
        u = jnp.dot(hn, wu_ref[:, cs], preferred_element_type=F32)
        a = (jax.nn.silu(g) * u).astype(BF16)
        acc = acc + jnp.dot(a, wd_ref[cs, :], preferred_element_type=F32)
    o_ref[...] = acc


def _ffn(hn, x2, wg, wu, wd):
    tm = TM_FRONT
    row = lambda n: pl.BlockSpec((tm, n), lambda i: (i, 0))
    return pl.pallas_call(
        _ffn_kernel,
        grid=(TOKENS // tm,),
        in_specs=[row(D_MODEL), row(D_MODEL), _const_spec((D_MODEL, D_FF)), _const_spec((D_MODEL, D_FF)),
                  _const_spec((D_FF, D_MODEL))],
        out_specs=row(D_MODEL),
        out_shape=jax.ShapeDtypeStruct((TOKENS, D_MODEL), F32),
        compiler_params=pltpu.CompilerParams(dimension_semantics=("arbitrary",), vmem_limit_bytes=VMEM_LIMIT),
        name="ffn_dense",
    )(hn, x2, wg, wu, wd)


def _pieces(n, sizes):
    return [((n & z) != 0, n & ~(2 * z - 1), z) for z in sizes]


def _aligned(rows, z):
    return pl.ds(rows if isinstance(rows, int) else pl.multiple_of(rows, SUBLANES), z)


def _start_slab_copies(tile, base_ref, offp_ref, np_ref, make_copy):
    for e in range(N_EXPERTS):
        n = np_ref[tile * N_EXPERTS + e]
        src = offp_ref[tile * N_EXPERTS + e]
        dst = base_ref[tile * N_EXPERTS + e]
        for pred, off, z in _pieces(n, SLAB_PIECES):
            @pl.when(pred)
            def _(off=off, z=z):
                make_copy(src + off, dst + off, z).start()


def _wait_rows(total, make_copy):
    for pred, _, z in _pieces(total, WAIT_PIECES):
        @pl.when(pred)
        def _(z=z):
            make_copy(0, 0, z).wait()


def _dispatch_kernel(base_ref, offp_ref, np_ref, tot_ref, tail_ref, hn_ref, rt_ref, xs_hbm, xs_scr, zbuf, sem, zsem):
    i = pl.program_id(0)
    slot = i % 2

    def copy(buf_slot):
        return lambda s, d, z: pltpu.make_async_copy(xs_scr.at[buf_slot, _aligned(s, z), :],
                                                     xs_hbm.at[_aligned(d, z), :], sem.at[buf_slot])

    def zero_copy(d, z):
        return pltpu.make_async_copy(zbuf.at[pl.ds(0, z), :], xs_hbm.at[_aligned(d, z), :], zsem)

    @pl.when(i == 0)
    def _():
        zbuf[...] = jnp.zeros_like(zbuf)
        for phase in ("start", "wait"):
            for e in range(N_EXPERTS):
                t0 = tail_ref[e]
                n = tail_ref[N_EXPERTS + e]
                full = lax.shift_right_logical(n, jnp.int32(TM_MOE.bit_length() - 1))

                def chunk(c, carry, t0=t0, phase=phase):
                    cp = zero_copy(t0 + c * TM_MOE, TM_MOE)
                    cp.start() if phase == "start" else cp.wait()
                    return carry

                lax.fori_loop(0, full, chunk, 0)
                rest = n & (TM_MOE - 1)
                for pred, off, z in _pieces(rest, SLAB_PIECES[1:]):
                    @pl.when(pred)
                    def _(off=off, z=z, t0=t0, full=full, phase=phase):
                        cp = zero_copy(t0 + full * TM_MOE + off, z)
                        cp.start() if phase == "start" else cp.wait()

    e0 = rt_ref[0:1, :].astype(jnp.int32)
    e1 = rt_ref[1:2, :].astype(jnp.int32)
    eid = lax.broadcasted_iota(jnp.int32, (E_PAD, TT), 0)
    sel = jnp.where(e0 == eid, 1.0, jnp.where(e1 == eid, 1.0, 0.0)).astype(BF16)
    r = lax.broadcasted_iota(jnp.int32, (TT, TT), 0)
    c = lax.broadcasted_iota(jnp.int32, (TT, TT), 1)
    earlier = jnp.where(r < c, 1.0, 0.0).astype(BF16)
    rank = jnp.dot(sel, earlier, preferred_element_type=F32)
    ecol = lax.broadcasted_iota(jnp.int32, (E_PAD, 1), 0)
    offp = jnp.zeros((E_PAD, 1), F32)
    for e in range(N_EXPERTS):
        offp = jnp.where(ecol == e, offp_ref[i * N_EXPERTS + e].astype(F32), offp)
    slot_all = rank + offp
    slot0 = jnp.sum(jnp.where(e0 == eid, slot_all, 0.0), axis=0, keepdims=True)
    slot1 = jnp.sum(jnp.where(e1 == eid, slot_all, 0.0), axis=0, keepdims=True)
    sidx = lax.broadcasted_iota(jnp.int32, (SLOTS, TT), 0).astype(F32)
    place = jnp.where(sidx == slot0, 1.0, jnp.where(sidx == slot1, 1.0, 0.0)).astype(BF16)
    xs = jnp.dot(place, hn_ref[...], preferred_element_type=F32)
    lo = lax.shift_right_logical(pltpu.bitcast(xs[:, :PACKED], jnp.uint32), jnp.uint32(16))
    hi = pltpu.bitcast(xs[:, PACKED:], jnp.uint32) & jnp.uint32(0xFFFF0000)
    xs_scr[slot] = lo | hi

    _start_slab_copies(i, base_ref, offp_ref, np_ref, copy(slot))

    @pl.when(i > 0)
    def _():
        _wait_rows(tot_ref[i - 1], copy(1 - slot))

    @pl.when(i == pl.num_programs(0) - 1)
    def _():
        _wait_rows(tot_ref[i], copy(slot))


def _dispatch(tables, hn, route_t):
    grid_spec = pltpu.PrefetchScalarGridSpec(
        num_scalar_prefetch=5,
        grid=(N_TT,),
        in_specs=[pl.BlockSpec((TT, D_MODEL), lambda i, *_: (i, 0)),
                  pl.BlockSpec((SUBLANES, TT), lambda i, *_: (0, i))],
        out_specs=pl.BlockSpec(memory_space=pl.ANY),
        scratch_shapes=[pltpu.VMEM((2, SLOTS, PACKED), jnp.uint32), pltpu.VMEM((TM_MOE, PACKED), jnp.uint32),
                        pltpu.SemaphoreType.DMA((2,)), pltpu.SemaphoreType.DMA],
    )
    return pl.pallas_call(
        _dispatch_kernel,
        grid_spec=grid_spec,
        out_shape=jax.ShapeDtypeStruct((P_PAD, PACKED), jnp.uint32),
        compiler_params=pltpu.CompilerParams(dimension_semantics=("arbitrary",), vmem_limit_bytes=VMEM_LIMIT),
        name="moe_dispatch",
    )(tables["base"], tables["offp"], tables["np"], tables["total"], tables["tail"], hn, route_t)


def _moe_kernel(te_ref, nv_ref, x_ref, wg_ref, wu_ref, wd_ref, y_ref):
    i = pl.program_id(0)

    @pl.when(i < nv_ref[0])
    def _():
        w = x_ref[...]
        lo = pltpu.bitcast(lax.shift_left(w, jnp.uint32(16)), F32).astype(BF16)
        hi = pltpu.bitcast(w & jnp.uint32(0xFFFF0000), F32).astype(BF16)

        def up(w_ref):
            return (jnp.dot(lo, w_ref[0, :PACKED, :], preferred_element_type=F32)
                    + jnp.dot(hi, w_ref[0, PACKED:, :], preferred_element_type=F32))

        a = (jax.nn.silu(up(wg_ref)) * up(wu_ref)).astype(BF16)
        y_ref[...] = jnp.dot(a, wd_ref[0], preferred_element_type=F32)

    @pl.when(i >= nv_ref[0])
    def _():
        y_ref[...] = jnp.zeros_like(y_ref)


def _moe(tile_expert, n_valid, xs, wg, wu, wd):
    grid_spec = pltpu.PrefetchScalarGridSpec(
        num_scalar_prefetch=2,
        grid=(N_MOE_TILES,),
        in_specs=[pl.BlockSpec((TM_MOE, PACKED), lambda i, te, nv: (i, 0)),
                  pl.BlockSpec((1, D_MODEL, D_FF_EXPERT), lambda i, te, nv: (te[i], 0, 0)),
                  pl.BlockSpec((1, D_MODEL, D_FF_EXPERT), lambda i, te, nv: (te[i], 0, 0)),
                  pl.BlockSpec((1, D_FF_EXPERT, D_MODEL), lambda i, te, nv: (te[i], 0, 0))],
        out_specs=pl.BlockSpec((TM_MOE, D_MODEL), lambda i, te, nv: (i, 0)),
    )
    return pl.pallas_call(
        _moe_kernel,
        grid_spec=grid_spec,
        out_shape=jax.ShapeDtypeStruct((P_PAD, D_MODEL), F32),
        compiler_params=pltpu.CompilerParams(dimension_semantics=("arbitrary",), vmem_limit_bytes=VMEM_LIMIT),
        name="moe_experts",
    )(tile_expert, n_valid, xs, wg, wu, wd)


def _combine_kernel(base_ref, offp_ref, np_ref, tot_ref, x_ref, route_ref, y_hbm, fn_ref, o_ref, ybuf, sem):
    i = pl.program_id(0)
    slot = i % 2

    def copy(buf_slot):
        return lambda s, d, z: pltpu.make_async_copy(y_hbm.at[_aligned(d, z), :],
                                                     ybuf.at[buf_slot, _aligned(s, z), :], sem.at[buf_slot])

    @pl.when(i == 0)
    def _():
        _start_slab_copies(0, base_ref, offp_ref, np_ref, copy(0))

    @pl.when(i + 1 < pl.num_programs(0))
    def _():
        _start_slab_copies(i + 1, base_ref, offp_ref, np_ref, copy(1 - slot))

    route = route_ref[...]
    e0 = route[:, 0:1].astype(jnp.int32)
    e1 = route[:, 1:2].astype(jnp.int32)
    w0 = route[:, 2:3]
    w1 = route[:, 3:4]
    eid = lax.broadcasted_iota(jnp.int32, (TT, LANES), 1)
    sel = jnp.where(e0 == eid, 1.0, jnp.where(e1 == eid, 1.0, 0.0)).astype(BF16)
    r = lax.broadcasted_iota(jnp.int32, (TT, TT), 0)
    c = lax.broadcasted_iota(jnp.int32, (TT, TT), 1)
    earlier = jnp.where(c < r, 1.0, 0.0).astype(BF16)
    rank = jnp.dot(earlier, sel, preferred_element_type=F32)
    erow = lax.broadcasted_iota(jnp.int32, (1, LANES), 1)
    offp = jnp.zeros((1, LANES), F32)
    for e in range(N_EXPERTS):
        offp = jnp.where(erow == e, offp_ref[i * N_EXPERTS + e].astype(F32), offp)
    slot_all = rank + offp
    slot0 = jnp.sum(jnp.where(e0 == eid, slot_all, 0.0), axis=1, keepdims=True)
    slot1 = jnp.sum(jnp.where(e1 == eid, slot_all, 0.0), axis=1, keepdims=True)
    sidx = lax.broadcasted_iota(jnp.int32, (TT, SLOTS), 1).astype(F32)
    pick0 = jnp.where(sidx == slot0, 1.0, 0.0).astype(BF16)
    pick1 = jnp.where(sidx == slot1, 1.0, 0.0).astype(BF16)

    total = tot_ref[i]
    _wait_rows(total, copy(slot))
    srow = lax.broadcasted_iota(jnp.int32, (SLOTS, 1), 0)
    y = jnp.where(srow < total, ybuf[slot], 0.0)
    yh = y.astype(BF16)
    yl = (y - yh.astype(F32)).astype(BF16)

    def pick(p):
        return jnp.dot(p, yh, preferred_element_type=F32) + jnp.dot(p, yl, preferred_element_type=F32)

    o_ref[...] = _rms(x_ref[...] + (w0 * pick(pick0) + w1 * pick(pick1)), fn_ref[...])


def _combine(tables, x2, route, y_sorted, fn):
    grid_spec = pltpu.PrefetchScalarGridSpec(
        num_scalar_prefetch=4,
        grid=(N_TT,),
        in_specs=[pl.BlockSpec((TT, D_MODEL), lambda i, *_: (i, 0)),
                  pl.BlockSpec((TT, LANES), lambda i, *_: (i, 0)),
                  pl.BlockSpec(memory_space=pl.ANY),
                  pl.BlockSpec((1, D_MODEL), lambda i, *_: (0, 0))],
        out_specs=pl.BlockSpec((TT, D_MODEL), lambda i, *_: (i, 0)),
        scratch_shapes=[pltpu.VMEM((2, SLOTS, D_MODEL), F32), pltpu.SemaphoreType.DMA((2,))],
    )
    return pl.pallas_call(
        _combine_kernel,
        grid_spec=grid_spec,
        out_shape=jax.ShapeDtypeStruct((TOKENS, D_MODEL), F32),
        compiler_params=pltpu.CompilerParams(dimension_semantics=("arbitrary",), vmem_limit_bytes=VMEM_LIMIT),
        name="moe_combine",
    )(tables["base"], tables["offp"], tables["np"], tables["total"], x2, route, y_sorted, fn)


def _routing_tables(route):
    experts = route[:, :TOP_K].astype(jnp.int32)
    chosen = jnp.sum((experts[:, :, None] == jnp.arange(N_EXPERTS, dtype=jnp.int32)).astype(jnp.int32), axis=1)
    cnt = jnp.sum(chosen.reshape(N_TT, TT, N_EXPERTS), axis=1)
    npad = (cnt + SUBLANES - 1) // SUBLANES * SUBLANES
    offp = jnp.cumsum(npad, axis=1) - npad
    total = jnp.sum(npad, axis=1)
    group = jnp.sum(npad, axis=0)
    group_pad = (group + TM_MOE - 1) // TM_MOE * TM_MOE
    group_end = jnp.cumsum(group_pad)
    group_start = group_end - group_pad
    base = group_start[None, :] + jnp.cumsum(npad, axis=0) - npad
    tail_start = group_start + group
    is_last = jnp.arange(N_EXPERTS) == N_EXPERTS - 1
    tail_len = jnp.where(is_last, P_PAD - tail_start, group_pad - group)
    tile_start = jnp.arange(N_MOE_TILES, dtype=jnp.int32) * TM_MOE
    tile_expert = jnp.minimum(jnp.sum((tile_start[:, None] >= group_end[None, :]).astype(jnp.int32), axis=1),
                              N_EXPERTS - 1)
    n_valid = (group_end[-1] // TM_MOE).reshape(1)
    tables = {"base": base.reshape(-1), "offp": offp.reshape(-1), "np": npad.reshape(-1), "total": total,
              "tail": jnp.concatenate([tail_start, tail_len])}
    return {k: v.astype(jnp.int32) for k, v in tables.items()}, tile_expert.astype(jnp.int32), n_valid.astype(jnp.int32)


def _layer_weights(w_in, w_uq, w_ukv):
    kr = w_in[:, OFF_GATE:OFF_GATE + QK_ROPE]
    z_nope = jnp.zeros((D_MODEL, QK_NOPE), F32)
    z_pad = jnp.zeros((D_MODEL, HEAD_PAD - QK_DIM), F32)
    kr_tile = jnp.concatenate([z_nope, kr[:, :HALF_ROPE], kr[:, HALF_ROPE:], z_pad], axis=1)
    kr_rot = jnp.concatenate([z_nope, kr[:, HALF_ROPE:], kr[:, :HALF_ROPE], z_pad], axis=1)
    win = jnp.concatenate([w_in[:, :OFF_GATE], w_in[:, OFF_GATE + QK_ROPE:], kr_tile, kr_rot], axis=1).astype(BF16)

    wq = w_uq.reshape(Q_LORA, N_HEADS, QK_DIM)
    nope, x1, x2 = wq[:, :, :QK_NOPE], wq[:, :, QK_NOPE:QK_NOPE + HALF_ROPE], wq[:, :, QK_NOPE + HALF_ROPE:]
    zq_pad = jnp.zeros((Q_LORA, N_HEADS, HEAD_PAD - QK_DIM), F32)
    wqa = jnp.concatenate([nope, x1, x2, zq_pad], axis=2).reshape(Q_LORA, N_HEADS * HEAD_PAD).T.astype(BF16)
    wqr = jnp.concatenate([jnp.zeros_like(nope), x2, x1, zq_pad], axis=2)
    wqr = wqr.reshape(Q_LORA, N_HEADS * HEAD_PAD).T.astype(BF16)

    wkv = w_ukv.reshape(KV_LORA, N_HEADS, QK_NOPE + V_HEAD)
    wuk = jnp.concatenate([wkv[:, :, :QK_NOPE], jnp.zeros((KV_LORA, N_HEADS, HEAD_PAD - QK_NOPE), F32)], axis=2)
    wuk = wuk.reshape(KV_LORA, N_HEADS * HEAD_PAD).astype(BF16)
    wuv = wkv[:, :, QK_NOPE:].reshape(KV_LORA, N_HEADS * V_HEAD).T.astype(BF16)
    return win, wqa, wqr, wuk, wuv


def kernel(x, positions, attn_norm, w_in, conv_w, w_conv_out, q_norm, w_uq, kv_norm, w_ukv, w_mla_out, w_o, ffn_norm,
           w_gate, w_up, w_down, router, w_gate_e, w_up_e, w_down_e, final_norm):
    assert x.shape == (BATCH, SEQ, D_MODEL) and positions.shape == (BATCH, SEQ)
    ck, sk, ckt, skt = _rope_tiles(positions)
    xt = x.reshape(TOKENS, D_MODEL)
    out = None
    for l in range(DEPTH):
        win, wqa, wqr, wuk, wuv = _layer_weights(w_in[l], w_uq[l], w_ukv[l])
        q, k, v, gc, sg = _front(xt, attn_norm[l].reshape(1, D_MODEL), win, conv_w[l], w_conv_out[l].astype(BF16),
                                 q_norm[l].reshape(1, Q_LORA), wqa, wqr, kv_norm[l].reshape(1, KV_LORA), wuk, wuv,
                                 ck, sk, ckt, skt)
        attn = _attention(q, k, v)
        wmo = w_mla_out[l].astype(BF16)
        wo = w_o[l].astype(BF16)
        fn = ffn_norm[l].reshape(1, D_MODEL)
        if l % 2 == 0:
            j = l // 2
            x2, hn = _merge(attn, gc, sg, xt, wmo, wo, fn)
            xt = _ffn(hn, x2, w_gate[j].astype(BF16), w_up[j].astype(BF16), w_down[j].astype(BF16))
        else:
            j = l // 2
            router_pad = jnp.pad(router[j], ((0, 0), (0, LANES - N_EXPERTS)))
            x2, hn, route, route_t = _merge(attn, gc, sg, xt, wmo, wo, fn, router_pad)
            tables, tile_expert, n_valid = _routing_tables(route)
            xs = _dispatch(tables, hn, route_t)
            y_sorted = _moe(tile_expert, n_valid, xs, w_gate_e[j].astype(BF16), w_up_e[j].astype(BF16),
                            w_down_e[j].astype(BF16))
            out = _combine(tables, x2, route, y_sorted, final_norm.reshape(1, D_MODEL))
    return out.reshape(BATCH, SEQ, D_MODEL)
```

```python
import math

import jax
import jax.numpy as jnp
from jax import lax
from jax.experimental import pallas as pl
from jax.experimental.pallas import tpu as pltpu

F32 = jnp.float32
BF16 = jnp.bfloat16

D_MODEL = 1024
BATCH = 8
SEQ = 2048
TOKENS = BATCH * SEQ
DEPTH = 2
D_CONV = 512
CONV_WIDTH = 3
N_HEADS = 8
QK_NOPE = 64
QK_ROPE = 32
HALF_ROPE = QK_ROPE // 2
V_HEAD = 64
Q_LORA = 384
KV_LORA = 256
ROPE_THETA = 10000.0
D_FF = 2816
N_EXPERTS = 8
TOP_K = 2
D_FF_EXPERT = 1408
EPS = 1e-6

LANES = 128
HEAD_PAD = LANES
QK_DIM = QK_NOPE + QK_ROPE
Q_SCALE = (1.0 / math.sqrt(QK_DIM)) * math.log2(math.e)
NEG_BIG = -1e30

OFF_BCU = 0
OFF_CQ = 3 * D_CONV
OFF_CKV = OFF_CQ + Q_LORA
OFF_GATE = OFF_CKV + KV_LORA
OFF_KR = OFF_GATE + 2 * D_MODEL
OFF_KRR = OFF_KR + LANES
D_IN_PAD = OFF_KRR + LANES

TM_FRONT = 512
TQ = 256
TK = 512
FF_CHUNK = 256
VMEM_LIMIT = 56 * 1024 * 1024

SUBLANES = 8
TM_MOE = 256
TT = 256
N_TT = TOKENS // TT
E_PAD = 16
N_PAIRS = TOKENS * TOP_K
SLOTS = TT * TOP_K + N_EXPERTS * SUBLANES
_MAX_ROWS = N_PAIRS + N_TT * N_EXPERTS * (SUBLANES - 1) + N_EXPERTS * (TM_MOE - SUBLANES)
N_MOE_TILES = -(-_MAX_ROWS // TM_MOE)
P_PAD = N_MOE_TILES * TM_MOE
SLAB_PIECES = tuple(TT >> s for s in range(6))
WAIT_PIECES = (512,) + SLAB_PIECES
assert SLAB_PIECES[-1] == SUBLANES and SLOTS < 2 * WAIT_PIECES[0]


def _rms(x, g):
    return x * lax.rsqrt(jnp.mean(x * x, axis=-1, keepdims=True) + EPS) * g


def _const_spec(shape):
    nd = len(shape)
    return pl.BlockSpec(shape, lambda *_: (0,) * nd)


def _trig_kernel(ang_ref, cos_ref, sin_ref):
    a = ang_ref[...]
    cos_ref[...] = jnp.cos(a)
    sin_ref[...] = jnp.sin(a)


def _rope_tiles(positions):
    inv_freq = ROPE_THETA ** (-jnp.arange(0, QK_ROPE, 2, dtype=F32) / QK_ROPE)
    ang = positions.astype(F32).reshape(TOKENS, 1) * inv_freq
    ang = ang.reshape(TOKENS * HALF_ROPE // LANES, LANES)
    cos, sin = pl.pallas_call(
        _trig_kernel,
        out_shape=(jax.ShapeDtypeStruct(ang.shape, F32),) * 2,
        name="rope_trig",
    )(ang)
    cos = cos.reshape(TOKENS, HALF_ROPE)
    sin = sin.reshape(TOKENS, HALF_ROPE)
    z_nope = jnp.zeros((TOKENS, QK_NOPE), F32)
    z_pad = jnp.zeros((TOKENS, HEAD_PAD - QK_DIM), F32)
    ck = jnp.concatenate([z_nope, cos, cos, z_pad], axis=1)
    sk = jnp.concatenate([z_nope, -sin, sin, z_pad], axis=1)
    return ck, sk, ck.T, sk.T


def _dot_nt(a, b):
    return lax.dot_general(a, b, (((1,), (1,)), ((), ())), preferred_element_type=F32)


def _front_kernel(x_ref, an_ref, win_ref, cw_ref, wco_ref, qn_ref, wqa_ref, wqr_ref, kvn_ref, wuk_ref, wuv_ref,
                  ck_ref, sk_ref, ckt_ref, skt_ref, q_out, k_out, v_out, gc_out, sg_out, cu_scr):
    tm = x_ref.shape[0]
    i = pl.program_id(0)
    xn = _rms(x_ref[...], an_ref[...]).astype(BF16)

    def proj(a, b):
        return jnp.dot(xn, win_ref[:, a:b], preferred_element_type=F32)

    bcu = proj(OFF_BCU, OFF_CQ)
    b_g = bcu[:, :D_CONV]
    cu = bcu[:, D_CONV:2 * D_CONV] * bcu[:, 2 * D_CONV:]

    @pl.when(i % (SEQ // tm) == 0)
    def _():
        cu_scr[0:8, :] = jnp.zeros((8, D_CONV), F32)

    cu_scr[8:8 + tm, :] = cu
    prev1 = cu_scr[7:7 + tm, :]
    prev2 = cu_scr[6:6 + tm, :]
    cw = cw_ref[...]
    conv = prev2 * cw[0:1, :] + prev1 * cw[1:2, :] + cu * cw[2:3, :]
    cu_scr[0:8, :] = cu[tm - 8:, :]
    y_conv = jnp.dot((b_g * conv).astype(BF16), wco_ref[...], preferred_element_type=F32)

    gates = proj(OFF_GATE, OFF_KR)
    gc_out[...] = (jax.nn.sigmoid(gates[:, :D_MODEL]) * y_conv).astype(BF16)
    sg_out[...] = jax.nn.sigmoid(gates[:, D_MODEL:]).astype(BF16)

    feat = lax.broadcasted_iota(jnp.int32, (HEAD_PAD, 1), 0)
    nope_mask = (feat < QK_NOPE).astype(F32)
    cq_t = (ckt_ref[...] + nope_mask) * Q_SCALE
    sq_t = skt_ref[...] * Q_SCALE
    cqn = _rms(proj(OFF_CQ, OFF_CKV), qn_ref[...]).astype(BF16)
    qa = _dot_nt(wqa_ref[...], cqn)
    qr = _dot_nt(wqr_ref[...], cqn)
    for h in range(N_HEADS):
        blk = slice(h * HEAD_PAD, (h + 1) * HEAD_PAD)
        q_out[blk, :] = (qa[blk, :] * cq_t + qr[blk, :] * sq_t).astype(BF16)

    ck = ck_ref[...]
    sk = sk_ref[...]
    ckvn = _rms(proj(OFF_CKV, OFF_GATE), kvn_ref[...]).astype(BF16)
    kn = jnp.dot(ckvn, wuk_ref[...], preferred_element_type=F32)
    kpe = proj(OFF_KR, OFF_KRR) * ck + proj(OFF_KRR, D_IN_PAD) * sk
    for h in range(N_HEADS):
        blk = slice(h * HEAD_PAD, (h + 1) * HEAD_PAD)
        k_out[:, blk] = (kn[:, blk] + kpe).astype(BF16)
    vt = _dot_nt(wuv_ref[...], ckvn).astype(BF16)
    for c in range(tm // TK):
        v_out[c] = vt[:, c * TK:(c + 1) * TK]


def _front(x, an, win, cw, wco, qn, wqa, wqr, kvn, wuk, wuv, ck, sk, ckt, skt):
    tm = TM_FRONT
    row = lambda n: pl.BlockSpec((tm, n), lambda i: (i, 0))
    col = lambda n: pl.BlockSpec((n, tm), lambda i: (0, i))
    return pl.pallas_call(
        _front_kernel,
        grid=(TOKENS // tm,),
        in_specs=[row(D_MODEL), _const_spec((1, D_MODEL)), _const_spec((D_MODEL, D_IN_PAD)),
                  _const_spec((CONV_WIDTH, D_CONV)), _const_spec((D_CONV, D_MODEL)),
                  _const_spec((1, Q_LORA)), _const_spec((N_HEADS * HEAD_PAD, Q_LORA)),
                  _const_spec((N_HEADS * HEAD_PAD, Q_LORA)),
                  _const_spec((1, KV_LORA)), _const_spec((KV_LORA, N_HEADS * HEAD_PAD)),
                  _const_spec((N_HEADS * V_HEAD, KV_LORA)),
                  row(HEAD_PAD), row(HEAD_PAD), col(HEAD_PAD), col(HEAD_PAD)],
        out_specs=[col(N_HEADS * HEAD_PAD), row(N_HEADS * HEAD_PAD),
                   pl.BlockSpec((tm // TK, N_HEADS * V_HEAD, TK), lambda i: (i, 0, 0)),
                   row(D_MODEL), row(D_MODEL)],
        out_shape=[jax.ShapeDtypeStruct((N_HEADS * HEAD_PAD, TOKENS), BF16),
                   jax.ShapeDtypeStruct((TOKENS, N_HEADS * HEAD_PAD), BF16),
                   jax.ShapeDtypeStruct((TOKENS // TK, N_HEADS * V_HEAD, TK), BF16),
                   jax.ShapeDtypeStruct((TOKENS, D_MODEL), BF16),
                   jax.ShapeDtypeStruct((TOKENS, D_MODEL), BF16)],
        scratch_shapes=[pltpu.VMEM((tm + 8, D_CONV), F32)],
        compiler_params=pltpu.CompilerParams(dimension_semantics=("arbitrary",), vmem_limit_bytes=VMEM_LIMIT),
        name="front",
    )(x, an, win, cw, wco, qn, wqa, wqr, kvn, wuk, wuv, ck, sk, ckt, skt)


def _attn_items():
    items = []
    for i in range(SEQ // TQ):
        ks = 0
        while ks + TK <= i * TQ:
            items.append((i, ks, TK, False))
            ks += TK
        items.append((i, ks, (i + 1) * TQ - ks, True))
    return items


def _attn_kernel(qt_ref, k_ref, vt_ref, o_ref):
    row = lax.broadcasted_iota(jnp.int32, (TQ, TQ), 0)
    col = lax.broadcasted_iota(jnp.int32, (TQ, TQ), 1)
    causal = row <= col

    def scores(item):
        i, ks, n, diag = item
        sts = []
        for h in range(2):
            hs = slice(h * HEAD_PAD, (h + 1) * HEAD_PAD)
            st = jnp.dot(k_ref[ks:ks + n, hs], qt_ref[hs, i * TQ:(i + 1) * TQ], preferred_element_type=F32)
            if diag:
                tail = jnp.where(causal, st[n - TQ:, :], NEG_BIG)
                st = tail if n == TQ else jnp.concatenate([st[:n - TQ, :], tail], axis=0)
            sts.append(st)
        return sts

    def consume(item, sts, carry):
        i, ks, n, _ = item
        stats = []
        for h in range(2):
            m, l, _ = carry[h]
            m_new = jnp.maximum(m, jnp.max(sts[h], axis=0, keepdims=True))
            alpha = jnp.exp2(m - m_new)
            p = jnp.exp2(sts[h] - m_new)
            l = alpha * l + jnp.sum(p, axis=0, keepdims=True)
            stats.append((m_new, l, alpha, p.astype(BF16)))
        new = []
        for h in range(2):
            m_new, l, alpha, p = stats[h]
            vt = vt_ref[ks // TK, h * V_HEAD:(h + 1) * V_HEAD, ks % TK:ks % TK + n]
            acc = alpha * carry[h][2] + jnp.dot(vt, p, preferred_element_type=F32)
            new.append((m_new, l, acc))
        return new

    items = _attn_items()
    init = [(jnp.full((1, TQ), NEG_BIG, F32), jnp.zeros((1, TQ), F32), jnp.zeros((V_HEAD, TQ), F32))
            for _ in range(2)]
    cur = scores(items[0])
    carry = init
    for t, item in enumerate(items):
        nxt = scores(items[t + 1]) if t + 1 < len(items) else None
        carry = consume(item, cur, carry)
        if item[3]:
            i = item[0]
            out_t = jnp.concatenate([acc / l for (_, l, acc) in carry], axis=0)
            o_ref[i * TQ:(i + 1) * TQ, :] = out_t.T.astype(BF16)
            carry = init
        cur = nxt


def _attention(qt, k, vt):
    return pl.pallas_call(
        _attn_kernel,
        grid=(BATCH, N_HEADS // 2),
        in_specs=[pl.BlockSpec((2 * HEAD_PAD, SEQ), lambda b, hp: (hp, b)),
                  pl.BlockSpec((SEQ, 2 * HEAD_PAD), lambda b, hp: (b, hp)),
                  pl.BlockSpec((SEQ // TK, 2 * V_HEAD, TK), lambda b, hp: (b, hp, 0))],
        out_specs=pl.BlockSpec((SEQ, 2 * V_HEAD), lambda b, hp: (b, hp)),
        out_shape=jax.ShapeDtypeStruct((TOKENS, N_HEADS * V_HEAD), BF16),
        compiler_params=pltpu.CompilerParams(dimension_semantics=("arbitrary",) * 2, vmem_limit_bytes=VMEM_LIMIT),
        name="attention",
    )(qt, k, vt)


def _merge_core(attn_ref, gc_ref, sg_ref, x_ref, wmo_ref, wo_ref, fn_ref):
    y_mla = jnp.dot(attn_ref[...], wmo_ref[...], preferred_element_type=F32)
    merged = gc_ref[...].astype(F32) + sg_ref[...].astype(F32) * y_mla
    x2 = x_ref[...] + jnp.dot(merged.astype(BF16), wo_ref[...], preferred_element_type=F32)
    return x2, _rms(x2, fn_ref[...])


def _merge_dense_kernel(attn_ref, gc_ref, sg_ref, x_ref, wmo_ref, wo_ref, fn_ref, x2_out, hn_out):
    x2, hn = _merge_core(attn_ref, gc_ref, sg_ref, x_ref, wmo_ref, wo_ref, fn_ref)
    x2_out[...] = x2
    hn_out[...] = hn.astype(BF16)


def _merge_route_kernel(attn_ref, gc_ref, sg_ref, x_ref, wmo_ref, wo_ref, fn_ref, router_ref,
                        x2_out, hn_out, route_out, route_t_out):
    x2, hn = _merge_core(attn_ref, gc_ref, sg_ref, x_ref, wmo_ref, wo_ref, fn_ref)
    x2_out[...] = x2
    hb = hn.astype(BF16)
    hn_out[...] = hb
    hl = (hn - hb.astype(F32)).astype(BF16)
    r = router_ref[...]
    rh = r.astype(BF16)
    rl = (r - rh.astype(F32)).astype(BF16)
    logits = (jnp.dot(hb, rh, preferred_element_type=F32) + jnp.dot(hl, rh, preferred_element_type=F32)
              + jnp.dot(hb, rl, preferred_element_type=F32))
    lane = lax.broadcasted_iota(jnp.int32, logits.shape, 1)
    logits = jnp.where(lane < N_EXPERTS, logits, -jnp.inf)
    m1 = jnp.max(logits, axis=-1, keepdims=True)
    i1 = jnp.min(jnp.where(logits == m1, lane, LANES), axis=-1, keepdims=True)
    rest = jnp.where(lane == i1, -jnp.inf, logits)
    m2 = jnp.max(rest, axis=-1, keepdims=True)
    i2 = jnp.min(jnp.where(rest == m2, lane, LANES), axis=-1, keepdims=True)
    t = jnp.exp(m2 - m1)
    w1 = 1.0 / (1.0 + t)
    w2 = t * w1
    route = jnp.where(lane == 0, i1.astype(F32),
                      jnp.where(lane == 1, i2.astype(F32), jnp.where(lane == 2, w1, jnp.where(lane == 3, w2, 0.0))))
    route_out[...] = route
    route_t_out[...] = route.T[0:SUBLANES, :]


def _merge(attn, gc, sg, x, wmo, wo, fn, router=None):
    tm = TM_FRONT
    row = lambda n: pl.BlockSpec((tm, n), lambda i: (i, 0))
    in_specs = [row(N_HEADS * V_HEAD), row(D_MODEL), row(D_MODEL), row(D_MODEL),
                _const_spec((N_HEADS * V_HEAD, D_MODEL)), _const_spec((D_MODEL, D_MODEL)), _const_spec((1, D_MODEL))]
    args = [attn, gc, sg, x, wmo, wo, fn]
    if router is None:
        kern = _merge_dense_kernel
        out_specs = [row(D_MODEL), row(D_MODEL)]
        out_shape = [jax.ShapeDtypeStruct((TOKENS, D_MODEL), F32), jax.ShapeDtypeStruct((TOKENS, D_MODEL), BF16)]
        name = "merge_dense"
    else:
        kern = _merge_route_kernel
        in_specs.append(_const_spec((D_MODEL, LANES)))
        args.append(router)
        out_specs = [row(D_MODEL), row(D_MODEL), row(LANES), pl.BlockSpec((SUBLANES, tm), lambda i: (0, i))]
        out_shape = [jax.ShapeDtypeStruct((TOKENS, D_MODEL), F32), jax.ShapeDtypeStruct((TOKENS, D_MODEL), BF16),
                     jax.ShapeDtypeStruct((TOKENS, LANES), F32), jax.ShapeDtypeStruct((SUBLANES, TOKENS), F32)]
        name = "merge_route"
    return pl.pallas_call(
        kern, grid=(TOKENS // tm,), in_specs=in_specs, out_specs=out_specs, out_shape=out_shape,
        compiler_params=pltpu.CompilerParams(dimension_semantics=("arbitrary",), vmem_limit_bytes=VMEM_LIMIT),
        name=name,
    )(*args)


def _ffn_kernel(hn_ref, x_ref, wg_ref, wu_ref, wd_ref, o_ref):
    hn = hn_ref[...]
    acc = x_ref[...]
    for c in range(D_FF // FF_CHUNK):
        cs = slice(c * FF_CHUNK, (c + 1) * FF_CHUNK)
        g = jnp.dot(hn, wg_ref[:, cs], preferred_element_type=F32)
        u = jnp.dot(hn, wu_ref[:, cs], preferred_element_type=F32)
        a = (jax.nn.silu(g) * u).astype(BF16)
        acc = acc + jnp.dot(a, wd_ref[cs, :], preferred_element_type=F32)
    o_ref[...] = acc


def _ffn(hn, x2, wg, wu, wd):
    tm = TM_FRONT
    row = lambda n: pl.BlockSpec((tm, n), lambda i: (i, 0))
    return pl.pallas_call(
        _ffn_kernel,
        grid=(TOKENS // tm,),
        in_specs=[row(D_MODEL), row(D_MODEL), _const_spec((D_MODEL, D_FF)), _const_spec((D_MODEL, D_FF)),
                  _const_spec((D_FF, D_MODEL))],
        out_specs=row(D_MODEL),
        out_shape=jax.ShapeDtypeStruct((TOKENS, D_MODEL), F32),
        compiler_params=pltpu.CompilerParams(dimension_semantics=("arbitrary",), vmem_limit_bytes=VMEM_LIMIT),
        name="ffn_dense",
    )(hn, x2, wg, wu, wd)


def _pieces(n, sizes):
    return [((n & z) != 0, n & ~(2 * z - 1), z) for z in sizes]


def _aligned(rows, z):
    return pl.ds(rows if isinstance(rows, int) else pl.multiple_of(rows, SUBLANES), z)


def _start_slab_copies(tile, base_ref, offp_ref, np_ref, make_copy):
    for e in range(N_EXPERTS):
        n = np_ref[tile * N_EXPERTS + e]
        src = offp_ref[tile * N_EXPERTS + e]
        dst = base_ref[tile * N_EXPERTS + e]
        for pred, off, z in _pieces(n, SLAB_PIECES):
            @pl.when(pred)
            def _(off=off, z=z):
                make_copy(src + off, dst + off, z).start()


def _wait_rows(total, make_copy):
    for pred, _, z in _pieces(total, WAIT_PIECES):
        @pl.when(pred)
        def _(z=z):
            make_copy(0, 0, z).wait()


def _dispatch_kernel(base_ref, offp_ref, np_ref, tot_ref, tail_ref, hn_ref, rt_ref, xs_hbm, xs_scr, zbuf, sem, zsem):
    i = pl.program_id(0)
    slot = i % 2

    def copy(buf_slot):
        return lambda s, d, z: pltpu.make_async_copy(xs_scr.at[buf_slot, _aligned(s, z), :],
                                                     xs_hbm.at[_aligned(d, z), :], sem.at[buf_slot])

    def zero_copy(d, z):
        return pltpu.make_async_copy(zbuf.at[pl.ds(0, z), :], xs_hbm.at[_aligned(d, z), :], zsem)

    @pl.when(i == 0)
    def _():
        zbuf[...] = jnp.zeros_like(zbuf)
        for phase in ("start", "wait"):
            for e in range(N_EXPERTS):
                t0 = tail_ref[e]
                n = tail_ref[N_EXPERTS + e]
                full = lax.shift_right_logical(n, jnp.int32(TM_MOE.bit_length() - 1))

                def chunk(c, carry, t0=t0, phase=phase):
                    cp = zero_copy(t0 + c * TM_MOE, TM_MOE)
                    cp.start() if phase == "start" else cp.wait()
                    return carry

                lax.fori_loop(0, full, chunk, 0)
                rest = n & (TM_MOE - 1)
                for pred, off, z in _pieces(rest, SLAB_PIECES[1:]):
                    @pl.when(pred)
                    def _(off=off, z=z, t0=t0, full=full, phase=phase):
                        cp = zero_copy(t0 + full * TM_MOE + off, z)
                        cp.start() if phase == "start" else cp.wait()

    e0 = rt_ref[0:1, :].astype(jnp.int32)
    e1 = rt_ref[1:2, :].astype(jnp.int32)
    eid = lax.broadcasted_iota(jnp.int32, (E_PAD, TT), 0)
    sel = jnp.where(e0 == eid, 1.0, jnp.where(e1 == eid, 1.0, 0.0)).astype(BF16)
    r = lax.broadcasted_iota(jnp.int32, (TT, TT), 0)
    c = lax.broadcasted_iota(jnp.int32, (TT, TT), 1)
    earlier = jnp.where(r < c, 1.0, 0.0).astype(BF16)
    rank = jnp.dot(sel, earlier, preferred_element_type=F32)
    ecol = lax.broadcasted_iota(jnp.int32, (E_PAD, 1), 0)
    offp = jnp.zeros((E_PAD, 1), F32)
    for e in range(N_EXPERTS):
        offp = jnp.where(ecol == e, offp_ref[i * N_EXPERTS + e].astype(F32), offp)
    slot_all = rank + offp
    slot0 = jnp.sum(jnp.where(e0 == eid, slot_all, 0.0), axis=0, keepdims=True)
    slot1 = jnp.sum(jnp.where(e1 == eid, slot_all, 0.0), axis=0, keepdims=True)
    sidx = lax.broadcasted_iota(jnp.int32, (SLOTS, TT), 0).astype(F32)
    place = jnp.where(sidx == slot0, 1.0, jnp.where(sidx == slot1, 1.0, 0.0)).astype(BF16)
    xs_scr[slot] = jnp.dot(place, hn_ref[...], preferred_element_type=F32)

    _start_slab_copies(i, base_ref, offp_ref, np_ref, copy(slot))

    @pl.when(i > 0)
    def _():
        _wait_rows(tot_ref[i - 1], copy(1 - slot))

    @pl.when(i == pl.num_programs(0) - 1)
    def _():
        _wait_rows(tot_ref[i], copy(slot))


def _dispatch(tables, hn, route_t):
    grid_spec = pltpu.PrefetchScalarGridSpec(
        num_scalar_prefetch=5,
        grid=(N_TT,),
        in_specs=[pl.BlockSpec((TT, D_MODEL), lambda i, *_: (i, 0)),
                  pl.BlockSpec((SUBLANES, TT), lambda i, *_: (0, i))],
        out_specs=pl.BlockSpec(memory_space=pl.ANY),
        scratch_shapes=[pltpu.VMEM((2, SLOTS, D_MODEL), F32), pltpu.VMEM((TM_MOE, D_MODEL), F32),
                        pltpu.SemaphoreType.DMA((2,)), pltpu.SemaphoreType.DMA],
    )
    return pl.pallas_call(
        _dispatch_kernel,
        grid_spec=grid_spec,
        out_shape=jax.ShapeDtypeStruct((P_PAD, D_MODEL), F32),
        compiler_params=pltpu.CompilerParams(dimension_semantics=("arbitrary",), vmem_limit_bytes=VMEM_LIMIT),
        name="moe_dispatch",
    )(tables["base"], tables["offp"], tables["np"], tables["total"], tables["tail"], hn, route_t)


def _moe_kernel(te_ref, nv_ref, x_ref, wgu_ref, wd_ref, y_ref):
    i = pl.program_id(0)

    @pl.when(i < nv_ref[0])
    def _():
        x = x_ref[...].astype(BF16)
        h = jnp.dot(x, wgu_ref[0], preferred_element_type=F32)
        a = (jax.nn.silu(h[:, :D_FF_EXPERT]) * h[:, D_FF_EXPERT:]).astype(BF16)
        y_ref[...] = jnp.dot(a, wd_ref[0], preferred_element_type=F32)

    @pl.when(i >= nv_ref[0])
    def _():
        y_ref[...] = jnp.zeros_like(y_ref)


def _moe(tile_expert, n_valid, xs, wgu, wd):
    grid_spec = pltpu.PrefetchScalarGridSpec(
        num_scalar_prefetch=2,
        grid=(N_MOE_TILES,),
        in_specs=[pl.BlockSpec((TM_MOE, D_MODEL), lambda i, te, nv: (i, 0)),
                  pl.BlockSpec((1, D_MODEL, 2 * D_FF_EXPERT), lambda i, te, nv: (te[i], 0, 0)),
                  pl.BlockSpec((1, D_FF_EXPERT, D_MODEL), lambda i, te, nv: (te[i], 0, 0))],
        out_specs=pl.BlockSpec((TM_MOE, D_MODEL), lambda i, te, nv: (i, 0)),
    )
    return pl.pallas_call(
        _moe_kernel,
        grid_spec=grid_spec,
        out_shape=jax.ShapeDtypeStruct((P_PAD, D_MODEL), F32),
        compiler_params=pltpu.CompilerParams(dimension_semantics=("arbitrary",), vmem_limit_bytes=VMEM_LIMIT),
        name="moe_experts",
    )(tile_expert, n_valid, xs, wgu, wd)


def _combine_kernel(base_ref, offp_ref, np_ref, tot_ref, x_ref, route_ref, y_hbm, fn_ref, o_ref, ybuf, sem):
    i = pl.program_id(0)
    slot = i % 2

    def copy(buf_slot):
        return lambda s, d, z: pltpu.make_async_copy(y_hbm.at[_aligned(d, z), :],
                                                     ybuf.at[buf_slot, _aligned(s, z), :], sem.at[buf_slot])

    @pl.when(i == 0)
    def _():
        _start_slab_copies(0, base_ref, offp_ref, np_ref, copy(0))

    @pl.when(i + 1 < pl.num_programs(0))
    def _():
        _start_slab_copies(i + 1, base_ref, offp_ref, np_ref, copy(1 - slot))

    route = route_ref[...]
    e0 = route[:, 0:1].astype(jnp.int32)
    e1 = route[:, 1:2].astype(jnp.int32)
    w0 = route[:, 2:3]
    w1 = route[:, 3:4]
    eid = lax.broadcasted_iota(jnp.int32, (TT, LANES), 1)
    sel = jnp.where(e0 == eid, 1.0, jnp.where(e1 == eid, 1.0, 0.0)).astype(BF16)
    r = lax.broadcasted_iota(jnp.int32, (TT, TT), 0)
    c = lax.broadcasted_iota(jnp.int32, (TT, TT), 1)
    earlier = jnp.where(c < r, 1.0, 0.0).astype(BF16)
    rank = jnp.dot(earlier, sel, preferred_element_type=F32)
    erow = lax.broadcasted_iota(jnp.int32, (1, LANES), 1)
    offp = jnp.zeros((1, LANES), F32)
    for e in range(N_EXPERTS):
        offp = jnp.where(erow == e, offp_ref[i * N_EXPERTS + e].astype(F32), offp)
    slot_all = rank + offp
    slot0 = jnp.sum(jnp.where(e0 == eid, slot_all, 0.0), axis=1, keepdims=True)
    slot1 = jnp.sum(jnp.where(e1 == eid, slot_all, 0.0), axis=1, keepdims=True)
    sidx = lax.broadcasted_iota(jnp.int32, (TT, SLOTS), 1).astype(F32)
    pick0 = jnp.where(sidx == slot0, 1.0, 0.0).astype(BF16)
    pick1 = jnp.where(sidx == slot1, 1.0, 0.0).astype(BF16)

    total = tot_ref[i]
    _wait_rows(total, copy(slot))
    srow = lax.broadcasted_iota(jnp.int32, (SLOTS, 1), 0)
    y = jnp.where(srow < total, ybuf[slot], 0.0)
    yh = y.astype(BF16)
    yl = (y - yh.astype(F32)).astype(BF16)

    def pick(p):
        return jnp.dot(p, yh, preferred_element_type=F32) + jnp.dot(p, yl, preferred_element_type=F32)

    o_ref[...] = _rms(x_ref[...] + (w0 * pick(pick0) + w1 * pick(pick1)), fn_ref[...])


def _combine(tables, x2, route, y_sorted, fn):
    grid_spec = pltpu.PrefetchScalarGridSpec(
        num_scalar_prefetch=4,
        grid=(N_TT,),
        in_specs=[pl.BlockSpec((TT, D_MODEL), lambda i, *_: (i, 0)),
                  pl.BlockSpec((TT, LANES), lambda i, *_: (i, 0)),
                  pl.BlockSpec(memory_space=pl.ANY),
                  pl.BlockSpec((1, D_MODEL), lambda i, *_: (0, 0))],
        out_specs=pl.BlockSpec((TT, D_MODEL), lambda i, *_: (i, 0)),
        scratch_shapes=[pltpu.VMEM((2, SLOTS, D_MODEL), F32), pltpu.SemaphoreType.DMA((2,))],
    )
    return pl.pallas_call(
        _combine_kernel,
        grid_spec=grid_spec,
        out_shape=jax.ShapeDtypeStruct((TOKENS, D_MODEL), F32),
        compiler_params=pltpu.CompilerParams(dimension_semantics=("arbitrary",), vmem_limit_bytes=VMEM_LIMIT),
        name="moe_combine",
    )(tables["base"], tables["offp"], tables["np"], tables["total"], x2, route, y_sorted, fn)


def _routing_tables(route):
    experts = route[:, :TOP_K].astype(jnp.int32)
    chosen = jnp.sum((experts[:, :, None] == jnp.arange(N_EXPERTS, dtype=jnp.int32)).astype(jnp.int32), axis=1)
    cnt = jnp.sum(chosen.reshape(N_TT, TT, N_EXPERTS), axis=1)
    npad = (cnt + SUBLANES - 1) // SUBLANES * SUBLANES
    offp = jnp.cumsum(npad, axis=1) - npad
    total = jnp.sum(npad, axis=1)
    group = jnp.sum(npad, axis=0)
    group_pad = (group + TM_MOE - 1) // TM_MOE * TM_MOE
    group_end = jnp.cumsum(group_pad)
    group_start = group_end - group_pad
    base = group_start[None, :] + jnp.cumsum(npad, axis=0) - npad
    tail_start = group_start + group
    is_last = jnp.arange(N_EXPERTS) == N_EXPERTS - 1
    tail_len = jnp.where(is_last, P_PAD - tail_start, group_pad - group)
    tile_start = jnp.arange(N_MOE_TILES, dtype=jnp.int32) * TM_MOE
    tile_expert = jnp.minimum(jnp.sum((tile_start[:, None] >= group_end[None, :]).astype(jnp.int32), axis=1),
                              N_EXPERTS - 1)
    n_valid = (group_end[-1] // TM_MOE).reshape(1)
    tables = {"base": base.reshape(-1), "offp": offp.reshape(-1), "np": npad.reshape(-1), "total": total,
              "tail": jnp.concatenate([tail_start, tail_len])}
    return {k: v.astype(jnp.int32) for k, v in tables.items()}, tile_expert.astype(jnp.int32), n_valid.astype(jnp.int32)


def _layer_weights(w_in, w_uq, w_ukv):
    kr = w_in[:, OFF_GATE:OFF_GATE + QK_ROPE]
    z_nope = jnp.zeros((D_MODEL, QK_NOPE), F32)
    z_pad = jnp.zeros((D_MODEL, HEAD_PAD - QK_DIM), F32)
    kr_tile = jnp.concatenate([z_nope, kr[:, :HALF_ROPE], kr[:, HALF_ROPE:], z_pad], axis=1)
    kr_rot = jnp.concatenate([z_nope, kr[:, HALF_ROPE:], kr[:, :HALF_ROPE], z_pad], axis=1)
    win = jnp.concatenate([w_in[:, :OFF_GATE], w_in[:, OFF_GATE + QK_ROPE:], kr_tile, kr_rot], axis=1).astype(BF16)

    wq = w_uq.reshape(Q_LORA, N_HEADS, QK_DIM)
    nope, x1, x2 = wq[:, :, :QK_NOPE], wq[:, :, QK_NOPE:QK_NOPE + HALF_ROPE], wq[:, :, QK_NOPE + HALF_ROPE:]
    zq_pad = jnp.zeros((Q_LORA, N_HEADS, HEAD_PAD - QK_DIM), F32)
    wqa = jnp.concatenate([nope, x1, x2, zq_pad], axis=2).reshape(Q_LORA, N_HEADS * HEAD_PAD).T.astype(BF16)
    wqr = jnp.concatenate([jnp.zeros_like(nope), x2, x1, zq_pad], axis=2)
    wqr = wqr.reshape(Q_LORA, N_HEADS * HEAD_PAD).T.astype(BF16)

    wkv = w_ukv.reshape(KV_LORA, N_HEADS, QK_NOPE + V_HEAD)
    wuk = jnp.concatenate([wkv[:, :, :QK_NOPE], jnp.zeros((KV_LORA, N_HEADS, HEAD_PAD - QK_NOPE), F32)], axis=2)
    wuk = wuk.reshape(KV_LORA, N_HEADS * HEAD_PAD).astype(BF16)
    wuv = wkv[:, :, QK_NOPE:].reshape(KV_LORA, N_HEADS * V_HEAD).T.astype(BF16)
    return win, wqa, wqr, wuk, wuv


def kernel(x, positions, attn_norm, w_in, conv_w, w_conv_out, q_norm, w_uq, kv_norm, w_ukv, w_mla_out, w_o, ffn_norm,
           w_gate, w_up, w_down, router, w_gate_e, w_up_e, w_down_e, final_norm):
    assert x.shape == (BATCH, SEQ, D_MODEL) and positions.shape == (BATCH, SEQ)
    ck, sk, ckt, skt = _rope_tiles(positions)
    xt = x.reshape(TOKENS, D_MODEL)
    out = None
    for l in range(DEPTH):
        win, wqa, wqr, wuk, wuv = _layer_weights(w_in[l], w_uq[l], w_ukv[l])
        q, k, v, gc, sg = _front(xt, attn_norm[l].reshape(1, D_MODEL), win, conv_w[l], w_conv_out[l].astype(BF16),
                                 q_norm[l].reshape(1, Q_LORA), wqa, wqr, kv_norm[l].reshape(1, KV_LORA), wuk, wuv,
                                 ck, sk, ckt, skt)
        attn = _attention(q, k, v)
        wmo = w_mla_out[l].astype(BF16)
        wo = w_o[l].astype(BF16)
        fn = ffn_norm[l].reshape(1, D_MODEL)
        if l % 2 == 0:
            j = l // 2
            x2, hn = _merge(attn, gc, sg, xt, wmo, wo, fn)
            xt = _ffn(hn, x2, w_gate[j].astype(BF16), w_up[j].astype(BF16), w_down[j].astype(BF16))
        else:
            j = l // 2
            router_pad = jnp.pad(router[j], ((0, 0), (0, LANES - N_EXPERTS)))
            x2, hn, route, route_t = _merge(attn, gc, sg, xt, wmo, wo, fn, router_pad)
            tables, tile_expert, n_valid = _routing_tables(route)
            xs = _dispatch(tables, hn, route_t)
            wgu = jnp.concatenate([w_gate_e[j], w_up_e[j]], axis=2).astype(BF16)
            y_sorted = _moe(tile_expert, n_valid, xs, wgu, w_down_e[j].astype(BF16))
            out = _combine(tables, x2, route, y_sorted, final_norm.reshape(1, D_MODEL))
    return out.reshape(BATCH, SEQ, D_MODEL)
```

```python
import math

import jax
import jax.numpy as jnp
from jax import lax
from jax.experimental import pallas as pl
from jax.experimental.pallas import tpu as pltpu

F32 = jnp.float32
BF16 = jnp.bfloat16

D_MODEL = 1024
BATCH = 8
SEQ = 2048
TOKENS = BATCH * SEQ
DEPTH = 2
D_CONV = 512
CONV_WIDTH = 3
N_HEADS = 8
QK_NOPE = 64
QK_ROPE = 32
HALF_ROPE = QK_ROPE // 2
V_HEAD = 64
Q_LORA = 384
KV_LORA = 256
ROPE_THETA = 10000.0
D_FF = 2816
N_EXPERTS = 8
TOP_K = 2
D_FF_EXPERT = 1408
EPS = 1e-6

LANES = 128
HEAD_PAD = LANES
QK_DIM = QK_NOPE + QK_ROPE
Q_SCALE = (1.0 / math.sqrt(QK_DIM)) * math.log2(math.e)
NEG_BIG = -1e30

OFF_BCU = 0
OFF_GATE = 3 * D_CONV
OFF_SMALL = OFF_GATE + 2 * D_MODEL
D_IN_PAD = OFF_SMALL + Q_LORA + KV_LORA + LANES
REF_CQ = 3 * D_CONV
REF_KR = REF_CQ + Q_LORA + KV_LORA
REF_GATE = REF_KR + QK_ROPE

TM_FRONT = 512
TQ = 256
TK = 512
FF_CHUNK = 256
VMEM_LIMIT = 56 * 1024 * 1024

SUBLANES = 8
TM_MOE = 256
TT = 256
N_TT = TOKENS // TT
E_PAD = 16
XS_WIDTH = D_MODEL + LANES
N_PAIRS = TOKENS * TOP_K
SLOTS =TT * TOP_K + N_EXPERTS * SUBLANES
_MAX_ROWS = N_PAIRS + N_TT * N_EXPERTS * (SUBLANES - 1) + N_EXPERTS * (TM_MOE - SUBLANES)
N_MOE_TILES = -(-_MAX_ROWS // TM_MOE)
P_PAD = N_MOE_TILES * TM_MOE
SLAB_PIECES = tuple(TT >> s for s in range(6))
WAIT_PIECES = (512,) + SLAB_PIECES
assert SLAB_PIECES[-1] == SUBLANES and SLOTS < 2 * WAIT_PIECES[0]


def _rms(x, g):
    return x * lax.rsqrt(jnp.mean(x * x, axis=-1, keepdims=True) + EPS) * g


def _const_spec(shape):
    nd = len(shape)
    return pl.BlockSpec(shape, lambda *_: (0,) * nd)


def _trig_kernel(ang_ref, cos_ref, sin_ref):
    a = ang_ref[...]
    cos_ref[...] = jnp.cos(a)
    sin_ref[...] = jnp.sin(a)


def _rope_tiles(positions):
    inv_freq = ROPE_THETA ** (-jnp.arange(0, QK_ROPE, 2, dtype=F32) / QK_ROPE)
    ang = positions.astype(F32).reshape(TOKENS, 1) * inv_freq
    ang = ang.reshape(TOKENS * HALF_ROPE // LANES, LANES)
    cos, sin = pl.pallas_call(
        _trig_kernel,
        out_shape=(jax.ShapeDtypeStruct(ang.shape, F32),) * 2,
        name="rope_trig",
    )(ang)
    cos = cos.reshape(TOKENS, HALF_ROPE)
    sin = sin.reshape(TOKENS, HALF_ROPE)
    z_nope = jnp.zeros((TOKENS, QK_NOPE), F32)
    z_pad = jnp.zeros((TOKENS, HEAD_PAD - QK_DIM), F32)
    ck = jnp.concatenate([z_nope, cos, cos, z_pad], axis=1)
    sk = jnp.concatenate([z_nope, -sin, sin, z_pad], axis=1)
    return ck, sk, cos.T, sin.T


def _dot_nt(a, b):
    return lax.dot_general(a, b, (((1,), (1,)), ((), ())), preferred_element_type=F32)


def _front_kernel(x_ref, an_ref, win_ref, cw_ref, wco_ref, qn_ref, wq_ref, kvn_ref, wuk_ref, wuv_ref,
                  ck_ref, sk_ref, cost_ref, sint_ref, q_out, k_out, v_out, gc_out, sg_out, cu_scr):
    tm = x_ref.shape[0]
    i = pl.program_id(0)
    xn = _rms(x_ref[...], an_ref[...]).astype(BF16)

    def proj(a, b):
        return jnp.dot(xn, win_ref[:, a:b], preferred_element_type=F32)

    bcu = proj(OFF_BCU, OFF_GATE)
    gates = proj(OFF_GATE, OFF_SMALL)
    small = proj(OFF_SMALL, D_IN_PAD)

    b_g = bcu[:, :D_CONV]
    cu = bcu[:, D_CONV:2 * D_CONV] * bcu[:, 2 * D_CONV:]

    @pl.when(i % (SEQ // tm) == 0)
    def _():
        cu_scr[0:8, :] = jnp.zeros((8, D_CONV), F32)

    cu_scr[8:8 + tm, :] = cu
    prev1 = cu_scr[7:7 + tm, :]
    prev2 = cu_scr[6:6 + tm, :]
    cw = cw_ref[...]
    conv = prev2 * cw[0:1, :] + prev1 * cw[1:2, :] + cu * cw[2:3, :]
    cu_scr[0:8, :] = cu[tm - 8:, :]
    y_conv = jnp.dot((b_g * conv).astype(BF16), wco_ref[...], preferred_element_type=F32)

    gc_out[...] = (jax.nn.sigmoid(gates[:, :D_MODEL]) * y_conv).astype(BF16)
    sg_out[...] = jax.nn.sigmoid(gates[:, D_MODEL:]).astype(BF16)

    cq = cost_ref[...] * Q_SCALE
    sq = sint_ref[...] * Q_SCALE
    cqn = _rms(small[:, :Q_LORA], qn_ref[...]).astype(BF16)
    qt = _dot_nt(wq_ref[...], cqn)
    zero_pad = jnp.zeros((HEAD_PAD - QK_DIM, tm), BF16)
    for h in range(N_HEADS):
        src = h * QK_DIM
        dst = h * HEAD_PAD
        x1 = qt[src + QK_NOPE:src + QK_NOPE + HALF_ROPE, :]
        x2 = qt[src + QK_NOPE + HALF_ROPE:src + QK_DIM, :]
        q_out[dst:dst + QK_NOPE, :] = (qt[src:src + QK_NOPE, :] * Q_SCALE).astype(BF16)
        q_out[dst + QK_NOPE:dst + QK_NOPE + HALF_ROPE, :] = (x1 * cq - x2 * sq).astype(BF16)
        q_out[dst + QK_NOPE + HALF_ROPE:dst + QK_DIM, :] = (x2 * cq + x1 * sq).astype(BF16)
        q_out[dst + QK_DIM:dst + HEAD_PAD, :] = zero_pad

    ck = ck_ref[...]
    sk = sk_ref[...]
    lane = lax.broadcasted_iota(jnp.int32, (1, HEAD_PAD), 1)
    sk_x1 = jnp.where(lane < QK_NOPE + HALF_ROPE, sk, 0.0)
    sk_x2 = sk - sk_x1
    kr = small[:, Q_LORA + KV_LORA:]
    kpe = (kr * ck + pltpu.roll(kr, HEAD_PAD - HALF_ROPE, axis=1) * sk_x1 + pltpu.roll(kr, HALF_ROPE, axis=1) * sk_x2)
    ckvn = _rms(small[:, Q_LORA:Q_LORA + KV_LORA], kvn_ref[...]).astype(BF16)
    kn = jnp.dot(ckvn, wuk_ref[...], preferred_element_type=F32)
    for h in range(N_HEADS):
        blk = slice(h * HEAD_PAD, (h + 1) * HEAD_PAD)
        k_out[:, blk] = (kn[:, blk] + kpe).astype(BF16)
    vt = _dot_nt(wuv_ref[...], ckvn).astype(BF16)
    for c in range(tm // TK):
        v_out[c] = vt[:, c * TK:(c + 1) * TK]


def _front(x, an, win, cw, wco, qn, wq, kvn, wuk, wuv, ck, sk, cos_t, sin_t):
    tm = TM_FRONT
    row = lambda n: pl.BlockSpec((tm, n), lambda i: (i, 0))
    col = lambda n: pl.BlockSpec((n, tm), lambda i: (0, i))
    return pl.pallas_call(
        _front_kernel,
        grid=(TOKENS // tm,),
        in_specs=[row(D_MODEL), _const_spec((1, D_MODEL)), _const_spec((D_MODEL, D_IN_PAD)),
                  _const_spec((CONV_WIDTH, D_CONV)), _const_spec((D_CONV, D_MODEL)),
                  _const_spec((1, Q_LORA)), _const_spec((N_HEADS * QK_DIM, Q_LORA)),
                  _const_spec((1, KV_LORA)), _const_spec((KV_LORA, N_HEADS * HEAD_PAD)),
                  _const_spec((N_HEADS * V_HEAD, KV_LORA)),
                  row(HEAD_PAD), row(HEAD_PAD), col(HALF_ROPE), col(HALF_ROPE)],
        out_specs=[col(N_HEADS * HEAD_PAD), row(N_HEADS * HEAD_PAD),
                   pl.BlockSpec((tm // TK, N_HEADS * V_HEAD, TK), lambda i: (i, 0, 0)),
                   row(D_MODEL), row(D_MODEL)],
        out_shape=[jax.ShapeDtypeStruct((N_HEADS * HEAD_PAD, TOKENS), BF16),
                   jax.ShapeDtypeStruct((TOKENS, N_HEADS * HEAD_PAD), BF16),
                   jax.ShapeDtypeStruct((TOKENS // TK, N_HEADS * V_HEAD, TK), BF16),
                   jax.ShapeDtypeStruct((TOKENS, D_MODEL), BF16),
                   jax.ShapeDtypeStruct((TOKENS, D_MODEL), BF16)],
        scratch_shapes=[pltpu.VMEM((tm + 8, D_CONV), F32)],
        compiler_params=pltpu.CompilerParams(dimension_semantics=("arbitrary",), vmem_limit_bytes=VMEM_LIMIT),
        name="front",
    )(x, an, win, cw, wco, qn, wq, kvn, wuk, wuv, ck, sk, cos_t, sin_t)


def _attn_items():
    items = []
    for i in range(SEQ // TQ):
        ks = 0
        while ks + TK <= i * TQ:
            items.append((i, ks, TK, False))
            ks += TK
        items.append((i, ks, (i + 1) * TQ - ks, True))
    return items


def _attn_kernel(qt_ref, k_ref, vt_ref, o_ref):
    row = lax.broadcasted_iota(jnp.int32, (TQ, TQ), 0)
    col = lax.broadcasted_iota(jnp.int32, (TQ, TQ), 1)
    causal = row <= col

    def scores(item):
        i, ks, n, diag = item
        sts = []
        for h in range(2):
            hs = slice(h * HEAD_PAD, (h + 1) * HEAD_PAD)
            st = jnp.dot(k_ref[ks:ks + n, hs], qt_ref[hs, i * TQ:(i + 1) * TQ], preferred_element_type=F32)
            if diag:
                tail = jnp.where(causal, st[n - TQ:, :], NEG_BIG)
                st = tail if n == TQ else jnp.concatenate([st[:n - TQ, :], tail], axis=0)
            sts.append(st)
        return sts

    def consume(item, sts, carry):
        i, ks, n, _ = item
        stats = []
        for h in range(2):
            m, l, _ = carry[h]
            m_new = jnp.maximum(m, jnp.max(sts[h], axis=0, keepdims=True))
            alpha = jnp.exp2(m - m_new)
            p = jnp.exp2(sts[h] - m_new)
            l = alpha * l + jnp.sum(p, axis=0, keepdims=True)
            stats.append((m_new, l, alpha, p.astype(BF16)))
        new = []
        for h in range(2):
            m_new, l, alpha, p = stats[h]
            vt = vt_ref[ks // TK, h * V_HEAD:(h + 1) * V_HEAD, ks % TK:ks % TK + n]
            acc = alpha * carry[h][2] + jnp.dot(vt, p, preferred_element_type=F32)
            new.append((m_new, l, acc))
        return new

    items = _attn_items()
    init = [(jnp.full((1, TQ), NEG_BIG, F32), jnp.zeros((1, TQ), F32), jnp.zeros((V_HEAD, TQ), F32))
            for _ in range(2)]
    cur = scores(items[0])
    carry = init
    for t, item in enumerate(items):
        nxt = scores(items[t + 1]) if t + 1 < len(items) else None
        carry = consume(item, cur, carry)
        if item[3]:
            i = item[0]
            out_t = jnp.concatenate([acc / l for (_, l, acc) in carry], axis=0)
            o_ref[i * TQ:(i + 1) * TQ, :] = out_t.T.astype(BF16)
            carry = init
        cur = nxt


def _attention(qt, k, vt):
    return pl.pallas_call(
        _attn_kernel,
        grid=(BATCH, N_HEADS // 2),
        in_specs=[pl.BlockSpec((2 * HEAD_PAD, SEQ), lambda b, hp: (hp, b)),
                  pl.BlockSpec((SEQ, 2 * HEAD_PAD), lambda b, hp: (b, hp)),
                  pl.BlockSpec((SEQ // TK, 2 * V_HEAD, TK), lambda b, hp: (b, hp, 0))],
        out_specs=pl.BlockSpec((SEQ, 2 * V_HEAD), lambda b, hp: (b, hp)),
        out_shape=jax.ShapeDtypeStruct((TOKENS, N_HEADS * V_HEAD), BF16),
        compiler_params=pltpu.CompilerParams(dimension_semantics=("arbitrary",) * 2, vmem_limit_bytes=VMEM_LIMIT),
        name="attention",
    )(qt, k, vt)


def _merge_core(attn_ref, gc_ref, sg_ref, x_ref, wmo_ref, wo_ref, fn_ref):
    y_mla = jnp.dot(attn_ref[...], wmo_ref[...], preferred_element_type=F32)
    merged = gc_ref[...].astype(F32) + sg_ref[...].astype(F32) * y_mla
    x2 = x_ref[...] + jnp.dot(merged.astype(BF16), wo_ref[...], preferred_element_type=F32)
    return x2, _rms(x2, fn_ref[...])


def _merge_dense_kernel(attn_ref, gc_ref, sg_ref, x_ref, wmo_ref, wo_ref, fn_ref, x2_out, hn_out):
    x2, hn = _merge_core(attn_ref, gc_ref, sg_ref, x_ref, wmo_ref, wo_ref, fn_ref)
    x2_out[...] = x2
    hn_out[...] = hn.astype(BF16)


def _merge_route_kernel(attn_ref, gc_ref, sg_ref, x_ref, wmo_ref, wo_ref, fn_ref, router_ref,
                        x2_out, hn_out, route_out, route_t_out):
    x2, hn = _merge_core(attn_ref, gc_ref, sg_ref, x_ref, wmo_ref, wo_ref, fn_ref)
    x2_out[...] = x2
    hb = hn.astype(BF16)
    hn_out[...] = hb
    hl = (hn - hb.astype(F32)).astype(BF16)
    r = router_ref[...]
    rh = r.astype(BF16)
    rl = (r - rh.astype(F32)).astype(BF16)
    logits = _dot_nt(rh, hb) + _dot_nt(rh, hl) + _dot_nt(rl, hb)
    erow = lax.broadcasted_iota(jnp.int32, logits.shape, 0)
    logits = jnp.where(erow < N_EXPERTS, logits, -jnp.inf)
    m1 = jnp.max(logits, axis=0, keepdims=True)
    i1 = jnp.min(jnp.where(logits == m1, erow, E_PAD), axis=0, keepdims=True)
    rest = jnp.where(erow == i1, -jnp.inf, logits)
    m2 = jnp.max(rest, axis=0, keepdims=True)
    i2 = jnp.min(jnp.where(rest == m2, erow, E_PAD), axis=0, keepdims=True)
    t = jnp.exp(m2 - m1)
    w1 = 1.0 / (1.0 + t)
    w2 = t * w1
    route_t = jnp.where(erow == 0, i1.astype(F32),
                        jnp.where(erow == 1, i2.astype(F32), jnp.where(erow == 2, w1, jnp.where(erow == 3, w2, 0.0))))
    route_t_out[...] = route_t[0:SUBLANES, :]
    pad = jnp.zeros((LANES - E_PAD, route_t.shape[1]), F32)
    route_out[...] = jnp.concatenate([route_t, pad], axis=0).T


def _merge(attn, gc, sg, x, wmo, wo, fn, router=None):
    tm = TM_FRONT
    row = lambda n: pl.BlockSpec((tm, n), lambda i: (i, 0))
    in_specs = [row(N_HEADS * V_HEAD), row(D_MODEL), row(D_MODEL), row(D_MODEL),
                _const_spec((N_HEADS * V_HEAD, D_MODEL)), _const_spec((D_MODEL, D_MODEL)), _const_spec((1, D_MODEL))]
    args = [attn, gc, sg, x, wmo, wo, fn]
    if router is None:
        kern = _merge_dense_kernel
        out_specs = [row(D_MODEL), row(D_MODEL)]
        out_shape = [jax.ShapeDtypeStruct((TOKENS, D_MODEL), F32), jax.ShapeDtypeStruct((TOKENS, D_MODEL), BF16)]
        name = "merge_dense"
    else:
        kern = _merge_route_kernel
        in_specs.append(_const_spec((E_PAD, D_MODEL)))
        args.append(router)
        out_specs = [row(D_MODEL), row(D_MODEL), row(LANES), pl.BlockSpec((SUBLANES, tm), lambda i: (0, i))]
        out_shape = [jax.ShapeDtypeStruct((TOKENS, D_MODEL), F32), jax.ShapeDtypeStruct((TOKENS, D_MODEL), BF16),
                     jax.ShapeDtypeStruct((TOKENS, LANES), F32), jax.ShapeDtypeStruct((SUBLANES, TOKENS), F32)]
        name = "merge_route"
    return pl.pallas_call(
        kern, grid=(TOKENS // tm,), in_specs=in_specs, out_specs=out_specs, out_shape=out_shape,
        compiler_params=pltpu.CompilerParams(dimension_semantics=("arbitrary",), vmem_limit_bytes=VMEM_LIMIT),
        name=name,
    )(*args)


def _ffn_kernel(hn_ref, x_ref, wg_ref, wu_ref, wd_ref, o_ref):
    hn = hn_ref[...]
    acc = x_ref[...]
    for c in range(D_FF // FF_CHUNK):
        cs = slice(c * FF_CHUNK, (c + 1) * FF_CHUNK)
        g = jnp.dot(hn, wg_ref[:, cs], preferred_element_type=F32)
        u = jnp.dot(hn, wu_ref[:, cs], preferred_element_type=F32)
        a = (jax.nn.silu(g) * u).astype(BF16)
        acc = acc + jnp.dot(a, wd_ref[cs, :], preferred_element_type=F32)
    o_ref[...] = acc


def _ffn(hn, x2, wg, wu, wd):
    tm = TM_FRONT
    row = lambda n: pl.BlockSpec((tm, n), lambda i: (i, 0))
    return pl.pallas_call(
        _ffn_kernel,
        grid=(TOKENS // tm,),
        in_specs=[row(D_MODEL), row(D_MODEL), _const_spec((D_MODEL, D_FF)), _const_spec((D_MODEL, D_FF)),
                  _const_spec((D_FF, D_MODEL))],
        out_specs=row(D_MODEL),
        out_shape=jax.ShapeDtypeStruct((TOKENS, D_MODEL), F32),
        compiler_params=pltpu.CompilerParams(dimension_semantics=("arbitrary",), vmem_limit_bytes=VMEM_LIMIT),
        name="ffn_dense",
    )(hn, x2, wg, wu, wd)


def _pieces(n, sizes):
    return [((n & z) != 0, n & ~(2 * z - 1), z) for z in sizes]


def _aligned(rows, z):
    return pl.ds(rows if isinstance(rows, int) else pl.multiple_of(rows, SUBLANES), z)


def _start_slab_copies(tile, base_ref, offp_ref, np_ref, make_copy):
    for e in range(N_EXPERTS):
        n = np_ref[tile * N_EXPERTS + e]
        src = offp_ref[tile * N_EXPERTS + e]
        dst = base_ref[tile * N_EXPERTS + e]
        for pred, off, z in _pieces(n, SLAB_PIECES):
            @pl.when(pred)
            def _(off=off, z=z):
                make_copy(src + off, dst + off, z).start()


def _wait_rows(total, make_copy):
    for pred, _, z in _pieces(total, WAIT_PIECES):
        @pl.when(pred)
        def _(z=z):
            make_copy(0, 0, z).wait()


def _dispatch_kernel(base_ref, offp_ref, np_ref, tot_ref, tail_ref, hn_ref, rt_ref, xs_hbm, xs_scr, zbuf, sem, zsem):
    i = pl.program_id(0)
    slot = i % 2

    def copy(buf_slot):
        return lambda s, d, z: pltpu.make_async_copy(xs_scr.at[buf_slot, _aligned(s, z), :],
                                                     xs_hbm.at[_aligned(d, z), :], sem.at[buf_slot])

    def zero_copy(d, z):
        return pltpu.make_async_copy(zbuf.at[pl.ds(0, z), :], xs_hbm.at[_aligned(d, z), :], zsem)

    @pl.when(i == 0)
    def _():
        zbuf[...] = jnp.zeros_like(zbuf)
        for phase in ("start", "wait"):
            for e in range(N_EXPERTS):
                t0 = tail_ref[e]
                n = tail_ref[N_EXPERTS + e]
                full = lax.shift_right_logical(n, jnp.int32(TM_MOE.bit_length() - 1))

                def chunk(c, carry, t0=t0, phase=phase):
                    cp = zero_copy(t0 + c * TM_MOE, TM_MOE)
                    cp.start() if phase == "start" else cp.wait()
                    return carry

                lax.fori_loop(0, full, chunk, 0)
                rest = n & (TM_MOE - 1)
                for pred, off, z in _pieces(rest, SLAB_PIECES[1:]):
                    @pl.when(pred)
                    def _(off=off, z=z, t0=t0, full=full, phase=phase):
                        cp = zero_copy(t0 + full * TM_MOE + off, z)
                        cp.start() if phase == "start" else cp.wait()

    e0 = rt_ref[0:1, :].astype(jnp.int32)
    e1 = rt_ref[1:2, :].astype(jnp.int32)
    eid = lax.broadcasted_iota(jnp.int32, (E_PAD, TT), 0)
    sel = jnp.where(e0 == eid, 1.0, jnp.where(e1 == eid, 1.0, 0.0)).astype(BF16)
    r = lax.broadcasted_iota(jnp.int32, (TT, TT), 0)
    c = lax.broadcasted_iota(jnp.int32, (TT, TT), 1)
    earlier = jnp.where(r < c, 1.0, 0.0).astype(BF16)
    rank = jnp.dot(sel, earlier, preferred_element_type=F32)
    ecol = lax.broadcasted_iota(jnp.int32, (E_PAD, 1), 0)
    offp = jnp.zeros((E_PAD, 1), F32)
    for e in range(N_EXPERTS):
        offp = jnp.where(ecol == e, offp_ref[i * N_EXPERTS + e].astype(F32), offp)
    slot_all = rank + offp
    slot0 = jnp.sum(jnp.where(e0 == eid, slot_all, 0.0), axis=0, keepdims=True)
    slot1 = jnp.sum(jnp.where(e1 == eid, slot_all, 0.0), axis=0, keepdims=True)
    sidx = lax.broadcasted_iota(jnp.int32, (SLOTS, TT), 0).astype(F32)
    place = jnp.where(sidx == slot0, 1.0, jnp.where(sidx == slot1, 1.0, 0.0)).astype(BF16)
    xs_scr[slot, :, 0:D_MODEL] = jnp.dot(place, hn_ref[...], preferred_element_type=F32)
    w_slot = jnp.sum(jnp.where(sidx == slot0, rt_ref[2:3, :], jnp.where(sidx == slot1, rt_ref[3:4, :], 0.0)),
                     axis=1, keepdims=True)
    xs_scr[slot, :, D_MODEL:] = jnp.broadcast_to(w_slot, (SLOTS, LANES))

    _start_slab_copies(i, base_ref, offp_ref, np_ref, copy(slot))

    @pl.when(i > 0)
    def _():
        _wait_rows(tot_ref[i - 1], copy(1 - slot))

    @pl.when(i == pl.num_programs(0) - 1)
    def _():
        _wait_rows(tot_ref[i], copy(slot))


def _dispatch(tables, hn, route_t):
    grid_spec = pltpu.PrefetchScalarGridSpec(
        num_scalar_prefetch=5,
        grid=(N_TT,),
        in_specs=[pl.BlockSpec((TT, D_MODEL), lambda i, *_: (i, 0)),
                  pl.BlockSpec((SUBLANES, TT), lambda i, *_: (0, i))],
        out_specs=pl.BlockSpec(memory_space=pl.ANY),
        scratch_shapes=[pltpu.VMEM((2, SLOTS, XS_WIDTH), F32), pltpu.VMEM((TM_MOE, XS_WIDTH), F32),
                        pltpu.SemaphoreType.DMA((2,)), pltpu.SemaphoreType.DMA],
    )
    return pl.pallas_call(
        _dispatch_kernel,
        grid_spec=grid_spec,
        out_shape=jax.ShapeDtypeStruct((P_PAD, XS_WIDTH), F32),
        compiler_params=pltpu.CompilerParams(dimension_semantics=("arbitrary",), vmem_limit_bytes=VMEM_LIMIT),
        name="moe_dispatch",
    )(tables["base"], tables["offp"], tables["np"], tables["total"], tables["tail"], hn, route_t)


def _moe_kernel(te_ref, nv_ref, x_ref, wgu_ref, wd_ref, y_ref):
    i = pl.program_id(0)

    @pl.when(i < nv_ref[0])
    def _():
        x = x_ref[:, 0:D_MODEL].astype(BF16)
        h = jnp.dot(x, wgu_ref[0], preferred_element_type=F32)
        a = (jax.nn.silu(h[:, :D_FF_EXPERT]) * h[:, D_FF_EXPERT:]).astype(BF16)
        y_ref[...] = jnp.dot(a, wd_ref[0], preferred_element_type=F32) * x_ref[:, D_MODEL:D_MODEL + 1]

    @pl.when(i >= nv_ref[0])
    def _():
        y_ref[...] = jnp.zeros_like(y_ref)


def _moe(tile_expert, n_valid, xs, wgu, wd):
    grid_spec = pltpu.PrefetchScalarGridSpec(
        num_scalar_prefetch=2,
        grid=(N_MOE_TILES,),
        in_specs=[pl.BlockSpec((TM_MOE, XS_WIDTH), lambda i, te, nv: (i, 0)),
                  pl.BlockSpec((1, D_MODEL, 2 * D_FF_EXPERT), lambda i, te, nv: (te[i], 0, 0)),
                  pl.BlockSpec((1, D_FF_EXPERT, D_MODEL), lambda i, te, nv: (te[i], 0, 0))],
        out_specs=pl.BlockSpec((TM_MOE, D_MODEL), lambda i, te, nv: (i, 0)),
    )
    return pl.pallas_call(
        _moe_kernel,
        grid_spec=grid_spec,
        out_shape=jax.ShapeDtypeStruct((P_PAD, D_MODEL), F32),
        compiler_params=pltpu.CompilerParams(dimension_semantics=("arbitrary",), vmem_limit_bytes=VMEM_LIMIT),
        name="moe_experts",
    )(tile_expert, n_valid, xs, wgu, wd)


def _combine_kernel(base_ref, offp_ref, np_ref, tot_ref, x_ref, route_ref, y_hbm, fn_ref, o_ref, ybuf, sem):
    i = pl.program_id(0)
    slot = i % 2

    def copy(buf_slot):
        return lambda s, d, z: pltpu.make_async_copy(y_hbm.at[_aligned(d, z), :],
                                                     ybuf.at[buf_slot, _aligned(s, z), :], sem.at[buf_slot])

    @pl.when(i == 0)
    def _():
        _start_slab_copies(0, base_ref, offp_ref, np_ref, copy(0))

    @pl.when(i + 1 < pl.num_programs(0))
    def _():
        _start_slab_copies(i + 1, base_ref, offp_ref, np_ref, copy(1 - slot))

    route = route_ref[...]
    e0 = route[:, 0:1].astype(jnp.int32)
    e1 = route[:, 1:2].astype(jnp.int32)
    eid =lax.broadcasted_iota(jnp.int32, (TT, LANES), 1)
    sel = jnp.where(e0 == eid, 1.0, jnp.where(e1 == eid, 1.0, 0.0)).astype(BF16)
    r = lax.broadcasted_iota(jnp.int32, (TT, TT), 0)
    c = lax.broadcasted_iota(jnp.int32, (TT, TT), 1)
    earlier = jnp.where(c < r, 1.0, 0.0).astype(BF16)
    rank = jnp.dot(earlier, sel, preferred_element_type=F32)
    erow = lax.broadcasted_iota(jnp.int32, (1, LANES), 1)
    offp = jnp.zeros((1, LANES), F32)
    for e in range(N_EXPERTS):
        offp = jnp.where(erow == e, offp_ref[i * N_EXPERTS + e].astype(F32), offp)
    slot_all = rank + offp
    slot0 = jnp.sum(jnp.where(e0 == eid, slot_all, 0.0), axis=1, keepdims=True)
    slot1 = jnp.sum(jnp.where(e1 == eid, slot_all, 0.0), axis=1, keepdims=True)
    sidx = lax.broadcasted_iota(jnp.int32, (TT, SLOTS), 1).astype(F32)
    pick = jnp.where(sidx == slot0, 1.0, jnp.where(sidx == slot1, 1.0, 0.0)).astype(BF16)

    total = tot_ref[i]
    _wait_rows(total, copy(slot))
    srow = lax.broadcasted_iota(jnp.int32, (SLOTS, 1), 0)
    y = jnp.where(srow < total, ybuf[slot], 0.0)
    yh = y.astype(BF16)
    yl = (y - yh.astype(F32)).astype(BF16)
    moe = jnp.dot(pick, yh, preferred_element_type=F32) + jnp.dot(pick, yl, preferred_element_type=F32)
    o_ref[...] = _rms(x_ref[...] + moe, fn_ref[...])


def _combine(tables, x2, route, y_sorted, fn):
    grid_spec = pltpu.PrefetchScalarGridSpec(
        num_scalar_prefetch=4,
        grid=(N_TT,),
        in_specs=[pl.BlockSpec((TT, D_MODEL), lambda i, *_: (i, 0)),
                  pl.BlockSpec((TT, LANES), lambda i, *_: (i, 0)),
                  pl.BlockSpec(memory_space=pl.ANY),
                  pl.BlockSpec((1, D_MODEL), lambda i, *_: (0, 0))],
        out_specs=pl.BlockSpec((TT, D_MODEL), lambda i, *_: (i, 0)),
        scratch_shapes=[pltpu.VMEM((2, SLOTS, D_MODEL), F32), pltpu.SemaphoreType.DMA((2,))],
    )
    return pl.pallas_call(
        _combine_kernel,
        grid_spec=grid_spec,
        out_shape=jax.ShapeDtypeStruct((TOKENS, D_MODEL), F32),
        compiler_params=pltpu.CompilerParams(dimension_semantics=("arbitrary",), vmem_limit_bytes=VMEM_LIMIT),
        name="moe_combine",
    )(tables["base"], tables["offp"], tables["np"], tables["total"], x2, route, y_sorted, fn)


def _routing_tables(route):
    experts = route[:, :TOP_K].astype(jnp.int32)
    chosen = jnp.sum((experts[:, :, None] == jnp.arange(N_EXPERTS, dtype=jnp.int32)).astype(jnp.int32), axis=1)
    cnt = jnp.sum(chosen.reshape(N_TT, TT, N_EXPERTS), axis=1)
    npad = (cnt + SUBLANES - 1) // SUBLANES * SUBLANES
    offp = jnp.cumsum(npad, axis=1) - npad
    total = jnp.sum(npad, axis=1)
    group = jnp.sum(npad, axis=0)
    group_pad = (group + TM_MOE - 1) // TM_MOE * TM_MOE
    group_end = jnp.cumsum(group_pad)
    group_start = group_end - group_pad
    base = group_start[None, :] + jnp.cumsum(npad, axis=0) - npad
    tail_start = group_start + group
    is_last = jnp.arange(N_EXPERTS) == N_EXPERTS - 1
    tail_len = jnp.where(is_last, P_PAD - tail_start, group_pad - group)
    tile_start = jnp.arange(N_MOE_TILES, dtype=jnp.int32) * TM_MOE
    tile_expert = jnp.minimum(jnp.sum((tile_start[:, None] >= group_end[None, :]).astype(jnp.int32), axis=1),
                              N_EXPERTS - 1)
    n_valid = (group_end[-1] // TM_MOE).reshape(1)
    tables = {"base": base.reshape(-1), "offp": offp.reshape(-1), "np": npad.reshape(-1), "total": total,
              "tail": jnp.concatenate([tail_start, tail_len])}
    return {k: v.astype(jnp.int32) for k, v in tables.items()}, tile_expert.astype(jnp.int32), n_valid.astype(jnp.int32)


def _layer_weights(w_in, w_uq, w_ukv):
    z_nope = jnp.zeros((D_MODEL, QK_NOPE), F32)
    z_pad = jnp.zeros((D_MODEL, HEAD_PAD - QK_DIM), F32)
    kr_tile = jnp.concatenate([z_nope, w_in[:, REF_KR:REF_GATE], z_pad], axis=1)
    win = jnp.concatenate([w_in[:, :REF_CQ], w_in[:, REF_GATE:], w_in[:, REF_CQ:REF_KR], kr_tile],
                          axis=1).astype(BF16)
    wq = w_uq.T.astype(BF16)

    wkv = w_ukv.reshape(KV_LORA, N_HEADS, QK_NOPE + V_HEAD)
    wuk = jnp.concatenate([wkv[:, :, :QK_NOPE], jnp.zeros((KV_LORA, N_HEADS, HEAD_PAD - QK_NOPE), F32)], axis=2)
    wuk = wuk.reshape(KV_LORA, N_HEADS * HEAD_PAD).astype(BF16)
    wuv = wkv[:, :, QK_NOPE:].reshape(KV_LORA, N_HEADS * V_HEAD).T.astype(BF16)
    return win, wq, wuk, wuv


def kernel(x, positions, attn_norm, w_in, conv_w, w_conv_out, q_norm, w_uq, kv_norm, w_ukv, w_mla_out, w_o, ffn_norm,
           w_gate, w_up, w_down, router, w_gate_e, w_up_e, w_down_e, final_norm):
    assert x.shape == (BATCH, SEQ, D_MODEL) and positions.shape == (BATCH, SEQ)
    ck, sk, cos_t, sin_t = _rope_tiles(positions)
    xt = x.reshape(TOKENS, D_MODEL)
    out = None
    for l in range(DEPTH):
        win, wq, wuk, wuv = _layer_weights(w_in[l], w_uq[l], w_ukv[l])
        q, k, v, gc, sg = _front(xt, attn_norm[l].reshape(1, D_MODEL), win, conv_w[l], w_conv_out[l].astype(BF16),
                                 q_norm[l].reshape(1, Q_LORA), wq, kv_norm[l].reshape(1, KV_LORA), wuk, wuv,
                                 ck, sk, cos_t, sin_t)
        attn = _attention(q, k, v)
        wmo = w_mla_out[l].astype(BF16)
        wo = w_o[l].astype(BF16)
        fn = ffn_norm[l].reshape(1, D_MODEL)
        if l % 2 == 0:
            j = l // 2
            x2, hn = _merge(attn, gc, sg, xt, wmo, wo, fn)
            xt = _ffn(hn, x2, w_gate[j].astype(BF16), w_up[j].astype(BF16), w_down[j].astype(BF16))
        else:
            j = l // 2
            router_pad = jnp.pad(router[j].T, ((0, E_PAD - N_EXPERTS), (0, 0)))
            x2, hn, route, route_t = _merge(attn, gc, sg, xt, wmo, wo, fn, router_pad)
            tables, tile_expert, n_valid = _routing_tables(route)
            xs = _dispatch(tables, hn, route_t)
            wgu = jnp.concatenate([w_gate_e[j], w_up_e[j]], axis=2).astype(BF16)
            y_sorted = _moe(tile_expert, n_valid, xs, wgu, w_down_e[j].astype(BF16))
            out = _combine(tables, x2, route, y_sorted, final_norm.reshape(1, D_MODEL))
    return out.reshape(BATCH, SEQ, D_MODEL)
```

```python
import math

import jax
import jax.numpy as jnp
from jax import lax
from jax.experimental import pallas as pl
from jax.experimental.pallas import tpu as pltpu

F32 = jnp.float32
BF16 = jnp.bfloat16

D_MODEL = 1024
BATCH = 8
SEQ = 2048
TOKENS = BATCH * SEQ
DEPTH = 2
D_CONV = 512
CONV_WIDTH = 3
N_HEADS = 8
QK_NOPE = 64
QK_ROPE = 32
HALF_ROPE = QK_ROPE // 2
V_HEAD = 64
Q_LORA = 384
KV_LORA = 256
ROPE_THETA = 10000.0
D_FF = 2816
N_EXPERTS = 8
TOP_K = 2
D_FF_EXPERT = 1408
EPS = 1e-6

LANES = 128
HEAD_PAD = LANES
QK_DIM = QK_NOPE + QK_ROPE
Q_SCALE = (1.0 / math.sqrt(QK_DIM)) * math.log2(math.e)
NEG_BIG = -1e30

OFF_BCU = 0
OFF_GATE = 3 * D_CONV
OFF_SMALL = OFF_GATE + 2 * D_MODEL
D_IN_PAD = OFF_SMALL + Q_LORA + KV_LORA + LANES
REF_CQ = 3 * D_CONV
REF_KR = REF_CQ + Q_LORA + KV_LORA
REF_GATE = REF_KR + QK_ROPE

TM_FRONT = 512
TQ = 256
TK = 512
FF_CHUNK = 256
VMEM_LIMIT = 56 * 1024 * 1024

SUBLANES = 8
TM_MOE = 256
TT = 256
N_TT = TOKENS // TT
E_PAD = 16
XS_WIDTH = D_MODEL + LANES
N_PAIRS = TOKENS * TOP_K
SLOTS =TT * TOP_K + N_EXPERTS * SUBLANES
_MAX_ROWS = N_PAIRS + N_TT * N_EXPERTS * (SUBLANES - 1) + N_EXPERTS * (TM_MOE - SUBLANES)
N_MOE_TILES = -(-_MAX_ROWS // TM_MOE)
P_PAD = N_MOE_TILES * TM_MOE
SLAB_PIECES = tuple(TT >> s for s in range(6))
WAIT_PIECES = (512,) + SLAB_PIECES
assert SLAB_PIECES[-1] == SUBLANES and SLOTS < 2 * WAIT_PIECES[0]


def _rms(x, g):
    return x * lax.rsqrt(jnp.mean(x * x, axis=-1, keepdims=True) + EPS) * g


def _const_spec(shape):
    nd = len(shape)
    return pl.BlockSpec(shape, lambda *_: (0,) * nd, pipeline_mode=pl.Buffered(1))


def _trig_kernel(ang_ref, cos_ref, sin_ref):
    a = ang_ref[...]
    cos_ref[...] = jnp.cos(a)
    sin_ref[...] = jnp.sin(a)


def _rope_tiles(positions):
    inv_freq = ROPE_THETA ** (-jnp.arange(0, QK_ROPE, 2, dtype=F32) / QK_ROPE)
    ang = positions.astype(F32).reshape(TOKENS, 1) * inv_freq
    ang = ang.reshape(TOKENS * HALF_ROPE // LANES, LANES)
    cos, sin = pl.pallas_call(
        _trig_kernel,
        out_shape=(jax.ShapeDtypeStruct(ang.shape, F32),) * 2,
        name="rope_trig",
    )(ang)
    cos = cos.reshape(TOKENS, HALF_ROPE)
    sin = sin.reshape(TOKENS, HALF_ROPE)
    z_nope = jnp.zeros((TOKENS, QK_NOPE), F32)
    z_pad = jnp.zeros((TOKENS, HEAD_PAD - QK_DIM), F32)
    ck = jnp.concatenate([z_nope, cos, cos, z_pad], axis=1)
    sk = jnp.concatenate([z_nope, -sin, sin, z_pad], axis=1)
    return ck, sk, cos.T, sin.T


def _dot_nt(a, b):
    return lax.dot_general(a, b, (((1,), (1,)), ((), ())), preferred_element_type=F32)


def _front_kernel(x_ref, an_ref, win_ref, cw_ref, wco_ref, qn_ref, wq_ref, kvn_ref, wuk_ref, wuv_ref,
                  ck_ref, sk_ref, cost_ref, sint_ref, q_out, k_out, v_out, gc_out, sg_out, cu_scr):
    tm = x_ref.shape[0]
    i = pl.program_id(0)
    xn = _rms(x_ref[...], an_ref[...]).astype(BF16)

    def proj(a, b):
        return jnp.dot(xn, win_ref[:, a:b], preferred_element_type=F32)

    bcu = proj(OFF_BCU, OFF_GATE)
    gates = proj(OFF_GATE, OFF_SMALL)
    small = proj(OFF_SMALL, D_IN_PAD)

    b_g = bcu[:, :D_CONV]
    cu = bcu[:, D_CONV:2 * D_CONV] * bcu[:, 2 * D_CONV:]

    @pl.when(i % (SEQ // tm) == 0)
    def _():
        cu_scr[0:8, :] = jnp.zeros((8, D_CONV), F32)

    cu_scr[8:8 + tm, :] = cu
    prev1 = cu_scr[7:7 + tm, :]
    prev2 = cu_scr[6:6 + tm, :]
    cw = cw_ref[...]
    conv = prev2 * cw[0:1, :] + prev1 * cw[1:2, :] + cu * cw[2:3, :]
    cu_scr[0:8, :] = cu[tm - 8:, :]
    y_conv = jnp.dot((b_g * conv).astype(BF16), wco_ref[...], preferred_element_type=F32)

    gc_out[...] = (jax.nn.sigmoid(gates[:, :D_MODEL]) * y_conv).astype(BF16)
    sg_out[...] = jax.nn.sigmoid(gates[:, D_MODEL:]).astype(BF16)

    cq = cost_ref[...] * Q_SCALE
    sq = sint_ref[...] * Q_SCALE
    cqn = _rms(small[:, :Q_LORA], qn_ref[...]).astype(BF16)
    qt = _dot_nt(wq_ref[...], cqn)
    zero_pad = jnp.zeros((HEAD_PAD - QK_DIM, tm), BF16)
    for h in range(N_HEADS):
        src = h * QK_DIM
        dst = h * HEAD_PAD
        x1 = qt[src + QK_NOPE:src + QK_NOPE + HALF_ROPE, :]
        x2 = qt[src + QK_NOPE + HALF_ROPE:src + QK_DIM, :]
        q_out[dst:dst + QK_NOPE, :] = (qt[src:src + QK_NOPE, :] * Q_SCALE).astype(BF16)
        q_out[dst + QK_NOPE:dst + QK_NOPE + HALF_ROPE, :] = (x1 * cq - x2 * sq).astype(BF16)
        q_out[dst + QK_NOPE + HALF_ROPE:dst + QK_DIM, :] = (x2 * cq + x1 * sq).astype(BF16)
        q_out[dst + QK_DIM:dst + HEAD_PAD, :] = zero_pad

    ck = ck_ref[...]
    sk = sk_ref[...]
    lane = lax.broadcasted_iota(jnp.int32, (1, HEAD_PAD), 1)
    sk_x1 = jnp.where(lane < QK_NOPE + HALF_ROPE, sk, 0.0)
    sk_x2 = sk - sk_x1
    kr = small[:, Q_LORA + KV_LORA:]
    kpe = (kr * ck + pltpu.roll(kr, HEAD_PAD - HALF_ROPE, axis=1) * sk_x1 + pltpu.roll(kr, HALF_ROPE, axis=1) * sk_x2)
    ckvn = _rms(small[:, Q_LORA:Q_LORA + KV_LORA], kvn_ref[...]).astype(BF16)
    kn = jnp.dot(ckvn, wuk_ref[...], preferred_element_type=F32)
    for h in range(N_HEADS):
        blk = slice(h * HEAD_PAD, (h + 1) * HEAD_PAD)
        k_out[:, blk] = (kn[:, blk] + kpe).astype(BF16)
    vt = _dot_nt(wuv_ref[...], ckvn).astype(BF16)
    for c in range(tm // TK):
        v_out[c] = vt[:, c * TK:(c + 1) * TK]


def _front(x, an, win, cw, wco, qn, wq, kvn, wuk, wuv, ck, sk, cos_t, sin_t):
    tm = TM_FRONT
    row = lambda n: pl.BlockSpec((tm, n), lambda i: (i, 0))
    col = lambda n: pl.BlockSpec((n, tm), lambda i: (0, i))
    return pl.pallas_call(
        _front_kernel,
        grid=(TOKENS // tm,),
        in_specs=[row(D_MODEL), _const_spec((1, D_MODEL)), _const_spec((D_MODEL, D_IN_PAD)),
                  _const_spec((CONV_WIDTH, D_CONV)), _const_spec((D_CONV, D_MODEL)),
                  _const_spec((1, Q_LORA)), _const_spec((N_HEADS * QK_DIM, Q_LORA)),
                  _const_spec((1, KV_LORA)), _const_spec((KV_LORA, N_HEADS * HEAD_PAD)),
                  _const_spec((N_HEADS * V_HEAD, KV_LORA)),
                  row(HEAD_PAD), row(HEAD_PAD), col(HALF_ROPE), col(HALF_ROPE)],
        out_specs=[col(N_HEADS * HEAD_PAD), row(N_HEADS * HEAD_PAD),
                   pl.BlockSpec((tm // TK, N_HEADS * V_HEAD, TK), lambda i: (i, 0, 0)),
                   row(D_MODEL), row(D_MODEL)],
        out_shape=[jax.ShapeDtypeStruct((N_HEADS * HEAD_PAD, TOKENS), BF16),
                   jax.ShapeDtypeStruct((TOKENS, N_HEADS * HEAD_PAD), BF16),
                   jax.ShapeDtypeStruct((TOKENS // TK, N_HEADS * V_HEAD, TK), BF16),
                   jax.ShapeDtypeStruct((TOKENS, D_MODEL), BF16),
                   jax.ShapeDtypeStruct((TOKENS, D_MODEL), BF16)],
        scratch_shapes=[pltpu.VMEM((tm + 8, D_CONV), F32)],
        compiler_params=pltpu.CompilerParams(dimension_semantics=("arbitrary",), vmem_limit_bytes=VMEM_LIMIT),
        name="front",
    )(x, an, win, cw, wco, qn, wq, kvn, wuk, wuv, ck, sk, cos_t, sin_t)


def _attn_items():
    items = []
    for i in range(SEQ // TQ):
        ks = 0
        while ks + TK <= i * TQ:
            items.append((i, ks, TK, False))
            ks += TK
        items.append((i, ks, (i + 1) * TQ - ks, True))
    return items


def _attn_kernel(qt_ref, k_ref, vt_ref, o_ref):
    row = lax.broadcasted_iota(jnp.int32, (TQ, TQ), 0)
    col = lax.broadcasted_iota(jnp.int32, (TQ, TQ), 1)
    causal = row <= col

    def scores(item):
        i, ks, n, diag = item
        sts = []
        for h in range(2):
            hs = slice(h * HEAD_PAD, (h + 1) * HEAD_PAD)
            st = jnp.dot(k_ref[ks:ks + n, hs], qt_ref[hs, i * TQ:(i + 1) * TQ], preferred_element_type=F32)
            if diag:
                tail = jnp.where(causal, st[n - TQ:, :], NEG_BIG)
                st = tail if n == TQ else jnp.concatenate([st[:n - TQ, :], tail], axis=0)
            sts.append(st)
        return sts

    def consume(item, sts, carry):
        i, ks, n, _ = item
        stats = []
        for h in range(2):
            m, l, _ = carry[h]
            m_new = jnp.maximum(m, jnp.max(sts[h], axis=0, keepdims=True))
            alpha = jnp.exp2(m - m_new)
            p = jnp.exp2(sts[h] - m_new)
            l = alpha * l + jnp.sum(p, axis=0, keepdims=True)
            stats.append((m_new, l, alpha, p.astype(BF16)))
        new = []
        for h in range(2):
            m_new, l, alpha, p = stats[h]
            vt = vt_ref[ks // TK, h * V_HEAD:(h + 1) * V_HEAD, ks % TK:ks % TK + n]
            acc = alpha * carry[h][2] + jnp.dot(vt, p, preferred_element_type=F32)
            new.append((m_new, l, acc))
        return new

    items = _attn_items()
    init = [(jnp.full((1, TQ), NEG_BIG, F32), jnp.zeros((1, TQ), F32), jnp.zeros((V_HEAD, TQ), F32))
            for _ in range(2)]
    cur = scores(items[0])
    carry = init
    for t, item in enumerate(items):
        nxt = scores(items[t + 1]) if t + 1 < len(items) else None
        carry = consume(item, cur, carry)
        if item[3]:
            i = item[0]
            out_t = jnp.concatenate([acc / l for (_, l, acc) in carry], axis=0)
            o_ref[i * TQ:(i + 1) * TQ, :] = out_t.T.astype(BF16)
            carry = init
        cur = nxt


def _attention(qt, k, vt):
    return pl.pallas_call(
        _attn_kernel,
        grid=(BATCH, N_HEADS // 2),
        in_specs=[pl.BlockSpec((2 * HEAD_PAD, SEQ), lambda b, hp: (hp, b)),
                  pl.BlockSpec((SEQ, 2 * HEAD_PAD), lambda b, hp: (b, hp)),
                  pl.BlockSpec((SEQ // TK, 2 * V_HEAD, TK), lambda b, hp: (b, hp, 0))],
        out_specs=pl.BlockSpec((SEQ, 2 * V_HEAD), lambda b, hp: (b, hp)),
        out_shape=jax.ShapeDtypeStruct((TOKENS, N_HEADS * V_HEAD), BF16),
        compiler_params=pltpu.CompilerParams(dimension_semantics=("arbitrary",) * 2, vmem_limit_bytes=VMEM_LIMIT),
        name="attention",
    )(qt, k, vt)


def _merge_core(attn_ref, gc_ref, sg_ref, x_ref, wmo_ref, wo_ref, fn_ref):
    y_mla = jnp.dot(attn_ref[...], wmo_ref[...], preferred_element_type=F32)
    merged = gc_ref[...].astype(F32) + sg_ref[...].astype(F32) * y_mla
    x2 = x_ref[...] + jnp.dot(merged.astype(BF16), wo_ref[...], preferred_element_type=F32)
    return x2, _rms(x2, fn_ref[...])


def _merge_ffn_kernel(attn_ref, gc_ref, sg_ref, x_ref, wmo_ref, wo_ref, fn_ref, wg_ref, wu_ref, wd_ref, o_ref):
    x2, hn = _merge_core(attn_ref, gc_ref, sg_ref, x_ref, wmo_ref, wo_ref, fn_ref)
    hn = hn.astype(BF16)
    acc = x2
    for c in range(D_FF // FF_CHUNK):
        cs = slice(c * FF_CHUNK, (c + 1) * FF_CHUNK)
        g = jnp.dot(hn, wg_ref[:, cs], preferred_element_type=F32)
        u = jnp.dot(hn, wu_ref[:, cs], preferred_element_type=F32)
        a = (jax.nn.silu(g) * u).astype(BF16)
        acc = acc + jnp.dot(a, wd_ref[cs, :], preferred_element_type=F32)
    o_ref[...] = acc


def _merge_route_kernel(attn_ref, gc_ref, sg_ref, x_ref, wmo_ref, wo_ref, fn_ref, router_ref,
                        x2_out, hn_out, route_out, route_t_out):
    x2, hn = _merge_core(attn_ref, gc_ref, sg_ref, x_ref, wmo_ref, wo_ref, fn_ref)
    x2_out[...] = x2
    hb = hn.astype(BF16)
    hn_out[...] = hb
    hl = (hn - hb.astype(F32)).astype(BF16)
    r = router_ref[...]
    rh = r.astype(BF16)
    rl = (r - rh.astype(F32)).astype(BF16)
    logits = _dot_nt(rh, hb) + _dot_nt(rh, hl) + _dot_nt(rl, hb)
    erow = lax.broadcasted_iota(jnp.int32, logits.shape, 0)
    logits = jnp.where(erow < N_EXPERTS, logits, -jnp.inf)
    m1 = jnp.max(logits, axis=0, keepdims=True)
    i1 = jnp.min(jnp.where(logits == m1, erow, E_PAD), axis=0, keepdims=True)
    rest = jnp.where(erow == i1, -jnp.inf, logits)
    m2 = jnp.max(rest, axis=0, keepdims=True)
    i2 = jnp.min(jnp.where(rest == m2, erow, E_PAD), axis=0, keepdims=True)
    t = jnp.exp(m2 - m1)
    w1 = 1.0 / (1.0 + t)
    w2 = t * w1
    route_t = jnp.where(erow == 0, i1.astype(F32),
                        jnp.where(erow == 1, i2.astype(F32), jnp.where(erow == 2, w1, jnp.where(erow == 3, w2, 0.0))))
    route_t_out[...] = route_t[0:SUBLANES, :]
    pad = jnp.zeros((LANES - E_PAD, route_t.shape[1]), F32)
    route_out[...] = jnp.concatenate([route_t, pad], axis=0).T


def _merge(attn, gc, sg, x, wmo, wo, fn, router=None, ffn=None):
    tm = TM_FRONT
    row = lambda n: pl.BlockSpec((tm, n), lambda i: (i, 0))
    in_specs = [row(N_HEADS * V_HEAD), row(D_MODEL), row(D_MODEL), row(D_MODEL),
                _const_spec((N_HEADS * V_HEAD, D_MODEL)), _const_spec((D_MODEL, D_MODEL)), _const_spec((1, D_MODEL))]
    args = [attn, gc, sg, x, wmo, wo, fn]
    if router is None:
        kern = _merge_ffn_kernel
        in_specs += [_const_spec((D_MODEL, D_FF)), _const_spec((D_MODEL, D_FF)), _const_spec((D_FF, D_MODEL))]
        args += list(ffn)
        out_specs = row(D_MODEL)
        out_shape = jax.ShapeDtypeStruct((TOKENS, D_MODEL), F32)
        name = "merge_ffn"
    else:
        kern = _merge_route_kernel
        in_specs.append(_const_spec((E_PAD, D_MODEL)))
        args.append(router)
        out_specs = [row(D_MODEL), row(D_MODEL), row(LANES), pl.BlockSpec((SUBLANES, tm), lambda i: (0, i))]
        out_shape = [jax.ShapeDtypeStruct((TOKENS, D_MODEL), F32), jax.ShapeDtypeStruct((TOKENS, D_MODEL), BF16),
                     jax.ShapeDtypeStruct((TOKENS, LANES), F32), jax.ShapeDtypeStruct((SUBLANES, TOKENS), F32)]
        name = "merge_route"
    return pl.pallas_call(
        kern, grid=(TOKENS // tm,), in_specs=in_specs, out_specs=out_specs, out_shape=out_shape,
        compiler_params=pltpu.CompilerParams(dimension_semantics=("arbitrary",), vmem_limit_bytes=VMEM_LIMIT),
        name=name,
    )(*args)


def _pieces(n, sizes):
    return [((n & z) != 0, n & ~(2 * z - 1), z) for z in sizes]


def _aligned(rows, z):
    return pl.ds(rows if isinstance(rows, int) else pl.multiple_of(rows, SUBLANES), z)


def _start_slab_copies(tile, base_ref, offp_ref, np_ref, make_copy):
    for e in range(N_EXPERTS):
        n = np_ref[tile * N_EXPERTS + e]
        src = offp_ref[tile * N_EXPERTS + e]
        dst = base_ref[tile * N_EXPERTS + e]
        for pred, off, z in _pieces(n, SLAB_PIECES):
            @pl.when(pred)
            def _(off=off, z=z):
                make_copy(src + off, dst + off, z).start()


def _wait_rows(total, make_copy):
    for pred, _, z in _pieces(total, WAIT_PIECES):
        @pl.when(pred)
        def _(z=z):
            make_copy(0, 0, z).wait()


def _dispatch_kernel(base_ref, offp_ref, np_ref, tot_ref, tail_ref, hn_ref, rt_ref, xs_hbm, xs_scr, zbuf, sem, zsem):
    i = pl.program_id(0)
    slot = i % 2

    def copy(buf_slot):
        return lambda s, d, z: pltpu.make_async_copy(xs_scr.at[buf_slot, _aligned(s, z), :],
                                                     xs_hbm.at[_aligned(d, z), :], sem.at[buf_slot])

    def zero_copy(d, z):
        return pltpu.make_async_copy(zbuf.at[pl.ds(0, z), :], xs_hbm.at[_aligned(d, z), :], zsem)

    @pl.when(i == 0)
    def _():
        zbuf[...] = jnp.zeros_like(zbuf)
        for phase in ("start", "wait"):
            for e in range(N_EXPERTS):
                t0 = tail_ref[e]
                n = tail_ref[N_EXPERTS + e]
                full = lax.shift_right_logical(n, jnp.int32(TM_MOE.bit_length() - 1))

                def chunk(c, carry, t0=t0, phase=phase):
                    cp = zero_copy(t0 + c * TM_MOE, TM_MOE)
                    cp.start() if phase == "start" else cp.wait()
                    return carry

                lax.fori_loop(0, full, chunk, 0)
                rest = n & (TM_MOE - 1)
                for pred, off, z in _pieces(rest, SLAB_PIECES[1:]):
                    @pl.when(pred)
                    def _(off=off, z=z, t0=t0, full=full, phase=phase):
                        cp = zero_copy(t0 + full * TM_MOE + off, z)
                        cp.start() if phase == "start" else cp.wait()

    e0 = rt_ref[0:1, :].astype(jnp.int32)
    e1 = rt_ref[1:2, :].astype(jnp.int32)
    eid = lax.broadcasted_iota(jnp.int32, (E_PAD, TT), 0)
    sel = jnp.where(e0 == eid, 1.0, jnp.where(e1 == eid, 1.0, 0.0)).astype(BF16)
    r = lax.broadcasted_iota(jnp.int32, (TT, TT), 0)
    c = lax.broadcasted_iota(jnp.int32, (TT, TT), 1)
    earlier = jnp.where(r < c, 1.0, 0.0).astype(BF16)
    rank = jnp.dot(sel, earlier, preferred_element_type=F32)
    ecol = lax.broadcasted_iota(jnp.int32, (E_PAD, 1), 0)
    offp = jnp.zeros((E_PAD, 1), F32)
    for e in range(N_EXPERTS):
        offp = jnp.where(ecol == e, offp_ref[i * N_EXPERTS + e].astype(F32), offp)
    slot_all = rank + offp
    slot0 = jnp.sum(jnp.where(e0 == eid, slot_all, 0.0), axis=0, keepdims=True)
    slot1 = jnp.sum(jnp.where(e1 == eid, slot_all, 0.0), axis=0, keepdims=True)
    sidx = lax.broadcasted_iota(jnp.int32, (SLOTS, TT), 0).astype(F32)
    place = jnp.where(sidx == slot0, 1.0, jnp.where(sidx == slot1, 1.0, 0.0)).astype(BF16)
    xs_scr[slot, :, 0:D_MODEL] = jnp.dot(place, hn_ref[...], preferred_element_type=F32)
    w_slot = jnp.sum(jnp.where(sidx == slot0, rt_ref[2:3, :], jnp.where(sidx == slot1, rt_ref[3:4, :], 0.0)),
                     axis=1, keepdims=True)
    xs_scr[slot, :, D_MODEL:] = jnp.broadcast_to(w_slot, (SLOTS, LANES))

    _start_slab_copies(i, base_ref, offp_ref, np_ref, copy(slot))

    @pl.when(i > 0)
    def _():
        _wait_rows(tot_ref[i - 1], copy(1 - slot))

    @pl.when(i == pl.num_programs(0) - 1)
    def _():
        _wait_rows(tot_ref[i], copy(slot))


def _dispatch(tables, hn, route_t):
    grid_spec = pltpu.PrefetchScalarGridSpec(
        num_scalar_prefetch=5,
        grid=(N_TT,),
        in_specs=[pl.BlockSpec((TT, D_MODEL), lambda i, *_: (i, 0)),
                  pl.BlockSpec((SUBLANES, TT), lambda i, *_: (0, i))],
        out_specs=pl.BlockSpec(memory_space=pl.ANY),
        scratch_shapes=[pltpu.VMEM((2, SLOTS, XS_WIDTH), F32), pltpu.VMEM((TM_MOE, XS_WIDTH), F32),
                        pltpu.SemaphoreType.DMA((2,)), pltpu.SemaphoreType.DMA],
    )
    return pl.pallas_call(
        _dispatch_kernel,
        grid_spec=grid_spec,
        out_shape=jax.ShapeDtypeStruct((P_PAD, XS_WIDTH), F32),
        compiler_params=pltpu.CompilerParams(dimension_semantics=("arbitrary",), vmem_limit_bytes=VMEM_LIMIT),
        name="moe_dispatch",
    )(tables["base"], tables["offp"], tables["np"], tables["total"], tables["tail"], hn, route_t)


def _moe_kernel(te_ref, nv_ref, x_ref, wg_ref, wu_ref, wd_ref, y_ref, wgu_scr, wd_scr):
    i = pl.program_id(0)

    @pl.when(jnp.logical_or(i == 0, te_ref[i] != te_ref[jnp.maximum(i - 1, 0)]))
    def _():
        wgu_scr[:, 0:D_FF_EXPERT] = wg_ref[0].astype(BF16)
        wgu_scr[:, D_FF_EXPERT:] = wu_ref[0].astype(BF16)
        wd_scr[...] = wd_ref[0].astype(BF16)

    @pl.when(i < nv_ref[0])
    def _():
        x = x_ref[:, 0:D_MODEL].astype(BF16)
        h = jnp.dot(x, wgu_scr[...], preferred_element_type=F32)
        a = (jax.nn.silu(h[:, :D_FF_EXPERT]) * h[:, D_FF_EXPERT:]).astype(BF16)
        y_ref[...] = jnp.dot(a, wd_scr[...], preferred_element_type=F32) * x_ref[:, D_MODEL:D_MODEL + 1]

    @pl.when(i >= nv_ref[0])
    def _():
        y_ref[...] = jnp.zeros_like(y_ref)


def _moe(tile_expert, n_valid, xs, wg, wu, wd):
    up_spec = pl.BlockSpec((1, D_MODEL, D_FF_EXPERT), lambda i, te, nv: (te[i], 0, 0))
    grid_spec = pltpu.PrefetchScalarGridSpec(
        num_scalar_prefetch=2,
        grid=(N_MOE_TILES,),
        in_specs=[pl.BlockSpec((TM_MOE, XS_WIDTH), lambda i, te, nv: (i, 0)), up_spec, up_spec,
                  pl.BlockSpec((1, D_FF_EXPERT, D_MODEL), lambda i, te, nv: (te[i], 0, 0))],
        out_specs=pl.BlockSpec((TM_MOE, D_MODEL), lambda i, te, nv: (i, 0)),
        scratch_shapes=[pltpu.VMEM((D_MODEL, 2 * D_FF_EXPERT), BF16), pltpu.VMEM((D_FF_EXPERT, D_MODEL), BF16)],
    )
    return pl.pallas_call(
        _moe_kernel,
        grid_spec=grid_spec,
        out_shape=jax.ShapeDtypeStruct((P_PAD, D_MODEL), F32),
        compiler_params=pltpu.CompilerParams(dimension_semantics=("arbitrary",), vmem_limit_bytes=VMEM_LIMIT),
        name="moe_experts",
    )(tile_expert, n_valid, xs, wg, wu, wd)


def _combine_kernel(base_ref, offp_ref, np_ref, tot_ref, x_ref, route_ref, y_hbm, fn_ref, o_ref, ybuf, sem):
    i = pl.program_id(0)
    slot = i % 2

    def copy(buf_slot):
        return lambda s, d, z: pltpu.make_async_copy(y_hbm.at[_aligned(d, z), :],
                                                     ybuf.at[buf_slot, _aligned(s, z), :], sem.at[buf_slot])

    @pl.when(i == 0)
    def _():
        ybuf[...] = jnp.zeros_like(ybuf)
        _start_slab_copies(0, base_ref, offp_ref, np_ref, copy(0))

    @pl.when(i + 1 < pl.num_programs(0))
    def _():
        _start_slab_copies(i + 1, base_ref, offp_ref, np_ref, copy(1 - slot))

    route = route_ref[...]
    e0 = route[:, 0:1].astype(jnp.int32)
    e1 = route[:, 1:2].astype(jnp.int32)
    eid =lax.broadcasted_iota(jnp.int32, (TT, LANES), 1)
    sel = jnp.where(e0 == eid, 1.0, jnp.where(e1 == eid, 1.0, 0.0)).astype(BF16)
    r = lax.broadcasted_iota(jnp.int32, (TT, TT), 0)
    c = lax.broadcasted_iota(jnp.int32, (TT, TT), 1)
    earlier = jnp.where(c < r, 1.0, 0.0).astype(BF16)
    rank = jnp.dot(earlier, sel, preferred_element_type=F32)
    erow = lax.broadcasted_iota(jnp.int32, (1, LANES), 1)
    offp = jnp.zeros((1, LANES), F32)
    for e in range(N_EXPERTS):
        offp = jnp.where(erow == e, offp_ref[i * N_EXPERTS + e].astype(F32), offp)
    slot_all = rank + offp
    slot0 = jnp.sum(jnp.where(e0 == eid, slot_all, 0.0), axis=1, keepdims=True)
    slot1 = jnp.sum(jnp.where(e1 == eid, slot_all, 0.0), axis=1, keepdims=True)
    sidx = lax.broadcasted_iota(jnp.int32, (TT, SLOTS), 1).astype(F32)
    pick = jnp.where(sidx == slot0, 1.0, jnp.where(sidx == slot1, 1.0, 0.0)).astype(BF16)

    _wait_rows(tot_ref[i], copy(slot))
    y = ybuf[slot]
    yh = y.astype(BF16)
    yl = (y - yh.astype(F32)).astype(BF16)
    moe = jnp.dot(pick, yh, preferred_element_type=F32) + jnp.dot(pick, yl, preferred_element_type=F32)
    o_ref[...] = _rms(x_ref[...] + moe, fn_ref[...])


def _combine(tables, x2, route, y_sorted, fn):
    grid_spec = pltpu.PrefetchScalarGridSpec(
        num_scalar_prefetch=4,
        grid=(N_TT,),
        in_specs=[pl.BlockSpec((TT, D_MODEL), lambda i, *_: (i, 0)),
                  pl.BlockSpec((TT, LANES), lambda i, *_: (i, 0)),
                  pl.BlockSpec(memory_space=pl.ANY),
                  pl.BlockSpec((1, D_MODEL), lambda i, *_: (0, 0))],
        out_specs=pl.BlockSpec((TT, D_MODEL), lambda i, *_: (i, 0)),
        scratch_shapes=[pltpu.VMEM((2, SLOTS, D_MODEL), F32), pltpu.SemaphoreType.DMA((2,))],
    )
    return pl.pallas_call(
        _combine_kernel,
        grid_spec=grid_spec,
        out_shape=jax.ShapeDtypeStruct((TOKENS, D_MODEL), F32),
        compiler_params=pltpu.CompilerParams(dimension_semantics=("arbitrary",), vmem_limit_bytes=VMEM_LIMIT),
        name="moe_combine",
    )(tables["base"], tables["offp"], tables["np"], tables["total"], x2, route, y_sorted, fn)


def _routing_tables(route_t):
    experts = route_t[:TOP_K, :].astype(jnp.int32)
    eid = jnp.arange(N_EXPERTS, dtype=jnp.int32)[:, None, None]
    chosen = jnp.sum((experts[None, :, :] == eid).astype(jnp.int32), axis=1)
    cnt = jnp.sum(chosen.reshape(N_EXPERTS, N_TT, TT), axis=2).T
    npad = (cnt + SUBLANES - 1) // SUBLANES * SUBLANES
    offp = jnp.cumsum(npad, axis=1) - npad
    total = jnp.sum(npad, axis=1)
    group = jnp.sum(npad, axis=0)
    group_pad = (group + TM_MOE - 1) // TM_MOE * TM_MOE
    group_end = jnp.cumsum(group_pad)
    group_start = group_end - group_pad
    base = group_start[None, :] + jnp.cumsum(npad, axis=0) - npad
    tail_start = group_start + group
    is_last = jnp.arange(N_EXPERTS) == N_EXPERTS - 1
    tail_len = jnp.where(is_last, P_PAD - tail_start, group_pad - group)
    tile_start = jnp.arange(N_MOE_TILES, dtype=jnp.int32) * TM_MOE
    tile_expert = jnp.minimum(jnp.sum((tile_start[:, None] >= group_end[None, :]).astype(jnp.int32), axis=1),
                              N_EXPERTS - 1)
    n_valid = (group_end[-1] // TM_MOE).reshape(1)
    tables = {"base": base.reshape(-1), "offp": offp.reshape(-1), "np": npad.reshape(-1), "total": total,
              "tail": jnp.concatenate([tail_start, tail_len])}
    return {k: v.astype(jnp.int32) for k, v in tables.items()}, tile_expert.astype(jnp.int32), n_valid.astype(jnp.int32)


def _layer_weights(w_in, w_uq, w_ukv):
    z_nope = jnp.zeros((D_MODEL, QK_NOPE), F32)
    z_pad = jnp.zeros((D_MODEL, HEAD_PAD - QK_DIM), F32)
    kr_tile = jnp.concatenate([z_nope, w_in[:, REF_KR:REF_GATE], z_pad], axis=1)
    win = jnp.concatenate([w_in[:, :REF_CQ], w_in[:, REF_GATE:], w_in[:, REF_CQ:REF_KR], kr_tile],
                          axis=1).astype(BF16)
    wq = w_uq.T.astype(BF16)

    wkv = w_ukv.reshape(KV_LORA, N_HEADS, QK_NOPE + V_HEAD)
    wuk = jnp.concatenate([wkv[:, :, :QK_NOPE], jnp.zeros((KV_LORA, N_HEADS, HEAD_PAD - QK_NOPE), F32)], axis=2)
    wuk = wuk.reshape(KV_LORA, N_HEADS * HEAD_PAD).astype(BF16)
    wuv = wkv[:, :, QK_NOPE:].reshape(KV_LORA, N_HEADS * V_HEAD).T.astype(BF16)
    return win, wq, wuk, wuv


def kernel(x, positions, attn_norm, w_in, conv_w, w_conv_out, q_norm, w_uq, kv_norm, w_ukv, w_mla_out, w_o, ffn_norm,
           w_gate, w_up, w_down, router, w_gate_e, w_up_e, w_down_e, final_norm):
    assert x.shape == (BATCH, SEQ, D_MODEL) and positions.shape == (BATCH, SEQ)
    ck, sk, cos_t, sin_t = _rope_tiles(positions)
    xt = x.reshape(TOKENS, D_MODEL)
    out = None
    for l in range(DEPTH):
        win, wq, wuk, wuv = _layer_weights(w_in[l], w_uq[l], w_ukv[l])
        q, k, v, gc, sg = _front(xt, attn_norm[l].reshape(1, D_MODEL), win, conv_w[l], w_conv_out[l].astype(BF16),
                                 q_norm[l].reshape(1, Q_LORA), wq, kv_norm[l].reshape(1, KV_LORA), wuk, wuv,
                                 ck, sk, cos_t, sin_t)
        attn = _attention(q, k, v)
        wmo = w_mla_out[l].astype(BF16)
        wo = w_o[l].astype(BF16)
        fn = ffn_norm[l].reshape(1, D_MODEL)
        if l % 2 == 0:
            j = l // 2
            xt = _merge(attn, gc, sg, xt, wmo, wo, fn,
                        ffn=(w_gate[j].astype(BF16), w_up[j].astype(BF16), w_down[j].astype(BF16)))
        else:
            j = l // 2
            router_pad = jnp.pad(router[j].T, ((0, E_PAD - N_EXPERTS), (0, 0)))
            x2, hn, route, route_t = _merge(attn, gc, sg, xt, wmo, wo, fn, router_pad)
            tables, tile_expert, n_valid = _routing_tables(route_t)
            xs = _dispatch(tables, hn, route_t)
            y_sorted = _moe(tile_expert, n_valid, xs, w_gate_e[j], w_up_e[j], w_down_e[j])
            out = _combine(tables, x2, route, y_sorted, final_norm.reshape(1, D_MODEL))
    return out.reshape(BATCH, SEQ, D_MODEL)
```

```python
import math

import jax
import jax.numpy as jnp
from jax import lax
from jax.experimental import pallas as pl
from jax.experimental.pallas import tpu as pltpu

F32 = jnp.float32
BF16 = jnp.bfloat16

D_MODEL = 1024
BATCH = 8
SEQ = 2048
TOKENS = BATCH * SEQ
DEPTH = 2
D_CONV = 512
CONV_WIDTH = 3
N_HEADS = 8
QK_NOPE = 64
QK_ROPE = 32
HALF_ROPE = QK_ROPE // 2
V_HEAD = 64
Q_LORA = 384
KV_LORA = 256
ROPE_THETA = 10000.0
D_FF = 2816
N_EXPERTS = 8
TOP_K = 2
D_FF_EXPERT = 1408
EPS = 1e-6

LANES = 128
HEAD_PAD = LANES
QK_DIM = QK_NOPE + QK_ROPE
Q_SCALE = (1.0 / math.sqrt(QK_DIM)) * math.log2(math.e)
NEG_BIG = -1e30

OFF_BCU = 0
OFF_GATE = 3 * D_CONV
OFF_SMALL = OFF_GATE + 2 * D_MODEL
D_IN_PAD = OFF_SMALL + Q_LORA + KV_LORA + LANES
REF_CQ = 3 * D_CONV
REF_KR = REF_CQ + Q_LORA + KV_LORA
REF_GATE = REF_KR + QK_ROPE

TM_FRONT = 512
TQ = 256
TK = 512
FF_CHUNK = 256
VMEM_LIMIT = 56 * 1024 * 1024

SUBLANES = 8
TM_MOE = 256
TT = 256
N_TT = TOKENS // TT
E_PAD = 16
XS_WIDTH = D_MODEL + LANES
N_PAIRS = TOKENS * TOP_K
SLOTS =TT * TOP_K + N_EXPERTS * SUBLANES
_MAX_ROWS = N_PAIRS + N_TT * N_EXPERTS * (SUBLANES - 1) + N_EXPERTS * (TM_MOE - SUBLANES)
N_MOE_TILES = -(-_MAX_ROWS // TM_MOE)
P_PAD = N_MOE_TILES * TM_MOE
SLAB_PIECES = tuple(TT >> s for s in range(6))
WAIT_PIECES = (512,) + SLAB_PIECES
assert SLAB_PIECES[-1] == SUBLANES and SLOTS < 2 * WAIT_PIECES[0]


def _rms(x, g):
    return x * lax.rsqrt(jnp.mean(x * x, axis=-1, keepdims=True) + EPS) * g


def _const_spec(shape):
    nd = len(shape)
    return pl.BlockSpec(shape, lambda *_: (0,) * nd, pipeline_mode=pl.Buffered(1))


def _trig_kernel(pos_ref, invf_ref, cos_ref, sin_ref):
    ang = invf_ref[...] * pos_ref[...].astype(F32)
    cos_ref[...] = jnp.cos(ang)
    sin_ref[...] = jnp.sin(ang)


def _rope_tables(positions):
    inv_freq = ROPE_THETA ** (-jnp.arange(0, QK_ROPE, 2, dtype=F32) / QK_ROPE)
    return pl.pallas_call(
        _trig_kernel,
        out_shape=(jax.ShapeDtypeStruct((HALF_ROPE, TOKENS), F32),) * 2,
        name="rope_trig",
    )(positions.reshape(1, TOKENS), inv_freq.reshape(HALF_ROPE, 1))


def _dot_nt(a, b):
    return lax.dot_general(a, b, (((1,), (1,)), ((), ())), preferred_element_type=F32)


def _front_kernel(x_ref, an_ref, win_ref, cw_ref, wco_ref, qn_ref, wq_ref, kvn_ref, wuk_ref, wuv_ref,
                  cost_ref, sint_ref, q_out, k_out, v_out, gc_out, sg_out, cu_scr):
    tm = x_ref.shape[0]
    i = pl.program_id(0)
    xn = _rms(x_ref[...], an_ref[...]).astype(BF16)

    def proj(a, b):
        return jnp.dot(xn, win_ref[0, :, a:b], preferred_element_type=F32)

    bcu = proj(OFF_BCU, OFF_GATE)
    gates = proj(OFF_GATE, OFF_SMALL)
    small = proj(OFF_SMALL, D_IN_PAD)

    b_g = bcu[:, :D_CONV]
    cu = bcu[:, D_CONV:2 * D_CONV] * bcu[:, 2 * D_CONV:]

    @pl.when(i % (SEQ // tm) == 0)
    def _():
        cu_scr[0:8, :] = jnp.zeros((8, D_CONV), F32)

    cu_scr[8:8 + tm, :] = cu
    prev1 = cu_scr[7:7 + tm, :]
    prev2 = cu_scr[6:6 + tm, :]
    cw = cw_ref[...]
    conv = prev2 * cw[0:1, :] + prev1 * cw[1:2, :] + cu * cw[2:3, :]
    cu_scr[0:8, :] = cu[tm - 8:, :]
    y_conv = jnp.dot((b_g * conv).astype(BF16), wco_ref[...], preferred_element_type=F32)

    gc_out[...] = (jax.nn.sigmoid(gates[:, :D_MODEL]) * y_conv).astype(BF16)
    sg_out[...] = jax.nn.sigmoid(gates[:, D_MODEL:]).astype(BF16)

    cq = cost_ref[...] * Q_SCALE
    sq = sint_ref[...] * Q_SCALE
    cqn = _rms(small[:, :Q_LORA], qn_ref[...]).astype(BF16)
    qt = _dot_nt(wq_ref[...], cqn)
    zero_pad = jnp.zeros((HEAD_PAD - QK_DIM, tm), BF16)
    for h in range(N_HEADS):
        src = h * QK_DIM
        dst = h * HEAD_PAD
        x1 = qt[src + QK_NOPE:src + QK_NOPE + HALF_ROPE, :]
        x2 = qt[src + QK_NOPE + HALF_ROPE:src + QK_DIM, :]
        q_out[dst:dst + QK_NOPE, :] = (qt[src:src + QK_NOPE, :] * Q_SCALE).astype(BF16)
        q_out[dst + QK_NOPE:dst + QK_NOPE + HALF_ROPE, :] = (x1 * cq - x2 * sq).astype(BF16)
        q_out[dst + QK_NOPE + HALF_ROPE:dst + QK_DIM, :] = (x2 * cq + x1 * sq).astype(BF16)
        q_out[dst + QK_DIM:dst + HEAD_PAD, :] = zero_pad

    cos_t = cost_ref[...]
    sin_t = sint_ref[...]
    z_nope = jnp.zeros((QK_NOPE, tm), F32)
    z_rope = jnp.zeros((HALF_ROPE, tm), F32)
    z_pad = jnp.zeros((HEAD_PAD - QK_DIM, tm), F32)
    ck = jnp.concatenate([z_nope, cos_t, cos_t, z_pad], axis=0).T
    sk_x1 = jnp.concatenate([z_nope, -sin_t, z_rope, z_pad], axis=0).T
    sk_x2 = jnp.concatenate([z_nope, z_rope, sin_t, z_pad], axis=0).T
    kr = small[:, Q_LORA + KV_LORA:]
    kpe = (kr * ck + pltpu.roll(kr, HEAD_PAD - HALF_ROPE, axis=1) * sk_x1 + pltpu.roll(kr, HALF_ROPE, axis=1) * sk_x2)
    ckvn = _rms(small[:, Q_LORA:Q_LORA + KV_LORA], kvn_ref[...]).astype(BF16)
    kn = jnp.dot(ckvn, wuk_ref[...], preferred_element_type=F32)
    for h in range(N_HEADS):
        blk = slice(h * HEAD_PAD, (h + 1) * HEAD_PAD)
        k_out[:, blk] = (kn[:, blk] + kpe).astype(BF16)
    vt = _dot_nt(wuv_ref[...], ckvn).astype(BF16)
    for c in range(tm // TK):
        v_out[c] = vt[:, c * TK:(c + 1) * TK]


def _front(layer, x, an, win_all, cw, wco, qn, wq, kvn, wuk, wuv, cos_t, sin_t):
    tm = TM_FRONT
    row = lambda n: pl.BlockSpec((tm, n), lambda i: (i, 0))
    col = lambda n: pl.BlockSpec((n, tm), lambda i: (0, i))
    win_spec = pl.BlockSpec((1, D_MODEL, D_IN_PAD), lambda i: (layer, 0, 0), pipeline_mode=pl.Buffered(1))
    return pl.pallas_call(
        _front_kernel,
        grid=(TOKENS // tm,),
        in_specs=[row(D_MODEL), _const_spec((1, D_MODEL)), win_spec,
                  _const_spec((CONV_WIDTH, D_CONV)), _const_spec((D_CONV, D_MODEL)),
                  _const_spec((1, Q_LORA)), _const_spec((N_HEADS * QK_DIM, Q_LORA)),
                  _const_spec((1, KV_LORA)), _const_spec((KV_LORA, N_HEADS * HEAD_PAD)),
                  _const_spec((N_HEADS * V_HEAD, KV_LORA)),
                  col(HALF_ROPE), col(HALF_ROPE)],
        out_specs=[col(N_HEADS * HEAD_PAD), row(N_HEADS * HEAD_PAD),
                   pl.BlockSpec((tm // TK, N_HEADS * V_HEAD, TK), lambda i: (i, 0, 0)),
                   row(D_MODEL), row(D_MODEL)],
        out_shape=[jax.ShapeDtypeStruct((N_HEADS * HEAD_PAD, TOKENS), BF16),
                   jax.ShapeDtypeStruct((TOKENS, N_HEADS * HEAD_PAD), BF16),
                   jax.ShapeDtypeStruct((TOKENS // TK, N_HEADS * V_HEAD, TK), BF16),
                   jax.ShapeDtypeStruct((TOKENS, D_MODEL), BF16),
                   jax.ShapeDtypeStruct((TOKENS, D_MODEL), BF16)],
        scratch_shapes=[pltpu.VMEM((tm + 8, D_CONV), F32)],
        compiler_params=pltpu.CompilerParams(dimension_semantics=("arbitrary",), vmem_limit_bytes=VMEM_LIMIT),
        name="front",
    )(x, an, win_all, cw, wco, qn, wq, kvn, wuk, wuv, cos_t, sin_t)


def _attn_items():
    items = []
    for i in range(SEQ // TQ):
        ks = 0
        while ks + TK <= i * TQ:
            items.append((i, ks, TK, False))
            ks += TK
        items.append((i, ks, (i + 1) * TQ - ks, True))
    return items


def _attn_kernel(qt_ref, k_ref, vt_ref, o_ref):
    row = lax.broadcasted_iota(jnp.int32, (TQ, TQ), 0)
    col = lax.broadcasted_iota(jnp.int32, (TQ, TQ), 1)
    causal = row <= col

    def scores(item):
        i, ks, n, diag = item
        sts = []
        for h in range(2):
            hs = slice(h * HEAD_PAD, (h + 1) * HEAD_PAD)
            st = jnp.dot(k_ref[ks:ks + n, hs], qt_ref[hs, i * TQ:(i + 1) * TQ], preferred_element_type=F32)
            if diag:
                tail = jnp.where(causal, st[n - TQ:, :], NEG_BIG)
                st = tail if n == TQ else jnp.concatenate([st[:n - TQ, :], tail], axis=0)
            sts.append(st)
        return sts

    def consume(item, sts, carry):
        i, ks, n, _ = item
        stats = []
        for h in range(2):
            m, l, _ = carry[h]
            m_new = jnp.maximum(m, jnp.max(sts[h], axis=0, keepdims=True))
            alpha = jnp.exp2(m - m_new)
            p = jnp.exp2(sts[h] - m_new)
            l = alpha * l + jnp.sum(p, axis=0, keepdims=True)
            stats.append((m_new, l, alpha, p.astype(BF16)))
        new = []
        for h in range(2):
            m_new, l, alpha, p = stats[h]
            vt = vt_ref[ks // TK, h * V_HEAD:(h + 1) * V_HEAD, ks % TK:ks % TK + n]
            acc = alpha * carry[h][2] + jnp.dot(vt, p, preferred_element_type=F32)
            new.append((m_new, l, acc))
        return new

    items = _attn_items()
    init = [(jnp.full((1, TQ), NEG_BIG, F32), jnp.zeros((1, TQ), F32), jnp.zeros((V_HEAD, TQ), F32))
            for _ in range(2)]
    cur = scores(items[0])
    carry = init
    for t, item in enumerate(items):
        nxt = scores(items[t + 1]) if t + 1 < len(items) else None
        carry = consume(item, cur, carry)
        if item[3]:
            i = item[0]
            out_t = jnp.concatenate([acc / l for (_, l, acc) in carry], axis=0)
            o_ref[i * TQ:(i + 1) * TQ, :] = out_t.T.astype(BF16)
            carry = init
        cur = nxt


def _attention(qt, k, vt):
    return pl.pallas_call(
        _attn_kernel,
        grid=(BATCH, N_HEADS // 2),
        in_specs=[pl.BlockSpec((2 * HEAD_PAD, SEQ), lambda b, hp: (hp, b)),
                  pl.BlockSpec((SEQ, 2 * HEAD_PAD), lambda b, hp: (b, hp)),
                  pl.BlockSpec((SEQ // TK, 2 * V_HEAD, TK), lambda b, hp: (b, hp, 0))],
        out_specs=pl.BlockSpec((SEQ, 2 * V_HEAD), lambda b, hp: (b, hp)),
        out_shape=jax.ShapeDtypeStruct((TOKENS, N_HEADS * V_HEAD), BF16),
        compiler_params=pltpu.CompilerParams(dimension_semantics=("arbitrary",) * 2, vmem_limit_bytes=VMEM_LIMIT),
        name="attention",
    )(qt, k, vt)


def _merge_core(attn_ref, gc_ref, sg_ref, x_ref, wmo_ref, wo_ref, fn_ref):
    y_mla = jnp.dot(attn_ref[...], wmo_ref[...], preferred_element_type=F32)
    merged = gc_ref[...].astype(F32) + sg_ref[...].astype(F32) * y_mla
    x2 = x_ref[...] + jnp.dot(merged.astype(BF16), wo_ref[...], preferred_element_type=F32)
    return x2, _rms(x2, fn_ref[...])


def _merge_ffn_kernel(attn_ref, gc_ref, sg_ref, x_ref, wmo_ref, wo_ref, fn_ref, wg_ref, wu_ref, wd_ref, o_ref):
    x2, hn = _merge_core(attn_ref, gc_ref, sg_ref, x_ref, wmo_ref, wo_ref, fn_ref)
    hn = hn.astype(BF16)
    acc = x2
    for c in range(D_FF // FF_CHUNK):
        cs = slice(c * FF_CHUNK, (c + 1) * FF_CHUNK)
        g = jnp.dot(hn, wg_ref[:, cs], preferred_element_type=F32)
        u = jnp.dot(hn, wu_ref[:, cs], preferred_element_type=F32)
        a = (jax.nn.silu(g) * u).astype(BF16)
        acc = acc + jnp.dot(a, wd_ref[cs, :], preferred_element_type=F32)
    o_ref[...] = acc


def _merge_route_kernel(attn_ref, gc_ref, sg_ref, x_ref, wmo_ref, wo_ref, fn_ref, router_ref,
                        x2_out, hn_out, route_out, route_t_out):
    x2, hn = _merge_core(attn_ref, gc_ref, sg_ref, x_ref, wmo_ref, wo_ref, fn_ref)
    x2_out[...] = x2
    hb = hn.astype(BF16)
    hn_out[...] = hb
    hl = (hn - hb.astype(F32)).astype(BF16)
    r = router_ref[...]
    rh = r.astype(BF16)
    rl = (r - rh.astype(F32)).astype(BF16)
    logits = _dot_nt(rh, hb) + _dot_nt(rh, hl) + _dot_nt(rl, hb)
    erow = lax.broadcasted_iota(jnp.int32, logits.shape, 0)
    logits = jnp.where(erow < N_EXPERTS, logits, -jnp.inf)
    m1 = jnp.max(logits, axis=0, keepdims=True)
    i1 = jnp.min(jnp.where(logits == m1, erow, E_PAD), axis=0, keepdims=True)
    rest = jnp.where(erow == i1, -jnp.inf, logits)
    m2 = jnp.max(rest, axis=0, keepdims=True)
    i2 = jnp.min(jnp.where(rest == m2, erow, E_PAD), axis=0, keepdims=True)
    t = jnp.exp(m2 - m1)
    w1 = 1.0 / (1.0 + t)
    w2 = t * w1
    route_t = jnp.where(erow == 0, i1.astype(F32),
                        jnp.where(erow == 1, i2.astype(F32), jnp.where(erow == 2, w1, jnp.where(erow == 3, w2, 0.0))))
    route_t_out[...] = route_t[0:SUBLANES, :]
    pad = jnp.zeros((LANES - E_PAD, route_t.shape[1]), F32)
    route_out[...] = jnp.concatenate([route_t, pad], axis=0).T


def _merge(attn, gc, sg, x, wmo, wo, fn, router=None, ffn=None):
    tm = TM_FRONT
    row = lambda n: pl.BlockSpec((tm, n), lambda i: (i, 0))
    in_specs = [row(N_HEADS * V_HEAD), row(D_MODEL), row(D_MODEL), row(D_MODEL),
                _const_spec((N_HEADS * V_HEAD, D_MODEL)), _const_spec((D_MODEL, D_MODEL)), _const_spec((1, D_MODEL))]
    args = [attn, gc, sg, x, wmo, wo, fn]
    if router is None:
        kern = _merge_ffn_kernel
        in_specs += [_const_spec((D_MODEL, D_FF)), _const_spec((D_MODEL, D_FF)), _const_spec((D_FF, D_MODEL))]
        args += list(ffn)
        out_specs = row(D_MODEL)
        out_shape = jax.ShapeDtypeStruct((TOKENS, D_MODEL), F32)
        name = "merge_ffn"
    else:
        kern = _merge_route_kernel
        in_specs.append(_const_spec((E_PAD, D_MODEL)))
        args.append(router)
        out_specs = [row(D_MODEL), row(D_MODEL), row(LANES), pl.BlockSpec((SUBLANES, tm), lambda i: (0, i))]
        out_shape = [jax.ShapeDtypeStruct((TOKENS, D_MODEL), F32), jax.ShapeDtypeStruct((TOKENS, D_MODEL), BF16),
                     jax.ShapeDtypeStruct((TOKENS, LANES), F32), jax.ShapeDtypeStruct((SUBLANES, TOKENS), F32)]
        name = "merge_route"
    return pl.pallas_call(
        kern, grid=(TOKENS // tm,), in_specs=in_specs, out_specs=out_specs, out_shape=out_shape,
        compiler_params=pltpu.CompilerParams(dimension_semantics=("arbitrary",), vmem_limit_bytes=VMEM_LIMIT),
        name=name,
    )(*args)


def _pieces(n, sizes):
    return [((n & z) != 0, n & ~(2 * z - 1), z) for z in sizes]


def _aligned(rows, z):
    return pl.ds(rows if isinstance(rows, int) else pl.multiple_of(rows, SUBLANES), z)


def _start_slab_copies(tile, base_ref, offp_ref, np_ref, make_copy):
    for e in range(N_EXPERTS):
        n = np_ref[tile * N_EXPERTS + e]
        src = offp_ref[tile * N_EXPERTS + e]
        dst = base_ref[tile * N_EXPERTS + e]
        for pred, off, z in _pieces(n, SLAB_PIECES):
            @pl.when(pred)
            def _(off=off, z=z):
                make_copy(src + off, dst + off, z).start()


def _wait_rows(total, make_copy):
    for pred, _, z in _pieces(total, WAIT_PIECES):
        @pl.when(pred)
        def _(z=z):
            make_copy(0, 0, z).wait()


def _dispatch_kernel(base_ref, offp_ref, np_ref, tot_ref, tail_ref, hn_ref, rt_ref, xs_hbm, xs_scr, zbuf, sem, zsem):
    i = pl.program_id(0)
    slot = i % 2

    def copy(buf_slot):
        return lambda s, d, z: pltpu.make_async_copy(xs_scr.at[buf_slot, _aligned(s, z), :],
                                                     xs_hbm.at[_aligned(d, z), :], sem.at[buf_slot])

    def zero_copy(d, z):
        return pltpu.make_async_copy(zbuf.at[pl.ds(0, z), :], xs_hbm.at[_aligned(d, z), :], zsem)

    @pl.when(i == 0)
    def _():
        zbuf[...] = jnp.zeros_like(zbuf)
        for phase in ("start", "wait"):
            for e in range(N_EXPERTS):
                t0 = tail_ref[e]
                n = tail_ref[N_EXPERTS + e]
                full = lax.shift_right_logical(n, jnp.int32(TM_MOE.bit_length() - 1))

                def chunk(c, carry, t0=t0, phase=phase):
                    cp = zero_copy(t0 + c * TM_MOE, TM_MOE)
                    cp.start() if phase == "start" else cp.wait()
                    return carry

                lax.fori_loop(0, full, chunk, 0)
                rest = n & (TM_MOE - 1)
                for pred, off, z in _pieces(rest, SLAB_PIECES[1:]):
                    @pl.when(pred)
                    def _(off=off, z=z, t0=t0, full=full, phase=phase):
                        cp = zero_copy(t0 + full * TM_MOE + off, z)
                        cp.start() if phase == "start" else cp.wait()

    e0 = rt_ref[0:1, :].astype(jnp.int32)
    e1 = rt_ref[1:2, :].astype(jnp.int32)
    eid = lax.broadcasted_iota(jnp.int32, (E_PAD, TT), 0)
    sel = jnp.where(e0 == eid, 1.0, jnp.where(e1 == eid, 1.0, 0.0)).astype(BF16)
    r = lax.broadcasted_iota(jnp.int32, (TT, TT), 0)
    c = lax.broadcasted_iota(jnp.int32, (TT, TT), 1)
    earlier = jnp.where(r < c, 1.0, 0.0).astype(BF16)
    rank = jnp.dot(sel, earlier, preferred_element_type=F32)
    ecol = lax.broadcasted_iota(jnp.int32, (E_PAD, 1), 0)
    offp = jnp.zeros((E_PAD, 1), F32)
    for e in range(N_EXPERTS):
        offp = jnp.where(ecol == e, offp_ref[i * N_EXPERTS + e].astype(F32), offp)
    slot_all = rank + offp
    slot0 = jnp.sum(jnp.where(e0 == eid, slot_all, 0.0), axis=0, keepdims=True)
    slot1 = jnp.sum(jnp.where(e1 == eid, slot_all, 0.0), axis=0, keepdims=True)
    sidx = lax.broadcasted_iota(jnp.int32, (SLOTS, TT), 0).astype(F32)
    place = jnp.where(sidx == slot0, 1.0, jnp.where(sidx == slot1, 1.0, 0.0)).astype(BF16)
    xs_scr[slot, :, 0:D_MODEL] = jnp.dot(place, hn_ref[...], preferred_element_type=F32)
    w_slot = jnp.sum(jnp.where(sidx == slot0, rt_ref[2:3, :], jnp.where(sidx == slot1, rt_ref[3:4, :], 0.0)),
                     axis=1, keepdims=True)
    xs_scr[slot, :, D_MODEL:] = jnp.broadcast_to(w_slot, (SLOTS, LANES))

    _start_slab_copies(i, base_ref, offp_ref, np_ref, copy(slot))

    @pl.when(i > 0)
    def _():
        _wait_rows(tot_ref[i - 1], copy(1 - slot))

    @pl.when(i == pl.num_programs(0) - 1)
    def _():
        _wait_rows(tot_ref[i], copy(slot))


def _dispatch(tables, hn, route_t):
    grid_spec = pltpu.PrefetchScalarGridSpec(
        num_scalar_prefetch=5,
        grid=(N_TT,),
        in_specs=[pl.BlockSpec((TT, D_MODEL), lambda i, *_: (i, 0)),
                  pl.BlockSpec((SUBLANES, TT), lambda i, *_: (0, i))],
        out_specs=pl.BlockSpec(memory_space=pl.ANY),
        scratch_shapes=[pltpu.VMEM((2, SLOTS, XS_WIDTH), F32), pltpu.VMEM((TM_MOE, XS_WIDTH), F32),
                        pltpu.SemaphoreType.DMA((2,)), pltpu.SemaphoreType.DMA],
    )
    return pl.pallas_call(
        _dispatch_kernel,
        grid_spec=grid_spec,
        out_shape=jax.ShapeDtypeStruct((P_PAD, XS_WIDTH), F32),
        compiler_params=pltpu.CompilerParams(dimension_semantics=("arbitrary",), vmem_limit_bytes=VMEM_LIMIT),
        name="moe_dispatch",
    )(tables["base"], tables["offp"], tables["np"], tables["total"], tables["tail"], hn, route_t)


def _moe_kernel(te_ref, nv_ref, x_ref, wg_ref, wu_ref, wd_ref, y_ref, wgu_scr, wd_scr):
    i = pl.program_id(0)

    @pl.when(jnp.logical_or(i == 0, te_ref[i] != te_ref[jnp.maximum(i - 1, 0)]))
    def _():
        wgu_scr[:, 0:D_FF_EXPERT] = wg_ref[0].astype(BF16)
        wgu_scr[:, D_FF_EXPERT:] = wu_ref[0].astype(BF16)
        wd_scr[...] = wd_ref[0].astype(BF16)

    @pl.when(i < nv_ref[0])
    def _():
        x = x_ref[:, 0:D_MODEL].astype(BF16)
        h = jnp.dot(x, wgu_scr[...], preferred_element_type=F32)
        a = (jax.nn.silu(h[:, :D_FF_EXPERT]) * h[:, D_FF_EXPERT:]).astype(BF16)
        y_ref[...] = jnp.dot(a, wd_scr[...], preferred_element_type=F32) * x_ref[:, D_MODEL:D_MODEL + 1]

    @pl.when(i >= nv_ref[0])
    def _():
        y_ref[...] = jnp.zeros_like(y_ref)


def _moe(tile_expert, n_valid, xs, wg, wu, wd):
    up_spec = pl.BlockSpec((1, D_MODEL, D_FF_EXPERT), lambda i, te, nv: (te[i], 0, 0))
    grid_spec = pltpu.PrefetchScalarGridSpec(
        num_scalar_prefetch=2,
        grid=(N_MOE_TILES,),
        in_specs=[pl.BlockSpec((TM_MOE, XS_WIDTH), lambda i, te, nv: (i, 0)), up_spec, up_spec,
                  pl.BlockSpec((1, D_FF_EXPERT, D_MODEL), lambda i, te, nv: (te[i], 0, 0))],
        out_specs=pl.BlockSpec((TM_MOE, D_MODEL), lambda i, te, nv: (i, 0)),
        scratch_shapes=[pltpu.VMEM((D_MODEL, 2 * D_FF_EXPERT), BF16), pltpu.VMEM((D_FF_EXPERT, D_MODEL), BF16)],
    )
    return pl.pallas_call(
        _moe_kernel,
        grid_spec=grid_spec,
        out_shape=jax.ShapeDtypeStruct((P_PAD, D_MODEL), F32),
        compiler_params=pltpu.CompilerParams(dimension_semantics=("arbitrary",), vmem_limit_bytes=VMEM_LIMIT),
        name="moe_experts",
    )(tile_expert, n_valid, xs, wg, wu, wd)


def _combine_kernel(base_ref, offp_ref, np_ref, tot_ref, x_ref, route_ref, y_hbm, fn_ref, o_ref, ybuf, sem):
    i = pl.program_id(0)
    slot = i % 2

    def copy(buf_slot):
        return lambda s, d, z: pltpu.make_async_copy(y_hbm.at[_aligned(d, z), :],
                                                     ybuf.at[buf_slot, _aligned(s, z), :], sem.at[buf_slot])

    @pl.when(i == 0)
    def _():
        ybuf[...] = jnp.zeros_like(ybuf)
        _start_slab_copies(0, base_ref, offp_ref, np_ref, copy(0))

    @pl.when(i + 1 < pl.num_programs(0))
    def _():
        _start_slab_copies(i + 1, base_ref, offp_ref, np_ref, copy(1 - slot))

    route = route_ref[...]
    e0 = route[:, 0:1].astype(jnp.int32)
    e1 = route[:, 1:2].astype(jnp.int32)
    eid =lax.broadcasted_iota(jnp.int32, (TT, LANES), 1)
    sel = jnp.where(e0 == eid, 1.0, jnp.where(e1 == eid, 1.0, 0.0)).astype(BF16)
    r = lax.broadcasted_iota(jnp.int32, (TT, TT), 0)
    c = lax.broadcasted_iota(jnp.int32, (TT, TT), 1)
    earlier = jnp.where(c < r, 1.0, 0.0).astype(BF16)
    rank = jnp.dot(earlier, sel, preferred_element_type=F32)
    erow = lax.broadcasted_iota(jnp.int32, (1, LANES), 1)
    offp = jnp.zeros((1, LANES), F32)
    for e in range(N_EXPERTS):
        offp = jnp.where(erow == e, offp_ref[i * N_EXPERTS + e].astype(F32), offp)
    slot_all = rank + offp
    slot0 = jnp.sum(jnp.where(e0 == eid, slot_all, 0.0), axis=1, keepdims=True)
    slot1 = jnp.sum(jnp.where(e1 == eid, slot_all, 0.0), axis=1, keepdims=True)
    sidx = lax.broadcasted_iota(jnp.int32, (TT, SLOTS), 1).astype(F32)
    pick = jnp.where(sidx == slot0, 1.0, jnp.where(sidx == slot1, 1.0, 0.0)).astype(BF16)

    _wait_rows(tot_ref[i], copy(slot))
    y = ybuf[slot]
    yh = y.astype(BF16)
    yl = (y - yh.astype(F32)).astype(BF16)
    moe = jnp.dot(pick, yh, preferred_element_type=F32) + jnp.dot(pick, yl, preferred_element_type=F32)
    o_ref[...] = _rms(x_ref[...] + moe, fn_ref[...])


def _combine(tables, x2, route, y_sorted, fn):
    grid_spec = pltpu.PrefetchScalarGridSpec(
        num_scalar_prefetch=4,
        grid=(N_TT,),
        in_specs=[pl.BlockSpec((TT, D_MODEL), lambda i, *_: (i, 0)),
                  pl.BlockSpec((TT, LANES), lambda i, *_: (i, 0)),
                  pl.BlockSpec(memory_space=pl.ANY),
                  pl.BlockSpec((1, D_MODEL), lambda i, *_: (0, 0))],
        out_specs=pl.BlockSpec((TT, D_MODEL), lambda i, *_: (i, 0)),
        scratch_shapes=[pltpu.VMEM((2, SLOTS, D_MODEL), F32), pltpu.SemaphoreType.DMA((2,))],
    )
    return pl.pallas_call(
        _combine_kernel,
        grid_spec=grid_spec,
        out_shape=jax.ShapeDtypeStruct((TOKENS, D_MODEL), F32),
        compiler_params=pltpu.CompilerParams(dimension_semantics=("arbitrary",), vmem_limit_bytes=VMEM_LIMIT),
        name="moe_combine",
    )(tables["base"], tables["offp"], tables["np"], tables["total"], x2, route, y_sorted, fn)


def _routing_tables(route_t):
    experts = route_t[:TOP_K, :].astype(jnp.int32)
    eid = jnp.arange(N_EXPERTS, dtype=jnp.int32)[:, None, None]
    chosen = jnp.sum((experts[None, :, :] == eid).astype(jnp.int32), axis=1)
    cnt = jnp.sum(chosen.reshape(N_EXPERTS, N_TT, TT), axis=2).T
    npad = (cnt + SUBLANES - 1) // SUBLANES * SUBLANES
    offp = jnp.cumsum(npad, axis=1) - npad
    total = jnp.sum(npad, axis=1)
    group = jnp.sum(npad, axis=0)
    group_pad = (group + TM_MOE - 1) // TM_MOE * TM_MOE
    group_end = jnp.cumsum(group_pad)
    group_start = group_end - group_pad
    base = group_start[None, :] + jnp.cumsum(npad, axis=0) - npad
    tail_start = group_start + group
    is_last = jnp.arange(N_EXPERTS) == N_EXPERTS - 1
    tail_len = jnp.where(is_last, P_PAD - tail_start, group_pad - group)
    tile_start = jnp.arange(N_MOE_TILES, dtype=jnp.int32) * TM_MOE
    tile_expert = jnp.minimum(jnp.sum((tile_start[:, None] >= group_end[None, :]).astype(jnp.int32), axis=1),
                              N_EXPERTS - 1)
    n_valid = (group_end[-1] // TM_MOE).reshape(1)
    tables = {"base": base.reshape(-1), "offp": offp.reshape(-1), "np": npad.reshape(-1), "total": total,
              "tail": jnp.concatenate([tail_start, tail_len])}
    return {k: v.astype(jnp.int32) for k, v in tables.items()}, tile_expert.astype(jnp.int32), n_valid.astype(jnp.int32)


WIN_ROWS = 256


def _win_layout_kernel(w_ref, o_ref):
    w = w_ref[0]
    rows = w.shape[0]
    o_ref[0, :, 0:OFF_GATE] = w[:, 0:REF_CQ].astype(BF16)
    o_ref[0, :, OFF_GATE:OFF_SMALL] = w[:, REF_GATE:].astype(BF16)
    o_ref[0, :, OFF_SMALL:OFF_SMALL + Q_LORA + KV_LORA] = w[:, REF_CQ:REF_KR].astype(BF16)
    kr_tile = jnp.concatenate([jnp.zeros((rows, QK_NOPE), F32), w[:, REF_KR:REF_GATE],
                               jnp.zeros((rows, HEAD_PAD - QK_DIM), F32)], axis=1)
    o_ref[0, :, OFF_SMALL + Q_LORA + KV_LORA:] = kr_tile.astype(BF16)


def _win_layout(w_in):
    d_in = w_in.shape[-1]
    return pl.pallas_call(
        _win_layout_kernel,
        grid=(DEPTH, D_MODEL // WIN_ROWS),
        in_specs=[pl.BlockSpec((1, WIN_ROWS, d_in), lambda l, r: (l, r, 0))],
        out_specs=pl.BlockSpec((1, WIN_ROWS, D_IN_PAD), lambda l, r: (l, r, 0)),
        out_shape=jax.ShapeDtypeStruct((DEPTH, D_MODEL, D_IN_PAD), BF16),
        compiler_params=pltpu.CompilerParams(dimension_semantics=("arbitrary",) * 2, vmem_limit_bytes=VMEM_LIMIT),
        name="win_layout",
    )(w_in)


def _layer_weights(w_uq, w_ukv):
    wq = w_uq.T.astype(BF16)

    wkv = w_ukv.reshape(KV_LORA, N_HEADS, QK_NOPE + V_HEAD)
    wuk = jnp.concatenate([wkv[:, :, :QK_NOPE], jnp.zeros((KV_LORA, N_HEADS, HEAD_PAD - QK_NOPE), F32)], axis=2)
    wuk = wuk.reshape(KV_LORA, N_HEADS * HEAD_PAD).astype(BF16)
    wuv = wkv[:, :, QK_NOPE:].reshape(KV_LORA, N_HEADS * V_HEAD).T.astype(BF16)
    return wq, wuk, wuv


def kernel(x, positions, attn_norm, w_in, conv_w, w_conv_out, q_norm, w_uq, kv_norm, w_ukv, w_mla_out, w_o, ffn_norm,
           w_gate, w_up, w_down, router, w_gate_e, w_up_e, w_down_e, final_norm):
    assert x.shape == (BATCH, SEQ, D_MODEL) and positions.shape == (BATCH, SEQ)
    cos_t, sin_t = _rope_tables(positions)
    xt = x.reshape(TOKENS, D_MODEL)
    out = None
    win_all = _win_layout(w_in)
    for l in range(DEPTH):
        wq, wuk, wuv = _layer_weights(w_uq[l], w_ukv[l])
        q, k, v, gc, sg = _front(l, xt, attn_norm[l].reshape(1, D_MODEL), win_all, conv_w[l],
                                 w_conv_out[l].astype(BF16),
                                 q_norm[l].reshape(1, Q_LORA), wq, kv_norm[l].reshape(1, KV_LORA), wuk, wuv,
                                 cos_t, sin_t)
        attn = _attention(q, k, v)
        wmo = w_mla_out[l].astype(BF16)
        wo = w_o[l].astype(BF16)
        fn = ffn_norm[l].reshape(1, D_MODEL)
        if l % 2 == 0:
            j = l // 2
            xt = _merge(attn, gc, sg, xt, wmo, wo, fn,
                        ffn=(w_gate[j].astype(BF16), w_up[j].astype(BF16), w_down[j].astype(BF16)))
        else:
            j = l // 2
            router_pad = jnp.pad(router[j].T, ((0, E_PAD - N_EXPERTS), (0, 0)))
            x2, hn, route, route_t = _merge(attn, gc, sg, xt, wmo, wo, fn, router_pad)
            tables, tile_expert, n_valid = _routing_tables(route_t)
            xs = _dispatch(tables, hn, route_t)
            y_sorted = _moe(tile_expert, n_valid, xs, w_gate_e[j], w_up_e[j], w_down_e[j])
            out = _combine(tables, x2, route, y_sorted, final_norm.reshape(1, D_MODEL))
    return out.reshape(BATCH, SEQ, D_MODEL)
```

```python
import math

import jax
import jax.numpy as jnp
from jax import lax
from jax.experimental import pallas as pl
from jax.experimental.pallas import tpu as pltpu

F32 = jnp.float32
BF16 = jnp.bfloat16

D_MODEL = 1024
BATCH = 8
SEQ = 2048
TOKENS = BATCH * SEQ
DEPTH = 2
D_CONV = 512
CONV_WIDTH = 3
N_HEADS = 8
QK_NOPE = 64
QK_ROPE = 32
HALF_ROPE = QK_ROPE // 2
V_HEAD = 64
Q_LORA = 384
KV_LORA = 256
ROPE_THETA = 10000.0
D_FF = 2816
N_EXPERTS = 8
TOP_K = 2
D_FF_EXPERT = 1408
EPS = 1e-6

LANES = 128
HEAD_PAD = LANES
QK_DIM = QK_NOPE + QK_ROPE
Q_SCALE = (1.0 / math.sqrt(QK_DIM)) * math.log2(math.e)
NEG_BIG = -1e30

OFF_BCU = 0
OFF_GATE = 3 * D_CONV
OFF_SMALL = OFF_GATE + 2 * D_MODEL
D_IN_PAD = OFF_SMALL + Q_LORA + KV_LORA + LANES
REF_CQ = 3 * D_CONV
REF_KR = REF_CQ + Q_LORA + KV_LORA
REF_GATE = REF_KR + QK_ROPE

TM_FRONT = 512
TQ = 256
TK = 512
ATTN_LOOKAHEAD = 3
V_ROWS = V_HEAD + 16
FF_CHUNK = 256
VMEM_LIMIT = 56 * 1024 * 1024

SUBLANES = 8
TM_MOE = 256
TT = 256
N_TT = TOKENS // TT
E_PAD = 16
XS_WIDTH = D_MODEL + LANES
N_PAIRS = TOKENS * TOP_K
SLOTS =TT * TOP_K + N_EXPERTS * SUBLANES
_MAX_ROWS = N_PAIRS + N_TT * N_EXPERTS * (SUBLANES - 1) + N_EXPERTS * (TM_MOE - SUBLANES)
N_MOE_TILES = -(-_MAX_ROWS // TM_MOE)
P_PAD = N_MOE_TILES * TM_MOE
SLAB_PIECES = tuple(TT >> s for s in range(6))
WAIT_PIECES = (512,) + SLAB_PIECES
assert SLAB_PIECES[-1] == SUBLANES and SLOTS < 2 * WAIT_PIECES[0]


def _rms(x, g):
    return x * lax.rsqrt(jnp.mean(x * x, axis=-1, keepdims=True) + EPS) * g


def _const_spec(shape):
    nd = len(shape)
    return pl.BlockSpec(shape, lambda *_: (0,) * nd, pipeline_mode=pl.Buffered(1))


def _trig_kernel(pos_ref, invf_ref, cos_ref, sin_ref):
    ang = invf_ref[...] * pos_ref[...].astype(F32)
    cos_ref[...] = jnp.cos(ang)
    sin_ref[...] = jnp.sin(ang)


def _rope_tables(positions):
    inv_freq = ROPE_THETA ** (-jnp.arange(0, QK_ROPE, 2, dtype=F32) / QK_ROPE)
    return pl.pallas_call(
        _trig_kernel,
        out_shape=(jax.ShapeDtypeStruct((HALF_ROPE, TOKENS), F32),) * 2,
        name="rope_trig",
    )(positions.reshape(1, TOKENS), inv_freq.reshape(HALF_ROPE, 1))


def _dot_nt(a, b):
    return lax.dot_general(a, b, (((1,), (1,)), ((), ())), preferred_element_type=F32)


def _front_kernel(x_ref, an_ref, win_ref, cw_ref, wco_ref, qn_ref, wq_ref, kvn_ref, wuk_ref, wuv_ref,
                  cost_ref, sint_ref, q_out, k_out, v_out, gc_out, sg_out, cu_scr):
    tm = x_ref.shape[0]
    i = pl.program_id(0)
    xn = _rms(x_ref[...], an_ref[...]).astype(BF16)

    def proj(a, b):
        return jnp.dot(xn, win_ref[0, :, a:b], preferred_element_type=F32)

    small = proj(OFF_SMALL, D_IN_PAD)

    cq = cost_ref[...] * Q_SCALE
    sq = sint_ref[...] * Q_SCALE
    cqn = _rms(small[:, :Q_LORA], qn_ref[...]).astype(BF16)
    qt = _dot_nt(wq_ref[...], cqn)
    zero_pad = jnp.zeros((HEAD_PAD - QK_DIM, tm), BF16)
    for h in range(N_HEADS):
        src = h * QK_DIM
        dst = h * HEAD_PAD
        x1 = qt[src + QK_NOPE:src + QK_NOPE + HALF_ROPE, :]
        x2 = qt[src + QK_NOPE + HALF_ROPE:src + QK_DIM, :]
        q_out[dst:dst + QK_NOPE, :] = (qt[src:src + QK_NOPE, :] * Q_SCALE).astype(BF16)
        q_out[dst + QK_NOPE:dst + QK_NOPE + HALF_ROPE, :] = (x1 * cq - x2 * sq).astype(BF16)
        q_out[dst + QK_NOPE + HALF_ROPE:dst + QK_DIM, :] = (x2 * cq + x1 * sq).astype(BF16)
        q_out[dst + QK_DIM:dst + HEAD_PAD, :] = zero_pad

    cos_t = cost_ref[...]
    sin_t = sint_ref[...]
    z_nope = jnp.zeros((QK_NOPE, tm), F32)
    z_rope = jnp.zeros((HALF_ROPE, tm), F32)
    z_pad = jnp.zeros((HEAD_PAD - QK_DIM, tm), F32)
    ck = jnp.concatenate([z_nope, cos_t, cos_t, z_pad], axis=0).T
    sk_x1 = jnp.concatenate([z_nope, -sin_t, z_rope, z_pad], axis=0).T
    sk_x2 = jnp.concatenate([z_nope, z_rope, sin_t, z_pad], axis=0).T
    kr = small[:, Q_LORA + KV_LORA:]
    kpe = (kr * ck + pltpu.roll(kr, HEAD_PAD - HALF_ROPE, axis=1) * sk_x1 + pltpu.roll(kr, HALF_ROPE, axis=1) * sk_x2)
    ckvn = _rms(small[:, Q_LORA:Q_LORA + KV_LORA], kvn_ref[...]).astype(BF16)
    kn = jnp.dot(ckvn, wuk_ref[...], preferred_element_type=F32)
    for h in range(N_HEADS):
        blk = slice(h * HEAD_PAD, (h + 1) * HEAD_PAD)
        k_out[:, blk] = (kn[:, blk] + kpe).astype(BF16)
    vt = _dot_nt(wuv_ref[...], ckvn).astype(BF16)
    sub = lax.broadcasted_iota(jnp.int32, (V_ROWS - V_HEAD, TK), 0)
    ones_rows = jnp.where(sub == 0, 1.0, 0.0).astype(BF16)
    for c in range(tm // TK):
        for h in range(N_HEADS):
            v_out[c, h * V_ROWS:h * V_ROWS + V_HEAD, :] = vt[h * V_HEAD:(h + 1) * V_HEAD, c * TK:(c + 1) * TK]
            v_out[c, h * V_ROWS + V_HEAD:(h + 1) * V_ROWS, :] = ones_rows

    bcu = proj(OFF_BCU, OFF_GATE)
    b_g = bcu[:, :D_CONV]
    cu = bcu[:, D_CONV:2 * D_CONV] * bcu[:, 2 * D_CONV:]

    @pl.when(i % (SEQ // tm) == 0)
    def _():
        cu_scr[0:8, :] = jnp.zeros((8, D_CONV), F32)

    cu_scr[8:8 + tm, :] = cu
    prev1 = cu_scr[7:7 + tm, :]
    prev2 = cu_scr[6:6 + tm, :]
    cw = cw_ref[...]
    conv = prev2 * cw[0:1, :] + prev1 * cw[1:2, :] + cu * cw[2:3, :]
    cu_scr[0:8, :] = cu[tm - 8:, :]
    y_conv = jnp.dot((b_g * conv).astype(BF16), wco_ref[...], preferred_element_type=F32)

    gates = proj(OFF_GATE, OFF_SMALL)
    gc_out[...] = (jax.nn.sigmoid(gates[:, :D_MODEL]) * y_conv).astype(BF16)
    sg_out[...] = jax.nn.sigmoid(gates[:, D_MODEL:]).astype(BF16)


def _front(layer, x, an, win_all, cw, wco, qn, wq, kvn, wuk, wuv, cos_t, sin_t):
    tm = TM_FRONT
    row = lambda n: pl.BlockSpec((tm, n), lambda i: (i, 0))
    col = lambda n: pl.BlockSpec((n, tm), lambda i: (0, i))
    win_spec = pl.BlockSpec((1, D_MODEL, D_IN_PAD), lambda i: (layer, 0, 0), pipeline_mode=pl.Buffered(1))
    return pl.pallas_call(
        _front_kernel,
        grid=(TOKENS // tm,),
        in_specs=[row(D_MODEL), _const_spec((1, D_MODEL)), win_spec,
                  _const_spec((CONV_WIDTH, D_CONV)), _const_spec((D_CONV, D_MODEL)),
                  _const_spec((1, Q_LORA)), _const_spec((N_HEADS * QK_DIM, Q_LORA)),
                  _const_spec((1, KV_LORA)), _const_spec((KV_LORA, N_HEADS * HEAD_PAD)),
                  _const_spec((N_HEADS * V_HEAD, KV_LORA)),
                  col(HALF_ROPE), col(HALF_ROPE)],
        out_specs=[col(N_HEADS * HEAD_PAD), row(N_HEADS * HEAD_PAD),
                   pl.BlockSpec((tm // TK, N_HEADS * V_ROWS, TK), lambda i: (i, 0, 0)),
                   row(D_MODEL), row(D_MODEL)],
        out_shape=[jax.ShapeDtypeStruct((N_HEADS * HEAD_PAD, TOKENS), BF16),
                   jax.ShapeDtypeStruct((TOKENS, N_HEADS * HEAD_PAD), BF16),
                   jax.ShapeDtypeStruct((TOKENS // TK, N_HEADS * V_ROWS, TK), BF16),
                   jax.ShapeDtypeStruct((TOKENS, D_MODEL), BF16),
                   jax.ShapeDtypeStruct((TOKENS, D_MODEL), BF16)],
        scratch_shapes=[pltpu.VMEM((tm + 8, D_CONV), F32)],
        compiler_params=pltpu.CompilerParams(dimension_semantics=("arbitrary",), vmem_limit_bytes=VMEM_LIMIT),
        name="front",
    )(x, an, win_all, cw, wco, qn, wq, kvn, wuk, wuv, cos_t, sin_t)


def _attn_items():
    items = []
    for i in range(SEQ // TQ):
        ks = 0
        while ks + TK <= i * TQ:
            items.append((i, ks, TK, False))
            ks += TK
        items.append((i, ks, (i + 1) * TQ - ks, True))
    return items


def _attn_kernel(qt_ref, k_ref, vt_ref, o_ref):
    row = lax.broadcasted_iota(jnp.int32, (TQ, TQ), 0)
    col = lax.broadcasted_iota(jnp.int32, (TQ, TQ), 1)
    causal = row <= col

    def scores(item):
        i, ks, n, diag = item
        sts = []
        for h in range(2):
            hs = slice(h * HEAD_PAD, (h + 1) * HEAD_PAD)
            st = jnp.dot(k_ref[ks:ks + n, hs], qt_ref[hs, i * TQ:(i + 1) * TQ], preferred_element_type=F32)
            if diag:
                tail = jnp.where(causal, st[n - TQ:, :], NEG_BIG)
                st = tail if n == TQ else jnp.concatenate([st[:n - TQ, :], tail], axis=0)
            sts.append(st)
        return sts

    def consume(item, sts, carry):
        i, ks, n, _ = item
        stats = []
        for h in range(2):
            m, _ = carry[h]
            m_new = jnp.maximum(m, jnp.max(sts[h], axis=0, keepdims=True))
            stats.append((m_new, jnp.exp2(m - m_new), jnp.exp2(sts[h] - m_new).astype(BF16)))
        new = []
        for h in range(2):
            m_new, alpha, p = stats[h]
            vt = vt_ref[ks // TK, h * V_ROWS:(h + 1) * V_ROWS, ks % TK:ks % TK + n]
            new.append((m_new, alpha * carry[h][1] + jnp.dot(vt, p, preferred_element_type=F32)))
        return new

    items = _attn_items()
    init = [(jnp.full((1, TQ), NEG_BIG, F32), jnp.zeros((V_ROWS, TQ), F32)) for _ in range(2)]
    pending = [scores(it) for it in items[:ATTN_LOOKAHEAD]]
    carry = init
    for t, item in enumerate(items):
        if t + ATTN_LOOKAHEAD < len(items):
            pending.append(scores(items[t + ATTN_LOOKAHEAD]))
        carry = consume(item, pending.pop(0), carry)
        if item[3]:
            i = item[0]
            out_t = jnp.concatenate([acc[0:V_HEAD] / acc[V_HEAD:V_HEAD + 1] for (_, acc) in carry], axis=0)
            o_ref[i * TQ:(i + 1) * TQ, :] = out_t.T.astype(BF16)
            carry = init


def _attention(qt, k, vt):
    return pl.pallas_call(
        _attn_kernel,
        grid=(BATCH, N_HEADS // 2),
        in_specs=[pl.BlockSpec((2 * HEAD_PAD, SEQ), lambda b, hp: (hp, b)),
                  pl.BlockSpec((SEQ, 2 * HEAD_PAD), lambda b, hp: (b, hp)),
                  pl.BlockSpec((SEQ // TK, 2 * V_ROWS, TK), lambda b, hp: (b, hp, 0))],
        out_specs=pl.BlockSpec((SEQ, 2 * V_HEAD), lambda b, hp: (b, hp)),
        out_shape=jax.ShapeDtypeStruct((TOKENS, N_HEADS * V_HEAD), BF16),
        compiler_params=pltpu.CompilerParams(dimension_semantics=("arbitrary",) * 2, vmem_limit_bytes=VMEM_LIMIT),
        name="attention",
    )(qt, k, vt)


def _merge_core(attn_ref, gc_ref, sg_ref, x_ref, wmo_ref, wo_ref, fn_ref):
    y_mla = jnp.dot(attn_ref[...], wmo_ref[...], preferred_element_type=F32)
    merged = gc_ref[...].astype(F32) + sg_ref[...].astype(F32) * y_mla
    x2 = x_ref[...] + jnp.dot(merged.astype(BF16), wo_ref[...], preferred_element_type=F32)
    return x2, _rms(x2, fn_ref[...])


def _merge_ffn_kernel(attn_ref, gc_ref, sg_ref, x_ref, wmo_ref, wo_ref, fn_ref, wg_ref, wu_ref, wd_ref, o_ref):
    x2, hn = _merge_core(attn_ref, gc_ref, sg_ref, x_ref, wmo_ref, wo_ref, fn_ref)
    hn = hn.astype(BF16)
    acc = x2
    for c in range(D_FF // FF_CHUNK):
        cs = slice(c * FF_CHUNK, (c + 1) * FF_CHUNK)
        g = jnp.dot(hn, wg_ref[:, cs], preferred_element_type=F32)
        u = jnp.dot(hn, wu_ref[:, cs], preferred_element_type=F32)
        a = (jax.nn.silu(g) * u).astype(BF16)
        acc = acc + jnp.dot(a, wd_ref[cs, :], preferred_element_type=F32)
    o_ref[...] = acc


def _merge_route_kernel(attn_ref, gc_ref, sg_ref, x_ref, wmo_ref, wo_ref, fn_ref, router_ref,
                        x2_out, hn_out, route_out, route_t_out):
    x2, hn = _merge_core(attn_ref, gc_ref, sg_ref, x_ref, wmo_ref, wo_ref, fn_ref)
    x2_out[...] = x2
    hb = hn.astype(BF16)
    hn_out[...] = hb
    hl = (hn - hb.astype(F32)).astype(BF16)
    r = router_ref[...]
    rh = r.astype(BF16)
    rl = (r - rh.astype(F32)).astype(BF16)
    logits = _dot_nt(rh, hb) + _dot_nt(rh, hl) + _dot_nt(rl, hb)
    erow = lax.broadcasted_iota(jnp.int32, logits.shape, 0)
    logits = jnp.where(erow < N_EXPERTS, logits, -jnp.inf)
    m1 = jnp.max(logits, axis=0, keepdims=True)
    i1 = jnp.min(jnp.where(logits == m1, erow, E_PAD), axis=0, keepdims=True)
    rest = jnp.where(erow == i1, -jnp.inf, logits)
    m2 = jnp.max(rest, axis=0, keepdims=True)
    i2 = jnp.min(jnp.where(rest == m2, erow, E_PAD), axis=0, keepdims=True)
    t = jnp.exp(m2 - m1)
    w1 = 1.0 / (1.0 + t)
    w2 = t * w1
    route_t = jnp.where(erow == 0, i1.astype(F32),
                        jnp.where(erow == 1, i2.astype(F32), jnp.where(erow == 2, w1, jnp.where(erow == 3, w2, 0.0))))
    route_t_out[...] = route_t[0:SUBLANES, :]
    pad = jnp.zeros((LANES - E_PAD, route_t.shape[1]), F32)
    route_out[...] = jnp.concatenate([route_t, pad], axis=0).T


def _merge(attn, gc, sg, x, wmo, wo, fn, router=None, ffn=None):
    tm = TM_FRONT
    row = lambda n: pl.BlockSpec((tm, n), lambda i: (i, 0))
    in_specs = [row(N_HEADS * V_HEAD), row(D_MODEL), row(D_MODEL), row(D_MODEL),
                _const_spec((N_HEADS * V_HEAD, D_MODEL)), _const_spec((D_MODEL, D_MODEL)), _const_spec((1, D_MODEL))]
    args = [attn, gc, sg, x, wmo, wo, fn]
    if router is None:
        kern = _merge_ffn_kernel
        in_specs += [_const_spec((D_MODEL, D_FF)), _const_spec((D_MODEL, D_FF)), _const_spec((D_FF, D_MODEL))]
        args += list(ffn)
        out_specs = row(D_MODEL)
        out_shape = jax.ShapeDtypeStruct((TOKENS, D_MODEL), F32)
        name = "merge_ffn"
    else:
        kern = _merge_route_kernel
        in_specs.append(_const_spec((E_PAD, D_MODEL)))
        args.append(router)
        out_specs = [row(D_MODEL), row(D_MODEL), row(LANES), pl.BlockSpec((SUBLANES, tm), lambda i: (0, i))]
        out_shape = [jax.ShapeDtypeStruct((TOKENS, D_MODEL), F32), jax.ShapeDtypeStruct((TOKENS, D_MODEL), BF16),
                     jax.ShapeDtypeStruct((TOKENS, LANES), F32), jax.ShapeDtypeStruct((SUBLANES, TOKENS), F32)]
        name = "merge_route"
    return pl.pallas_call(
        kern, grid=(TOKENS // tm,), in_specs=in_specs, out_specs=out_specs, out_shape=out_shape,
        compiler_params=pltpu.CompilerParams(dimension_semantics=("arbitrary",), vmem_limit_bytes=VMEM_LIMIT),
        name=name,
    )(*args)


def _pieces(n, sizes):
    return [((n & z) != 0, n & ~(2 * z - 1), z) for z in sizes]


def _aligned(rows, z):
    return pl.ds(rows if isinstance(rows, int) else pl.multiple_of(rows, SUBLANES), z)


def _start_slab_copies(tile, base_ref, offp_ref, np_ref, make_copy):
    for e in range(N_EXPERTS):
        n = np_ref[tile * N_EXPERTS + e]
        src = offp_ref[tile * N_EXPERTS + e]
        dst = base_ref[tile * N_EXPERTS + e]
        for pred, off, z in _pieces(n, SLAB_PIECES):
            @pl.when(pred)
            def _(off=off, z=z):
                make_copy(src + off, dst + off, z).start()


def _wait_rows(total, make_copy):
    for pred, _, z in _pieces(total, WAIT_PIECES):
        @pl.when(pred)
        def _(z=z):
            make_copy(0, 0, z).wait()


def _dispatch_kernel(base_ref, offp_ref, np_ref, tot_ref, tail_ref, hn_ref, rt_ref, xs_hbm, xs_scr, zbuf, sem, zsem):
    i = pl.program_id(0)
    slot = i % 2

    def copy(buf_slot):
        return lambda s, d, z: pltpu.make_async_copy(xs_scr.at[buf_slot, _aligned(s, z), :],
                                                     xs_hbm.at[_aligned(d, z), :], sem.at[buf_slot])

    def zero_copy(d, z):
        return pltpu.make_async_copy(zbuf.at[pl.ds(0, z), :], xs_hbm.at[_aligned(d, z), :], zsem)

    @pl.when(i == 0)
    def _():
        zbuf[...] = jnp.zeros_like(zbuf)
        for phase in ("start", "wait"):
            for e in range(N_EXPERTS):
                t0 = tail_ref[e]
                n = tail_ref[N_EXPERTS + e]
                full = lax.shift_right_logical(n, jnp.int32(TM_MOE.bit_length() - 1))

                def chunk(c, carry, t0=t0, phase=phase):
                    cp = zero_copy(t0 + c * TM_MOE, TM_MOE)
                    cp.start() if phase == "start" else cp.wait()
                    return carry

                lax.fori_loop(0, full, chunk, 0)
                rest = n & (TM_MOE - 1)
                for pred, off, z in _pieces(rest, SLAB_PIECES[1:]):
                    @pl.when(pred)
                    def _(off=off, z=z, t0=t0, full=full, phase=phase):
                        cp = zero_copy(t0 + full * TM_MOE + off, z)
                        cp.start() if phase == "start" else cp.wait()

    e0 = rt_ref[0:1, :].astype(jnp.int32)
    e1 = rt_ref[1:2, :].astype(jnp.int32)
    eid = lax.broadcasted_iota(jnp.int32, (E_PAD, TT), 0)
    sel = jnp.where(e0 == eid, 1.0, jnp.where(e1 == eid, 1.0, 0.0)).astype(BF16)
    r = lax.broadcasted_iota(jnp.int32, (TT, TT), 0)
    c = lax.broadcasted_iota(jnp.int32, (TT, TT), 1)
    earlier = jnp.where(r < c, 1.0, 0.0).astype(BF16)
    rank = jnp.dot(sel, earlier, preferred_element_type=F32)
    ecol = lax.broadcasted_iota(jnp.int32, (E_PAD, 1), 0)
    offp = jnp.zeros((E_PAD, 1), F32)
    for e in range(N_EXPERTS):
        offp = jnp.where(ecol == e, offp_ref[i * N_EXPERTS + e].astype(F32), offp)
    slot_all = rank + offp
    slot0 = jnp.sum(jnp.where(e0 == eid, slot_all, 0.0), axis=0, keepdims=True)
    slot1 = jnp.sum(jnp.where(e1 == eid, slot_all, 0.0), axis=0, keepdims=True)
    sidx = lax.broadcasted_iota(jnp.int32, (SLOTS, TT), 0).astype(F32)
    place = jnp.where(sidx == slot0, 1.0, jnp.where(sidx == slot1, 1.0, 0.0)).astype(BF16)
    xs_scr[slot, :, 0:D_MODEL] = jnp.dot(place, hn_ref[...], preferred_element_type=F32)
    w_slot = jnp.sum(jnp.where(sidx == slot0, rt_ref[2:3, :], jnp.where(sidx == slot1, rt_ref[3:4, :], 0.0)),
                     axis=1, keepdims=True)
    xs_scr[slot, :, D_MODEL:] = jnp.broadcast_to(w_slot, (SLOTS, LANES))

    _start_slab_copies(i, base_ref, offp_ref, np_ref, copy(slot))

    @pl.when(i > 0)
    def _():
        _wait_rows(tot_ref[i - 1], copy(1 - slot))

    @pl.when(i == pl.num_programs(0) - 1)
    def _():
        _wait_rows(tot_ref[i], copy(slot))


def _dispatch(tables, hn, route_t):
    grid_spec = pltpu.PrefetchScalarGridSpec(
        num_scalar_prefetch=5,
        grid=(N_TT,),
        in_specs=[pl.BlockSpec((TT, D_MODEL), lambda i, *_: (i, 0)),
                  pl.BlockSpec((SUBLANES, TT), lambda i, *_: (0, i))],
        out_specs=pl.BlockSpec(memory_space=pl.ANY),
        scratch_shapes=[pltpu.VMEM((2, SLOTS, XS_WIDTH), F32), pltpu.VMEM((TM_MOE, XS_WIDTH), F32),
                        pltpu.SemaphoreType.DMA((2,)), pltpu.SemaphoreType.DMA],
    )
    return pl.pallas_call(
        _dispatch_kernel,
        grid_spec=grid_spec,
        out_shape=jax.ShapeDtypeStruct((P_PAD, XS_WIDTH), F32),
        compiler_params=pltpu.CompilerParams(dimension_semantics=("arbitrary",), vmem_limit_bytes=VMEM_LIMIT),
        name="moe_dispatch",
    )(tables["base"], tables["offp"], tables["np"], tables["total"], tables["tail"], hn, route_t)


def _moe_kernel(te_ref, nv_ref, x_ref, wg_ref, wu_ref, wd_ref, y_ref, wgu_scr, wd_scr):
    i = pl.program_id(0)

    @pl.when(jnp.logical_or(i == 0, te_ref[i] != te_ref[jnp.maximum(i - 1, 0)]))
    def _():
        wgu_scr[:, 0:D_FF_EXPERT] = wg_ref[0].astype(BF16)
        wgu_scr[:, D_FF_EXPERT:] = wu_ref[0].astype(BF16)
        wd_scr[...] = wd_ref[0].astype(BF16)

    @pl.when(i < nv_ref[0])
    def _():
        x = x_ref[:, 0:D_MODEL].astype(BF16)
        h = jnp.dot(x, wgu_scr[...], preferred_element_type=F32)
        a = (jax.nn.silu(h[:, :D_FF_EXPERT]) * h[:, D_FF_EXPERT:]).astype(BF16)
        y_ref[...] = jnp.dot(a, wd_scr[...], preferred_element_type=F32) * x_ref[:, D_MODEL:D_MODEL + 1]

    @pl.when(i >= nv_ref[0])
    def _():
        y_ref[...] = jnp.zeros_like(y_ref)


def _moe(tile_expert, n_valid, xs, wg, wu, wd):
    up_spec = pl.BlockSpec((1, D_MODEL, D_FF_EXPERT), lambda i, te, nv: (te[i], 0, 0))
    grid_spec = pltpu.PrefetchScalarGridSpec(
        num_scalar_prefetch=2,
        grid=(N_MOE_TILES,),
        in_specs=[pl.BlockSpec((TM_MOE, XS_WIDTH), lambda i, te, nv: (i, 0)), up_spec, up_spec,
                  pl.BlockSpec((1, D_FF_EXPERT, D_MODEL), lambda i, te, nv: (te[i], 0, 0))],
        out_specs=pl.BlockSpec((TM_MOE, D_MODEL), lambda i, te, nv: (i, 0)),
        scratch_shapes=[pltpu.VMEM((D_MODEL, 2 * D_FF_EXPERT), BF16), pltpu.VMEM((D_FF_EXPERT, D_MODEL), BF16)],
    )
    return pl.pallas_call(
        _moe_kernel,
        grid_spec=grid_spec,
        out_shape=jax.ShapeDtypeStruct((P_PAD, D_MODEL), F32),
        compiler_params=pltpu.CompilerParams(dimension_semantics=("arbitrary",), vmem_limit_bytes=VMEM_LIMIT),
        name="moe_experts",
    )(tile_expert, n_valid, xs, wg, wu, wd)


def _combine_kernel(base_ref, offp_ref, np_ref, tot_ref, x_ref, route_ref, y_hbm, fn_ref, o_ref, ybuf, sem):
    i = pl.program_id(0)
    slot = i % 2

    def copy(buf_slot):
        return lambda s, d, z: pltpu.make_async_copy(y_hbm.at[_aligned(d, z), :],
                                                     ybuf.at[buf_slot, _aligned(s, z), :], sem.at[buf_slot])

    @pl.when(i == 0)
    def _():
        ybuf[...] = jnp.zeros_like(ybuf)
        _start_slab_copies(0, base_ref, offp_ref, np_ref, copy(0))

    @pl.when(i + 1 < pl.num_programs(0))
    def _():
        _start_slab_copies(i + 1, base_ref, offp_ref, np_ref, copy(1 - slot))

    route = route_ref[...]
    e0 = route[:, 0:1].astype(jnp.int32)
    e1 = route[:, 1:2].astype(jnp.int32)
    eid =lax.broadcasted_iota(jnp.int32, (TT, LANES), 1)
    sel = jnp.where(e0 == eid, 1.0, jnp.where(e1 == eid, 1.0, 0.0)).astype(BF16)
    r = lax.broadcasted_iota(jnp.int32, (TT, TT), 0)
    c = lax.broadcasted_iota(jnp.int32, (TT, TT), 1)
    earlier = jnp.where(c < r, 1.0, 0.0).astype(BF16)
    rank = jnp.dot(earlier, sel, preferred_element_type=F32)
    erow = lax.broadcasted_iota(jnp.int32, (1, LANES), 1)
    offp = jnp.zeros((1, LANES), F32)
    for e in range(N_EXPERTS):
        offp = jnp.where(erow == e, offp_ref[i * N_EXPERTS + e].astype(F32), offp)
    slot_all = rank + offp
    slot0 = jnp.sum(jnp.where(e0 == eid, slot_all, 0.0), axis=1, keepdims=True)
    slot1 = jnp.sum(jnp.where(e1 == eid, slot_all, 0.0), axis=1, keepdims=True)
    sidx = lax.broadcasted_iota(jnp.int32, (TT, SLOTS), 1).astype(F32)
    pick = jnp.where(sidx == slot0, 1.0, jnp.where(sidx == slot1, 1.0, 0.0)).astype(BF16)

    _wait_rows(tot_ref[i], copy(slot))
    y = ybuf[slot]
    yh = y.astype(BF16)
    yl = (y - yh.astype(F32)).astype(BF16)
    moe = jnp.dot(pick, yh, preferred_element_type=F32) + jnp.dot(pick, yl, preferred_element_type=F32)
    o_ref[...] = _rms(x_ref[...] + moe, fn_ref[...])


def _combine(tables, x2, route, y_sorted, fn):
    grid_spec = pltpu.PrefetchScalarGridSpec(
        num_scalar_prefetch=4,
        grid=(N_TT,),
        in_specs=[pl.BlockSpec((TT, D_MODEL), lambda i, *_: (i, 0)),
                  pl.BlockSpec((TT, LANES), lambda i, *_: (i, 0)),
                  pl.BlockSpec(memory_space=pl.ANY),
                  pl.BlockSpec((1, D_MODEL), lambda i, *_: (0, 0))],
        out_specs=pl.BlockSpec((TT, D_MODEL), lambda i, *_: (i, 0)),
        scratch_shapes=[pltpu.VMEM((2, SLOTS, D_MODEL), F32), pltpu.SemaphoreType.DMA((2,))],
    )
    return pl.pallas_call(
        _combine_kernel,
        grid_spec=grid_spec,
        out_shape=jax.ShapeDtypeStruct((TOKENS, D_MODEL), F32),
        compiler_params=pltpu.CompilerParams(dimension_semantics=("arbitrary",), vmem_limit_bytes=VMEM_LIMIT),
        name="moe_combine",
    )(tables["base"], tables["offp"], tables["np"], tables["total"], x2, route, y_sorted, fn)


def _routing_tables(route_t):
    experts = route_t[:TOP_K, :].astype(jnp.int32)
    eid = jnp.arange(N_EXPERTS, dtype=jnp.int32)[:, None, None]
    chosen = jnp.sum((experts[None, :, :] == eid).astype(jnp.int32), axis=1)
    cnt = jnp.sum(chosen.reshape(N_EXPERTS, N_TT, TT), axis=2).T
    npad = (cnt + SUBLANES - 1) // SUBLANES * SUBLANES
    offp = jnp.cumsum(npad, axis=1) - npad
    total = jnp.sum(npad, axis=1)
    group = jnp.sum(npad, axis=0)
    group_pad = (group + TM_MOE - 1) // TM_MOE * TM_MOE
    group_end = jnp.cumsum(group_pad)
    group_start = group_end - group_pad
    base = group_start[None, :] + jnp.cumsum(npad, axis=0) - npad
    tail_start = group_start + group
    is_last = jnp.arange(N_EXPERTS) == N_EXPERTS - 1
    tail_len = jnp.where(is_last, P_PAD - tail_start, group_pad - group)
    tile_start = jnp.arange(N_MOE_TILES, dtype=jnp.int32) * TM_MOE
    tile_expert = jnp.minimum(jnp.sum((tile_start[:, None] >= group_end[None, :]).astype(jnp.int32), axis=1),
                              N_EXPERTS - 1)
    n_valid = (group_end[-1] // TM_MOE).reshape(1)
    tables = {"base": base.reshape(-1), "offp": offp.reshape(-1), "np": npad.reshape(-1), "total": total,
              "tail": jnp.concatenate([tail_start, tail_len])}
    return {k: v.astype(jnp.int32) for k, v in tables.items()}, tile_expert.astype(jnp.int32), n_valid.astype(jnp.int32)


WIN_ROWS = 256


def _win_layout_kernel(w_ref, o_ref):
    w = w_ref[0].astype(F32)
    rows = w.shape[0]
    o_ref[0, :, 0:OFF_GATE] = w[:, 0:REF_CQ].astype(BF16)
    o_ref[0, :, OFF_GATE:OFF_SMALL] = w[:, REF_GATE:].astype(BF16)
    o_ref[0, :, OFF_SMALL:OFF_SMALL + Q_LORA + KV_LORA] = w[:, REF_CQ:REF_KR].astype(BF16)
    kr_tile = jnp.concatenate([jnp.zeros((rows, QK_NOPE), F32), w[:, REF_KR:REF_GATE],
                               jnp.zeros((rows, HEAD_PAD - QK_DIM), F32)], axis=1)
    o_ref[0, :, OFF_SMALL + Q_LORA + KV_LORA:] = kr_tile.astype(BF16)


def _win_layout(w_in):
    d_in = w_in.shape[-1]
    return pl.pallas_call(
        _win_layout_kernel,
        grid=(DEPTH, D_MODEL // WIN_ROWS),
        in_specs=[pl.BlockSpec((1, WIN_ROWS, d_in), lambda l, r: (l, r, 0))],
        out_specs=pl.BlockSpec((1, WIN_ROWS, D_IN_PAD), lambda l, r: (l, r, 0)),
        out_shape=jax.ShapeDtypeStruct((DEPTH, D_MODEL, D_IN_PAD), BF16),
        compiler_params=pltpu.CompilerParams(dimension_semantics=("arbitrary",) * 2, vmem_limit_bytes=VMEM_LIMIT),
        name="win_layout",
    )(w_in)


def _layer_weights(w_uq, w_ukv):
    wq = w_uq.T.astype(BF16)

    wkv = w_ukv.reshape(KV_LORA, N_HEADS, QK_NOPE + V_HEAD)
    wuk = jnp.concatenate([wkv[:, :, :QK_NOPE], jnp.zeros((KV_LORA, N_HEADS, HEAD_PAD - QK_NOPE), F32)], axis=2)
    wuk = wuk.reshape(KV_LORA, N_HEADS * HEAD_PAD).astype(BF16)
    wuv = wkv[:, :, QK_NOPE:].reshape(KV_LORA, N_HEADS * V_HEAD).T.astype(BF16)
    return wq, wuk, wuv


def kernel(x, positions, attn_norm, w_in, conv_w, w_conv_out, q_norm, w_uq, kv_norm, w_ukv, w_mla_out, w_o, ffn_norm,
           w_gate, w_up, w_down, router, w_gate_e, w_up_e, w_down_e, final_norm):
    assert x.shape == (BATCH, SEQ, D_MODEL) and positions.shape == (BATCH, SEQ)
    cos_t, sin_t = _rope_tables(positions)
    xt = x.reshape(TOKENS, D_MODEL)
    out = None
    win_all = _win_layout(w_in.astype(BF16))
    for l in range(DEPTH):
        wq, wuk, wuv = _layer_weights(w_uq[l], w_ukv[l])
        q, k, v, gc, sg = _front(l, xt, attn_norm[l].reshape(1, D_MODEL), win_all, conv_w[l],
                                 w_conv_out[l].astype(BF16),
                                 q_norm[l].reshape(1, Q_LORA), wq, kv_norm[l].reshape(1, KV_LORA), wuk, wuv,
                                 cos_t, sin_t)
        attn = _attention(q, k, v)
        wmo = w_mla_out[l].astype(BF16)
        wo = w_o[l].astype(BF16)
        fn = ffn_norm[l].reshape(1, D_MODEL)
        if l % 2 == 0:
            j = l // 2
            xt = _merge(attn, gc, sg, xt, wmo, wo, fn,
                        ffn=(w_gate[j].astype(BF16), w_up[j].astype(BF16), w_down[j].astype(BF16)))
        else:
            j = l // 2
            router_pad = jnp.pad(router[j].T, ((0, E_PAD - N_EXPERTS), (0, 0)))
            x2, hn, route, route_t = _merge(attn, gc, sg, xt, wmo, wo, fn, router_pad)
            tables, tile_expert, n_valid = _routing_tables(route_t)
            xs = _dispatch(tables, hn, route_t)
            y_sorted = _moe(tile_expert, n_valid, xs, w_gate_e[j], w_up_e[j], w_down_e[j])
            out = _combine(tables, x2, route, y_sorted, final_norm.reshape(1, D_MODEL))
    return out.reshape(BATCH, SEQ, D_MODEL)
```

```python
import math

import jax
import jax.numpy as jnp
from jax import lax
from jax.experimental import pallas as pl
from jax.experimental.pallas import tpu as pltpu

F32 = jnp.float32
BF16 = jnp.bfloat16

D_MODEL = 1024
BATCH = 8
SEQ = 2048
TOKENS = BATCH * SEQ
DEPTH = 2
D_CONV = 512
CONV_WIDTH = 3
N_HEADS = 8
QK_NOPE = 64
QK_ROPE = 32
HALF_ROPE = QK_ROPE // 2
V_HEAD = 64
Q_LORA = 384
KV_LORA = 256
ROPE_THETA = 10000.0
D_FF = 2816
N_EXPERTS = 8
TOP_K = 2
D_FF_EXPERT = 1408
EPS = 1e-6

LANES = 128
HEAD_PAD = LANES
QK_DIM = QK_NOPE + QK_ROPE
Q_SCALE = (1.0 / math.sqrt(QK_DIM)) * math.log2(math.e)
NEG_BIG = -1e30

OFF_BCU = 0
OFF_GATE = 3 * D_CONV
OFF_SMALL = OFF_GATE + 2 * D_MODEL
D_IN_PAD = OFF_SMALL + Q_LORA + KV_LORA + LANES
REF_CQ = 3 * D_CONV
REF_KR = REF_CQ + Q_LORA + KV_LORA
REF_GATE = REF_KR + QK_ROPE

TM_FRONT = 512
TQ = 256
TK = 512
ATTN_LOOKAHEAD = 3
V_ROWS = V_HEAD + 16
FF_CHUNK = 256
VMEM_LIMIT = 56 * 1024 * 1024

SUBLANES = 8
TM_MOE = 256
TT = 256
N_TT = TOKENS // TT
E_PAD = 16
XS_WIDTH = D_MODEL + LANES
N_PAIRS = TOKENS * TOP_K
SLOTS =TT * TOP_K + N_EXPERTS * SUBLANES
_MAX_ROWS = N_PAIRS + N_TT * N_EXPERTS * (SUBLANES - 1) + N_EXPERTS * (TM_MOE - SUBLANES)
N_MOE_TILES = -(-_MAX_ROWS // TM_MOE)
P_PAD = N_MOE_TILES * TM_MOE
SLAB_PIECES = tuple(TT >> s for s in range(6))
WAIT_PIECES = (512,) + SLAB_PIECES
assert SLAB_PIECES[-1] == SUBLANES and SLOTS < 2 * WAIT_PIECES[0]


def _rms(x, g):
    return x * lax.rsqrt(jnp.mean(x * x, axis=-1, keepdims=True) + EPS) * g


def _const_spec(shape):
    nd = len(shape)
    return pl.BlockSpec(shape, lambda *_: (0,) * nd, pipeline_mode=pl.Buffered(1))


def _trig_kernel(pos_ref, invf_ref, cos_ref, sin_ref):
    ang = invf_ref[...] * pos_ref[...].astype(F32)
    cos_ref[...] = jnp.cos(ang)
    sin_ref[...] = jnp.sin(ang)


def _rope_tables(positions):
    inv_freq = ROPE_THETA ** (-jnp.arange(0, QK_ROPE, 2, dtype=F32) / QK_ROPE)
    return pl.pallas_call(
        _trig_kernel,
        out_shape=(jax.ShapeDtypeStruct((HALF_ROPE, TOKENS), F32),) * 2,
        name="rope_trig",
    )(positions.reshape(1, TOKENS), inv_freq.reshape(HALF_ROPE, 1))


def _dot_nt(a, b):
    return lax.dot_general(a, b, (((1,), (1,)), ((), ())), preferred_element_type=F32)


def _front_kernel(x_ref, an_ref, win_ref, cw_ref, wco_ref, qn_ref, wq_ref, kvn_ref, wuk_ref, wuv_ref,
                  cost_ref, sint_ref, q_out, k_out, v_out, gc_out, sg_out, cu_scr):
    tm = x_ref.shape[0]
    i = pl.program_id(0)
    xn = _rms(x_ref[...], an_ref[...]).astype(BF16)

    def proj(a, b):
        return jnp.dot(xn, win_ref[0, :, a:b], preferred_element_type=F32)

    small = proj(OFF_SMALL, D_IN_PAD)

    cq = cost_ref[...] * Q_SCALE
    sq = sint_ref[...] * Q_SCALE
    cqn = _rms(small[:, :Q_LORA], qn_ref[...]).astype(BF16)
    qt = _dot_nt(wq_ref[...], cqn)
    zero_pad = jnp.zeros((HEAD_PAD - QK_DIM, tm), BF16)
    for h in range(N_HEADS):
        src = h * QK_DIM
        dst = h * HEAD_PAD
        x1 = qt[src + QK_NOPE:src + QK_NOPE + HALF_ROPE, :]
        x2 = qt[src + QK_NOPE + HALF_ROPE:src + QK_DIM, :]
        q_out[dst:dst + QK_NOPE, :] = (qt[src:src + QK_NOPE, :] * Q_SCALE).astype(BF16)
        q_out[dst + QK_NOPE:dst + QK_NOPE + HALF_ROPE, :] = (x1 * cq - x2 * sq).astype(BF16)
        q_out[dst + QK_NOPE + HALF_ROPE:dst + QK_DIM, :] = (x2 * cq + x1 * sq).astype(BF16)
        q_out[dst + QK_DIM:dst + HEAD_PAD, :] = zero_pad

    cos_t = cost_ref[...]
    sin_t = sint_ref[...]
    z_nope = jnp.zeros((QK_NOPE, tm), F32)
    z_rope = jnp.zeros((HALF_ROPE, tm), F32)
    z_pad = jnp.zeros((HEAD_PAD - QK_DIM, tm), F32)
    ck = jnp.concatenate([z_nope, cos_t, cos_t, z_pad], axis=0).T
    sk_x1 = jnp.concatenate([z_nope, -sin_t, z_rope, z_pad], axis=0).T
    sk_x2 = jnp.concatenate([z_nope, z_rope, sin_t, z_pad], axis=0).T
    kr = small[:, Q_LORA + KV_LORA:]
    kpe = (kr * ck + pltpu.roll(kr, HEAD_PAD - HALF_ROPE, axis=1) * sk_x1 + pltpu.roll(kr, HALF_ROPE, axis=1) * sk_x2)
    ckvn = _rms(small[:, Q_LORA:Q_LORA + KV_LORA], kvn_ref[...]).astype(BF16)
    kn = jnp.dot(ckvn, wuk_ref[...], preferred_element_type=F32)
    for h in range(N_HEADS):
        blk = slice(h * HEAD_PAD, (h + 1) * HEAD_PAD)
        k_out[:, blk] = (kn[:, blk] + kpe).astype(BF16)
    vt = _dot_nt(wuv_ref[...], ckvn).astype(BF16)
    sub = lax.broadcasted_iota(jnp.int32, (V_ROWS - V_HEAD, TK), 0)
    ones_rows = jnp.where(sub == 0, 1.0, 0.0).astype(BF16)
    for c in range(tm // TK):
        for h in range(N_HEADS):
            v_out[c, h * V_ROWS:h * V_ROWS + V_HEAD, :] = vt[h * V_HEAD:(h + 1) * V_HEAD, c * TK:(c + 1) * TK]
            v_out[c, h * V_ROWS + V_HEAD:(h + 1) * V_ROWS, :] = ones_rows

    bcu = proj(OFF_BCU, OFF_GATE)
    b_g = bcu[:, :D_CONV]
    cu = bcu[:, D_CONV:2 * D_CONV] * bcu[:, 2 * D_CONV:]

    @pl.when(i % (SEQ // tm) == 0)
    def _():
        cu_scr[0:8, :] = jnp.zeros((8, D_CONV), F32)

    cu_scr[8:8 + tm, :] = cu
    prev1 = cu_scr[7:7 + tm, :]
    prev2 = cu_scr[6:6 + tm, :]
    cw = cw_ref[...]
    conv = prev2 * cw[0:1, :] + prev1 * cw[1:2, :] + cu * cw[2:3, :]
    cu_scr[0:8, :] = cu[tm - 8:, :]
    y_conv = jnp.dot((b_g * conv).astype(BF16), wco_ref[...], preferred_element_type=F32)

    gates = proj(OFF_GATE, OFF_SMALL)
    gc_out[...] = (jax.nn.sigmoid(gates[:, :D_MODEL]) * y_conv).astype(BF16)
    sg_out[...] = jax.nn.sigmoid(gates[:, D_MODEL:]).astype(BF16)


def _front(layer, x, an, win_all, cw, wco, qn, wq, kvn, wuk, wuv, cos_t, sin_t):
    tm = TM_FRONT
    row = lambda n: pl.BlockSpec((tm, n), lambda i: (i, 0))
    col = lambda n: pl.BlockSpec((n, tm), lambda i: (0, i))
    win_spec = pl.BlockSpec((1, D_MODEL, D_IN_PAD), lambda i: (layer, 0, 0), pipeline_mode=pl.Buffered(1))
    return pl.pallas_call(
        _front_kernel,
        grid=(TOKENS // tm,),
        in_specs=[row(D_MODEL), _const_spec((1, D_MODEL)), win_spec,
                  _const_spec((CONV_WIDTH, D_CONV)), _const_spec((D_CONV, D_MODEL)),
                  _const_spec((1, Q_LORA)), _const_spec((N_HEADS * QK_DIM, Q_LORA)),
                  _const_spec((1, KV_LORA)), _const_spec((KV_LORA, N_HEADS * HEAD_PAD)),
                  _const_spec((N_HEADS * V_HEAD, KV_LORA)),
                  col(HALF_ROPE), col(HALF_ROPE)],
        out_specs=[col(N_HEADS * HEAD_PAD), row(N_HEADS * HEAD_PAD),
                   pl.BlockSpec((tm // TK, N_HEADS * V_ROWS, TK), lambda i: (i, 0, 0)),
                   row(D_MODEL), row(D_MODEL)],
        out_shape=[jax.ShapeDtypeStruct((N_HEADS * HEAD_PAD, TOKENS), BF16),
                   jax.ShapeDtypeStruct((TOKENS, N_HEADS * HEAD_PAD), BF16),
                   jax.ShapeDtypeStruct((TOKENS // TK, N_HEADS * V_ROWS, TK), BF16),
                   jax.ShapeDtypeStruct((TOKENS, D_MODEL), BF16),
                   jax.ShapeDtypeStruct((TOKENS, D_MODEL), BF16)],
        scratch_shapes=[pltpu.VMEM((tm + 8, D_CONV), F32)],
        compiler_params=pltpu.CompilerParams(dimension_semantics=("arbitrary",), vmem_limit_bytes=VMEM_LIMIT),
        name="front",
    )(x, an, win_all, cw, wco, qn, wq, kvn, wuk, wuv, cos_t, sin_t)


def _attn_items():
    items = []
    for i in range(SEQ // TQ):
        ks = 0
        while ks + TK <= i * TQ:
            items.append((i, ks, TK, False))
            ks += TK
        items.append((i, ks, (i + 1) * TQ - ks, True))
    return items


def _attn_kernel(qt_ref, k_ref, vt_ref, o_ref):
    row = lax.broadcasted_iota(jnp.int32, (TQ, TQ), 0)
    col = lax.broadcasted_iota(jnp.int32, (TQ, TQ), 1)
    causal = row <= col

    def scores(item):
        i, ks, n, diag = item
        sts = []
        for h in range(2):
            hs = slice(h * HEAD_PAD, (h + 1) * HEAD_PAD)
            st = jnp.dot(k_ref[ks:ks + n, hs], qt_ref[hs, i * TQ:(i + 1) * TQ], preferred_element_type=F32)
            if diag:
                tail = jnp.where(causal, st[n - TQ:, :], NEG_BIG)
                st = tail if n == TQ else jnp.concatenate([st[:n - TQ, :], tail], axis=0)
            sts.append(st)
        return sts

    def consume(item, sts, carry):
        i, ks, n, _ = item
        stats = []
        for h in range(2):
            m, _ = carry[h]
            m_new = jnp.maximum(m, jnp.max(sts[h], axis=0, keepdims=True))
            stats.append((m_new, jnp.exp2(m - m_new), jnp.exp2(sts[h] - m_new).astype(BF16)))
        new = []
        for h in range(2):
            m_new, alpha, p = stats[h]
            vt = vt_ref[ks // TK, h * V_ROWS:(h + 1) * V_ROWS, ks % TK:ks % TK + n]
            new.append((m_new, alpha * carry[h][1] + jnp.dot(vt, p, preferred_element_type=F32)))
        return new

    items = _attn_items()
    init = [(jnp.full((1, TQ), NEG_BIG, F32), jnp.zeros((V_ROWS, TQ), F32)) for _ in range(2)]
    pending = [scores(it) for it in items[:ATTN_LOOKAHEAD]]
    carry = init
    for t, item in enumerate(items):
        if t + ATTN_LOOKAHEAD < len(items):
            pending.append(scores(items[t + ATTN_LOOKAHEAD]))
        carry = consume(item, pending.pop(0), carry)
        if item[3]:
            i = item[0]
            out_t = jnp.concatenate([acc[0:V_HEAD] / acc[V_HEAD:V_HEAD + 1] for (_, acc) in carry], axis=0)
            o_ref[i * TQ:(i + 1) * TQ, :] = out_t.T.astype(BF16)
            carry = init


def _attention(qt, k, vt):
    return pl.pallas_call(
        _attn_kernel,
        grid=(BATCH, N_HEADS // 2),
        in_specs=[pl.BlockSpec((2 * HEAD_PAD, SEQ), lambda b, hp: (hp, b)),
                  pl.BlockSpec((SEQ, 2 * HEAD_PAD), lambda b, hp: (b, hp)),
                  pl.BlockSpec((SEQ // TK, 2 * V_ROWS, TK), lambda b, hp: (b, hp, 0))],
        out_specs=pl.BlockSpec((SEQ, 2 * V_HEAD), lambda b, hp: (b, hp)),
        out_shape=jax.ShapeDtypeStruct((TOKENS, N_HEADS * V_HEAD), BF16),
        compiler_params=pltpu.CompilerParams(dimension_semantics=("arbitrary",) * 2, vmem_limit_bytes=VMEM_LIMIT),
        name="attention",
    )(qt, k, vt)


def _merge_core(attn_ref, gc_ref, sg_ref, x_ref, wmo_ref, wo_ref, fn_ref):
    y_mla = jnp.dot(attn_ref[...], wmo_ref[...], preferred_element_type=F32)
    merged = gc_ref[...].astype(F32) + sg_ref[...].astype(F32) * y_mla
    x2 = x_ref[...] + jnp.dot(merged.astype(BF16), wo_ref[...], preferred_element_type=F32)
    return x2, _rms(x2, fn_ref[...])


def _merge_ffn_kernel(attn_ref, gc_ref, sg_ref, x_ref, wmo_ref, wo_ref, fn_ref, wg_ref, wu_ref, wd_ref, o_ref):
    x2, hn = _merge_core(attn_ref, gc_ref, sg_ref, x_ref, wmo_ref, wo_ref, fn_ref)
    hn = hn.astype(BF16)
    acc = x2
    for c in range(D_FF // FF_CHUNK):
        cs = slice(c * FF_CHUNK, (c + 1) * FF_CHUNK)
        g = jnp.dot(hn, wg_ref[:, cs], preferred_element_type=F32)
        u = jnp.dot(hn, wu_ref[:, cs], preferred_element_type=F32)
        a = (jax.nn.silu(g) * u).astype(BF16)
        acc = acc + jnp.dot(a, wd_ref[cs, :], preferred_element_type=F32)
    o_ref[...] = acc


def _merge_route_kernel(attn_ref, gc_ref, sg_ref, x_ref, wmo_ref, wo_ref, fn_ref, router_ref,
                        x2_out, hn_out, route_out, route_t_out):
    x2, hn = _merge_core(attn_ref, gc_ref, sg_ref, x_ref, wmo_ref, wo_ref, fn_ref)
    x2_out[...] = x2
    hb = hn.astype(BF16)
    hn_out[...] = hb
    hl = (hn - hb.astype(F32)).astype(BF16)
    r = router_ref[...]
    rh = r.astype(BF16)
    rl = (r - rh.astype(F32)).astype(BF16)
    logits = _dot_nt(rh, hb) + _dot_nt(rh, hl) + _dot_nt(rl, hb)
    erow = lax.broadcasted_iota(jnp.int32, logits.shape, 0)
    logits = jnp.where(erow < N_EXPERTS, logits, -jnp.inf)
    m1 = jnp.max(logits, axis=0, keepdims=True)
    i1 = jnp.min(jnp.where(logits == m1, erow, E_PAD), axis=0, keepdims=True)
    rest = jnp.where(erow == i1, -jnp.inf, logits)
    m2 = jnp.max(rest, axis=0, keepdims=True)
    i2 = jnp.min(jnp.where(rest == m2, erow, E_PAD), axis=0, keepdims=True)
    t = jnp.exp(m2 - m1)
    w1 = 1.0 / (1.0 + t)
    w2 = t * w1
    route_t = jnp.where(erow == 0, i1.astype(F32),
                        jnp.where(erow == 1, i2.astype(F32), jnp.where(erow == 2, w1, jnp.where(erow == 3, w2, 0.0))))
    route_t_out[...] = route_t[0:SUBLANES, :]
    pad = jnp.zeros((LANES - E_PAD, route_t.shape[1]), F32)
    route_out[...] = jnp.concatenate([route_t, pad], axis=0).T


def _merge(attn, gc, sg, x, wmo, wo, fn, router=None, ffn=None):
    tm = TM_FRONT
    row = lambda n: pl.BlockSpec((tm, n), lambda i: (i, 0))
    in_specs = [row(N_HEADS * V_HEAD), row(D_MODEL), row(D_MODEL), row(D_MODEL),
                _const_spec((N_HEADS * V_HEAD, D_MODEL)), _const_spec((D_MODEL, D_MODEL)), _const_spec((1, D_MODEL))]
    args = [attn, gc, sg, x, wmo, wo, fn]
    if router is None:
        kern = _merge_ffn_kernel
        in_specs += [_const_spec((D_MODEL, D_FF)), _const_spec((D_MODEL, D_FF)), _const_spec((D_FF, D_MODEL))]
        args += list(ffn)
        out_specs = row(D_MODEL)
        out_shape = jax.ShapeDtypeStruct((TOKENS, D_MODEL), F32)
        name = "merge_ffn"
    else:
        kern = _merge_route_kernel
        in_specs.append(_const_spec((E_PAD, D_MODEL)))
        args.append(router)
        out_specs = [row(D_MODEL), row(D_MODEL), row(LANES), pl.BlockSpec((SUBLANES, tm), lambda i: (0, i))]
        out_shape = [jax.ShapeDtypeStruct((TOKENS, D_MODEL), F32), jax.ShapeDtypeStruct((TOKENS, D_MODEL), BF16),
                     jax.ShapeDtypeStruct((TOKENS, LANES), F32), jax.ShapeDtypeStruct((SUBLANES, TOKENS), F32)]
        name = "merge_route"
    return pl.pallas_call(
        kern, grid=(TOKENS // tm,), in_specs=in_specs, out_specs=out_specs, out_shape=out_shape,
        compiler_params=pltpu.CompilerParams(dimension_semantics=("arbitrary",), vmem_limit_bytes=VMEM_LIMIT),
        name=name,
    )(*args)


def _pieces(n, sizes):
    return [((n & z) != 0, n & ~(2 * z - 1), z) for z in sizes]


def _aligned(rows, z):
    return pl.ds(rows if isinstance(rows, int) else pl.multiple_of(rows, SUBLANES), z)


def _start_slab_copies(tile, base_ref, offp_ref, np_ref, make_copy):
    for e in range(N_EXPERTS):
        n = np_ref[tile * N_EXPERTS + e]
        src = offp_ref[tile * N_EXPERTS + e]
        dst = base_ref[tile * N_EXPERTS + e]
        for pred, off, z in _pieces(n, SLAB_PIECES):
            @pl.when(pred)
            def _(off=off, z=z):
                make_copy(src + off, dst + off, z).start()


def _wait_rows(total, make_copy):
    for pred, _, z in _pieces(total, WAIT_PIECES):
        @pl.when(pred)
        def _(z=z):
            make_copy(0, 0, z).wait()


def _dispatch_kernel(base_ref, offp_ref, np_ref, tot_ref, tail_ref, hn_ref, rt_ref, xs_hbm, xs_scr, zbuf, sem, zsem):
    i = pl.program_id(0)
    slot = i % 2

    def copy(buf_slot):
        return lambda s, d, z: pltpu.make_async_copy(xs_scr.at[buf_slot, _aligned(s, z), :],
                                                     xs_hbm.at[_aligned(d, z), :], sem.at[buf_slot])

    def zero_copy(d, z):
        return pltpu.make_async_copy(zbuf.at[pl.ds(0, z), :], xs_hbm.at[_aligned(d, z), :], zsem)

    @pl.when(i == 0)
    def _():
        zbuf[...] = jnp.zeros_like(zbuf)
        for phase in ("start", "wait"):
            for e in range(N_EXPERTS):
                t0 = tail_ref[e]
                n = tail_ref[N_EXPERTS + e]
                full = lax.shift_right_logical(n, jnp.int32(TM_MOE.bit_length() - 1))

                def chunk(c, carry, t0=t0, phase=phase):
                    cp = zero_copy(t0 + c * TM_MOE, TM_MOE)
                    cp.start() if phase == "start" else cp.wait()
                    return carry

                lax.fori_loop(0, full, chunk, 0)
                rest = n & (TM_MOE - 1)
                for pred, off, z in _pieces(rest, SLAB_PIECES[1:]):
                    @pl.when(pred)
                    def _(off=off, z=z, t0=t0, full=full, phase=phase):
                        cp = zero_copy(t0 + full * TM_MOE + off, z)
                        cp.start() if phase == "start" else cp.wait()

    e0 = rt_ref[0:1, :].astype(jnp.int32)
    e1 = rt_ref[1:2, :].astype(jnp.int32)
    eid = lax.broadcasted_iota(jnp.int32, (E_PAD, TT), 0)
    sel = jnp.where(e0 == eid, 1.0, jnp.where(e1 == eid, 1.0, 0.0)).astype(BF16)
    r = lax.broadcasted_iota(jnp.int32, (TT, TT), 0)
    c = lax.broadcasted_iota(jnp.int32, (TT, TT), 1)
    earlier = jnp.where(r < c, 1.0, 0.0).astype(BF16)
    rank = jnp.dot(sel, earlier, preferred_element_type=F32)
    ecol = lax.broadcasted_iota(jnp.int32, (E_PAD, 1), 0)
    offp = jnp.zeros((E_PAD, 1), F32)
    for e in range(N_EXPERTS):
        offp = jnp.where(ecol == e, offp_ref[i * N_EXPERTS + e].astype(F32), offp)
    slot_all = rank + offp
    slot0 = jnp.sum(jnp.where(e0 == eid, slot_all, 0.0), axis=0, keepdims=True)
    slot1 = jnp.sum(jnp.where(e1 == eid, slot_all, 0.0), axis=0, keepdims=True)
    sidx = lax.broadcasted_iota(jnp.int32, (SLOTS, TT), 0).astype(F32)
    place = jnp.where(sidx == slot0, 1.0, jnp.where(sidx == slot1, 1.0, 0.0)).astype(BF16)
    xs_scr[slot, :, 0:D_MODEL] = jnp.dot(place, hn_ref[...], preferred_element_type=F32)
    w_slot = jnp.sum(jnp.where(sidx == slot0, rt_ref[2:3, :], jnp.where(sidx == slot1, rt_ref[3:4, :], 0.0)),
                     axis=1, keepdims=True)
    xs_scr[slot, :, D_MODEL:] = jnp.broadcast_to(w_slot, (SLOTS, LANES))

    _start_slab_copies(i, base_ref, offp_ref, np_ref, copy(slot))

    @pl.when(i > 0)
    def _():
        _wait_rows(tot_ref[i - 1], copy(1 - slot))

    @pl.when(i == pl.num_programs(0) - 1)
    def _():
        _wait_rows(tot_ref[i], copy(slot))


def _dispatch(tables, hn, route_t):
    grid_spec = pltpu.PrefetchScalarGridSpec(
        num_scalar_prefetch=5,
        grid=(N_TT,),
        in_specs=[pl.BlockSpec((TT, D_MODEL), lambda i, *_: (i, 0)),
                  pl.BlockSpec((SUBLANES, TT), lambda i, *_: (0, i))],
        out_specs=pl.BlockSpec(memory_space=pl.ANY),
        scratch_shapes=[pltpu.VMEM((2, SLOTS, XS_WIDTH), F32), pltpu.VMEM((TM_MOE, XS_WIDTH), F32),
                        pltpu.SemaphoreType.DMA((2,)), pltpu.SemaphoreType.DMA],
    )
    return pl.pallas_call(
        _dispatch_kernel,
        grid_spec=grid_spec,
        out_shape=jax.ShapeDtypeStruct((P_PAD, XS_WIDTH), F32),
        compiler_params=pltpu.CompilerParams(dimension_semantics=("arbitrary",), vmem_limit_bytes=VMEM_LIMIT),
        name="moe_dispatch",
    )(tables["base"], tables["offp"], tables["np"], tables["total"], tables["tail"], hn, route_t)


def _moe_kernel(te_ref, nv_ref, nxt_ref, par_ref, first_ref, last_ref, x_ref, wg_hbm, wu_hbm, wd_hbm, y_ref,
                stage_g, stage_u, stage_d, wgu_scr, wd_scr, sem):
    i = pl.program_id(0)
    valid = i < nv_ref[0]
    nxt = nxt_ref[i]

    def fetch(e):
        return [pltpu.make_async_copy(wg_hbm.at[e], stage_g, sem.at[0]),
                pltpu.make_async_copy(wu_hbm.at[e], stage_u, sem.at[1]),
                pltpu.make_async_copy(wd_hbm.at[e], stage_d, sem.at[2])]

    def round_into(s):
        wgu_scr[s, :, 0:D_FF_EXPERT] = stage_g[...].astype(BF16)
        wgu_scr[s, :, D_FF_EXPERT:] = stage_u[...].astype(BF16)
        wd_scr[s] = stage_d[...].astype(BF16)

    @pl.when(i == 0)
    def _():
        for cp in fetch(te_ref[0]):
            cp.start()
        for cp in fetch(te_ref[0]):
            cp.wait()
        round_into(par_ref[0])

    @pl.when(jnp.logical_and(valid, jnp.logical_and(first_ref[i] == 1, nxt >= 0)))
    def _():
        for cp in fetch(nxt):
            cp.start()

    @pl.when(valid)
    def _():
        s = par_ref[i]
        x = x_ref[:, 0:D_MODEL].astype(BF16)
        h = jnp.dot(x, wgu_scr[s], preferred_element_type=F32)
        a = (jax.nn.silu(h[:, :D_FF_EXPERT]) * h[:, D_FF_EXPERT:]).astype(BF16)
        y_ref[...] = jnp.dot(a, wd_scr[s], preferred_element_type=F32) * x_ref[:, D_MODEL:D_MODEL + 1]

    @pl.when(jnp.logical_and(valid, jnp.logical_and(last_ref[i] == 1, nxt >= 0)))
    def _():
        for cp in fetch(nxt):
            cp.wait()
        round_into(1 - par_ref[i])

    @pl.when(jnp.logical_not(valid))
    def _():
        y_ref[...] = jnp.zeros_like(y_ref)


def _moe(sched, xs, wg, wu, wd):
    grid_spec = pltpu.PrefetchScalarGridSpec(
        num_scalar_prefetch=6,
        grid=(N_MOE_TILES,),
        in_specs=[pl.BlockSpec((TM_MOE, XS_WIDTH), lambda i, *_: (i, 0)),
                  pl.BlockSpec(memory_space=pl.ANY), pl.BlockSpec(memory_space=pl.ANY),
                  pl.BlockSpec(memory_space=pl.ANY)],
        out_specs=pl.BlockSpec((TM_MOE, D_MODEL), lambda i, *_: (i, 0)),
        scratch_shapes=[pltpu.VMEM((D_MODEL, D_FF_EXPERT), F32), pltpu.VMEM((D_MODEL, D_FF_EXPERT), F32),
                        pltpu.VMEM((D_FF_EXPERT, D_MODEL), F32),
                        pltpu.VMEM((2, D_MODEL, 2 * D_FF_EXPERT), BF16), pltpu.VMEM((2, D_FF_EXPERT, D_MODEL), BF16),
                        pltpu.SemaphoreType.DMA((3,))],
    )
    return pl.pallas_call(
        _moe_kernel,
        grid_spec=grid_spec,
        out_shape=jax.ShapeDtypeStruct((P_PAD, D_MODEL), F32),
        compiler_params=pltpu.CompilerParams(dimension_semantics=("arbitrary",), vmem_limit_bytes=VMEM_LIMIT),
        name="moe_experts",
    )(sched["expert"], sched["n_valid"], sched["next"], sched["parity"], sched["first"], sched["last"],
      xs, wg, wu, wd)


def _combine_kernel(base_ref, offp_ref, np_ref, tot_ref, x_ref, route_ref, y_hbm, fn_ref, o_ref, ybuf, sem):
    i = pl.program_id(0)
    slot = i % 2

    def copy(buf_slot):
        return lambda s, d, z: pltpu.make_async_copy(y_hbm.at[_aligned(d, z), :],
                                                     ybuf.at[buf_slot, _aligned(s, z), :], sem.at[buf_slot])

    @pl.when(i == 0)
    def _():
        ybuf[...] = jnp.zeros_like(ybuf)
        _start_slab_copies(0, base_ref, offp_ref, np_ref, copy(0))

    @pl.when(i + 1 < pl.num_programs(0))
    def _():
        _start_slab_copies(i + 1, base_ref, offp_ref, np_ref, copy(1 - slot))

    route = route_ref[...]
    e0 = route[:, 0:1].astype(jnp.int32)
    e1 = route[:, 1:2].astype(jnp.int32)
    eid =lax.broadcasted_iota(jnp.int32, (TT, LANES), 1)
    sel = jnp.where(e0 == eid, 1.0, jnp.where(e1 == eid, 1.0, 0.0)).astype(BF16)
    r = lax.broadcasted_iota(jnp.int32, (TT, TT), 0)
    c = lax.broadcasted_iota(jnp.int32, (TT, TT), 1)
    earlier = jnp.where(c < r, 1.0, 0.0).astype(BF16)
    rank = jnp.dot(earlier, sel, preferred_element_type=F32)
    erow = lax.broadcasted_iota(jnp.int32, (1, LANES), 1)
    offp = jnp.zeros((1, LANES), F32)
    for e in range(N_EXPERTS):
        offp = jnp.where(erow == e, offp_ref[i * N_EXPERTS + e].astype(F32), offp)
    slot_all = rank + offp
    slot0 = jnp.sum(jnp.where(e0 == eid, slot_all, 0.0), axis=1, keepdims=True)
    slot1 = jnp.sum(jnp.where(e1 == eid, slot_all, 0.0), axis=1, keepdims=True)
    sidx = lax.broadcasted_iota(jnp.int32, (TT, SLOTS), 1).astype(F32)
    pick = jnp.where(sidx == slot0, 1.0, jnp.where(sidx == slot1, 1.0, 0.0)).astype(BF16)

    _wait_rows(tot_ref[i], copy(slot))
    y = ybuf[slot]
    yh = y.astype(BF16)
    yl = (y - yh.astype(F32)).astype(BF16)
    moe = jnp.dot(pick, yh, preferred_element_type=F32) + jnp.dot(pick, yl, preferred_element_type=F32)
    o_ref[...] = _rms(x_ref[...] + moe, fn_ref[...])


def _combine(tables, x2, route, y_sorted, fn):
    grid_spec = pltpu.PrefetchScalarGridSpec(
        num_scalar_prefetch=4,
        grid=(N_TT,),
        in_specs=[pl.BlockSpec((TT, D_MODEL), lambda i, *_: (i, 0)),
                  pl.BlockSpec((TT, LANES), lambda i, *_: (i, 0)),
                  pl.BlockSpec(memory_space=pl.ANY),
                  pl.BlockSpec((1, D_MODEL), lambda i, *_: (0, 0))],
        out_specs=pl.BlockSpec((TT, D_MODEL), lambda i, *_: (i, 0)),
        scratch_shapes=[pltpu.VMEM((2, SLOTS, D_MODEL), F32), pltpu.SemaphoreType.DMA((2,))],
    )
    return pl.pallas_call(
        _combine_kernel,
        grid_spec=grid_spec,
        out_shape=jax.ShapeDtypeStruct((TOKENS, D_MODEL), F32),
        compiler_params=pltpu.CompilerParams(dimension_semantics=("arbitrary",), vmem_limit_bytes=VMEM_LIMIT),
        name="moe_combine",
    )(tables["base"], tables["offp"], tables["np"], tables["total"], x2, route, y_sorted, fn)


def _routing_tables(route_t):
    experts = route_t[:TOP_K, :].astype(jnp.int32)
    eid = jnp.arange(N_EXPERTS, dtype=jnp.int32)[:, None, None]
    chosen = jnp.sum((experts[None, :, :] == eid).astype(jnp.int32), axis=1)
    cnt = jnp.sum(chosen.reshape(N_EXPERTS, N_TT, TT), axis=2).T
    npad = (cnt + SUBLANES - 1) // SUBLANES * SUBLANES
    offp = jnp.cumsum(npad, axis=1) - npad
    total = jnp.sum(npad, axis=1)
    group = jnp.sum(npad, axis=0)
    group_pad = (group + TM_MOE - 1) // TM_MOE * TM_MOE
    group_end = jnp.cumsum(group_pad)
    group_start = group_end - group_pad
    base = group_start[None, :] + jnp.cumsum(npad, axis=0) - npad
    tail_start = group_start + group
    is_last = jnp.arange(N_EXPERTS) == N_EXPERTS - 1
    tail_len = jnp.where(is_last, P_PAD - tail_start, group_pad - group)
    tile_start = jnp.arange(N_MOE_TILES, dtype=jnp.int32) * TM_MOE
    tile_expert = jnp.minimum(jnp.sum((tile_start[:, None] >= group_end[None, :]).astype(jnp.int32), axis=1),
                              N_EXPERTS - 1)
    ids = jnp.arange(N_EXPERTS, dtype=jnp.int32)
    nonempty = group_pad > 0
    ordinal = jnp.cumsum(nonempty.astype(jnp.int32)) - 1
    later = jnp.logical_and(nonempty[None, :], ids[None, :] > ids[:, None])
    next_expert = jnp.min(jnp.where(later, ids[None, :], N_EXPERTS), axis=1)
    next_expert = jnp.where(next_expert == N_EXPERTS, -1, next_expert)
    sched = {"expert": tile_expert, "n_valid": (group_end[-1] // TM_MOE).reshape(1),
             "next": next_expert[tile_expert], "parity": ordinal[tile_expert] % 2,
             "first": tile_start == group_start[tile_expert],
             "last": tile_start + TM_MOE == group_end[tile_expert]}
    tables = {"base": base.reshape(-1), "offp": offp.reshape(-1), "np": npad.reshape(-1), "total": total,
              "tail": jnp.concatenate([tail_start, tail_len])}
    as_i32 = lambda d: {k: v.astype(jnp.int32) for k, v in d.items()}
    return as_i32(tables), as_i32(sched)


WIN_ROWS = 256


def _win_layout_kernel(wt_ref, o_ref):
    wt = wt_ref[0]
    cols = wt.shape[1]
    kr_tile = jnp.concatenate([jnp.zeros((QK_NOPE, cols), F32), wt[REF_KR:REF_GATE, :],
                               jnp.zeros((HEAD_PAD - QK_DIM, cols), F32)], axis=0)
    grouped = jnp.concatenate([wt[0:REF_CQ, :], wt[REF_GATE:, :], wt[REF_CQ:REF_KR, :], kr_tile], axis=0)
    o_ref[0] = grouped.T.astype(BF16)


def _win_layout(w_in):
    wt = jnp.swapaxes(w_in, 1, 2)
    d_in = wt.shape[1]
    return pl.pallas_call(
        _win_layout_kernel,
        grid=(DEPTH, D_MODEL // WIN_ROWS),
        in_specs=[pl.BlockSpec((1, d_in, WIN_ROWS), lambda l, r: (l, 0, r))],
        out_specs=pl.BlockSpec((1, WIN_ROWS, D_IN_PAD), lambda l, r: (l, r, 0)),
        out_shape=jax.ShapeDtypeStruct((DEPTH, D_MODEL, D_IN_PAD), BF16),
        compiler_params=pltpu.CompilerParams(dimension_semantics=("arbitrary",) * 2, vmem_limit_bytes=VMEM_LIMIT),
        name="win_layout",
    )(wt)


def _layer_weights(w_uq, w_ukv):
    wq = w_uq.T.astype(BF16)

    wkv = w_ukv.reshape(KV_LORA, N_HEADS, QK_NOPE + V_HEAD)
    wuk = jnp.concatenate([wkv[:, :, :QK_NOPE], jnp.zeros((KV_LORA, N_HEADS, HEAD_PAD - QK_NOPE), F32)], axis=2)
    wuk = wuk.reshape(KV_LORA, N_HEADS * HEAD_PAD).astype(BF16)
    wuv = wkv[:, :, QK_NOPE:].reshape(KV_LORA, N_HEADS * V_HEAD).T.astype(BF16)
    return wq, wuk, wuv


def kernel(x, positions, attn_norm, w_in, conv_w, w_conv_out, q_norm, w_uq, kv_norm, w_ukv, w_mla_out, w_o, ffn_norm,
           w_gate, w_up, w_down, router, w_gate_e, w_up_e, w_down_e, final_norm):
    assert x.shape == (BATCH, SEQ, D_MODEL) and positions.shape == (BATCH, SEQ)
    cos_t, sin_t = _rope_tables(positions)
    xt = x.reshape(TOKENS, D_MODEL)
    out = None
    win_all = _win_layout(w_in)
    for l in range(DEPTH):
        wq, wuk, wuv = _layer_weights(w_uq[l], w_ukv[l])
        q, k, v, gc, sg = _front(l, xt, attn_norm[l].reshape(1, D_MODEL), win_all, conv_w[l],
                                 w_conv_out[l].astype(BF16),
                                 q_norm[l].reshape(1, Q_LORA), wq, kv_norm[l].reshape(1, KV_LORA), wuk, wuv,
                                 cos_t, sin_t)
        attn = _attention(q, k, v)
        wmo = w_mla_out[l].astype(BF16)
        wo = w_o[l].astype(BF16)
        fn = ffn_norm[l].reshape(1, D_MODEL)
        if l % 2 == 0:
            j = l // 2
            xt = _merge(attn, gc, sg, xt, wmo, wo, fn,
                        ffn=(w_gate[j].astype(BF16), w_up[j].astype(BF16), w_down[j].astype(BF16)))
        else:
            j = l // 2
            router_pad = jnp.pad(router[j].T, ((0, E_PAD - N_EXPERTS), (0, 0)))
            x2, hn, route, route_t = _merge(attn, gc, sg, xt, wmo, wo, fn, router_pad)
            tables, sched = _routing_tables(route_t)
            xs = _dispatch(tables, hn, route_t)
            y_sorted = _moe(sched, xs, w_gate_e[j], w_up_e[j], w_down_e[j])
            out = _combine(tables, x2, route, y_sorted, final_norm.reshape(1, D_MODEL))
    return out.reshape(BATCH, SEQ, D_MODEL)
```

```python
import math

import jax
import jax.numpy as jnp
from jax import lax
from jax.experimental import pallas as pl
from jax.experimental.pallas import tpu as pltpu

F32 = jnp.float32
BF16 = jnp.bfloat16

D_MODEL = 1024
BATCH = 8
SEQ = 2048
TOKENS = BATCH * SEQ
DEPTH = 2
D_CONV = 512
CONV_WIDTH = 3
N_HEADS = 8
QK_NOPE = 64
QK_ROPE = 32
HALF_ROPE = QK_ROPE // 2
V_HEAD = 64
Q_LORA = 384
KV_LORA = 256
ROPE_THETA = 10000.0
D_FF = 2816
N_EXPERTS = 8
TOP_K = 2
D_FF_EXPERT = 1408
EPS = 1e-6

LANES = 128
HEAD_PAD = LANES
QK_DIM = QK_NOPE + QK_ROPE
Q_SCALE = (1.0 / math.sqrt(QK_DIM)) * math.log2(math.e)
NEG_BIG = -1e30

OFF_BCU = 0
OFF_GATE = 3 * D_CONV
OFF_SMALL = OFF_GATE + 2 * D_MODEL
D_IN_PAD = OFF_SMALL + Q_LORA + KV_LORA + LANES
REF_CQ = 3 * D_CONV
REF_KR = REF_CQ + Q_LORA + KV_LORA
REF_GATE = REF_KR + QK_ROPE

CONV_CHUNK = 256
TM_FRONT = 512
TQ = 256
TK = 512
ATTN_PAIRS = 2
ATTN_LOOKAHEAD = 3
V_ROWS = V_HEAD + 16
FF_CHUNK = 256
VMEM_LIMIT = 56 * 1024 * 1024

SUBLANES = 8
TM_MOE = 256
TT = 256
N_TT = TOKENS // TT
E_PAD = 16
XS_WIDTH = D_MODEL + LANES
N_PAIRS = TOKENS * TOP_K
SLOTS =TT * TOP_K + N_EXPERTS * SUBLANES
_MAX_ROWS = N_PAIRS + N_TT * N_EXPERTS * (SUBLANES - 1) + N_EXPERTS * (TM_MOE - SUBLANES)
N_MOE_TILES = -(-_MAX_ROWS // TM_MOE)
P_PAD = N_MOE_TILES * TM_MOE
SLAB_PIECES = tuple(TT >> s for s in range(6))
WAIT_PIECES = (512,) + SLAB_PIECES
assert SLAB_PIECES[-1] == SUBLANES and SLOTS < 2 * WAIT_PIECES[0]


def _rms(x, g):
    return x * lax.rsqrt(jnp.mean(x * x, axis=-1, keepdims=True) + EPS) * g


def _const_spec(shape):
    nd = len(shape)
    return pl.BlockSpec(shape, lambda *_: (0,) * nd, pipeline_mode=pl.Buffered(1))


def _trig_kernel(pos_ref, invf_ref, cos_ref, sin_ref):
    ang = invf_ref[...] * pos_ref[...].astype(F32)
    cos_ref[...] = jnp.cos(ang)
    sin_ref[...] = jnp.sin(ang)


def _rope_tables(positions):
    inv_freq = ROPE_THETA ** (-jnp.arange(0, QK_ROPE, 2, dtype=F32) / QK_ROPE)
    return pl.pallas_call(
        _trig_kernel,
        out_shape=(jax.ShapeDtypeStruct((HALF_ROPE, TOKENS), F32),) * 2,
        name="rope_trig",
    )(positions.reshape(1, TOKENS), inv_freq.reshape(HALF_ROPE, 1))


def _dot_nt(a, b):
    return lax.dot_general(a, b, (((1,), (1,)), ((), ())), preferred_element_type=F32)


def _front_kernel(x_ref, an_ref, win_ref, cw_ref, wco_ref, qn_ref, wq_ref, kvn_ref, wuk_ref, wuv_ref,
                  cost_ref, sint_ref, q_out, k_out, v_out, gc_out, sg_out, cu_scr):
    tm = x_ref.shape[0]
    i = pl.program_id(0)
    xn = _rms(x_ref[...], an_ref[...]).astype(BF16)

    def proj(a, b):
        return jnp.dot(xn, win_ref[0, :, a:b], preferred_element_type=F32)

    small = proj(OFF_SMALL, D_IN_PAD)

    cq = cost_ref[...] * Q_SCALE
    sq = sint_ref[...] * Q_SCALE
    cqn = _rms(small[:, :Q_LORA], qn_ref[...]).astype(BF16)
    qt = _dot_nt(wq_ref[...], cqn)
    zero_pad = jnp.zeros((HEAD_PAD - QK_DIM, tm), BF16)
    for h in range(N_HEADS):
        src = h * QK_DIM
        dst = h * HEAD_PAD
        x1 = qt[src + QK_NOPE:src + QK_NOPE + HALF_ROPE, :]
        x2 = qt[src + QK_NOPE + HALF_ROPE:src + QK_DIM, :]
        q_out[dst:dst + QK_NOPE, :] = (qt[src:src + QK_NOPE, :] * Q_SCALE).astype(BF16)
        q_out[dst + QK_NOPE:dst + QK_NOPE + HALF_ROPE, :] = (x1 * cq - x2 * sq).astype(BF16)
        q_out[dst + QK_NOPE + HALF_ROPE:dst + QK_DIM, :] = (x2 * cq + x1 * sq).astype(BF16)
        q_out[dst + QK_DIM:dst + HEAD_PAD, :] = zero_pad

    cos_t = cost_ref[...]
    sin_t = sint_ref[...]
    z_nope = jnp.zeros((QK_NOPE, tm), F32)
    z_rope = jnp.zeros((HALF_ROPE, tm), F32)
    z_pad = jnp.zeros((HEAD_PAD - QK_DIM, tm), F32)
    ck = jnp.concatenate([z_nope, cos_t, cos_t, z_pad], axis=0).T
    sk_x1 = jnp.concatenate([z_nope, -sin_t, z_rope, z_pad], axis=0).T
    sk_x2 = jnp.concatenate([z_nope, z_rope, sin_t, z_pad], axis=0).T
    kr = small[:, Q_LORA + KV_LORA:]
    kpe = (kr * ck + pltpu.roll(kr, HEAD_PAD - HALF_ROPE, axis=1) * sk_x1 + pltpu.roll(kr, HALF_ROPE, axis=1) * sk_x2)
    ckvn = _rms(small[:, Q_LORA:Q_LORA + KV_LORA], kvn_ref[...]).astype(BF16)
    kn = jnp.dot(ckvn, wuk_ref[...], preferred_element_type=F32)
    for h in range(N_HEADS):
        blk = slice(h * HEAD_PAD, (h + 1) * HEAD_PAD)
        k_out[:, blk] = (kn[:, blk] + kpe).astype(BF16)
    vt = _dot_nt(wuv_ref[...], ckvn).astype(BF16)
    sub = lax.broadcasted_iota(jnp.int32, (V_ROWS - V_HEAD, TK), 0)
    ones_rows = jnp.where(sub == 0, 1.0, 0.0).astype(BF16)
    for c in range(tm // TK):
        for h in range(N_HEADS):
            v_out[c, h * V_ROWS:h * V_ROWS + V_HEAD, :] = vt[h * V_HEAD:(h + 1) * V_HEAD, c * TK:(c + 1) * TK]
            v_out[c, h * V_ROWS + V_HEAD:(h + 1) * V_ROWS, :] = ones_rows

    @pl.when(i % (SEQ // tm) == 0)
    def _():
        cu_scr[0:8, :] = jnp.zeros((8, D_CONV), F32)

    cw = cw_ref[...]
    y_conv = None
    for c in range(D_CONV // CONV_CHUNK):
        ch = slice(c * CONV_CHUNK, (c + 1) * CONV_CHUNK)
        bcu = proj(OFF_BCU + 3 * c * CONV_CHUNK, OFF_BCU + 3 * (c + 1) * CONV_CHUNK)
        b_g = bcu[:, :CONV_CHUNK]
        cu = bcu[:, CONV_CHUNK:2 * CONV_CHUNK] * bcu[:, 2 * CONV_CHUNK:]
        cu_scr[8:8 + tm, ch] = cu
        prev1 = cu_scr[7:7 + tm, ch]
        prev2 = cu_scr[6:6 + tm, ch]
        conv = prev2 * cw[0:1, ch] + prev1 * cw[1:2, ch] + cu * cw[2:3, ch]
        cu_scr[0:8, ch] = cu[tm - 8:, :]
        part = jnp.dot((b_g * conv).astype(BF16), wco_ref[ch, :], preferred_element_type=F32)
        y_conv = part if y_conv is None else y_conv + part

    gates = proj(OFF_GATE, OFF_SMALL)
    gc_out[...] = (jax.nn.sigmoid(gates[:, :D_MODEL]) * y_conv).astype(BF16)
    sg_out[...] = jax.nn.sigmoid(gates[:, D_MODEL:]).astype(BF16)


def _front(layer, x, an, win_all, cw, wco, qn, wq, kvn, wuk, wuv, cos_t, sin_t):
    tm = TM_FRONT
    row = lambda n: pl.BlockSpec((tm, n), lambda i: (i, 0))
    col = lambda n: pl.BlockSpec((n, tm), lambda i: (0, i))
    win_spec = pl.BlockSpec((1, D_MODEL, D_IN_PAD), lambda i: (layer, 0, 0), pipeline_mode=pl.Buffered(1))
    return pl.pallas_call(
        _front_kernel,
        grid=(TOKENS // tm,),
        in_specs=[row(D_MODEL), _const_spec((1, D_MODEL)), win_spec,
                  _const_spec((CONV_WIDTH, D_CONV)), _const_spec((D_CONV, D_MODEL)),
                  _const_spec((1, Q_LORA)), _const_spec((N_HEADS * QK_DIM, Q_LORA)),
                  _const_spec((1, KV_LORA)), _const_spec((KV_LORA, N_HEADS * HEAD_PAD)),
                  _const_spec((N_HEADS * V_HEAD, KV_LORA)),
                  col(HALF_ROPE), col(HALF_ROPE)],
        out_specs=[col(N_HEADS * HEAD_PAD), row(N_HEADS * HEAD_PAD),
                   pl.BlockSpec((tm // TK, N_HEADS * V_ROWS, TK), lambda i: (i, 0, 0)),
                   row(D_MODEL), row(D_MODEL)],
        out_shape=[jax.ShapeDtypeStruct((N_HEADS * HEAD_PAD, TOKENS), BF16),
                   jax.ShapeDtypeStruct((TOKENS, N_HEADS * HEAD_PAD), BF16),
                   jax.ShapeDtypeStruct((TOKENS // TK, N_HEADS * V_ROWS, TK), BF16),
                   jax.ShapeDtypeStruct((TOKENS, D_MODEL), BF16),
                   jax.ShapeDtypeStruct((TOKENS, D_MODEL), BF16)],
        scratch_shapes=[pltpu.VMEM((tm + 8, D_CONV), F32)],
        compiler_params=pltpu.CompilerParams(dimension_semantics=("arbitrary",), vmem_limit_bytes=VMEM_LIMIT),
        name="front",
    )(x, an, win_all, cw, wco, qn, wq, kvn, wuk, wuv, cos_t, sin_t)


def _attn_items():
    items = []
    for pair in range(ATTN_PAIRS):
        for i in range(SEQ // TQ):
            ks = 0
            while ks + TK <= i * TQ:
                items.append((pair, i, ks, TK, False))
                ks += TK
            items.append((pair, i, ks, (i + 1) * TQ - ks, True))
    return items


def _attn_kernel(qt_ref, k_ref, vt_ref, o_ref):
    row = lax.broadcasted_iota(jnp.int32, (TQ, TQ), 0)
    col = lax.broadcasted_iota(jnp.int32, (TQ, TQ), 1)
    causal = row <= col

    def scores(item):
        pair, i, ks, n, diag = item
        sts = []
        for h in range(2 * pair, 2 * pair + 2):
            hs = slice(h * HEAD_PAD, (h + 1) * HEAD_PAD)
            st = jnp.dot(k_ref[ks:ks + n, hs], qt_ref[hs, i * TQ:(i + 1) * TQ], preferred_element_type=F32)
            if diag:
                tail = jnp.where(causal, st[n - TQ:, :], NEG_BIG)
                st = tail if n == TQ else jnp.concatenate([st[:n - TQ, :], tail], axis=0)
            sts.append(st)
        return sts

    def consume(item, sts, carry):
        pair, i, ks, n, _ = item
        stats = []
        for j in range(2):
            m, _ = carry[j]
            m_new = jnp.maximum(m, jnp.max(sts[j], axis=0, keepdims=True))
            stats.append((m_new, jnp.exp2(m - m_new), jnp.exp2(sts[j] - m_new).astype(BF16)))
        new = []
        for j in range(2):
            h = 2 * pair + j
            m_new, alpha, p = stats[j]
            vt = vt_ref[ks // TK, h * V_ROWS:(h + 1) * V_ROWS, ks % TK:ks % TK + n]
            new.append((m_new, alpha * carry[j][1] + jnp.dot(vt, p, preferred_element_type=F32)))
        return new

    items = _attn_items()
    init = [(jnp.full((1, TQ), NEG_BIG, F32), jnp.zeros((V_ROWS, TQ), F32)) for _ in range(2)]
    pending = [scores(it) for it in items[:ATTN_LOOKAHEAD]]
    carry = [init] * ATTN_PAIRS
    for t, item in enumerate(items):
        if t + ATTN_LOOKAHEAD < len(items):
            pending.append(scores(items[t + ATTN_LOOKAHEAD]))
        pair, i = item[0], item[1]
        carry[pair] = consume(item, pending.pop(0), carry[pair])
        if item[4]:
            out_t = jnp.concatenate([acc[0:V_HEAD] / acc[V_HEAD:V_HEAD + 1] for (_, acc) in carry[pair]], axis=0)
            o_ref[i * TQ:(i + 1) * TQ, pair * 2 * V_HEAD:(pair + 1) * 2 * V_HEAD] = out_t.T.astype(BF16)
            carry[pair] = init


def _attention(qt, k, vt):
    heads = 2 * ATTN_PAIRS
    return pl.pallas_call(
        _attn_kernel,
        grid=(BATCH, N_HEADS // heads),
        in_specs=[pl.BlockSpec((heads * HEAD_PAD, SEQ), lambda b, hp: (hp, b)),
                  pl.BlockSpec((SEQ, heads * HEAD_PAD), lambda b, hp: (b, hp)),
                  pl.BlockSpec((SEQ // TK, heads * V_ROWS, TK), lambda b, hp: (b, hp, 0))],
        out_specs=pl.BlockSpec((SEQ, heads * V_HEAD), lambda b, hp: (b, hp)),
        out_shape=jax.ShapeDtypeStruct((TOKENS, N_HEADS * V_HEAD), BF16),
        compiler_params=pltpu.CompilerParams(dimension_semantics=("arbitrary",) * 2, vmem_limit_bytes=VMEM_LIMIT),
        name="attention",
    )(qt, k, vt)


def _merge_core(attn_ref, gc_ref, sg_ref, x_ref, wmo_ref, wo_ref, fn_ref):
    y_mla = jnp.dot(attn_ref[...], wmo_ref[...], preferred_element_type=F32)
    merged = gc_ref[...].astype(F32) + sg_ref[...].astype(F32) * y_mla
    x2 = x_ref[...] + jnp.dot(merged.astype(BF16), wo_ref[...], preferred_element_type=F32)
    return x2, _rms(x2, fn_ref[...])


def _merge_ffn_kernel(attn_ref, gc_ref, sg_ref, x_ref, wmo_ref, wo_ref, fn_ref, wg_ref, wu_ref, wd_ref, o_ref):
    x2, hn = _merge_core(attn_ref, gc_ref, sg_ref, x_ref, wmo_ref, wo_ref, fn_ref)
    hn = hn.astype(BF16)
    acc = x2
    for c in range(D_FF // FF_CHUNK):
        cs = slice(c * FF_CHUNK, (c + 1) * FF_CHUNK)
        g = jnp.dot(hn, wg_ref[:, cs], preferred_element_type=F32)
        u = jnp.dot(hn, wu_ref[:, cs], preferred_element_type=F32)
        a = (jax.nn.silu(g) * u).astype(BF16)
        acc = acc + jnp.dot(a, wd_ref[cs, :], preferred_element_type=F32)
    o_ref[...] = acc


def _merge_route_kernel(attn_ref, gc_ref, sg_ref, x_ref, wmo_ref, wo_ref, fn_ref, router_ref,
                        x2_out, hn_out, route_out, route_t_out):
    x2, hn = _merge_core(attn_ref, gc_ref, sg_ref, x_ref, wmo_ref, wo_ref, fn_ref)
    x2_out[...] = x2
    hb = hn.astype(BF16)
    hn_out[...] = hb
    hl = (hn - hb.astype(F32)).astype(BF16)
    r = router_ref[...]
    rh = r.astype(BF16)
    rl = (r - rh.astype(F32)).astype(BF16)
    logits = _dot_nt(rh, hb) + _dot_nt(rh, hl) + _dot_nt(rl, hb)
    erow = lax.broadcasted_iota(jnp.int32, logits.shape, 0)
    logits = jnp.where(erow < N_EXPERTS, logits, -jnp.inf)
    m1 = jnp.max(logits, axis=0, keepdims=True)
    i1 = jnp.min(jnp.where(logits == m1, erow, E_PAD), axis=0, keepdims=True)
    rest = jnp.where(erow == i1, -jnp.inf, logits)
    m2 = jnp.max(rest, axis=0, keepdims=True)
    i2 = jnp.min(jnp.where(rest == m2, erow, E_PAD), axis=0, keepdims=True)
    t = jnp.exp(m2 - m1)
    w1 = 1.0 / (1.0 + t)
    w2 = t * w1
    route_t = jnp.where(erow == 0, i1.astype(F32),
                        jnp.where(erow == 1, i2.astype(F32), jnp.where(erow == 2, w1, jnp.where(erow == 3, w2, 0.0))))
    route_t_out[...] = route_t[0:SUBLANES, :]
    pad = jnp.zeros((LANES - E_PAD, route_t.shape[1]), F32)
    route_out[...] = jnp.concatenate([route_t, pad], axis=0).T


def _merge(attn, gc, sg, x, wmo, wo, fn, router=None, ffn=None):
    tm = TM_FRONT
    row = lambda n: pl.BlockSpec((tm, n), lambda i: (i, 0))
    in_specs = [row(N_HEADS * V_HEAD), row(D_MODEL), row(D_MODEL), row(D_MODEL),
                _const_spec((N_HEADS * V_HEAD, D_MODEL)), _const_spec((D_MODEL, D_MODEL)), _const_spec((1, D_MODEL))]
    args = [attn, gc, sg, x, wmo, wo, fn]
    if router is None:
        kern = _merge_ffn_kernel
        in_specs += [_const_spec((D_MODEL, D_FF)), _const_spec((D_MODEL, D_FF)), _const_spec((D_FF, D_MODEL))]
        args += list(ffn)
        out_specs = row(D_MODEL)
        out_shape = jax.ShapeDtypeStruct((TOKENS, D_MODEL), F32)
        name = "merge_ffn"
    else:
        kern = _merge_route_kernel
        in_specs.append(_const_spec((E_PAD, D_MODEL)))
        args.append(router)
        out_specs = [row(D_MODEL), row(D_MODEL), row(LANES), pl.BlockSpec((SUBLANES, tm), lambda i: (0, i))]
        out_shape = [jax.ShapeDtypeStruct((TOKENS, D_MODEL), F32), jax.ShapeDtypeStruct((TOKENS, D_MODEL), BF16),
                     jax.ShapeDtypeStruct((TOKENS, LANES), F32), jax.ShapeDtypeStruct((SUBLANES, TOKENS), F32)]
        name = "merge_route"
    return pl.pallas_call(
        kern, grid=(TOKENS // tm,), in_specs=in_specs, out_specs=out_specs, out_shape=out_shape,
        compiler_params=pltpu.CompilerParams(dimension_semantics=("arbitrary",), vmem_limit_bytes=VMEM_LIMIT),
        name=name,
    )(*args)


def _pieces(n, sizes):
    return [((n & z) != 0, n & ~(2 * z - 1), z) for z in sizes]


def _aligned(rows, z):
    return pl.ds(rows if isinstance(rows, int) else pl.multiple_of(rows, SUBLANES), z)


def _start_slab_copies(tile, base_ref, offp_ref, np_ref, make_copy):
    for e in range(N_EXPERTS):
        n = np_ref[tile * N_EXPERTS + e]
        src = offp_ref[tile * N_EXPERTS + e]
        dst = base_ref[tile * N_EXPERTS + e]
        for pred, off, z in _pieces(n, SLAB_PIECES):
            @pl.when(pred)
            def _(off=off, z=z):
                make_copy(src + off, dst + off, z).start()


def _wait_rows(total, make_copy):
    for pred, _, z in _pieces(total, WAIT_PIECES):
        @pl.when(pred)
        def _(z=z):
            make_copy(0, 0, z).wait()


def _dispatch_kernel(base_ref, offp_ref, np_ref, tot_ref, tail_ref, hn_ref, rt_ref, xs_hbm, xs_scr, zbuf, sem, zsem):
    i = pl.program_id(0)
    slot = i % 2

    def copy(buf_slot):
        return lambda s, d, z: pltpu.make_async_copy(xs_scr.at[buf_slot, _aligned(s, z), :],
                                                     xs_hbm.at[_aligned(d, z), :], sem.at[buf_slot])

    def zero_copy(d, z):
        return pltpu.make_async_copy(zbuf.at[pl.ds(0, z), :], xs_hbm.at[_aligned(d, z), :], zsem)

    @pl.when(i == 0)
    def _():
        zbuf[...] = jnp.zeros_like(zbuf)
        for phase in ("start", "wait"):
            for e in range(N_EXPERTS):
                t0 = tail_ref[e]
                n = tail_ref[N_EXPERTS + e]
                full = lax.shift_right_logical(n, jnp.int32(TM_MOE.bit_length() - 1))

                def chunk(c, carry, t0=t0, phase=phase):
                    cp = zero_copy(t0 + c * TM_MOE, TM_MOE)
                    cp.start() if phase == "start" else cp.wait()
                    return carry

                lax.fori_loop(0, full, chunk, 0)
                rest = n & (TM_MOE - 1)
                for pred, off, z in _pieces(rest, SLAB_PIECES[1:]):
                    @pl.when(pred)
                    def _(off=off, z=z, t0=t0, full=full, phase=phase):
                        cp = zero_copy(t0 + full * TM_MOE + off, z)
                        cp.start() if phase == "start" else cp.wait()

    e0 = rt_ref[0:1, :].astype(jnp.int32)
    e1 = rt_ref[1:2, :].astype(jnp.int32)
    eid = lax.broadcasted_iota(jnp.int32, (E_PAD, TT), 0)
    sel = jnp.where(e0 == eid, 1.0, jnp.where(e1 == eid, 1.0, 0.0)).astype(BF16)
    r = lax.broadcasted_iota(jnp.int32, (TT, TT), 0)
    c = lax.broadcasted_iota(jnp.int32, (TT, TT), 1)
    earlier = jnp.where(r < c, 1.0, 0.0).astype(BF16)
    rank = jnp.dot(sel, earlier, preferred_element_type=F32)
    ecol = lax.broadcasted_iota(jnp.int32, (E_PAD, 1), 0)
    offp = jnp.zeros((E_PAD, 1), F32)
    for e in range(N_EXPERTS):
        offp = jnp.where(ecol == e, offp_ref[i * N_EXPERTS + e].astype(F32), offp)
    slot_all = rank + offp
    slot0 = jnp.sum(jnp.where(e0 == eid, slot_all, 0.0), axis=0, keepdims=True)
    slot1 = jnp.sum(jnp.where(e1 == eid, slot_all, 0.0), axis=0, keepdims=True)
    sidx = lax.broadcasted_iota(jnp.int32, (SLOTS, TT), 0).astype(F32)
    place = jnp.where(sidx == slot0, 1.0, jnp.where(sidx == slot1, 1.0, 0.0)).astype(BF16)
    xs_scr[slot, :, 0:D_MODEL] = jnp.dot(place, hn_ref[...], preferred_element_type=F32)
    w_slot = jnp.sum(jnp.where(sidx == slot0, rt_ref[2:3, :], jnp.where(sidx == slot1, rt_ref[3:4, :], 0.0)),
                     axis=1, keepdims=True)
    xs_scr[slot, :, D_MODEL:] = jnp.broadcast_to(w_slot, (SLOTS, LANES))

    _start_slab_copies(i, base_ref, offp_ref, np_ref, copy(slot))

    @pl.when(i > 0)
    def _():
        _wait_rows(tot_ref[i - 1], copy(1 - slot))

    @pl.when(i == pl.num_programs(0) - 1)
    def _():
        _wait_rows(tot_ref[i], copy(slot))


def _dispatch(tables, hn, route_t):
    grid_spec = pltpu.PrefetchScalarGridSpec(
        num_scalar_prefetch=5,
        grid=(N_TT,),
        in_specs=[pl.BlockSpec((TT, D_MODEL), lambda i, *_: (i, 0)),
                  pl.BlockSpec((SUBLANES, TT), lambda i, *_: (0, i))],
        out_specs=pl.BlockSpec(memory_space=pl.ANY),
        scratch_shapes=[pltpu.VMEM((2, SLOTS, XS_WIDTH), F32), pltpu.VMEM((TM_MOE, XS_WIDTH), F32),
                        pltpu.SemaphoreType.DMA((2,)), pltpu.SemaphoreType.DMA],
    )
    return pl.pallas_call(
        _dispatch_kernel,
        grid_spec=grid_spec,
        out_shape=jax.ShapeDtypeStruct((P_PAD, XS_WIDTH), F32),
        compiler_params=pltpu.CompilerParams(dimension_semantics=("arbitrary",), vmem_limit_bytes=VMEM_LIMIT),
        name="moe_dispatch",
    )(tables["base"], tables["offp"], tables["np"], tables["total"], tables["tail"], hn, route_t)


def _moe_kernel(te_ref, nv_ref, nxt_ref, par_ref, first_ref, last_ref, x_ref, wg_hbm, wu_hbm, wd_hbm, y_ref,
                stage_g, stage_u, stage_d, wgu_scr, wd_scr, sem):
    i = pl.program_id(0)
    valid = i < nv_ref[0]
    nxt = nxt_ref[i]

    def fetch(e):
        return [pltpu.make_async_copy(wg_hbm.at[e], stage_g, sem.at[0]),
                pltpu.make_async_copy(wu_hbm.at[e], stage_u, sem.at[1]),
                pltpu.make_async_copy(wd_hbm.at[e], stage_d, sem.at[2])]

    def round_into(s):
        wgu_scr[s, :, 0:D_FF_EXPERT] = stage_g[...].astype(BF16)
        wgu_scr[s, :, D_FF_EXPERT:] = stage_u[...].astype(BF16)
        wd_scr[s] = stage_d[...].astype(BF16)

    @pl.when(i == 0)
    def _():
        for cp in fetch(te_ref[0]):
            cp.start()
        for cp in fetch(te_ref[0]):
            cp.wait()
        round_into(par_ref[0])

    @pl.when(jnp.logical_and(valid, jnp.logical_and(first_ref[i] == 1, nxt >= 0)))
    def _():
        for cp in fetch(nxt):
            cp.start()

    @pl.when(valid)
    def _():
        s = par_ref[i]
        x = x_ref[:, 0:D_MODEL].astype(BF16)
        h = jnp.dot(x, wgu_scr[s], preferred_element_type=F32)
        a = (jax.nn.silu(h[:, :D_FF_EXPERT]) * h[:, D_FF_EXPERT:]).astype(BF16)
        y_ref[...] = jnp.dot(a, wd_scr[s], preferred_element_type=F32) * x_ref[:, D_MODEL:D_MODEL + 1]

    @pl.when(jnp.logical_and(valid, jnp.logical_and(last_ref[i] == 1, nxt >= 0)))
    def _():
        for cp in fetch(nxt):
            cp.wait()
        round_into(1 - par_ref[i])

    @pl.when(jnp.logical_not(valid))
    def _():
        y_ref[...] = jnp.zeros_like(y_ref)


def _moe(sched, xs, wg, wu, wd):
    grid_spec = pltpu.PrefetchScalarGridSpec(
        num_scalar_prefetch=6,
        grid=(N_MOE_TILES,),
        in_specs=[pl.BlockSpec((TM_MOE, XS_WIDTH), lambda i, *_: (i, 0)),
                  pl.BlockSpec(memory_space=pl.ANY), pl.BlockSpec(memory_space=pl.ANY),
                  pl.BlockSpec(memory_space=pl.ANY)],
        out_specs=pl.BlockSpec((TM_MOE, D_MODEL), lambda i, *_: (i, 0)),
        scratch_shapes=[pltpu.VMEM((D_MODEL, D_FF_EXPERT), F32), pltpu.VMEM((D_MODEL, D_FF_EXPERT), F32),
                        pltpu.VMEM((D_FF_EXPERT, D_MODEL), F32),
                        pltpu.VMEM((2, D_MODEL, 2 * D_FF_EXPERT), BF16), pltpu.VMEM((2, D_FF_EXPERT, D_MODEL), BF16),
                        pltpu.SemaphoreType.DMA((3,))],
    )
    return pl.pallas_call(
        _moe_kernel,
        grid_spec=grid_spec,
        out_shape=jax.ShapeDtypeStruct((P_PAD, D_MODEL), F32),
        compiler_params=pltpu.CompilerParams(dimension_semantics=("arbitrary",), vmem_limit_bytes=VMEM_LIMIT),
        name="moe_experts",
    )(sched["expert"], sched["n_valid"], sched["next"], sched["parity"], sched["first"], sched["last"],
      xs, wg, wu, wd)


def _combine_kernel(base_ref, offp_ref, np_ref, tot_ref, x_ref, route_ref, y_hbm, fn_ref, o_ref, ybuf, sem):
    i = pl.program_id(0)
    slot = i % 2

    def copy(buf_slot):
        return lambda s, d, z: pltpu.make_async_copy(y_hbm.at[_aligned(d, z), :],
                                                     ybuf.at[buf_slot, _aligned(s, z), :], sem.at[buf_slot])

    @pl.when(i == 0)
    def _():
        ybuf[...] = jnp.zeros_like(ybuf)
        _start_slab_copies(0, base_ref, offp_ref, np_ref, copy(0))

    @pl.when(i + 1 < pl.num_programs(0))
    def _():
        _start_slab_copies(i + 1, base_ref, offp_ref, np_ref, copy(1 - slot))

    route = route_ref[...]
    e0 = route[:, 0:1].astype(jnp.int32)
    e1 = route[:, 1:2].astype(jnp.int32)
    eid =lax.broadcasted_iota(jnp.int32, (TT, LANES), 1)
    sel = jnp.where(e0 == eid, 1.0, jnp.where(e1 == eid, 1.0, 0.0)).astype(BF16)
    r = lax.broadcasted_iota(jnp.int32, (TT, TT), 0)
    c = lax.broadcasted_iota(jnp.int32, (TT, TT), 1)
    earlier = jnp.where(c < r, 1.0, 0.0).astype(BF16)
    rank = jnp.dot(earlier, sel, preferred_element_type=F32)
    erow = lax.broadcasted_iota(jnp.int32, (1, LANES), 1)
    offp = jnp.zeros((1, LANES), F32)
    for e in range(N_EXPERTS):
        offp = jnp.where(erow == e, offp_ref[i * N_EXPERTS + e].astype(F32), offp)
    slot_all = rank + offp
    slot0 = jnp.sum(jnp.where(e0 == eid, slot_all, 0.0), axis=1, keepdims=True)
    slot1 = jnp.sum(jnp.where(e1 == eid, slot_all, 0.0), axis=1, keepdims=True)
    sidx = lax.broadcasted_iota(jnp.int32, (TT, SLOTS), 1).astype(F32)
    pick = jnp.where(sidx == slot0, 1.0, jnp.where(sidx == slot1, 1.0, 0.0)).astype(BF16)

    _wait_rows(tot_ref[i], copy(slot))
    y = ybuf[slot]
    yh = y.astype(BF16)
    yl = (y - yh.astype(F32)).astype(BF16)
    moe = jnp.dot(pick, yh, preferred_element_type=F32) + jnp.dot(pick, yl, preferred_element_type=F32)
    o_ref[...] = _rms(x_ref[...] + moe, fn_ref[...])


def _combine(tables, x2, route, y_sorted, fn):
    grid_spec = pltpu.PrefetchScalarGridSpec(
        num_scalar_prefetch=4,
        grid=(N_TT,),
        in_specs=[pl.BlockSpec((TT, D_MODEL), lambda i, *_: (i, 0)),
                  pl.BlockSpec((TT, LANES), lambda i, *_: (i, 0)),
                  pl.BlockSpec(memory_space=pl.ANY),
                  pl.BlockSpec((1, D_MODEL), lambda i, *_: (0, 0))],
        out_specs=pl.BlockSpec((TT, D_MODEL), lambda i, *_: (i, 0)),
        scratch_shapes=[pltpu.VMEM((2, SLOTS, D_MODEL), F32), pltpu.SemaphoreType.DMA((2,))],
    )
    return pl.pallas_call(
        _combine_kernel,
        grid_spec=grid_spec,
        out_shape=jax.ShapeDtypeStruct((TOKENS, D_MODEL), F32),
        compiler_params=pltpu.CompilerParams(dimension_semantics=("arbitrary",), vmem_limit_bytes=VMEM_LIMIT),
        name="moe_combine",
    )(tables["base"], tables["offp"], tables["np"], tables["total"], x2, route, y_sorted, fn)


def _routing_tables(route_t):
    experts = route_t[:TOP_K, :].astype(jnp.int32)
    eid = jnp.arange(N_EXPERTS, dtype=jnp.int32)[:, None, None]
    chosen = jnp.sum((experts[None, :, :] == eid).astype(jnp.int32), axis=1)
    cnt = jnp.sum(chosen.reshape(N_EXPERTS, N_TT, TT), axis=2).T
    npad = (cnt + SUBLANES - 1) // SUBLANES * SUBLANES
    offp = jnp.cumsum(npad, axis=1) - npad
    total = jnp.sum(npad, axis=1)
    group = jnp.sum(npad, axis=0)
    group_pad = (group + TM_MOE - 1) // TM_MOE * TM_MOE
    group_end = jnp.cumsum(group_pad)
    group_start = group_end - group_pad
    base = group_start[None, :] + jnp.cumsum(npad, axis=0) - npad
    tail_start = group_start + group
    is_last = jnp.arange(N_EXPERTS) == N_EXPERTS - 1
    tail_len = jnp.where(is_last, P_PAD - tail_start, group_pad - group)
    tile_start = jnp.arange(N_MOE_TILES, dtype=jnp.int32) * TM_MOE
    tile_expert = jnp.minimum(jnp.sum((tile_start[:, None] >= group_end[None, :]).astype(jnp.int32), axis=1),
                              N_EXPERTS - 1)
    ids = jnp.arange(N_EXPERTS, dtype=jnp.int32)
    nonempty = group_pad > 0
    ordinal = jnp.cumsum(nonempty.astype(jnp.int32)) - 1
    later = jnp.logical_and(nonempty[None, :], ids[None, :] > ids[:, None])
    next_expert = jnp.min(jnp.where(later, ids[None, :], N_EXPERTS), axis=1)
    next_expert = jnp.where(next_expert == N_EXPERTS, -1, next_expert)
    is_tile_expert = tile_expert[:, None] == ids[None, :]
    per_tile = lambda table: jnp.sum(jnp.where(is_tile_expert, table[None, :], 0), axis=1)
    sched = {"expert": tile_expert, "n_valid": (group_end[-1] // TM_MOE).reshape(1),
             "next": per_tile(next_expert), "parity": per_tile(ordinal) % 2,
             "first": tile_start == per_tile(group_start),
             "last": tile_start + TM_MOE == per_tile(group_end)}
    tables = {"base": base.reshape(-1), "offp": offp.reshape(-1), "np": npad.reshape(-1), "total": total,
              "tail": jnp.concatenate([tail_start, tail_len])}
    as_i32 = lambda d: {k: v.astype(jnp.int32) for k, v in d.items()}
    return as_i32(tables), as_i32(sched)


WIN_ROWS = 256


def _win_layout_kernel(wt_ref, o_ref):
    wt = wt_ref[0]
    cols = wt.shape[1]
    kr_tile = jnp.concatenate([jnp.zeros((QK_NOPE, cols), F32), wt[REF_KR:REF_GATE, :],
                               jnp.zeros((HEAD_PAD - QK_DIM, cols), F32)], axis=0)
    conv_rows = [wt[part * D_CONV + c * CONV_CHUNK:part * D_CONV + (c + 1) * CONV_CHUNK, :]
                 for c in range(D_CONV // CONV_CHUNK) for part in range(3)]
    grouped = jnp.concatenate(conv_rows + [wt[REF_GATE:, :], wt[REF_CQ:REF_KR, :], kr_tile], axis=0)
    o_ref[0] = grouped.T.astype(BF16)


def _win_layout(w_in):
    wt = jnp.swapaxes(w_in, 1, 2)
    d_in = wt.shape[1]
    return pl.pallas_call(
        _win_layout_kernel,
        grid=(DEPTH, D_MODEL // WIN_ROWS),
        in_specs=[pl.BlockSpec((1, d_in, WIN_ROWS), lambda l, r: (l, 0, r))],
        out_specs=pl.BlockSpec((1, WIN_ROWS, D_IN_PAD), lambda l, r: (l, r, 0)),
        out_shape=jax.ShapeDtypeStruct((DEPTH, D_MODEL, D_IN_PAD), BF16),
        compiler_params=pltpu.CompilerParams(dimension_semantics=("arbitrary",) * 2, vmem_limit_bytes=VMEM_LIMIT),
        name="win_layout",
    )(wt)


def _layer_weights(w_uq, w_ukv):
    wq = w_uq.T.astype(BF16)

    wkv = w_ukv.reshape(KV_LORA, N_HEADS, QK_NOPE + V_HEAD)
    wuk = jnp.concatenate([wkv[:, :, :QK_NOPE], jnp.zeros((KV_LORA, N_HEADS, HEAD_PAD - QK_NOPE), F32)], axis=2)
    wuk = wuk.reshape(KV_LORA, N_HEADS * HEAD_PAD).astype(BF16)
    wuv = wkv[:, :, QK_NOPE:].reshape(KV_LORA, N_HEADS * V_HEAD).T.astype(BF16)
    return wq, wuk, wuv


def kernel(x, positions, attn_norm, w_in, conv_w, w_conv_out, q_norm, w_uq, kv_norm, w_ukv, w_mla_out, w_o, ffn_norm,
           w_gate, w_up, w_down, router, w_gate_e, w_up_e, w_down_e, final_norm):
    assert x.shape == (BATCH, SEQ, D_MODEL) and positions.shape == (BATCH, SEQ)
    cos_t, sin_t = _rope_tables(positions)
    xt = x.reshape(TOKENS, D_MODEL)
    out = None
    win_all = _win_layout(w_in)
    for l in range(DEPTH):
        wq, wuk, wuv = _layer_weights(w_uq[l], w_ukv[l])
        q, k, v, gc, sg = _front(l, xt, attn_norm[l].reshape(1, D_MODEL), win_all, conv_w[l],
                                 w_conv_out[l].astype(BF16),
                                 q_norm[l].reshape(1, Q_LORA), wq, kv_norm[l].reshape(1, KV_LORA), wuk, wuv,
                                 cos_t, sin_t)
        attn = _attention(q, k, v)
        wmo = w_mla_out[l].astype(BF16)
        wo = w_o[l].astype(BF16)
        fn = ffn_norm[l].reshape(1, D_MODEL)
        if l % 2 == 0:
            j = l // 2
            xt = _merge(attn, gc, sg, xt, wmo, wo, fn,
                        ffn=(w_gate[j].astype(BF16), w_up[j].astype(BF16), w_down[j].astype(BF16)))
        else:
            j = l // 2
            router_pad = jnp.pad(router[j].T, ((0, E_PAD - N_EXPERTS), (0, 0)))
            x2, hn, route, route_t = _merge(attn, gc, sg, xt, wmo, wo, fn, router_pad)
            tables, sched = _routing_tables(route_t)
            xs = _dispatch(tables, hn, route_t)
            y_sorted = _moe(sched, xs, w_gate_e[j], w_up_e[j], w_down_e[j])
            out = _combine(tables, x2, route, y_sorted, final_norm.reshape(1, D_MODEL))
    return out.reshape(BATCH, SEQ, D_MODEL)
```

```python
import math

import jax
import jax.numpy as jnp
from jax import lax
from jax.experimental import pallas as pl
from jax.experimental.pallas import tpu as pltpu

F32 = jnp.float32
BF16 = jnp.bfloat16

D_MODEL = 1024
BATCH = 8
SEQ = 2048
TOKENS = BATCH * SEQ
DEPTH = 2
D_CONV = 512
CONV_WIDTH = 3
N_HEADS = 8
QK_NOPE = 64
QK_ROPE = 32
HALF_ROPE = QK_ROPE // 2
V_HEAD = 64
Q_LORA = 384
KV_LORA = 256
ROPE_THETA = 10000.0
D_FF = 2816
N_EXPERTS = 8
TOP_K = 2
D_FF_EXPERT = 1408
EPS = 1e-6

LANES = 128
HEAD_PAD = LANES
QK_DIM = QK_NOPE + QK_ROPE
Q_SCALE = (1.0 / math.sqrt(QK_DIM)) * math.log2(math.e)
NEG_BIG = -1e30

OFF_BCU = 0
OFF_GATE = 3 * D_CONV
OFF_SMALL = OFF_GATE + 2 * D_MODEL
D_IN_PAD = OFF_SMALL + Q_LORA + KV_LORA + LANES
REF_CQ = 3 * D_CONV
REF_KR = REF_CQ + Q_LORA + KV_LORA
REF_GATE = REF_KR + QK_ROPE

CONV_CHUNK = 256
TM_FRONT = 512
TQ = 256
TK = 256
ATTN_PAIRS = 2
ATTN_LOOKAHEAD = 3
V_ROWS = V_HEAD + 16
FF_CHUNK = 256
VMEM_LIMIT = 56 * 1024 * 1024

SUBLANES = 8
TM_MOE = 256
TT = 256
N_TT = TOKENS // TT
E_PAD = 16
XS_WIDTH = D_MODEL + LANES
N_PAIRS = TOKENS * TOP_K
SLOTS =TT * TOP_K + N_EXPERTS * SUBLANES
_MAX_ROWS = N_PAIRS + N_TT * N_EXPERTS * (SUBLANES - 1) + N_EXPERTS * (TM_MOE - SUBLANES)
N_MOE_TILES = -(-_MAX_ROWS // TM_MOE)
P_PAD = N_MOE_TILES * TM_MOE
SLAB_PIECES = tuple(TT >> s for s in range(6))
WAIT_PIECES = (512,) + SLAB_PIECES
assert SLAB_PIECES[-1] == SUBLANES and SLOTS < 2 * WAIT_PIECES[0]


def _rms(x, g):
    return x * lax.rsqrt(jnp.mean(x * x, axis=-1, keepdims=True) + EPS) * g


def _const_spec(shape):
    nd = len(shape)
    return pl.BlockSpec(shape, lambda *_: (0,) * nd, pipeline_mode=pl.Buffered(1))


def _trig_kernel(pos_ref, invf_ref, cos_ref, sin_ref):
    ang = invf_ref[...] * pos_ref[...].astype(F32)
    cos_ref[...] = jnp.cos(ang)
    sin_ref[...] = jnp.sin(ang)


def _rope_tables(positions):
    inv_freq = ROPE_THETA ** (-jnp.arange(0, QK_ROPE, 2, dtype=F32) / QK_ROPE)
    return pl.pallas_call(
        _trig_kernel,
        out_shape=(jax.ShapeDtypeStruct((HALF_ROPE, TOKENS), F32),) * 2,
        name="rope_trig",
    )(positions.reshape(1, TOKENS), inv_freq.reshape(HALF_ROPE, 1))


def _dot_nt(a, b):
    return lax.dot_general(a, b, (((1,), (1,)), ((), ())), preferred_element_type=F32)


def _front_kernel(x_ref, an_ref, win_ref, cw_ref, wco_ref, qn_ref, wq_ref, kvn_ref, wuk_ref, wuv_ref,
                  cost_ref, sint_ref, q_out, k_out, v_out, gc_out, sg_out, cu_scr):
    tm = x_ref.shape[0]
    i = pl.program_id(0)
    xn = _rms(x_ref[...], an_ref[...]).astype(BF16)

    def proj(a, b):
        return jnp.dot(xn, win_ref[0, :, a:b], preferred_element_type=F32)

    small = proj(OFF_SMALL, D_IN_PAD)

    cq = cost_ref[...] * Q_SCALE
    sq = sint_ref[...] * Q_SCALE
    cqn = _rms(small[:, :Q_LORA], qn_ref[...]).astype(BF16)
    qt = _dot_nt(wq_ref[...], cqn)
    zero_pad = jnp.zeros((HEAD_PAD - QK_DIM, tm), BF16)
    for h in range(N_HEADS):
        src = h * QK_DIM
        dst = h * HEAD_PAD
        x1 = qt[src + QK_NOPE:src + QK_NOPE + HALF_ROPE, :]
        x2 = qt[src + QK_NOPE + HALF_ROPE:src + QK_DIM, :]
        q_out[dst:dst + QK_NOPE, :] = (qt[src:src + QK_NOPE, :] * Q_SCALE).astype(BF16)
        q_out[dst + QK_NOPE:dst + QK_NOPE + HALF_ROPE, :] = (x1 * cq - x2 * sq).astype(BF16)
        q_out[dst + QK_NOPE + HALF_ROPE:dst + QK_DIM, :] = (x2 * cq + x1 * sq).astype(BF16)
        q_out[dst + QK_DIM:dst + HEAD_PAD, :] = zero_pad

    cos_t = cost_ref[...]
    sin_t = sint_ref[...]
    z_nope = jnp.zeros((QK_NOPE, tm), F32)
    z_rope = jnp.zeros((HALF_ROPE, tm), F32)
    z_pad = jnp.zeros((HEAD_PAD - QK_DIM, tm), F32)
    ck = jnp.concatenate([z_nope, cos_t, cos_t, z_pad], axis=0).T
    sk_x1 = jnp.concatenate([z_nope, -sin_t, z_rope, z_pad], axis=0).T
    sk_x2 = jnp.concatenate([z_nope, z_rope, sin_t, z_pad], axis=0).T
    kr = small[:, Q_LORA + KV_LORA:]
    kpe = (kr * ck + pltpu.roll(kr, HEAD_PAD - HALF_ROPE, axis=1) * sk_x1 + pltpu.roll(kr, HALF_ROPE, axis=1) * sk_x2)
    ckvn = _rms(small[:, Q_LORA:Q_LORA + KV_LORA], kvn_ref[...]).astype(BF16)
    kn = jnp.dot(ckvn, wuk_ref[...], preferred_element_type=F32)
    for h in range(N_HEADS):
        blk = slice(h * HEAD_PAD, (h + 1) * HEAD_PAD)
        k_out[:, blk] = (kn[:, blk] + kpe).astype(BF16)
    vt = _dot_nt(wuv_ref[...], ckvn).astype(BF16)
    sub = lax.broadcasted_iota(jnp.int32, (V_ROWS - V_HEAD, TK), 0)
    ones_rows = jnp.where(sub == 0, 1.0, 0.0).astype(BF16)
    for c in range(tm // TK):
        for h in range(N_HEADS):
            v_out[c, h * V_ROWS:h * V_ROWS + V_HEAD, :] = vt[h * V_HEAD:(h + 1) * V_HEAD, c * TK:(c + 1) * TK]
            v_out[c, h * V_ROWS + V_HEAD:(h + 1) * V_ROWS, :] = ones_rows

    @pl.when(i % (SEQ // tm) == 0)
    def _():
        cu_scr[0:8, :] = jnp.zeros((8, D_CONV), F32)

    cw = cw_ref[...]
    y_conv = None
    for c in range(D_CONV // CONV_CHUNK):
        ch = slice(c * CONV_CHUNK, (c + 1) * CONV_CHUNK)
        bcu = proj(OFF_BCU + 3 * c * CONV_CHUNK, OFF_BCU + 3 * (c + 1) * CONV_CHUNK)
        b_g = bcu[:, :CONV_CHUNK]
        cu = bcu[:, CONV_CHUNK:2 * CONV_CHUNK] * bcu[:, 2 * CONV_CHUNK:]
        cu_scr[8:8 + tm, ch] = cu
        prev1 = cu_scr[7:7 + tm, ch]
        prev2 = cu_scr[6:6 + tm, ch]
        conv = prev2 * cw[0:1, ch] + prev1 * cw[1:2, ch] + cu * cw[2:3, ch]
        cu_scr[0:8, ch] = cu[tm - 8:, :]
        part = jnp.dot((b_g * conv).astype(BF16), wco_ref[ch, :], preferred_element_type=F32)
        y_conv = part if y_conv is None else y_conv + part

    gates = proj(OFF_GATE, OFF_SMALL)
    gc_out[...] = (jax.nn.sigmoid(gates[:, :D_MODEL]) * y_conv).astype(BF16)
    sg_out[...] = jax.nn.sigmoid(gates[:, D_MODEL:]).astype(BF16)


def _front(layer, x, an, win_all, cw, wco, qn, wq, kvn, wuk, wuv, cos_t, sin_t):
    tm = TM_FRONT
    row = lambda n: pl.BlockSpec((tm, n), lambda i: (i, 0))
    col = lambda n: pl.BlockSpec((n, tm), lambda i: (0, i))
    win_spec = pl.BlockSpec((1, D_MODEL, D_IN_PAD), lambda i: (layer, 0, 0), pipeline_mode=pl.Buffered(1))
    return pl.pallas_call(
        _front_kernel,
        grid=(TOKENS // tm,),
        in_specs=[row(D_MODEL), _const_spec((1, D_MODEL)), win_spec,
                  _const_spec((CONV_WIDTH, D_CONV)), _const_spec((D_CONV, D_MODEL)),
                  _const_spec((1, Q_LORA)), _const_spec((N_HEADS * QK_DIM, Q_LORA)),
                  _const_spec((1, KV_LORA)), _const_spec((KV_LORA, N_HEADS * HEAD_PAD)),
                  _const_spec((N_HEADS * V_HEAD, KV_LORA)),
                  col(HALF_ROPE), col(HALF_ROPE)],
        out_specs=[col(N_HEADS * HEAD_PAD), row(N_HEADS * HEAD_PAD),
                   pl.BlockSpec((tm // TK, N_HEADS * V_ROWS, TK), lambda i: (i, 0, 0)),
                   row(D_MODEL), row(D_MODEL)],
        out_shape=[jax.ShapeDtypeStruct((N_HEADS * HEAD_PAD, TOKENS), BF16),
                   jax.ShapeDtypeStruct((TOKENS, N_HEADS * HEAD_PAD), BF16),
                   jax.ShapeDtypeStruct((TOKENS // TK, N_HEADS * V_ROWS, TK), BF16),
                   jax.ShapeDtypeStruct((TOKENS, D_MODEL), BF16),
                   jax.ShapeDtypeStruct((TOKENS, D_MODEL), BF16)],
        scratch_shapes=[pltpu.VMEM((tm + 8, D_CONV), F32)],
        compiler_params=pltpu.CompilerParams(dimension_semantics=("arbitrary",), vmem_limit_bytes=VMEM_LIMIT),
        name="front",
    )(x, an, win_all, cw, wco, qn, wq, kvn, wuk, wuv, cos_t, sin_t)


def _attn_items():
    items = []
    for pair in range(ATTN_PAIRS):
        for i in range(SEQ // TQ):
            ks = 0
            while ks + TK <= i * TQ:
                items.append((pair, i, ks, TK, False))
                ks += TK
            items.append((pair, i, ks, (i + 1) * TQ - ks, True))
    return items


def _attn_kernel(qt_ref, k_ref, vt_ref, o_ref):
    row = lax.broadcasted_iota(jnp.int32, (TQ, TQ), 0)
    col = lax.broadcasted_iota(jnp.int32, (TQ, TQ), 1)
    causal = row <= col

    def scores(item):
        pair, i, ks, n, diag = item
        sts = []
        for h in range(2 * pair, 2 * pair + 2):
            hs = slice(h * HEAD_PAD, (h + 1) * HEAD_PAD)
            st = jnp.dot(k_ref[ks:ks + n, hs], qt_ref[hs, i * TQ:(i + 1) * TQ], preferred_element_type=F32)
            if diag:
                tail = jnp.where(causal, st[n - TQ:, :], NEG_BIG)
                st = tail if n == TQ else jnp.concatenate([st[:n - TQ, :], tail], axis=0)
            sts.append(st)
        return sts

    def consume(item, sts, carry):
        pair, i, ks, n, _ = item
        stats = []
        for j in range(2):
            m, _ = carry[j]
            m_new = jnp.maximum(m, jnp.max(sts[j], axis=0, keepdims=True))
            stats.append((m_new, jnp.exp2(m - m_new), jnp.exp2(sts[j] - m_new).astype(BF16)))
        new = []
        for j in range(2):
            h = 2 * pair + j
            m_new, alpha, p = stats[j]
            vt = vt_ref[ks // TK, h * V_ROWS:(h + 1) * V_ROWS, ks % TK:ks % TK + n]
            new.append((m_new, alpha * carry[j][1] + jnp.dot(vt, p, preferred_element_type=F32)))
        return new

    items = _attn_items()
    init = [(jnp.full((1, TQ), NEG_BIG, F32), jnp.zeros((V_ROWS, TQ), F32)) for _ in range(2)]
    pending = [scores(it) for it in items[:ATTN_LOOKAHEAD]]
    carry = [init] * ATTN_PAIRS
    for t, item in enumerate(items):
        if t + ATTN_LOOKAHEAD < len(items):
            pending.append(scores(items[t + ATTN_LOOKAHEAD]))
        pair, i = item[0], item[1]
        carry[pair] = consume(item, pending.pop(0), carry[pair])
        if item[4]:
            out_t = jnp.concatenate([acc[0:V_HEAD] / acc[V_HEAD:V_HEAD + 1] for (_, acc) in carry[pair]], axis=0)
            o_ref[i * TQ:(i + 1) * TQ, pair * 2 * V_HEAD:(pair + 1) * 2 * V_HEAD] = out_t.T.astype(BF16)
            carry[pair] = init


def _attention(qt, k, vt):
    heads = 2 * ATTN_PAIRS
    return pl.pallas_call(
        _attn_kernel,
        grid=(BATCH, N_HEADS // heads),
        in_specs=[pl.BlockSpec((heads * HEAD_PAD, SEQ), lambda b, hp: (hp, b)),
                  pl.BlockSpec((SEQ, heads * HEAD_PAD), lambda b, hp: (b, hp)),
                  pl.BlockSpec((SEQ // TK, heads * V_ROWS, TK), lambda b, hp: (b, hp, 0))],
        out_specs=pl.BlockSpec((SEQ, heads * V_HEAD), lambda b, hp: (b, hp)),
        out_shape=jax.ShapeDtypeStruct((TOKENS, N_HEADS * V_HEAD), BF16),
        compiler_params=pltpu.CompilerParams(dimension_semantics=("arbitrary",) * 2, vmem_limit_bytes=VMEM_LIMIT),
        name="attention",
    )(qt, k, vt)


def _merge_core(attn_ref, gc_ref, sg_ref, x_ref, wmo_ref, wo_ref, fn_ref):
    y_mla = jnp.dot(attn_ref[...], wmo_ref[...], preferred_element_type=F32)
    merged = gc_ref[...].astype(F32) + sg_ref[...].astype(F32) * y_mla
    x2 = x_ref[...] + jnp.dot(merged.astype(BF16), wo_ref[...], preferred_element_type=F32)
    return x2, _rms(x2, fn_ref[...])


def _merge_ffn_kernel(attn_ref, gc_ref, sg_ref, x_ref, wmo_ref, wo_ref, fn_ref, wg_ref, wu_ref, wd_ref, o_ref):
    x2, hn = _merge_core(attn_ref, gc_ref, sg_ref, x_ref, wmo_ref, wo_ref, fn_ref)
    hn = hn.astype(BF16)
    acc = x2
    for c in range(D_FF // FF_CHUNK):
        cs = slice(c * FF_CHUNK, (c + 1) * FF_CHUNK)
        g = jnp.dot(hn, wg_ref[:, cs], preferred_element_type=F32)
        u = jnp.dot(hn, wu_ref[:, cs], preferred_element_type=F32)
        a = (jax.nn.silu(g) * u).astype(BF16)
        acc = acc + jnp.dot(a, wd_ref[cs, :], preferred_element_type=F32)
    o_ref[...] = acc


def _merge_route_kernel(attn_ref, gc_ref, sg_ref, x_ref, wmo_ref, wo_ref, fn_ref, router_ref,
                        x2_out, hn_out, route_out, route_t_out):
    x2, hn = _merge_core(attn_ref, gc_ref, sg_ref, x_ref, wmo_ref, wo_ref, fn_ref)
    x2_out[...] = x2
    hb = hn.astype(BF16)
    hn_out[...] = hb
    hl = (hn - hb.astype(F32)).astype(BF16)
    r = router_ref[...]
    rh = r.astype(BF16)
    rl = (r - rh.astype(F32)).astype(BF16)
    logits = _dot_nt(rh, hb) + _dot_nt(rh, hl) + _dot_nt(rl, hb)
    erow = lax.broadcasted_iota(jnp.int32, logits.shape, 0)
    logits = jnp.where(erow < N_EXPERTS, logits, -jnp.inf)
    m1 = jnp.max(logits, axis=0, keepdims=True)
    i1 = jnp.min(jnp.where(logits == m1, erow, E_PAD), axis=0, keepdims=True)
    rest = jnp.where(erow == i1, -jnp.inf, logits)
    m2 = jnp.max(rest, axis=0, keepdims=True)
    i2 = jnp.min(jnp.where(rest == m2, erow, E_PAD), axis=0, keepdims=True)
    t = jnp.exp(m2 - m1)
    w1 = 1.0 / (1.0 + t)
    w2 = t * w1
    route_t = jnp.where(erow == 0, i1.astype(F32),
                        jnp.where(erow == 1, i2.astype(F32), jnp.where(erow == 2, w1, jnp.where(erow == 3, w2, 0.0))))
    route_t_out[...] = route_t[0:SUBLANES, :]
    pad = jnp.zeros((LANES - E_PAD, route_t.shape[1]), F32)
    route_out[...] = jnp.concatenate([route_t, pad], axis=0).T


def _merge(attn, gc, sg, x, wmo, wo, fn, router=None, ffn=None):
    tm = TM_FRONT
    row = lambda n: pl.BlockSpec((tm, n), lambda i: (i, 0))
    in_specs = [row(N_HEADS * V_HEAD), row(D_MODEL), row(D_MODEL), row(D_MODEL),
                _const_spec((N_HEADS * V_HEAD, D_MODEL)), _const_spec((D_MODEL, D_MODEL)), _const_spec((1, D_MODEL))]
    args = [attn, gc, sg, x, wmo, wo, fn]
    if router is None:
        kern = _merge_ffn_kernel
        in_specs += [_const_spec((D_MODEL, D_FF)), _const_spec((D_MODEL, D_FF)), _const_spec((D_FF, D_MODEL))]
        args += list(ffn)
        out_specs = row(D_MODEL)
        out_shape = jax.ShapeDtypeStruct((TOKENS, D_MODEL), F32)
        name = "merge_ffn"
    else:
        kern = _merge_route_kernel
        in_specs.append(_const_spec((E_PAD, D_MODEL)))
        args.append(router)
        out_specs = [row(D_MODEL), row(D_MODEL), row(LANES), pl.BlockSpec((SUBLANES, tm), lambda i: (0, i))]
        out_shape = [jax.ShapeDtypeStruct((TOKENS, D_MODEL), F32), jax.ShapeDtypeStruct((TOKENS, D_MODEL), BF16),
                     jax.ShapeDtypeStruct((TOKENS, LANES), F32), jax.ShapeDtypeStruct((SUBLANES, TOKENS), F32)]
        name = "merge_route"
    return pl.pallas_call(
        kern, grid=(TOKENS // tm,), in_specs=in_specs, out_specs=out_specs, out_shape=out_shape,
        compiler_params=pltpu.CompilerParams(dimension_semantics=("arbitrary",), vmem_limit_bytes=VMEM_LIMIT),
        name=name,
    )(*args)


def _pieces(n, sizes):
    return [((n & z) != 0, n & ~(2 * z - 1), z) for z in sizes]


def _aligned(rows, z):
    return pl.ds(rows if isinstance(rows, int) else pl.multiple_of(rows, SUBLANES), z)


def _start_slab_copies(tile, base_ref, offp_ref, np_ref, make_copy):
    for e in range(N_EXPERTS):
        n = np_ref[tile * N_EXPERTS + e]
        src = offp_ref[tile * N_EXPERTS + e]
        dst = base_ref[tile * N_EXPERTS + e]
        for pred, off, z in _pieces(n, SLAB_PIECES):
            @pl.when(pred)
            def _(off=off, z=z):
                make_copy(src + off, dst + off, z).start()


def _wait_rows(total, make_copy):
    for pred, _, z in _pieces(total, WAIT_PIECES):
        @pl.when(pred)
        def _(z=z):
            make_copy(0, 0, z).wait()


def _dispatch_kernel(base_ref, offp_ref, np_ref, tot_ref, tail_ref, hn_ref, rt_ref, xs_hbm, xs_scr, zbuf, sem, zsem):
    i = pl.program_id(0)
    slot = i % 2

    def copy(buf_slot):
        return lambda s, d, z: pltpu.make_async_copy(xs_scr.at[buf_slot, _aligned(s, z), :],
                                                     xs_hbm.at[_aligned(d, z), :], sem.at[buf_slot])

    def zero_copy(d, z):
        return pltpu.make_async_copy(zbuf.at[pl.ds(0, z), :], xs_hbm.at[_aligned(d, z), :], zsem)

    @pl.when(i == 0)
    def _():
        zbuf[...] = jnp.zeros_like(zbuf)
        for phase in ("start", "wait"):
            for e in range(N_EXPERTS):
                t0 = tail_ref[e]
                n = tail_ref[N_EXPERTS + e]
                full = lax.shift_right_logical(n, jnp.int32(TM_MOE.bit_length() - 1))

                def chunk(c, carry, t0=t0, phase=phase):
                    cp = zero_copy(t0 + c * TM_MOE, TM_MOE)
                    cp.start() if phase == "start" else cp.wait()
                    return carry

                lax.fori_loop(0, full, chunk, 0)
                rest = n & (TM_MOE - 1)
                for pred, off, z in _pieces(rest, SLAB_PIECES[1:]):
                    @pl.when(pred)
                    def _(off=off, z=z, t0=t0, full=full, phase=phase):
                        cp = zero_copy(t0 + full * TM_MOE + off, z)
                        cp.start() if phase == "start" else cp.wait()

    e0 = rt_ref[0:1, :].astype(jnp.int32)
    e1 = rt_ref[1:2, :].astype(jnp.int32)
    eid = lax.broadcasted_iota(jnp.int32, (E_PAD, TT), 0)
    sel = jnp.where(e0 == eid, 1.0, jnp.where(e1 == eid, 1.0, 0.0)).astype(BF16)
    r = lax.broadcasted_iota(jnp.int32, (TT, TT), 0)
    c = lax.broadcasted_iota(jnp.int32, (TT, TT), 1)
    earlier = jnp.where(r < c, 1.0, 0.0).astype(BF16)
    rank = jnp.dot(sel, earlier, preferred_element_type=F32)
    ecol = lax.broadcasted_iota(jnp.int32, (E_PAD, 1), 0)
    offp = jnp.zeros((E_PAD, 1), F32)
    for e in range(N_EXPERTS):
        offp = jnp.where(ecol == e, offp_ref[i * N_EXPERTS + e].astype(F32), offp)
    slot_all = rank + offp
    slot0 = jnp.sum(jnp.where(e0 == eid, slot_all, 0.0), axis=0, keepdims=True)
    slot1 = jnp.sum(jnp.where(e1 == eid, slot_all, 0.0), axis=0, keepdims=True)
    sidx = lax.broadcasted_iota(jnp.int32, (SLOTS, TT), 0).astype(F32)
    place = jnp.where(sidx == slot0, 1.0, jnp.where(sidx == slot1, 1.0, 0.0)).astype(BF16)
    xs_scr[slot, :, 0:D_MODEL] = jnp.dot(place, hn_ref[...], preferred_element_type=F32)
    w_slot = jnp.sum(jnp.where(sidx == slot0, rt_ref[2:3, :], jnp.where(sidx == slot1, rt_ref[3:4, :], 0.0)),
                     axis=1, keepdims=True)
    xs_scr[slot, :, D_MODEL:] = jnp.broadcast_to(w_slot, (SLOTS, LANES))

    _start_slab_copies(i, base_ref, offp_ref, np_ref, copy(slot))

    @pl.when(i > 0)
    def _():
        _wait_rows(tot_ref[i - 1], copy(1 - slot))

    @pl.when(i == pl.num_programs(0) - 1)
    def _():
        _wait_rows(tot_ref[i], copy(slot))


def _dispatch(tables, hn, route_t):
    grid_spec = pltpu.PrefetchScalarGridSpec(
        num_scalar_prefetch=5,
        grid=(N_TT,),
        in_specs=[pl.BlockSpec((TT, D_MODEL), lambda i, *_: (i, 0)),
                  pl.BlockSpec((SUBLANES, TT), lambda i, *_: (0, i))],
        out_specs=pl.BlockSpec(memory_space=pl.ANY),
        scratch_shapes=[pltpu.VMEM((2, SLOTS, XS_WIDTH), F32), pltpu.VMEM((TM_MOE, XS_WIDTH), F32),
                        pltpu.SemaphoreType.DMA((2,)), pltpu.SemaphoreType.DMA],
    )
    return pl.pallas_call(
        _dispatch_kernel,
        grid_spec=grid_spec,
        out_shape=jax.ShapeDtypeStruct((P_PAD, XS_WIDTH), F32),
        compiler_params=pltpu.CompilerParams(dimension_semantics=("arbitrary",), vmem_limit_bytes=VMEM_LIMIT),
        name="moe_dispatch",
    )(tables["base"], tables["offp"], tables["np"], tables["total"], tables["tail"], hn, route_t)


def _moe_kernel(te_ref, nv_ref, nxt_ref, par_ref, first_ref, last_ref, x_ref, wg_hbm, wu_hbm, wd_hbm, y_ref,
                stage_g, stage_u, stage_d, wgu_scr, wd_scr, sem):
    i = pl.program_id(0)
    valid = i < nv_ref[0]
    nxt = nxt_ref[i]

    def fetch(e):
        return [pltpu.make_async_copy(wg_hbm.at[e], stage_g, sem.at[0]),
                pltpu.make_async_copy(wu_hbm.at[e], stage_u, sem.at[1]),
                pltpu.make_async_copy(wd_hbm.at[e], stage_d, sem.at[2])]

    def round_into(s):
        wgu_scr[s, :, 0:D_FF_EXPERT] = stage_g[...].astype(BF16)
        wgu_scr[s, :, D_FF_EXPERT:] = stage_u[...].astype(BF16)
        wd_scr[s] = stage_d[...].astype(BF16)

    @pl.when(i == 0)
    def _():
        for cp in fetch(te_ref[0]):
            cp.start()
        for cp in fetch(te_ref[0]):
            cp.wait()
        round_into(par_ref[0])

    @pl.when(jnp.logical_and(valid, jnp.logical_and(first_ref[i] == 1, nxt >= 0)))
    def _():
        for cp in fetch(nxt):
            cp.start()

    @pl.when(valid)
    def _():
        s = par_ref[i]
        x = x_ref[:, 0:D_MODEL].astype(BF16)
        h = jnp.dot(x, wgu_scr[s], preferred_element_type=F32)
        a = (jax.nn.silu(h[:, :D_FF_EXPERT]) * h[:, D_FF_EXPERT:]).astype(BF16)
        y_ref[...] = jnp.dot(a, wd_scr[s], preferred_element_type=F32) * x_ref[:, D_MODEL:D_MODEL + 1]

    @pl.when(jnp.logical_and(valid, jnp.logical_and(last_ref[i] == 1, nxt >= 0)))
    def _():
        for cp in fetch(nxt):
            cp.wait()
        round_into(1 - par_ref[i])

    @pl.when(jnp.logical_not(valid))
    def _():
        y_ref[...] = jnp.zeros_like(y_ref)


def _moe(sched, xs, wg, wu, wd):
    grid_spec = pltpu.PrefetchScalarGridSpec(
        num_scalar_prefetch=6,
        grid=(N_MOE_TILES,),
        in_specs=[pl.BlockSpec((TM_MOE, XS_WIDTH), lambda i, *_: (i, 0)),
                  pl.BlockSpec(memory_space=pl.ANY), pl.BlockSpec(memory_space=pl.ANY),
                  pl.BlockSpec(memory_space=pl.ANY)],
        out_specs=pl.BlockSpec((TM_MOE, D_MODEL), lambda i, *_: (i, 0)),
        scratch_shapes=[pltpu.VMEM((D_MODEL, D_FF_EXPERT), F32), pltpu.VMEM((D_MODEL, D_FF_EXPERT), F32),
                        pltpu.VMEM((D_FF_EXPERT, D_MODEL), F32),
                        pltpu.VMEM((2, D_MODEL, 2 * D_FF_EXPERT), BF16), pltpu.VMEM((2, D_FF_EXPERT, D_MODEL), BF16),
                        pltpu.SemaphoreType.DMA((3,))],
    )
    return pl.pallas_call(
        _moe_kernel,
        grid_spec=grid_spec,
        out_shape=jax.ShapeDtypeStruct((P_PAD, D_MODEL), F32),
        compiler_params=pltpu.CompilerParams(dimension_semantics=("arbitrary",), vmem_limit_bytes=VMEM_LIMIT),
        name="moe_experts",
    )(sched["expert"], sched["n_valid"], sched["next"], sched["parity"], sched["first"], sched["last"],
      xs, wg, wu, wd)


def _combine_kernel(base_ref, offp_ref, np_ref, tot_ref, x_ref, route_ref, y_hbm, fn_ref, o_ref, ybuf, sem):
    i = pl.program_id(0)
    slot = i % 2

    def copy(buf_slot):
        return lambda s, d, z: pltpu.make_async_copy(y_hbm.at[_aligned(d, z), :],
                                                     ybuf.at[buf_slot, _aligned(s, z), :], sem.at[buf_slot])

    @pl.when(i == 0)
    def _():
        ybuf[...] = jnp.zeros_like(ybuf)
        _start_slab_copies(0, base_ref, offp_ref, np_ref, copy(0))

    @pl.when(i + 1 < pl.num_programs(0))
    def _():
        _start_slab_copies(i + 1, base_ref, offp_ref, np_ref, copy(1 - slot))

    route = route_ref[...]
    e0 = route[:, 0:1].astype(jnp.int32)
    e1 = route[:, 1:2].astype(jnp.int32)
    eid =lax.broadcasted_iota(jnp.int32, (TT, LANES), 1)
    sel = jnp.where(e0 == eid, 1.0, jnp.where(e1 == eid, 1.0, 0.0)).astype(BF16)
    r = lax.broadcasted_iota(jnp.int32, (TT, TT), 0)
    c = lax.broadcasted_iota(jnp.int32, (TT, TT), 1)
    earlier = jnp.where(c < r, 1.0, 0.0).astype(BF16)
    rank = jnp.dot(earlier, sel, preferred_element_type=F32)
    erow = lax.broadcasted_iota(jnp.int32, (1, LANES), 1)
    offp = jnp.zeros((1, LANES), F32)
    for e in range(N_EXPERTS):
        offp = jnp.where(erow == e, offp_ref[i * N_EXPERTS + e].astype(F32), offp)
    slot_all = rank + offp
    slot0 = jnp.sum(jnp.where(e0 == eid, slot_all, 0.0), axis=1, keepdims=True)
    slot1 = jnp.sum(jnp.where(e1 == eid, slot_all, 0.0), axis=1, keepdims=True)
    sidx = lax.broadcasted_iota(jnp.int32, (TT, SLOTS), 1).astype(F32)
    pick = jnp.where(sidx == slot0, 1.0, jnp.where(sidx == slot1, 1.0, 0.0)).astype(BF16)

    _wait_rows(tot_ref[i], copy(slot))
    y = ybuf[slot]
    yh = y.astype(BF16)
    yl = (y - yh.astype(F32)).astype(BF16)
    moe = jnp.dot(pick, yh, preferred_element_type=F32) + jnp.dot(pick, yl, preferred_element_type=F32)
    o_ref[...] = _rms(x_ref[...] + moe, fn_ref[...])


def _combine(tables, x2, route, y_sorted, fn):
    grid_spec = pltpu.PrefetchScalarGridSpec(
        num_scalar_prefetch=4,
        grid=(N_TT,),
        in_specs=[pl.BlockSpec((TT, D_MODEL), lambda i, *_: (i, 0)),
                  pl.BlockSpec((TT, LANES), lambda i, *_: (i, 0)),
                  pl.BlockSpec(memory_space=pl.ANY),
                  pl.BlockSpec((1, D_MODEL), lambda i, *_: (0, 0))],
        out_specs=pl.BlockSpec((TT, D_MODEL), lambda i, *_: (i, 0)),
        scratch_shapes=[pltpu.VMEM((2, SLOTS, D_MODEL), F32), pltpu.SemaphoreType.DMA((2,))],
    )
    return pl.pallas_call(
        _combine_kernel,
        grid_spec=grid_spec,
        out_shape=jax.ShapeDtypeStruct((TOKENS, D_MODEL), F32),
        compiler_params=pltpu.CompilerParams(dimension_semantics=("arbitrary",), vmem_limit_bytes=VMEM_LIMIT),
        name="moe_combine",
    )(tables["base"], tables["offp"], tables["np"], tables["total"], x2, route, y_sorted, fn)


def _routing_tables(route_t):
    experts = route_t[:TOP_K, :].astype(jnp.int32)
    eid = jnp.arange(N_EXPERTS, dtype=jnp.int32)[:, None, None]
    chosen = jnp.sum((experts[None, :, :] == eid).astype(jnp.int32), axis=1)
    cnt = jnp.sum(chosen.reshape(N_EXPERTS, N_TT, TT), axis=2).T
    npad = (cnt + SUBLANES - 1) // SUBLANES * SUBLANES
    offp = jnp.cumsum(npad, axis=1) - npad
    total = jnp.sum(npad, axis=1)
    group = jnp.sum(npad, axis=0)
    group_pad = (group + TM_MOE - 1) // TM_MOE * TM_MOE
    group_end = jnp.cumsum(group_pad)
    group_start = group_end - group_pad
    base = group_start[None, :] + jnp.cumsum(npad, axis=0) - npad
    tail_start = group_start + group
    is_last = jnp.arange(N_EXPERTS) == N_EXPERTS - 1
    tail_len = jnp.where(is_last, P_PAD - tail_start, group_pad - group)
    tile_start = jnp.arange(N_MOE_TILES, dtype=jnp.int32) * TM_MOE
    tile_expert = jnp.minimum(jnp.sum((tile_start[:, None] >= group_end[None, :]).astype(jnp.int32), axis=1),
                              N_EXPERTS - 1)
    ids = jnp.arange(N_EXPERTS, dtype=jnp.int32)
    nonempty = group_pad > 0
    ordinal = jnp.cumsum(nonempty.astype(jnp.int32)) - 1
    later = jnp.logical_and(nonempty[None, :], ids[None, :] > ids[:, None])
    next_expert = jnp.min(jnp.where(later, ids[None, :], N_EXPERTS), axis=1)
    next_expert = jnp.where(next_expert == N_EXPERTS, -1, next_expert)
    is_tile_expert = tile_expert[:, None] == ids[None, :]
    per_tile = lambda table: jnp.sum(jnp.where(is_tile_expert, table[None, :], 0), axis=1)
    sched = {"expert": tile_expert, "n_valid": (group_end[-1] // TM_MOE).reshape(1),
             "next": per_tile(next_expert), "parity": per_tile(ordinal) % 2,
             "first": tile_start == per_tile(group_start),
             "last": tile_start + TM_MOE == per_tile(group_end)}
    tables = {"base": base.reshape(-1), "offp": offp.reshape(-1), "np": npad.reshape(-1), "total": total,
              "tail": jnp.concatenate([tail_start, tail_len])}
    as_i32 = lambda d: {k: v.astype(jnp.int32) for k, v in d.items()}
    return as_i32(tables), as_i32(sched)


WIN_ROWS = 256


def _win_layout_kernel(wt_ref, o_ref):
    wt = wt_ref[0]
    cols = wt.shape[1]
    kr_tile = jnp.concatenate([jnp.zeros((QK_NOPE, cols), F32), wt[REF_KR:REF_GATE, :],
                               jnp.zeros((HEAD_PAD - QK_DIM, cols), F32)], axis=0)
    conv_rows = [wt[part * D_CONV + c * CONV_CHUNK:part * D_CONV + (c + 1) * CONV_CHUNK, :]
                 for c in range(D_CONV // CONV_CHUNK) for part in range(3)]
    grouped = jnp.concatenate(conv_rows + [wt[REF_GATE:, :], wt[REF_CQ:REF_KR, :], kr_tile], axis=0)
    o_ref[0] = grouped.T.astype(BF16)


def _win_layout(w_in):
    wt = jnp.swapaxes(w_in, 1, 2)
    d_in = wt.shape[1]
    return pl.pallas_call(
        _win_layout_kernel,
        grid=(DEPTH, D_MODEL // WIN_ROWS),
        in_specs=[pl.BlockSpec((1, d_in, WIN_ROWS), lambda l, r: (l, 0, r))],
        out_specs=pl.BlockSpec((1, WIN_ROWS, D_IN_PAD), lambda l, r: (l, r, 0)),
        out_shape=jax.ShapeDtypeStruct((DEPTH, D_MODEL, D_IN_PAD), BF16),
        compiler_params=pltpu.CompilerParams(dimension_semantics=("arbitrary",) * 2, vmem_limit_bytes=VMEM_LIMIT),
        name="win_layout",
    )(wt)


def _layer_weights(w_uq, w_ukv):
    wq = w_uq.T.astype(BF16)

    wkv = w_ukv.reshape(KV_LORA, N_HEADS, QK_NOPE + V_HEAD)
    wuk = jnp.concatenate([wkv[:, :, :QK_NOPE], jnp.zeros((KV_LORA, N_HEADS, HEAD_PAD - QK_NOPE), F32)], axis=2)
    wuk = wuk.reshape(KV_LORA, N_HEADS * HEAD_PAD).astype(BF16)
    wuv = wkv[:, :, QK_NOPE:].reshape(KV_LORA, N_HEADS * V_HEAD).T.astype(BF16)
    return wq, wuk, wuv


def kernel(x, positions, attn_norm, w_in, conv_w, w_conv_out, q_norm, w_uq, kv_norm, w_ukv, w_mla_out, w_o, ffn_norm,
           w_gate, w_up, w_down, router, w_gate_e, w_up_e, w_down_e, final_norm):
    assert x.shape == (BATCH, SEQ, D_MODEL) and positions.shape == (BATCH, SEQ)
    cos_t, sin_t = _rope_tables(positions)
    xt = x.reshape(TOKENS, D_MODEL)
    out = None
    win_all = _win_layout(w_in)
    for l in range(DEPTH):
        wq, wuk, wuv = _layer_weights(w_uq[l], w_ukv[l])
        q, k, v, gc, sg = _front(l, xt, attn_norm[l].reshape(1, D_MODEL), win_all, conv_w[l],
                                 w_conv_out[l].astype(BF16),
                                 q_norm[l].reshape(1, Q_LORA), wq, kv_norm[l].reshape(1, KV_LORA), wuk, wuv,
                                 cos_t, sin_t)
        attn = _attention(q, k, v)
        wmo = w_mla_out[l].astype(BF16)
        wo = w_o[l].astype(BF16)
        fn = ffn_norm[l].reshape(1, D_MODEL)
        if l % 2 == 0:
            j = l // 2
            xt = _merge(attn, gc, sg, xt, wmo, wo, fn,
                        ffn=(w_gate[j].astype(BF16), w_up[j].astype(BF16), w_down[j].astype(BF16)))
        else:
            j = l // 2
            router_pad = jnp.pad(router[j].T, ((0, E_PAD - N_EXPERTS), (0, 0)))
            x2, hn, route, route_t = _merge(attn, gc, sg, xt, wmo, wo, fn, router_pad)
            tables, sched = _routing_tables(route_t)
            xs = _dispatch(tables, hn, route_t)
            y_sorted = _moe(sched, xs, w_gate_e[j], w_up_e[j], w_down_e[j])
            out = _combine(tables, x2, route, y_sorted, final_norm.reshape(1, D_MODEL))
    return out.reshape(BATCH, SEQ, D_MODEL)
```

```python
import math

import jax
import jax.numpy as jnp
from jax import lax
from jax.experimental import pallas as pl
from jax.experimental.pallas import tpu as pltpu

F32 = jnp.float32
BF16 = jnp.bfloat16

D_MODEL = 1024
BATCH = 8
SEQ = 2048
TOKENS = BATCH * SEQ
DEPTH = 2
D_CONV = 512
CONV_WIDTH = 3
N_HEADS = 8
QK_NOPE = 64
QK_ROPE = 32
HALF_ROPE = QK_ROPE // 2
V_HEAD = 64
Q_LORA = 384
KV_LORA = 256
ROPE_THETA = 10000.0
D_FF = 2816
N_EXPERTS = 8
TOP_K = 2
D_FF_EXPERT = 1408
EPS = 1e-6

LANES = 128
HEAD_PAD = LANES
QK_DIM = QK_NOPE + QK_ROPE
Q_SCALE = (1.0 / math.sqrt(QK_DIM)) * math.log2(math.e)
NEG_BIG = -1e30

OFF_BCU = 0
OFF_GATE = 3 * D_CONV
OFF_SMALL = OFF_GATE + 2 * D_MODEL
D_IN_PAD = OFF_SMALL + Q_LORA + KV_LORA + LANES
REF_CQ = 3 * D_CONV
REF_KR = REF_CQ + Q_LORA + KV_LORA
REF_GATE = REF_KR + QK_ROPE

CONV_CHUNK = 256
TM_FRONT = 512
TQ = 256
TK = 128
ATTN_PAIRS = 2
ATTN_LOOKAHEAD = 3
V_ROWS = V_HEAD + 16
FF_CHUNK = 256
VMEM_LIMIT = 56 * 1024 * 1024

SUBLANES = 8
TM_MOE = 256
TT = 256
N_TT = TOKENS // TT
E_PAD = 16
XS_WIDTH = D_MODEL + LANES
N_PAIRS = TOKENS * TOP_K
SLOTS =TT * TOP_K + N_EXPERTS * SUBLANES
_MAX_ROWS = N_PAIRS + N_TT * N_EXPERTS * (SUBLANES - 1) + N_EXPERTS * (TM_MOE - SUBLANES)
N_MOE_TILES = -(-_MAX_ROWS // TM_MOE)
P_PAD = N_MOE_TILES * TM_MOE
SLAB_PIECES = tuple(TT >> s for s in range(6))
WAIT_PIECES = (512,) + SLAB_PIECES
assert SLAB_PIECES[-1] == SUBLANES and SLOTS < 2 * WAIT_PIECES[0]


def _rms(x, g):
    return x * lax.rsqrt(jnp.mean(x * x, axis=-1, keepdims=True) + EPS) * g


def _const_spec(shape):
    nd = len(shape)
    return pl.BlockSpec(shape, lambda *_: (0,) * nd, pipeline_mode=pl.Buffered(1))


def _trig_kernel(pos_ref, invf_ref, cos_ref, sin_ref):
    ang = invf_ref[...] * pos_ref[...].astype(F32)
    cos_ref[...] = jnp.cos(ang)
    sin_ref[...] = jnp.sin(ang)


def _rope_tables(positions):
    inv_freq = ROPE_THETA ** (-jnp.arange(0, QK_ROPE, 2, dtype=F32) / QK_ROPE)
    return pl.pallas_call(
        _trig_kernel,
        out_shape=(jax.ShapeDtypeStruct((HALF_ROPE, TOKENS), F32),) * 2,
        name="rope_trig",
    )(positions.reshape(1, TOKENS), inv_freq.reshape(HALF_ROPE, 1))


def _dot_nt(a, b):
    return lax.dot_general(a, b, (((1,), (1,)), ((), ())), preferred_element_type=F32)


def _front_kernel(x_ref, an_ref, win_ref, cw_ref, wco_ref, qn_ref, wq_ref, kvn_ref, wuk_ref, wuv_ref,
                  cost_ref, sint_ref, q_out, k_out, v_out, gc_out, sg_out, cu_scr):
    tm = x_ref.shape[0]
    i = pl.program_id(0)
    xn = _rms(x_ref[...], an_ref[...]).astype(BF16)

    def proj(a, b):
        return jnp.dot(xn, win_ref[0, :, a:b], preferred_element_type=F32)

    small = proj(OFF_SMALL, D_IN_PAD)

    cq = cost_ref[...] * Q_SCALE
    sq = sint_ref[...] * Q_SCALE
    cqn = _rms(small[:, :Q_LORA], qn_ref[...]).astype(BF16)
    qt = _dot_nt(wq_ref[...], cqn)
    zero_pad = jnp.zeros((HEAD_PAD - QK_DIM, tm), BF16)
    for h in range(N_HEADS):
        src = h * QK_DIM
        dst = h * HEAD_PAD
        x1 = qt[src + QK_NOPE:src + QK_NOPE + HALF_ROPE, :]
        x2 = qt[src + QK_NOPE + HALF_ROPE:src + QK_DIM, :]
        q_out[dst:dst + QK_NOPE, :] = (qt[src:src + QK_NOPE, :] * Q_SCALE).astype(BF16)
        q_out[dst + QK_NOPE:dst + QK_NOPE + HALF_ROPE, :] = (x1 * cq - x2 * sq).astype(BF16)
        q_out[dst + QK_NOPE + HALF_ROPE:dst + QK_DIM, :] = (x2 * cq + x1 * sq).astype(BF16)
        q_out[dst + QK_DIM:dst + HEAD_PAD, :] = zero_pad

    cos_t = cost_ref[...]
    sin_t = sint_ref[...]
    z_nope = jnp.zeros((QK_NOPE, tm), F32)
    z_rope = jnp.zeros((HALF_ROPE, tm), F32)
    z_pad = jnp.zeros((HEAD_PAD - QK_DIM, tm), F32)
    ck = jnp.concatenate([z_nope, cos_t, cos_t, z_pad], axis=0).T
    sk_x1 = jnp.concatenate([z_nope, -sin_t, z_rope, z_pad], axis=0).T
    sk_x2 = jnp.concatenate([z_nope, z_rope, sin_t, z_pad], axis=0).T
    kr = small[:, Q_LORA + KV_LORA:]
    kpe = (kr * ck + pltpu.roll(kr, HEAD_PAD - HALF_ROPE, axis=1) * sk_x1 + pltpu.roll(kr, HALF_ROPE, axis=1) * sk_x2)
    ckvn = _rms(small[:, Q_LORA:Q_LORA + KV_LORA], kvn_ref[...]).astype(BF16)
    kn = jnp.dot(ckvn, wuk_ref[...], preferred_element_type=F32)
    for h in range(N_HEADS):
        blk = slice(h * HEAD_PAD, (h + 1) * HEAD_PAD)
        k_out[:, blk] = (kn[:, blk] + kpe).astype(BF16)
    vt = _dot_nt(wuv_ref[...], ckvn).astype(BF16)
    sub = lax.broadcasted_iota(jnp.int32, (V_ROWS - V_HEAD, TK), 0)
    ones_rows = jnp.where(sub == 0, 1.0, 0.0).astype(BF16)
    for c in range(tm // TK):
        for h in range(N_HEADS):
            v_out[c, h * V_ROWS:h * V_ROWS + V_HEAD, :] = vt[h * V_HEAD:(h + 1) * V_HEAD, c * TK:(c + 1) * TK]
            v_out[c, h * V_ROWS + V_HEAD:(h + 1) * V_ROWS, :] = ones_rows

    @pl.when(i % (SEQ // tm) == 0)
    def _():
        cu_scr[0:8, :] = jnp.zeros((8, D_CONV), F32)

    cw = cw_ref[...]
    y_conv = None
    for c in range(D_CONV // CONV_CHUNK):
        ch = slice(c * CONV_CHUNK, (c + 1) * CONV_CHUNK)
        bcu = proj(OFF_BCU + 3 * c * CONV_CHUNK, OFF_BCU + 3 * (c + 1) * CONV_CHUNK)
        b_g = bcu[:, :CONV_CHUNK]
        cu = bcu[:, CONV_CHUNK:2 * CONV_CHUNK] * bcu[:, 2 * CONV_CHUNK:]
        cu_scr[8:8 + tm, ch] = cu
        prev1 = cu_scr[7:7 + tm, ch]
        prev2 = cu_scr[6:6 + tm, ch]
        conv = prev2 * cw[0:1, ch] + prev1 * cw[1:2, ch] + cu * cw[2:3, ch]
        cu_scr[0:8, ch] = cu[tm - 8:, :]
        part = jnp.dot((b_g * conv).astype(BF16), wco_ref[ch, :], preferred_element_type=F32)
        y_conv = part if y_conv is None else y_conv + part

    gates = proj(OFF_GATE, OFF_SMALL)
    gc_out[...] = (jax.nn.sigmoid(gates[:, :D_MODEL]) * y_conv).astype(BF16)
    sg_out[...] = jax.nn.sigmoid(gates[:, D_MODEL:]).astype(BF16)


def _front(layer, x, an, win_all, cw, wco, qn, wq, kvn, wuk, wuv, cos_t, sin_t):
    tm = TM_FRONT
    row = lambda n: pl.BlockSpec((tm, n), lambda i: (i, 0))
    col = lambda n: pl.BlockSpec((n, tm), lambda i: (0, i))
    win_spec = pl.BlockSpec((1, D_MODEL, D_IN_PAD), lambda i: (layer, 0, 0), pipeline_mode=pl.Buffered(1))
    return pl.pallas_call(
        _front_kernel,
        grid=(TOKENS // tm,),
        in_specs=[row(D_MODEL), _const_spec((1, D_MODEL)), win_spec,
                  _const_spec((CONV_WIDTH, D_CONV)), _const_spec((D_CONV, D_MODEL)),
                  _const_spec((1, Q_LORA)), _const_spec((N_HEADS * QK_DIM, Q_LORA)),
                  _const_spec((1, KV_LORA)), _const_spec((KV_LORA, N_HEADS * HEAD_PAD)),
                  _const_spec((N_HEADS * V_HEAD, KV_LORA)),
                  col(HALF_ROPE), col(HALF_ROPE)],
        out_specs=[col(N_HEADS * HEAD_PAD), row(N_HEADS * HEAD_PAD),
                   pl.BlockSpec((tm // TK, N_HEADS * V_ROWS, TK), lambda i: (i, 0, 0)),
                   row(D_MODEL), row(D_MODEL)],
        out_shape=[jax.ShapeDtypeStruct((N_HEADS * HEAD_PAD, TOKENS), BF16),
                   jax.ShapeDtypeStruct((TOKENS, N_HEADS * HEAD_PAD), BF16),
                   jax.ShapeDtypeStruct((TOKENS // TK, N_HEADS * V_ROWS, TK), BF16),
                   jax.ShapeDtypeStruct((TOKENS, D_MODEL), BF16),
                   jax.ShapeDtypeStruct((TOKENS, D_MODEL), BF16)],
        scratch_shapes=[pltpu.VMEM((tm + 8, D_CONV), F32)],
        compiler_params=pltpu.CompilerParams(dimension_semantics=("arbitrary",), vmem_limit_bytes=VMEM_LIMIT),
        name="front",
    )(x, an, win_all, cw, wco, qn, wq, kvn, wuk, wuv, cos_t, sin_t)


def _attn_items():
    items = []
    for pair in range(ATTN_PAIRS):
        for i in range(SEQ // TQ):
            for ks in range(0, (i + 1) * TQ, TK):
                items.append((pair, i, ks, TK, ks + TK == (i + 1) * TQ))
    return items


def _attn_kernel(qt_ref, k_ref, vt_ref, o_ref):
    row = lax.broadcasted_iota(jnp.int32, (TK, TQ), 0)
    col = lax.broadcasted_iota(jnp.int32, (TK, TQ), 1)

    def scores(item):
        pair, i, ks, n, _ = item
        sts = []
        for h in range(2 * pair, 2 * pair + 2):
            hs = slice(h * HEAD_PAD, (h + 1) * HEAD_PAD)
            st = jnp.dot(k_ref[ks:ks + n, hs], qt_ref[hs, i * TQ:(i + 1) * TQ], preferred_element_type=F32)
            if ks + n > i * TQ:
                st = jnp.where(row + (ks - i * TQ) <= col, st, NEG_BIG)
            sts.append(st)
        return sts

    def consume(item, sts, carry):
        pair, i, ks, n, _ = item
        stats = []
        for j in range(2):
            m, _ = carry[j]
            m_new = jnp.maximum(m, jnp.max(sts[j], axis=0, keepdims=True))
            stats.append((m_new, jnp.exp2(m - m_new), jnp.exp2(sts[j] - m_new).astype(BF16)))
        new = []
        for j in range(2):
            h = 2 * pair + j
            m_new, alpha, p = stats[j]
            vt = vt_ref[ks // TK, h * V_ROWS:(h + 1) * V_ROWS, ks % TK:ks % TK + n]
            new.append((m_new, alpha * carry[j][1] + jnp.dot(vt, p, preferred_element_type=F32)))
        return new

    items = _attn_items()
    init = [(jnp.full((1, TQ), NEG_BIG, F32), jnp.zeros((V_ROWS, TQ), F32)) for _ in range(2)]
    pending = [scores(it) for it in items[:ATTN_LOOKAHEAD]]
    carry = [init] * ATTN_PAIRS
    for t, item in enumerate(items):
        if t + ATTN_LOOKAHEAD < len(items):
            pending.append(scores(items[t + ATTN_LOOKAHEAD]))
        pair, i = item[0], item[1]
        carry[pair] = consume(item, pending.pop(0), carry[pair])
        if item[4]:
            out_t = jnp.concatenate([acc[0:V_HEAD] / acc[V_HEAD:V_HEAD + 1] for (_, acc) in carry[pair]], axis=0)
            o_ref[i * TQ:(i + 1) * TQ, pair * 2 * V_HEAD:(pair + 1) * 2 * V_HEAD] = out_t.T.astype(BF16)
            carry[pair] = init


def _attention(qt, k, vt):
    heads = 2 * ATTN_PAIRS
    return pl.pallas_call(
        _attn_kernel,
        grid=(BATCH, N_HEADS // heads),
        in_specs=[pl.BlockSpec((heads * HEAD_PAD, SEQ), lambda b, hp: (hp, b)),
                  pl.BlockSpec((SEQ, heads * HEAD_PAD), lambda b, hp: (b, hp)),
                  pl.BlockSpec((SEQ // TK, heads * V_ROWS, TK), lambda b, hp: (b, hp, 0))],
        out_specs=pl.BlockSpec((SEQ, heads * V_HEAD), lambda b, hp: (b, hp)),
        out_shape=jax.ShapeDtypeStruct((TOKENS, N_HEADS * V_HEAD), BF16),
        compiler_params=pltpu.CompilerParams(dimension_semantics=("arbitrary",) * 2, vmem_limit_bytes=VMEM_LIMIT),
        name="attention",
    )(qt, k, vt)


def _merge_core(attn_ref, gc_ref, sg_ref, x_ref, wmo_ref, wo_ref, fn_ref):
    y_mla = jnp.dot(attn_ref[...], wmo_ref[...], preferred_element_type=F32)
    merged = gc_ref[...].astype(F32) + sg_ref[...].astype(F32) * y_mla
    x2 = x_ref[...] + jnp.dot(merged.astype(BF16), wo_ref[...], preferred_element_type=F32)
    return x2, _rms(x2, fn_ref[...])


def _merge_ffn_kernel(attn_ref, gc_ref, sg_ref, x_ref, wmo_ref, wo_ref, fn_ref, wg_ref, wu_ref, wd_ref, o_ref):
    x2, hn = _merge_core(attn_ref, gc_ref, sg_ref, x_ref, wmo_ref, wo_ref, fn_ref)
    hn = hn.astype(BF16)
    acc = x2
    for c in range(D_FF // FF_CHUNK):
        cs = slice(c * FF_CHUNK, (c + 1) * FF_CHUNK)
        g = jnp.dot(hn, wg_ref[:, cs], preferred_element_type=F32)
        u = jnp.dot(hn, wu_ref[:, cs], preferred_element_type=F32)
        a = (jax.nn.silu(g) * u).astype(BF16)
        acc = acc + jnp.dot(a, wd_ref[cs, :], preferred_element_type=F32)
    o_ref[...] = acc


def _merge_route_kernel(attn_ref, gc_ref, sg_ref, x_ref, wmo_ref, wo_ref, fn_ref, router_ref,
                        x2_out, hn_out, route_t_out):
    x2, hn = _merge_core(attn_ref, gc_ref, sg_ref, x_ref, wmo_ref, wo_ref, fn_ref)
    x2_out[...] = x2
    hb = hn.astype(BF16)
    hn_out[...] = hb
    hl = (hn - hb.astype(F32)).astype(BF16)
    r = router_ref[...]
    rh = r.astype(BF16)
    rl = (r - rh.astype(F32)).astype(BF16)
    logits = _dot_nt(rh, hb) + _dot_nt(rh, hl) + _dot_nt(rl, hb)
    erow = lax.broadcasted_iota(jnp.int32, logits.shape, 0)
    logits = jnp.where(erow < N_EXPERTS, logits, -jnp.inf)
    m1 = jnp.max(logits, axis=0, keepdims=True)
    i1 = jnp.min(jnp.where(logits == m1, erow, E_PAD), axis=0, keepdims=True)
    rest = jnp.where(erow == i1, -jnp.inf, logits)
    m2 = jnp.max(rest, axis=0, keepdims=True)
    i2 = jnp.min(jnp.where(rest == m2, erow, E_PAD), axis=0, keepdims=True)
    t = jnp.exp(m2 - m1)
    w1 = 1.0 / (1.0 + t)
    w2 = t * w1
    route_t = jnp.where(erow == 0, i1.astype(F32),
                        jnp.where(erow == 1, i2.astype(F32), jnp.where(erow == 2, w1, jnp.where(erow == 3, w2, 0.0))))
    route_t_out[...] = route_t[0:SUBLANES, :]


def _merge(attn, gc, sg, x, wmo, wo, fn, router=None, ffn=None):
    tm = TM_FRONT
    row = lambda n: pl.BlockSpec((tm, n), lambda i: (i, 0))
    in_specs = [row(N_HEADS * V_HEAD), row(D_MODEL), row(D_MODEL), row(D_MODEL),
                _const_spec((N_HEADS * V_HEAD, D_MODEL)), _const_spec((D_MODEL, D_MODEL)), _const_spec((1, D_MODEL))]
    args = [attn, gc, sg, x, wmo, wo, fn]
    if router is None:
        kern = _merge_ffn_kernel
        in_specs += [_const_spec((D_MODEL, D_FF)), _const_spec((D_MODEL, D_FF)), _const_spec((D_FF, D_MODEL))]
        args += list(ffn)
        out_specs = row(D_MODEL)
        out_shape = jax.ShapeDtypeStruct((TOKENS, D_MODEL), F32)
        name = "merge_ffn"
    else:
        kern = _merge_route_kernel
        in_specs.append(_const_spec((E_PAD, D_MODEL)))
        args.append(router)
        out_specs = [row(D_MODEL), row(D_MODEL), pl.BlockSpec((SUBLANES, tm), lambda i: (0, i))]
        out_shape = [jax.ShapeDtypeStruct((TOKENS, D_MODEL), F32), jax.ShapeDtypeStruct((TOKENS, D_MODEL), BF16),
                     jax.ShapeDtypeStruct((SUBLANES, TOKENS), F32)]
        name = "merge_route"
    return pl.pallas_call(
        kern, grid=(TOKENS // tm,), in_specs=in_specs, out_specs=out_specs, out_shape=out_shape,
        compiler_params=pltpu.CompilerParams(dimension_semantics=("arbitrary",), vmem_limit_bytes=VMEM_LIMIT),
        name=name,
    )(*args)


def _pieces(n, sizes):
    return [((n & z) != 0, n & ~(2 * z - 1), z) for z in sizes]


def _aligned(rows, z):
    return pl.ds(rows if isinstance(rows, int) else pl.multiple_of(rows, SUBLANES), z)


def _start_slab_copies(tile, base_ref, offp_ref, np_ref, make_copy):
    for e in range(N_EXPERTS):
        n = np_ref[tile * N_EXPERTS + e]
        src = offp_ref[tile * N_EXPERTS + e]
        dst = base_ref[tile * N_EXPERTS + e]
        for pred, off, z in _pieces(n, SLAB_PIECES):
            @pl.when(pred)
            def _(off=off, z=z):
                make_copy(src + off, dst + off, z).start()


def _wait_rows(total, make_copy):
    for pred, _, z in _pieces(total, WAIT_PIECES):
        @pl.when(pred)
        def _(z=z):
            make_copy(0, 0, z).wait()


def _dispatch_kernel(base_ref, offp_ref, np_ref, tot_ref, tail_ref, hn_ref, rt_ref, xs_hbm, xs_scr, zbuf, sem, zsem):
    i = pl.program_id(0)
    slot = i % 2

    def copy(buf_slot):
        return lambda s, d, z: pltpu.make_async_copy(xs_scr.at[buf_slot, _aligned(s, z), :],
                                                     xs_hbm.at[_aligned(d, z), :], sem.at[buf_slot])

    def zero_copy(d, z):
        return pltpu.make_async_copy(zbuf.at[pl.ds(0, z), :], xs_hbm.at[_aligned(d, z), :], zsem)

    @pl.when(i == 0)
    def _():
        zbuf[...] = jnp.zeros_like(zbuf)
        for phase in ("start", "wait"):
            for e in range(N_EXPERTS):
                t0 = tail_ref[e]
                n = tail_ref[N_EXPERTS + e]
                full = lax.shift_right_logical(n, jnp.int32(TM_MOE.bit_length() - 1))

                def chunk(c, carry, t0=t0, phase=phase):
                    cp = zero_copy(t0 + c * TM_MOE, TM_MOE)
                    cp.start() if phase == "start" else cp.wait()
                    return carry

                lax.fori_loop(0, full, chunk, 0)
                rest = n & (TM_MOE - 1)
                for pred, off, z in _pieces(rest, SLAB_PIECES[1:]):
                    @pl.when(pred)
                    def _(off=off, z=z, t0=t0, full=full, phase=phase):
                        cp = zero_copy(t0 + full * TM_MOE + off, z)
                        cp.start() if phase == "start" else cp.wait()

    e0 = rt_ref[0:1, :].astype(jnp.int32)
    e1 = rt_ref[1:2, :].astype(jnp.int32)
    eid = lax.broadcasted_iota(jnp.int32, (E_PAD, TT), 0)
    sel = jnp.where(e0 == eid, 1.0, jnp.where(e1 == eid, 1.0, 0.0)).astype(BF16)
    r = lax.broadcasted_iota(jnp.int32, (TT, TT), 0)
    c = lax.broadcasted_iota(jnp.int32, (TT, TT), 1)
    earlier = jnp.where(r < c, 1.0, 0.0).astype(BF16)
    rank = jnp.dot(sel, earlier, preferred_element_type=F32)
    ecol = lax.broadcasted_iota(jnp.int32, (E_PAD, 1), 0)
    offp = jnp.zeros((E_PAD, 1), F32)
    for e in range(N_EXPERTS):
        offp = jnp.where(ecol == e, offp_ref[i * N_EXPERTS + e].astype(F32), offp)
    slot_all = rank + offp
    slot0 = jnp.sum(jnp.where(e0 == eid, slot_all, 0.0), axis=0, keepdims=True)
    slot1 = jnp.sum(jnp.where(e1 == eid, slot_all, 0.0), axis=0, keepdims=True)
    sidx = lax.broadcasted_iota(jnp.int32, (SLOTS, TT), 0).astype(F32)
    place = jnp.where(sidx == slot0, 1.0, jnp.where(sidx == slot1, 1.0, 0.0)).astype(BF16)
    xs_scr[slot, :, 0:D_MODEL] = jnp.dot(place, hn_ref[...], preferred_element_type=F32)
    w_slot = jnp.sum(jnp.where(sidx == slot0, rt_ref[2:3, :], jnp.where(sidx == slot1, rt_ref[3:4, :], 0.0)),
                     axis=1, keepdims=True)
    xs_scr[slot, :, D_MODEL:] = jnp.broadcast_to(w_slot, (SLOTS, LANES))

    _start_slab_copies(i, base_ref, offp_ref, np_ref, copy(slot))

    @pl.when(i > 0)
    def _():
        _wait_rows(tot_ref[i - 1], copy(1 - slot))

    @pl.when(i == pl.num_programs(0) - 1)
    def _():
        _wait_rows(tot_ref[i], copy(slot))


def _dispatch(tables, hn, route_t):
    grid_spec = pltpu.PrefetchScalarGridSpec(
        num_scalar_prefetch=5,
        grid=(N_TT,),
        in_specs=[pl.BlockSpec((TT, D_MODEL), lambda i, *_: (i, 0)),
                  pl.BlockSpec((SUBLANES, TT), lambda i, *_: (0, i))],
        out_specs=pl.BlockSpec(memory_space=pl.ANY),
        scratch_shapes=[pltpu.VMEM((2, SLOTS, XS_WIDTH), F32), pltpu.VMEM((TM_MOE, XS_WIDTH), F32),
                        pltpu.SemaphoreType.DMA((2,)), pltpu.SemaphoreType.DMA],
    )
    return pl.pallas_call(
        _dispatch_kernel,
        grid_spec=grid_spec,
        out_shape=jax.ShapeDtypeStruct((P_PAD, XS_WIDTH), F32),
        compiler_params=pltpu.CompilerParams(dimension_semantics=("arbitrary",), vmem_limit_bytes=VMEM_LIMIT),
        name="moe_dispatch",
    )(tables["base"], tables["offp"], tables["np"], tables["total"], tables["tail"], hn, route_t)


def _moe_kernel(te_ref, nv_ref, nxt_ref, par_ref, first_ref, last_ref, x_ref, wg_hbm, wu_hbm, wd_hbm, y_ref,
                stage_g, stage_u, stage_d, wgu_scr, wd_scr, sem):
    i = pl.program_id(0)
    valid = i < nv_ref[0]
    nxt = nxt_ref[i]

    def fetch(e):
        return [pltpu.make_async_copy(wg_hbm.at[e], stage_g, sem.at[0]),
                pltpu.make_async_copy(wu_hbm.at[e], stage_u, sem.at[1]),
                pltpu.make_async_copy(wd_hbm.at[e], stage_d, sem.at[2])]

    def round_into(s):
        wgu_scr[s, :, 0:D_FF_EXPERT] = stage_g[...].astype(BF16)
        wgu_scr[s, :, D_FF_EXPERT:] = stage_u[...].astype(BF16)
        wd_scr[s] = stage_d[...].astype(BF16)

    @pl.when(i == 0)
    def _():
        for cp in fetch(te_ref[0]):
            cp.start()
        for cp in fetch(te_ref[0]):
            cp.wait()
        round_into(par_ref[0])

    @pl.when(jnp.logical_and(valid, jnp.logical_and(first_ref[i] == 1, nxt >= 0)))
    def _():
        for cp in fetch(nxt):
            cp.start()

    @pl.when(valid)
    def _():
        s = par_ref[i]
        x = x_ref[:, 0:D_MODEL].astype(BF16)
        h = jnp.dot(x, wgu_scr[s], preferred_element_type=F32)
        a = (jax.nn.silu(h[:, :D_FF_EXPERT]) * h[:, D_FF_EXPERT:]).astype(BF16)
        y_ref[...] = jnp.dot(a, wd_scr[s], preferred_element_type=F32) * x_ref[:, D_MODEL:D_MODEL + 1]

    @pl.when(jnp.logical_and(valid, jnp.logical_and(last_ref[i] == 1, nxt >= 0)))
    def _():
        for cp in fetch(nxt):
            cp.wait()
        round_into(1 - par_ref[i])

    @pl.when(jnp.logical_not(valid))
    def _():
        y_ref[...] = jnp.zeros_like(y_ref)


def _moe(sched, xs, wg, wu, wd):
    grid_spec = pltpu.PrefetchScalarGridSpec(
        num_scalar_prefetch=6,
        grid=(N_MOE_TILES,),
        in_specs=[pl.BlockSpec((TM_MOE, XS_WIDTH), lambda i, *_: (i, 0)),
                  pl.BlockSpec(memory_space=pl.ANY), pl.BlockSpec(memory_space=pl.ANY),
                  pl.BlockSpec(memory_space=pl.ANY)],
        out_specs=pl.BlockSpec((TM_MOE, D_MODEL), lambda i, *_: (i, 0)),
        scratch_shapes=[pltpu.VMEM((D_MODEL, D_FF_EXPERT), F32), pltpu.VMEM((D_MODEL, D_FF_EXPERT), F32),
                        pltpu.VMEM((D_FF_EXPERT, D_MODEL), F32),
                        pltpu.VMEM((2, D_MODEL, 2 * D_FF_EXPERT), BF16), pltpu.VMEM((2, D_FF_EXPERT, D_MODEL), BF16),
                        pltpu.SemaphoreType.DMA((3,))],
    )
    return pl.pallas_call(
        _moe_kernel,
        grid_spec=grid_spec,
        out_shape=jax.ShapeDtypeStruct((P_PAD, D_MODEL), F32),
        compiler_params=pltpu.CompilerParams(dimension_semantics=("arbitrary",), vmem_limit_bytes=VMEM_LIMIT),
        name="moe_experts",
    )(sched["expert"], sched["n_valid"], sched["next"], sched["parity"], sched["first"], sched["last"],
      xs, wg, wu, wd)


def _combine_kernel(base_ref, offp_ref, np_ref, tot_ref, x_ref, route_ref, y_hbm, fn_ref, o_ref, ybuf, sem):
    i = pl.program_id(0)
    slot = i % 2

    def copy(buf_slot):
        return lambda s, d, z: pltpu.make_async_copy(y_hbm.at[_aligned(d, z), :],
                                                     ybuf.at[buf_slot, _aligned(s, z), :], sem.at[buf_slot])

    @pl.when(i == 0)
    def _():
        ybuf[...] = jnp.zeros_like(ybuf)
        _start_slab_copies(0, base_ref, offp_ref, np_ref, copy(0))

    @pl.when(i + 1 < pl.num_programs(0))
    def _():
        _start_slab_copies(i + 1, base_ref, offp_ref, np_ref, copy(1 - slot))

    pad = jnp.zeros((LANES - SUBLANES, TT), F32)
    route = jnp.concatenate([route_ref[...], pad], axis=0).T
    e0 = route[:, 0:1].astype(jnp.int32)
    e1 = route[:, 1:2].astype(jnp.int32)
    eid =lax.broadcasted_iota(jnp.int32, (TT, LANES), 1)
    sel = jnp.where(e0 == eid, 1.0, jnp.where(e1 == eid, 1.0, 0.0)).astype(BF16)
    r = lax.broadcasted_iota(jnp.int32, (TT, TT), 0)
    c = lax.broadcasted_iota(jnp.int32, (TT, TT), 1)
    earlier = jnp.where(c < r, 1.0, 0.0).astype(BF16)
    rank = jnp.dot(earlier, sel, preferred_element_type=F32)
    erow = lax.broadcasted_iota(jnp.int32, (1, LANES), 1)
    offp = jnp.zeros((1, LANES), F32)
    for e in range(N_EXPERTS):
        offp = jnp.where(erow == e, offp_ref[i * N_EXPERTS + e].astype(F32), offp)
    slot_all = rank + offp
    slot0 = jnp.sum(jnp.where(e0 == eid, slot_all, 0.0), axis=1, keepdims=True)
    slot1 = jnp.sum(jnp.where(e1 == eid, slot_all, 0.0), axis=1, keepdims=True)
    sidx = lax.broadcasted_iota(jnp.int32, (TT, SLOTS), 1).astype(F32)
    pick = jnp.where(sidx == slot0, 1.0, jnp.where(sidx == slot1, 1.0, 0.0)).astype(BF16)

    _wait_rows(tot_ref[i], copy(slot))
    y = ybuf[slot]
    yh = y.astype(BF16)
    yl = (y - yh.astype(F32)).astype(BF16)
    moe = jnp.dot(pick, yh, preferred_element_type=F32) + jnp.dot(pick, yl, preferred_element_type=F32)
    o_ref[...] = _rms(x_ref[...] + moe, fn_ref[...])


def _combine(tables, x2, route_t, y_sorted, fn):
    grid_spec = pltpu.PrefetchScalarGridSpec(
        num_scalar_prefetch=4,
        grid=(N_TT,),
        in_specs=[pl.BlockSpec((TT, D_MODEL), lambda i, *_: (i, 0)),
                  pl.BlockSpec((SUBLANES, TT), lambda i, *_: (0, i)),
                  pl.BlockSpec(memory_space=pl.ANY),
                  pl.BlockSpec((1, D_MODEL), lambda i, *_: (0, 0))],
        out_specs=pl.BlockSpec((TT, D_MODEL), lambda i, *_: (i, 0)),
        scratch_shapes=[pltpu.VMEM((2, SLOTS, D_MODEL), F32), pltpu.SemaphoreType.DMA((2,))],
    )
    return pl.pallas_call(
        _combine_kernel,
        grid_spec=grid_spec,
        out_shape=jax.ShapeDtypeStruct((TOKENS, D_MODEL), F32),
        compiler_params=pltpu.CompilerParams(dimension_semantics=("arbitrary",), vmem_limit_bytes=VMEM_LIMIT),
        name="moe_combine",
    )(tables["base"], tables["offp"], tables["np"], tables["total"], x2, route_t, y_sorted, fn)


def _routing_tables(route_t):
    experts = route_t[:TOP_K, :].astype(jnp.int32)
    eid = jnp.arange(N_EXPERTS, dtype=jnp.int32)[:, None, None]
    chosen = jnp.sum((experts[None, :, :] == eid).astype(jnp.int32), axis=1)
    cnt = jnp.sum(chosen.reshape(N_EXPERTS, N_TT, TT), axis=2).T
    npad = (cnt + SUBLANES - 1) // SUBLANES * SUBLANES
    offp = jnp.cumsum(npad, axis=1) - npad
    total = jnp.sum(npad, axis=1)
    group = jnp.sum(npad, axis=0)
    group_pad = (group + TM_MOE - 1) // TM_MOE * TM_MOE
    group_end = jnp.cumsum(group_pad)
    group_start = group_end - group_pad
    base = group_start[None, :] + jnp.cumsum(npad, axis=0) - npad
    tail_start = group_start + group
    is_last = jnp.arange(N_EXPERTS) == N_EXPERTS - 1
    tail_len = jnp.where(is_last, P_PAD - tail_start, group_pad - group)
    tile_start = jnp.arange(N_MOE_TILES, dtype=jnp.int32) * TM_MOE
    tile_expert = jnp.minimum(jnp.sum((tile_start[:, None] >= group_end[None, :]).astype(jnp.int32), axis=1),
                              N_EXPERTS - 1)
    ids = jnp.arange(N_EXPERTS, dtype=jnp.int32)
    nonempty = group_pad > 0
    ordinal = jnp.cumsum(nonempty.astype(jnp.int32)) - 1
    later = jnp.logical_and(nonempty[None, :], ids[None, :] > ids[:, None])
    next_expert = jnp.min(jnp.where(later, ids[None, :], N_EXPERTS), axis=1)
    next_expert = jnp.where(next_expert == N_EXPERTS, -1, next_expert)
    is_tile_expert = tile_expert[:, None] == ids[None, :]
    per_tile = lambda table: jnp.sum(jnp.where(is_tile_expert, table[None, :], 0), axis=1)
    sched = {"expert": tile_expert, "n_valid": (group_end[-1] // TM_MOE).reshape(1),
             "next": per_tile(next_expert), "parity": per_tile(ordinal) % 2,
             "first": tile_start == per_tile(group_start),
             "last": tile_start + TM_MOE == per_tile(group_end)}
    tables = {"base": base.reshape(-1), "offp": offp.reshape(-1), "np": npad.reshape(-1), "total": total,
              "tail": jnp.concatenate([tail_start, tail_len])}
    as_i32 = lambda d: {k: v.astype(jnp.int32) for k, v in d.items()}
    return as_i32(tables), as_i32(sched)


WIN_ROWS = 256


def _win_layout_kernel(wt_ref, o_ref):
    wt = wt_ref[0]
    cols = wt.shape[1]
    kr_tile = jnp.concatenate([jnp.zeros((QK_NOPE, cols), F32), wt[REF_KR:REF_GATE, :],
                               jnp.zeros((HEAD_PAD - QK_DIM, cols), F32)], axis=0)
    conv_rows = [wt[part * D_CONV + c * CONV_CHUNK:part * D_CONV + (c + 1) * CONV_CHUNK, :]
                 for c in range(D_CONV // CONV_CHUNK) for part in range(3)]
    grouped = jnp.concatenate(conv_rows + [wt[REF_GATE:, :], wt[REF_CQ:REF_KR, :], kr_tile], axis=0)
    o_ref[0] = grouped.T.astype(BF16)


def _win_layout(w_in):
    wt = jnp.swapaxes(w_in, 1, 2)
    d_in = wt.shape[1]
    return pl.pallas_call(
        _win_layout_kernel,
        grid=(DEPTH, D_MODEL // WIN_ROWS),
        in_specs=[pl.BlockSpec((1, d_in, WIN_ROWS), lambda l, r: (l, 0, r))],
        out_specs=pl.BlockSpec((1, WIN_ROWS, D_IN_PAD), lambda l, r: (l, r, 0)),
        out_shape=jax.ShapeDtypeStruct((DEPTH, D_MODEL, D_IN_PAD), BF16),
        compiler_params=pltpu.CompilerParams(dimension_semantics=("arbitrary",) * 2, vmem_limit_bytes=VMEM_LIMIT),
        name="win_layout",
    )(wt)


def _layer_weights(w_uq, w_ukv):
    wq = w_uq.T.astype(BF16)

    wkv = w_ukv.reshape(KV_LORA, N_HEADS, QK_NOPE + V_HEAD)
    wuk = jnp.concatenate([wkv[:, :, :QK_NOPE], jnp.zeros((KV_LORA, N_HEADS, HEAD_PAD - QK_NOPE), F32)], axis=2)
    wuk = wuk.reshape(KV_LORA, N_HEADS * HEAD_PAD).astype(BF16)
    wuv = wkv[:, :, QK_NOPE:].reshape(KV_LORA, N_HEADS * V_HEAD).T.astype(BF16)
    return wq, wuk, wuv


def kernel(x, positions, attn_norm, w_in, conv_w, w_conv_out, q_norm, w_uq, kv_norm, w_ukv, w_mla_out, w_o, ffn_norm,
           w_gate, w_up, w_down, router, w_gate_e, w_up_e, w_down_e, final_norm):
    assert x.shape == (BATCH, SEQ, D_MODEL) and positions.shape == (BATCH, SEQ)
    cos_t, sin_t = _rope_tables(positions)
    xt = x.reshape(TOKENS, D_MODEL)
    out = None
    win_all = _win_layout(w_in)
    for l in range(DEPTH):
        wq, wuk, wuv = _layer_weights(w_uq[l], w_ukv[l])
        q, k, v, gc, sg = _front(l, xt, attn_norm[l].reshape(1, D_MODEL), win_all, conv_w[l],
                                 w_conv_out[l].astype(BF16),
                                 q_norm[l].reshape(1, Q_LORA), wq, kv_norm[l].reshape(1, KV_LORA), wuk, wuv,
                                 cos_t, sin_t)
        attn = _attention(q, k, v)
        wmo = w_mla_out[l].astype(BF16)
        wo = w_o[l].astype(BF16)
        fn = ffn_norm[l].reshape(1, D_MODEL)
        if l % 2 == 0:
            j = l // 2
            xt = _merge(attn, gc, sg, xt, wmo, wo, fn,
                        ffn=(w_gate[j].astype(BF16), w_up[j].astype(BF16), w_down[j].astype(BF16)))
        else:
            j = l // 2
            router_pad = jnp.pad(router[j].T, ((0, E_PAD - N_EXPERTS), (0, 0)))
            x2, hn, route_t = _merge(attn, gc, sg, xt, wmo, wo, fn, router_pad)
            tables, sched = _routing_tables(route_t)
            xs = _dispatch(tables, hn, route_t)
            y_sorted = _moe(sched, xs, w_gate_e[j], w_up_e[j], w_down_e[j])
            out = _combine(tables, x2, route_t, y_sorted, final_norm.reshape(1, D_MODEL))
    return out.reshape(BATCH, SEQ, D_MODEL)
```

```python
import math

import jax
import jax.numpy as jnp
from jax import lax
from jax.experimental import pallas as pl
from jax.experimental.pallas import tpu as pltpu

F32 = jnp.float32
BF16 = jnp.bfloat16

D_MODEL = 1024
BATCH = 8
SEQ = 2048
TOKENS = BATCH * SEQ
DEPTH = 2
D_CONV = 512
CONV_WIDTH = 3
N_HEADS = 8
QK_NOPE = 64
QK_ROPE = 32
HALF_ROPE = QK_ROPE // 2
V_HEAD = 64
Q_LORA = 384
KV_LORA = 256
ROPE_THETA = 10000.0
D_FF = 2816
N_EXPERTS = 8
TOP_K = 2
D_FF_EXPERT = 1408
EPS = 1e-6

LANES = 128
HEAD_PAD = LANES
QK_DIM = QK_NOPE + QK_ROPE
Q_SCALE = (1.0 / math.sqrt(QK_DIM)) * math.log2(math.e)
NEG_BIG = -1e30

OFF_BCU = 0
OFF_GATE = 3 * D_CONV
OFF_SMALL = OFF_GATE + 2 * D_MODEL
D_IN_PAD = OFF_SMALL + Q_LORA + KV_LORA + LANES
REF_CQ = 3 * D_CONV
REF_KR = REF_CQ + Q_LORA + KV_LORA
REF_GATE = REF_KR + QK_ROPE

CONV_CHUNK = 256
TM_FRONT = 512
TQ = 256
TK = 128
ATTN_PAIRS = 2
ATTN_LOOKAHEAD = 3
V_ROWS = V_HEAD + 16
FF_CHUNK = 256
VMEM_LIMIT = 56 * 1024 * 1024

SUBLANES = 8
TM_MOE = 256
TT = 256
N_TT = TOKENS // TT
E_PAD = 16
XS_WIDTH = D_MODEL + LANES
N_PAIRS = TOKENS * TOP_K
SLOTS =TT * TOP_K + N_EXPERTS * SUBLANES
_MAX_ROWS = N_PAIRS + N_TT * N_EXPERTS * (SUBLANES - 1) + N_EXPERTS * (TM_MOE - SUBLANES)
N_MOE_TILES = -(-_MAX_ROWS // TM_MOE)
P_PAD = N_MOE_TILES * TM_MOE
SLAB_PIECES = tuple(TT >> s for s in range(6))
WAIT_PIECES = (512,) + SLAB_PIECES
assert SLAB_PIECES[-1] == SUBLANES and SLOTS < 2 * WAIT_PIECES[0]


def _rms(x, g):
    return x * lax.rsqrt(jnp.mean(x * x, axis=-1, keepdims=True) + EPS) * g


def _const_spec(shape):
    nd = len(shape)
    return pl.BlockSpec(shape, lambda *_: (0,) * nd, pipeline_mode=pl.Buffered(1))


def _trig_kernel(pos_ref, invf_ref, cos_ref, sin_ref):
    ang = invf_ref[...] * pos_ref[...].astype(F32)
    cos_ref[...] = jnp.cos(ang)
    sin_ref[...] = jnp.sin(ang)


def _rope_tables(positions):
    inv_freq = ROPE_THETA ** (-jnp.arange(0, QK_ROPE, 2, dtype=F32) / QK_ROPE)
    return pl.pallas_call(
        _trig_kernel,
        out_shape=(jax.ShapeDtypeStruct((HALF_ROPE, TOKENS), F32),) * 2,
        name="rope_trig",
    )(positions.reshape(1, TOKENS), inv_freq.reshape(HALF_ROPE, 1))


def _dot_nt(a, b):
    return lax.dot_general(a, b, (((1,), (1,)), ((), ())), preferred_element_type=F32)


def _front_kernel(x_ref, an_ref, win_ref, cw_ref, wco_ref, qn_ref, wq_ref, kvn_ref, wuk_ref, wuv_ref,
                  cost_ref, sint_ref, q_out, k_out, v_out, gc_out, sg_out, cu_scr):
    tm = x_ref.shape[0]
    i = pl.program_id(0)
    xn = _rms(x_ref[...], an_ref[...]).astype(BF16)

    def proj(a, b):
        return jnp.dot(xn, win_ref[0, :, a:b], preferred_element_type=F32)

    small = proj(OFF_SMALL, D_IN_PAD)

    cq = cost_ref[...] * Q_SCALE
    sq = sint_ref[...] * Q_SCALE
    cqn = _rms(small[:, :Q_LORA], qn_ref[...]).astype(BF16)
    qt = _dot_nt(wq_ref[...], cqn)
    zero_pad = jnp.zeros((HEAD_PAD - QK_DIM, tm), BF16)
    for h in range(N_HEADS):
        src = h * QK_DIM
        dst = h * HEAD_PAD
        x1 = qt[src + QK_NOPE:src + QK_NOPE + HALF_ROPE, :]
        x2 = qt[src + QK_NOPE + HALF_ROPE:src + QK_DIM, :]
        q_out[dst:dst + QK_NOPE, :] = (qt[src:src + QK_NOPE, :] * Q_SCALE).astype(BF16)
        q_out[dst + QK_NOPE:dst + QK_NOPE + HALF_ROPE, :] = (x1 * cq - x2 * sq).astype(BF16)
        q_out[dst + QK_NOPE + HALF_ROPE:dst + QK_DIM, :] = (x2 * cq + x1 * sq).astype(BF16)
        q_out[dst + QK_DIM:dst + HEAD_PAD, :] = zero_pad

    cos_t = cost_ref[...]
    sin_t = sint_ref[...]
    z_nope = jnp.zeros((QK_NOPE, tm), F32)
    z_rope = jnp.zeros((HALF_ROPE, tm), F32)
    z_pad = jnp.zeros((HEAD_PAD - QK_DIM, tm), F32)
    ck = jnp.concatenate([z_nope, cos_t, cos_t, z_pad], axis=0).T
    sk_x1 = jnp.concatenate([z_nope, -sin_t, z_rope, z_pad], axis=0).T
    sk_x2 = jnp.concatenate([z_nope, z_rope, sin_t, z_pad], axis=0).T
    kr = small[:, Q_LORA + KV_LORA:]
    kpe = (kr * ck + pltpu.roll(kr, HEAD_PAD - HALF_ROPE, axis=1) * sk_x1 + pltpu.roll(kr, HALF_ROPE, axis=1) * sk_x2)
    ckvn = _rms(small[:, Q_LORA:Q_LORA + KV_LORA], kvn_ref[...]).astype(BF16)
    kn = jnp.dot(ckvn, wuk_ref[...], preferred_element_type=F32)
    for h in range(N_HEADS):
        blk = slice(h * HEAD_PAD, (h + 1) * HEAD_PAD)
        k_out[:, blk] = (kn[:, blk] + kpe).astype(BF16)
    vt = _dot_nt(wuv_ref[...], ckvn).astype(BF16)
    sub = lax.broadcasted_iota(jnp.int32, (V_ROWS - V_HEAD, TK), 0)
    ones_rows = jnp.where(sub == 0, 1.0, 0.0).astype(BF16)
    for c in range(tm // TK):
        for h in range(N_HEADS):
            v_out[c, h * V_ROWS:h * V_ROWS + V_HEAD, :] = vt[h * V_HEAD:(h + 1) * V_HEAD, c * TK:(c + 1) * TK]
            v_out[c, h * V_ROWS + V_HEAD:(h + 1) * V_ROWS, :] = ones_rows

    @pl.when(i % (SEQ // tm) == 0)
    def _():
        cu_scr[0:8, :] = jnp.zeros((8, D_CONV), F32)

    cw = cw_ref[...]
    y_conv = None
    for c in range(D_CONV // CONV_CHUNK):
        ch = slice(c * CONV_CHUNK, (c + 1) * CONV_CHUNK)
        bcu = proj(OFF_BCU + 3 * c * CONV_CHUNK, OFF_BCU + 3 * (c + 1) * CONV_CHUNK)
        b_g = bcu[:, :CONV_CHUNK]
        cu = bcu[:, CONV_CHUNK:2 * CONV_CHUNK] * bcu[:, 2 * CONV_CHUNK:]
        cu_scr[8:8 + tm, ch] = cu
        prev1 = cu_scr[7:7 + tm, ch]
        prev2 = cu_scr[6:6 + tm, ch]
        conv = prev2 * cw[0:1, ch] + prev1 * cw[1:2, ch] + cu * cw[2:3, ch]
        cu_scr[0:8, ch] = cu[tm - 8:, :]
        part = jnp.dot((b_g * conv).astype(BF16), wco_ref[ch, :], preferred_element_type=F32)
        y_conv = part if y_conv is None else y_conv + part

    gates = proj(OFF_GATE, OFF_SMALL)
    gc_out[...] = (jax.nn.sigmoid(gates[:, :D_MODEL]) * y_conv).astype(BF16)
    sg_out[...] = jax.nn.sigmoid(gates[:, D_MODEL:]).astype(BF16)


def _front(layer, x, an, win_all, cw, wco, qn, wq, kvn, wuk, wuv, cos_t, sin_t):
    tm = TM_FRONT
    row = lambda n: pl.BlockSpec((tm, n), lambda i: (i, 0))
    col = lambda n: pl.BlockSpec((n, tm), lambda i: (0, i))
    win_spec = pl.BlockSpec((1, D_MODEL, D_IN_PAD), lambda i: (layer, 0, 0), pipeline_mode=pl.Buffered(1))
    return pl.pallas_call(
        _front_kernel,
        grid=(TOKENS // tm,),
        in_specs=[row(D_MODEL), _const_spec((1, D_MODEL)), win_spec,
                  _const_spec((CONV_WIDTH, D_CONV)), _const_spec((D_CONV, D_MODEL)),
                  _const_spec((1, Q_LORA)), _const_spec((N_HEADS * QK_DIM, Q_LORA)),
                  _const_spec((1, KV_LORA)), _const_spec((KV_LORA, N_HEADS * HEAD_PAD)),
                  _const_spec((N_HEADS * V_HEAD, KV_LORA)),
                  col(HALF_ROPE), col(HALF_ROPE)],
        out_specs=[col(N_HEADS * HEAD_PAD), row(N_HEADS * HEAD_PAD),
                   pl.BlockSpec((tm // TK, N_HEADS * V_ROWS, TK), lambda i: (i, 0, 0)),
                   row(D_MODEL), row(D_MODEL)],
        out_shape=[jax.ShapeDtypeStruct((N_HEADS * HEAD_PAD, TOKENS), BF16),
                   jax.ShapeDtypeStruct((TOKENS, N_HEADS * HEAD_PAD), BF16),
                   jax.ShapeDtypeStruct((TOKENS // TK, N_HEADS * V_ROWS, TK), BF16),
                   jax.ShapeDtypeStruct((TOKENS, D_MODEL), BF16),
                   jax.ShapeDtypeStruct((TOKENS, D_MODEL), BF16)],
        scratch_shapes=[pltpu.VMEM((tm + 8, D_CONV), F32)],
        compiler_params=pltpu.CompilerParams(dimension_semantics=("arbitrary",), vmem_limit_bytes=VMEM_LIMIT),
        name="front",
    )(x, an, win_all, cw, wco, qn, wq, kvn, wuk, wuv, cos_t, sin_t)


def _attn_items():
    items = []
    for pair in range(ATTN_PAIRS):
        for i in range(SEQ // TQ):
            for ks in range(0, (i + 1) * TQ, TK):
                items.append((pair, i, ks, TK, ks + TK == (i + 1) * TQ))
    return items


def _attn_kernel(qt_ref, k_ref, vt_ref, o_ref):
    row = lax.broadcasted_iota(jnp.int32, (TK, TQ), 0)
    col = lax.broadcasted_iota(jnp.int32, (TK, TQ), 1)

    def scores(item):
        pair, i, ks, n, _ = item
        sts = []
        for h in range(2 * pair, 2 * pair + 2):
            hs = slice(h * HEAD_PAD, (h + 1) * HEAD_PAD)
            st = jnp.dot(k_ref[ks:ks + n, hs], qt_ref[hs, i * TQ:(i + 1) * TQ], preferred_element_type=F32)
            if ks + n > i * TQ:
                st = jnp.where(row + (ks - i * TQ) <= col, st, NEG_BIG)
            sts.append(st)
        return sts

    def consume(item, sts, carry):
        pair, i, ks, n, _ = item
        stats = []
        for j in range(2):
            m, _ = carry[j]
            m_new = jnp.maximum(m, jnp.max(sts[j], axis=0, keepdims=True))
            stats.append((m_new, jnp.exp2(m - m_new), jnp.exp2(sts[j] - m_new).astype(BF16)))
        new = []
        for j in range(2):
            h = 2 * pair + j
            m_new, alpha, p = stats[j]
            vt = vt_ref[ks // TK, h * V_ROWS:(h + 1) * V_ROWS, ks % TK:ks % TK + n]
            new.append((m_new, alpha * carry[j][1] + jnp.dot(vt, p, preferred_element_type=F32)))
        return new

    items = _attn_items()
    init = [(jnp.full((1, TQ), NEG_BIG, F32), jnp.zeros((V_ROWS, TQ), F32)) for _ in range(2)]
    pending = [scores(it) for it in items[:ATTN_LOOKAHEAD]]
    carry = [init] * ATTN_PAIRS
    for t, item in enumerate(items):
        if t + ATTN_LOOKAHEAD < len(items):
            pending.append(scores(items[t + ATTN_LOOKAHEAD]))
        pair, i = item[0], item[1]
        carry[pair] = consume(item, pending.pop(0), carry[pair])
        if item[4]:
            out_t = jnp.concatenate([acc[0:V_HEAD] / acc[V_HEAD:V_HEAD + 1] for (_, acc) in carry[pair]], axis=0)
            o_ref[i * TQ:(i + 1) * TQ, pair * 2 * V_HEAD:(pair + 1) * 2 * V_HEAD] = out_t.T.astype(BF16)
            carry[pair] = init


def _attention(qt, k, vt):
    heads = 2 * ATTN_PAIRS
    return pl.pallas_call(
        _attn_kernel,
        grid=(BATCH, N_HEADS // heads),
        in_specs=[pl.BlockSpec((heads * HEAD_PAD, SEQ), lambda b, hp: (hp, b)),
                  pl.BlockSpec((SEQ, heads * HEAD_PAD), lambda b, hp: (b, hp)),
                  pl.BlockSpec((SEQ // TK, heads * V_ROWS, TK), lambda b, hp: (b, hp, 0))],
        out_specs=pl.BlockSpec((SEQ, heads * V_HEAD), lambda b, hp: (b, hp)),
        out_shape=jax.ShapeDtypeStruct((TOKENS, N_HEADS * V_HEAD), BF16),
        compiler_params=pltpu.CompilerParams(dimension_semantics=("arbitrary",) * 2, vmem_limit_bytes=VMEM_LIMIT),
        name="attention",
    )(qt, k, vt)


def _merge_core(attn_ref, gc_ref, sg_ref, x_ref, wmo_ref, wo_ref, fn_ref):
    y_mla = jnp.dot(attn_ref[...], wmo_ref[...], preferred_element_type=F32)
    merged = gc_ref[...].astype(F32) + sg_ref[...].astype(F32) * y_mla
    x2 = x_ref[...] + jnp.dot(merged.astype(BF16), wo_ref[...], preferred_element_type=F32)
    return x2, _rms(x2, fn_ref[...])


def _merge_ffn_kernel(attn_ref, gc_ref, sg_ref, x_ref, wmo_ref, wo_ref, fn_ref, wg_ref, wu_ref, wd_ref, o_ref):
    x2, hn = _merge_core(attn_ref, gc_ref, sg_ref, x_ref, wmo_ref, wo_ref, fn_ref)
    hn = hn.astype(BF16)
    acc = x2
    for c in range(D_FF // FF_CHUNK):
        cs = slice(c * FF_CHUNK, (c + 1) * FF_CHUNK)
        g = jnp.dot(hn, wg_ref[:, cs], preferred_element_type=F32)
        u = jnp.dot(hn, wu_ref[:, cs], preferred_element_type=F32)
        a = (jax.nn.silu(g) * u).astype(BF16)
        acc = acc + jnp.dot(a, wd_ref[cs, :], preferred_element_type=F32)
    o_ref[...] = acc


def _merge_route_kernel(attn_ref, gc_ref, sg_ref, x_ref, wmo_ref, wo_ref, fn_ref, router_ref,
                        x2_out, hn_out, route_t_out):
    x2, hn = _merge_core(attn_ref, gc_ref, sg_ref, x_ref, wmo_ref, wo_ref, fn_ref)
    x2_out[...] = x2
    hb = hn.astype(BF16)
    hn_out[...] = hb
    hl = (hn - hb.astype(F32)).astype(BF16)
    r = router_ref[...]
    rh = r.astype(BF16)
    rl = (r - rh.astype(F32)).astype(BF16)
    logits = _dot_nt(rh, hb) + _dot_nt(rh, hl) + _dot_nt(rl, hb)
    erow = lax.broadcasted_iota(jnp.int32, logits.shape, 0)
    logits = jnp.where(erow < N_EXPERTS, logits, -jnp.inf)
    m1 = jnp.max(logits, axis=0, keepdims=True)
    i1 = jnp.min(jnp.where(logits == m1, erow, E_PAD), axis=0, keepdims=True)
    rest = jnp.where(erow == i1, -jnp.inf, logits)
    m2 = jnp.max(rest, axis=0, keepdims=True)
    i2 = jnp.min(jnp.where(rest == m2, erow, E_PAD), axis=0, keepdims=True)
    t = jnp.exp(m2 - m1)
    w1 = 1.0 / (1.0 + t)
    w2 = t * w1
    route_t = jnp.where(erow == 0, i1.astype(F32),
                        jnp.where(erow == 1, i2.astype(F32), jnp.where(erow == 2, w1, jnp.where(erow == 3, w2, 0.0))))
    route_t_out[...] = route_t[0:SUBLANES, :]


def _merge(attn, gc, sg, x, wmo, wo, fn, router=None, ffn=None):
    tm = TM_FRONT
    row = lambda n: pl.BlockSpec((tm, n), lambda i: (i, 0))
    in_specs = [row(N_HEADS * V_HEAD), row(D_MODEL), row(D_MODEL), row(D_MODEL),
                _const_spec((N_HEADS * V_HEAD, D_MODEL)), _const_spec((D_MODEL, D_MODEL)), _const_spec((1, D_MODEL))]
    args = [attn, gc, sg, x, wmo, wo, fn]
    if router is None:
        kern = _merge_ffn_kernel
        in_specs += [_const_spec((D_MODEL, D_FF)), _const_spec((D_MODEL, D_FF)), _const_spec((D_FF, D_MODEL))]
        args += list(ffn)
        out_specs = row(D_MODEL)
        out_shape = jax.ShapeDtypeStruct((TOKENS, D_MODEL), F32)
        name = "merge_ffn"
    else:
        kern = _merge_route_kernel
        in_specs.append(_const_spec((E_PAD, D_MODEL)))
        args.append(router)
        out_specs = [row(D_MODEL), row(D_MODEL), pl.BlockSpec((SUBLANES, tm), lambda i: (0, i))]
        out_shape = [jax.ShapeDtypeStruct((TOKENS, D_MODEL), F32), jax.ShapeDtypeStruct((TOKENS, D_MODEL), BF16),
                     jax.ShapeDtypeStruct((SUBLANES, TOKENS), F32)]
        name = "merge_route"
    return pl.pallas_call(
        kern, grid=(TOKENS // tm,), in_specs=in_specs, out_specs=out_specs, out_shape=out_shape,
        compiler_params=pltpu.CompilerParams(dimension_semantics=("arbitrary",), vmem_limit_bytes=VMEM_LIMIT),
        name=name,
    )(*args)


def _pieces(n, sizes):
    return [((n & z) != 0, n & ~(2 * z - 1), z) for z in sizes]


def _aligned(rows, z):
    return pl.ds(rows if isinstance(rows, int) else pl.multiple_of(rows, SUBLANES), z)


def _start_slab_copies(tile, base_ref, offp_ref, np_ref, make_copy):
    for e in range(N_EXPERTS):
        n = np_ref[tile * N_EXPERTS + e]
        src = offp_ref[tile * N_EXPERTS + e]
        dst = base_ref[tile * N_EXPERTS + e]
        for pred, off, z in _pieces(n, SLAB_PIECES):
            @pl.when(pred)
            def _(off=off, z=z):
                make_copy(src + off, dst + off, z).start()


def _wait_rows(total, make_copy):
    for pred, _, z in _pieces(total, WAIT_PIECES):
        @pl.when(pred)
        def _(z=z):
            make_copy(0, 0, z).wait()


def _tile_placement(rt_ref, offp_ref, i):
    e0 = rt_ref[0:1, :].astype(jnp.int32)
    e1 = rt_ref[1:2, :].astype(jnp.int32)
    eid = lax.broadcasted_iota(jnp.int32, (E_PAD, TT), 0)
    sel = jnp.where(e0 == eid, 1.0, jnp.where(e1 == eid, 1.0, 0.0)).astype(BF16)
    r = lax.broadcasted_iota(jnp.int32, (TT, TT), 0)
    c = lax.broadcasted_iota(jnp.int32, (TT, TT), 1)
    earlier = jnp.where(r < c, 1.0, 0.0).astype(BF16)
    rank = jnp.dot(sel, earlier, preferred_element_type=F32)
    ecol = lax.broadcasted_iota(jnp.int32, (E_PAD, 1), 0)
    offp = jnp.zeros((E_PAD, 1), F32)
    for e in range(N_EXPERTS):
        offp = jnp.where(ecol == e, offp_ref[i * N_EXPERTS + e].astype(F32), offp)
    slot_all = rank + offp
    slot0 = jnp.sum(jnp.where(e0 == eid, slot_all, 0.0), axis=0, keepdims=True)
    slot1 = jnp.sum(jnp.where(e1 == eid, slot_all, 0.0), axis=0, keepdims=True)
    sidx = lax.broadcasted_iota(jnp.int32, (SLOTS, TT), 0).astype(F32)
    place = jnp.where(sidx == slot0, 1.0, jnp.where(sidx == slot1, 1.0, 0.0)).astype(BF16)
    return sidx, slot0, slot1, place


def _dispatch_kernel(base_ref, offp_ref, np_ref, tot_ref, tail_ref, hn_ref, rt_ref, xs_hbm, xs_scr, zbuf, sem, zsem):
    i = pl.program_id(0)
    slot = i % 2

    def copy(buf_slot):
        return lambda s, d, z: pltpu.make_async_copy(xs_scr.at[buf_slot, _aligned(s, z), :],
                                                     xs_hbm.at[_aligned(d, z), :], sem.at[buf_slot])

    def zero_copy(d, z):
        return pltpu.make_async_copy(zbuf.at[pl.ds(0, z), :], xs_hbm.at[_aligned(d, z), :], zsem)

    @pl.when(i == 0)
    def _():
        zbuf[...] = jnp.zeros_like(zbuf)
        for phase in ("start", "wait"):
            for e in range(N_EXPERTS):
                t0 = tail_ref[e]
                n = tail_ref[N_EXPERTS + e]
                full = lax.shift_right_logical(n, jnp.int32(TM_MOE.bit_length() - 1))

                def chunk(c, carry, t0=t0, phase=phase):
                    cp = zero_copy(t0 + c * TM_MOE, TM_MOE)
                    cp.start() if phase == "start" else cp.wait()
                    return carry

                lax.fori_loop(0, full, chunk, 0)
                rest = n & (TM_MOE - 1)
                for pred, off, z in _pieces(rest, SLAB_PIECES[1:]):
                    @pl.when(pred)
                    def _(off=off, z=z, t0=t0, full=full, phase=phase):
                        cp = zero_copy(t0 + full * TM_MOE + off, z)
                        cp.start() if phase == "start" else cp.wait()

    sidx, slot0, slot1, place = _tile_placement(rt_ref, offp_ref, i)
    xs_scr[slot, :, 0:D_MODEL] = jnp.dot(place, hn_ref[...], preferred_element_type=F32)
    w_slot = jnp.sum(jnp.where(sidx == slot0, rt_ref[2:3, :], jnp.where(sidx == slot1, rt_ref[3:4, :], 0.0)),
                     axis=1, keepdims=True)
    xs_scr[slot, :, D_MODEL:] = jnp.broadcast_to(w_slot, (SLOTS, LANES))

    _start_slab_copies(i, base_ref, offp_ref, np_ref, copy(slot))

    @pl.when(i > 0)
    def _():
        _wait_rows(tot_ref[i - 1], copy(1 - slot))

    @pl.when(i == pl.num_programs(0) - 1)
    def _():
        _wait_rows(tot_ref[i], copy(slot))


def _dispatch(tables, hn, route_t):
    grid_spec = pltpu.PrefetchScalarGridSpec(
        num_scalar_prefetch=5,
        grid=(N_TT,),
        in_specs=[pl.BlockSpec((TT, D_MODEL), lambda i, *_: (i, 0)),
                  pl.BlockSpec((SUBLANES, TT), lambda i, *_: (0, i))],
        out_specs=pl.BlockSpec(memory_space=pl.ANY),
        scratch_shapes=[pltpu.VMEM((2, SLOTS, XS_WIDTH), F32), pltpu.VMEM((TM_MOE, XS_WIDTH), F32),
                        pltpu.SemaphoreType.DMA((2,)), pltpu.SemaphoreType.DMA],
    )
    return pl.pallas_call(
        _dispatch_kernel,
        grid_spec=grid_spec,
        out_shape=jax.ShapeDtypeStruct((P_PAD, XS_WIDTH), F32),
        compiler_params=pltpu.CompilerParams(dimension_semantics=("arbitrary",), vmem_limit_bytes=VMEM_LIMIT),
        name="moe_dispatch",
    )(tables["base"], tables["offp"], tables["np"], tables["total"], tables["tail"], hn, route_t)


def _moe_kernel(te_ref, nv_ref, nxt_ref, par_ref, first_ref, last_ref, x_ref, wg_hbm, wu_hbm, wd_hbm, y_ref,
                stage_g, stage_u, stage_d, wgu_scr, wd_scr, sem):
    i = pl.program_id(0)
    valid = i < nv_ref[0]
    nxt = nxt_ref[i]

    def fetch(e):
        return [pltpu.make_async_copy(wg_hbm.at[e], stage_g, sem.at[0]),
                pltpu.make_async_copy(wu_hbm.at[e], stage_u, sem.at[1]),
                pltpu.make_async_copy(wd_hbm.at[e], stage_d, sem.at[2])]

    def round_into(s):
        wgu_scr[s, :, 0:D_FF_EXPERT] = stage_g[...].astype(BF16)
        wgu_scr[s, :, D_FF_EXPERT:] = stage_u[...].astype(BF16)
        wd_scr[s] = stage_d[...].astype(BF16)

    @pl.when(i == 0)
    def _():
        for cp in fetch(te_ref[0]):
            cp.start()
        for cp in fetch(te_ref[0]):
            cp.wait()
        round_into(par_ref[0])

    @pl.when(jnp.logical_and(valid, jnp.logical_and(first_ref[i] == 1, nxt >= 0)))
    def _():
        for cp in fetch(nxt):
            cp.start()

    @pl.when(valid)
    def _():
        s = par_ref[i]
        x = x_ref[:, 0:D_MODEL].astype(BF16)
        h = jnp.dot(x, wgu_scr[s], preferred_element_type=F32)
        a = (jax.nn.silu(h[:, :D_FF_EXPERT]) * h[:, D_FF_EXPERT:]).astype(BF16)
        y_ref[...] = jnp.dot(a, wd_scr[s], preferred_element_type=F32) * x_ref[:, D_MODEL:D_MODEL + 1]

    @pl.when(jnp.logical_and(valid, jnp.logical_and(last_ref[i] == 1, nxt >= 0)))
    def _():
        for cp in fetch(nxt):
            cp.wait()
        round_into(1 - par_ref[i])

    @pl.when(jnp.logical_not(valid))
    def _():
        y_ref[...] = jnp.zeros_like(y_ref)


def _moe(sched, xs, wg, wu, wd):
    grid_spec = pltpu.PrefetchScalarGridSpec(
        num_scalar_prefetch=6,
        grid=(N_MOE_TILES,),
        in_specs=[pl.BlockSpec((TM_MOE, XS_WIDTH), lambda i, *_: (i, 0)),
                  pl.BlockSpec(memory_space=pl.ANY), pl.BlockSpec(memory_space=pl.ANY),
                  pl.BlockSpec(memory_space=pl.ANY)],
        out_specs=pl.BlockSpec((TM_MOE, D_MODEL), lambda i, *_: (i, 0)),
        scratch_shapes=[pltpu.VMEM((D_MODEL, D_FF_EXPERT), F32), pltpu.VMEM((D_MODEL, D_FF_EXPERT), F32),
                        pltpu.VMEM((D_FF_EXPERT, D_MODEL), F32),
                        pltpu.VMEM((2, D_MODEL, 2 * D_FF_EXPERT), BF16), pltpu.VMEM((2, D_FF_EXPERT, D_MODEL), BF16),
                        pltpu.SemaphoreType.DMA((3,))],
    )
    return pl.pallas_call(
        _moe_kernel,
        grid_spec=grid_spec,
        out_shape=jax.ShapeDtypeStruct((P_PAD, D_MODEL), F32),
        compiler_params=pltpu.CompilerParams(dimension_semantics=("arbitrary",), vmem_limit_bytes=VMEM_LIMIT),
        name="moe_experts",
    )(sched["expert"], sched["n_valid"], sched["next"], sched["parity"], sched["first"], sched["last"],
      xs, wg, wu, wd)


def _combine_kernel(base_ref, offp_ref, np_ref, tot_ref, x_ref, route_ref, y_hbm, fn_ref, o_ref, ybuf, sem):
    i = pl.program_id(0)
    slot = i % 2

    def copy(buf_slot):
        return lambda s, d, z: pltpu.make_async_copy(y_hbm.at[_aligned(d, z), :],
                                                     ybuf.at[buf_slot, _aligned(s, z), :], sem.at[buf_slot])

    @pl.when(i == 0)
    def _():
        ybuf[...] = jnp.zeros_like(ybuf)
        _start_slab_copies(0, base_ref, offp_ref, np_ref, copy(0))

    @pl.when(i + 1 < pl.num_programs(0))
    def _():
        _start_slab_copies(i + 1, base_ref, offp_ref, np_ref, copy(1 - slot))

    _, _, _, place = _tile_placement(route_ref, offp_ref, i)

    _wait_rows(tot_ref[i], copy(slot))
    y = ybuf[slot]
    yh = y.astype(BF16)
    yl = (y - yh.astype(F32)).astype(BF16)
    tn = (((0,), (0,)), ((), ()))
    moe = (lax.dot_general(place, yh, tn, preferred_element_type=F32)
           + lax.dot_general(place, yl, tn, preferred_element_type=F32))
    o_ref[...] = _rms(x_ref[...] + moe, fn_ref[...])


def _combine(tables, x2, route_t, y_sorted, fn):
    grid_spec = pltpu.PrefetchScalarGridSpec(
        num_scalar_prefetch=4,
        grid=(N_TT,),
        in_specs=[pl.BlockSpec((TT, D_MODEL), lambda i, *_: (i, 0)),
                  pl.BlockSpec((SUBLANES, TT), lambda i, *_: (0, i)),
                  pl.BlockSpec(memory_space=pl.ANY),
                  pl.BlockSpec((1, D_MODEL), lambda i, *_: (0, 0))],
        out_specs=pl.BlockSpec((TT, D_MODEL), lambda i, *_: (i, 0)),
        scratch_shapes=[pltpu.VMEM((2, SLOTS, D_MODEL), F32), pltpu.SemaphoreType.DMA((2,))],
    )
    return pl.pallas_call(
        _combine_kernel,
        grid_spec=grid_spec,
        out_shape=jax.ShapeDtypeStruct((TOKENS, D_MODEL), F32),
        compiler_params=pltpu.CompilerParams(dimension_semantics=("arbitrary",), vmem_limit_bytes=VMEM_LIMIT),
        name="moe_combine",
    )(tables["base"], tables["offp"], tables["np"], tables["total"], x2, route_t, y_sorted, fn)


def _routing_tables(route_t):
    experts = route_t[:TOP_K, :].astype(jnp.int32)
    eid = jnp.arange(N_EXPERTS, dtype=jnp.int32)[:, None, None]
    chosen = jnp.sum((experts[None, :, :] == eid).astype(jnp.int32), axis=1)
    cnt = jnp.sum(chosen.reshape(N_EXPERTS, N_TT, TT), axis=2).T
    npad = (cnt + SUBLANES - 1) // SUBLANES * SUBLANES
    offp = jnp.cumsum(npad, axis=1) - npad
    total = jnp.sum(npad, axis=1)
    group = jnp.sum(npad, axis=0)
    group_pad = (group + TM_MOE - 1) // TM_MOE * TM_MOE
    group_end = jnp.cumsum(group_pad)
    group_start = group_end - group_pad
    base = group_start[None, :] + jnp.cumsum(npad, axis=0) - npad
    tail_start = group_start + group
    is_last = jnp.arange(N_EXPERTS) == N_EXPERTS - 1
    tail_len = jnp.where(is_last, P_PAD - tail_start, group_pad - group)
    tile_start = jnp.arange(N_MOE_TILES, dtype=jnp.int32) * TM_MOE
    tile_expert = jnp.minimum(jnp.sum((tile_start[:, None] >= group_end[None, :]).astype(jnp.int32), axis=1),
                              N_EXPERTS - 1)
    ids = jnp.arange(N_EXPERTS, dtype=jnp.int32)
    nonempty = group_pad > 0
    ordinal = jnp.cumsum(nonempty.astype(jnp.int32)) - 1
    later = jnp.logical_and(nonempty[None, :], ids[None, :] > ids[:, None])
    next_expert = jnp.min(jnp.where(later, ids[None, :], N_EXPERTS), axis=1)
    next_expert = jnp.where(next_expert == N_EXPERTS, -1, next_expert)
    is_tile_expert = tile_expert[:, None] == ids[None, :]
    per_tile = lambda table: jnp.sum(jnp.where(is_tile_expert, table[None, :], 0), axis=1)
    sched = {"expert": tile_expert, "n_valid": (group_end[-1] // TM_MOE).reshape(1),
             "next": per_tile(next_expert), "parity": per_tile(ordinal) % 2,
             "first": tile_start == per_tile(group_start),
             "last": tile_start + TM_MOE == per_tile(group_end)}
    tables = {"base": base.reshape(-1), "offp": offp.reshape(-1), "np": npad.reshape(-1), "total": total,
              "tail": jnp.concatenate([tail_start, tail_len])}
    as_i32 = lambda d: {k: v.astype(jnp.int32) for k, v in d.items()}
    return as_i32(tables), as_i32(sched)


WIN_ROWS = 256


def _win_layout_kernel(wt_ref, o_ref):
    wt = wt_ref[0]
    cols = wt.shape[1]
    kr_tile = jnp.concatenate([jnp.zeros((QK_NOPE, cols), F32), wt[REF_KR:REF_GATE, :],
                               jnp.zeros((HEAD_PAD - QK_DIM, cols), F32)], axis=0)
    conv_rows = [wt[part * D_CONV + c * CONV_CHUNK:part * D_CONV + (c + 1) * CONV_CHUNK, :]
                 for c in range(D_CONV // CONV_CHUNK) for part in range(3)]
    grouped = jnp.concatenate(conv_rows + [wt[REF_GATE:, :], wt[REF_CQ:REF_KR, :], kr_tile], axis=0)
    o_ref[0] = grouped.T.astype(BF16)


def _win_layout(w_in):
    wt = jnp.swapaxes(w_in, 1, 2)
    d_in = wt.shape[1]
    return pl.pallas_call(
        _win_layout_kernel,
        grid=(DEPTH, D_MODEL // WIN_ROWS),
        in_specs=[pl.BlockSpec((1, d_in, WIN_ROWS), lambda l, r: (l, 0, r))],
        out_specs=pl.BlockSpec((1, WIN_ROWS, D_IN_PAD), lambda l, r: (l, r, 0)),
        out_shape=jax.ShapeDtypeStruct((DEPTH, D_MODEL, D_IN_PAD), BF16),
        compiler_params=pltpu.CompilerParams(dimension_semantics=("arbitrary",) * 2, vmem_limit_bytes=VMEM_LIMIT),
        name="win_layout",
    )(wt)


def _layer_weights(w_uq, w_ukv):
    wq = w_uq.T.astype(BF16)

    wkv = w_ukv.reshape(KV_LORA, N_HEADS, QK_NOPE + V_HEAD)
    wuk = jnp.concatenate([wkv[:, :, :QK_NOPE], jnp.zeros((KV_LORA, N_HEADS, HEAD_PAD - QK_NOPE), F32)], axis=2)
    wuk = wuk.reshape(KV_LORA, N_HEADS * HEAD_PAD).astype(BF16)
    wuv = wkv[:, :, QK_NOPE:].reshape(KV_LORA, N_HEADS * V_HEAD).T.astype(BF16)
    return wq, wuk, wuv


def kernel(x, positions, attn_norm, w_in, conv_w, w_conv_out, q_norm, w_uq, kv_norm, w_ukv, w_mla_out, w_o, ffn_norm,
           w_gate, w_up, w_down, router, w_gate_e, w_up_e, w_down_e, final_norm):
    assert x.shape == (BATCH, SEQ, D_MODEL) and positions.shape == (BATCH, SEQ)
    cos_t, sin_t = _rope_tables(positions)
    xt = x.reshape(TOKENS, D_MODEL)
    out = None
    win_all = _win_layout(w_in)
    for l in range(DEPTH):
        wq, wuk, wuv = _layer_weights(w_uq[l], w_ukv[l])
        q, k, v, gc, sg = _front(l, xt, attn_norm[l].reshape(1, D_MODEL), win_all, conv_w[l],
                                 w_conv_out[l].astype(BF16),
                                 q_norm[l].reshape(1, Q_LORA), wq, kv_norm[l].reshape(1, KV_LORA), wuk, wuv,
                                 cos_t, sin_t)
        attn = _attention(q, k, v)
        wmo = w_mla_out[l].astype(BF16)
        wo = w_o[l].astype(BF16)
        fn = ffn_norm[l].reshape(1, D_MODEL)
        if l % 2 == 0:
            j = l // 2
            xt = _merge(attn, gc, sg, xt, wmo, wo, fn,
                        ffn=(w_gate[j].astype(BF16), w_up[j].astype(BF16), w_down[j].astype(BF16)))
        else:
            j = l // 2
            router_pad = jnp.pad(router[j].T, ((0, E_PAD - N_EXPERTS), (0, 0)))
            x2, hn, route_t = _merge(attn, gc, sg, xt, wmo, wo, fn, router_pad)
            tables, sched = _routing_tables(route_t)
            xs = _dispatch(tables, hn, route_t)
            y_sorted = _moe(sched, xs, w_gate_e[j], w_up_e[j], w_down_e[j])
            out = _combine(tables, x2, route_t, y_sorted, final_norm.reshape(1, D_MODEL))
    return out.reshape(BATCH, SEQ, D_MODEL)
```

```python
import math

import jax
import jax.numpy as jnp
from jax import lax
from jax.experimental import pallas as pl
from jax.experimental.pallas import tpu as pltpu

F32 = jnp.float32
BF16 = jnp.bfloat16

D_MODEL = 1024
BATCH = 8
SEQ = 2048
TOKENS = BATCH * SEQ
DEPTH = 2
D_CONV = 512
CONV_WIDTH = 3
N_HEADS = 8
QK_NOPE = 64
QK_ROPE = 32
HALF_ROPE = QK_ROPE // 2
V_HEAD = 64
Q_LORA = 384
KV_LORA = 256
ROPE_THETA = 10000.0
D_FF = 2816
N_EXPERTS = 8
TOP_K = 2
D_FF_EXPERT = 1408
EPS = 1e-6

LANES = 128
HEAD_PAD = LANES
QK_DIM = QK_NOPE + QK_ROPE
Q_SCALE = (1.0 / math.sqrt(QK_DIM)) * math.log2(math.e)
NEG_BIG = -1e30

OFF_BCU = 0
OFF_GATE = 3 * D_CONV
OFF_SMALL = OFF_GATE + 2 * D_MODEL
D_IN_PAD = OFF_SMALL + Q_LORA + KV_LORA + LANES
REF_CQ = 3 * D_CONV
REF_KR = REF_CQ + Q_LORA + KV_LORA
REF_GATE = REF_KR + QK_ROPE

CONV_CHUNK = 256
TM_FRONT = 512
TM_MERGE = 512
TQ = 256
TK = 128
ATTN_PAIRS = 2
ATTN_LOOKAHEAD = 3
V_ROWS = V_HEAD + 16
FF_CHUNK = 256
VMEM_LIMIT = 56 * 1024 * 1024

SUBLANES = 8
TM_MOE = 512
TT = 256
N_TT = TOKENS // TT
E_PAD = 16
XS_WIDTH = D_MODEL + LANES
N_PAIRS = TOKENS * TOP_K
SLOTS =TT * TOP_K + N_EXPERTS * SUBLANES
_MAX_ROWS = N_PAIRS + N_TT * N_EXPERTS * (SUBLANES - 1) + N_EXPERTS * (TM_MOE - SUBLANES)
N_MOE_TILES = -(-_MAX_ROWS // TM_MOE)
P_PAD = N_MOE_TILES * TM_MOE
SLAB_PIECES = tuple(TT >> s for s in range(6))
WAIT_PIECES = (512,) + SLAB_PIECES
TAIL_PIECES = tuple(TM_MOE >> s for s in range(1, (TM_MOE // SUBLANES).bit_length()))
assert SLAB_PIECES[-1] == SUBLANES and TAIL_PIECES[-1] == SUBLANES and SLOTS < 2 * WAIT_PIECES[0]


def _rms(x, g):
    return x * lax.rsqrt(jnp.mean(x * x, axis=-1, keepdims=True) + EPS) * g


def _const_spec(shape):
    nd = len(shape)
    return pl.BlockSpec(shape, lambda *_: (0,) * nd, pipeline_mode=pl.Buffered(1))


def _trig_kernel(pos_ref, invf_ref, cos_ref, sin_ref):
    ang = invf_ref[...] * pos_ref[...].astype(F32)
    cos_ref[...] = jnp.cos(ang)
    sin_ref[...] = jnp.sin(ang)


def _rope_tables(positions):
    inv_freq = ROPE_THETA ** (-jnp.arange(0, QK_ROPE, 2, dtype=F32) / QK_ROPE)
    return pl.pallas_call(
        _trig_kernel,
        out_shape=(jax.ShapeDtypeStruct((HALF_ROPE, TOKENS), F32),) * 2,
        name="rope_trig",
    )(positions.reshape(1, TOKENS), inv_freq.reshape(HALF_ROPE, 1))


def _dot_nt(a, b):
    return lax.dot_general(a, b, (((1,), (1,)), ((), ())), preferred_element_type=F32)


def _front_kernel(x_ref, an_ref, win_ref, cw_ref, wco_ref, qn_ref, wq_ref, kvn_ref, wuk_ref, wuv_ref,
                  cost_ref, sint_ref, q_out, k_out, v_out, gc_out, sg_out, cu_scr):
    tm = x_ref.shape[0]
    i = pl.program_id(0)
    xn = _rms(x_ref[...], an_ref[...]).astype(BF16)

    def proj(a, b):
        return jnp.dot(xn, win_ref[0, :, a:b], preferred_element_type=F32)

    small = proj(OFF_SMALL, D_IN_PAD)

    cq = cost_ref[...] * Q_SCALE
    sq = sint_ref[...] * Q_SCALE
    cqn = _rms(small[:, :Q_LORA], qn_ref[...]).astype(BF16)
    qt = _dot_nt(wq_ref[...], cqn)
    zero_pad = jnp.zeros((HEAD_PAD - QK_DIM, tm), BF16)
    for h in range(N_HEADS):
        src = h * QK_DIM
        dst = h * HEAD_PAD
        x1 = qt[src + QK_NOPE:src + QK_NOPE + HALF_ROPE, :]
        x2 = qt[src + QK_NOPE + HALF_ROPE:src + QK_DIM, :]
        q_out[dst:dst + QK_NOPE, :] = (qt[src:src + QK_NOPE, :] * Q_SCALE).astype(BF16)
        q_out[dst + QK_NOPE:dst + QK_NOPE + HALF_ROPE, :] = (x1 * cq - x2 * sq).astype(BF16)
        q_out[dst + QK_NOPE + HALF_ROPE:dst + QK_DIM, :] = (x2 * cq + x1 * sq).astype(BF16)
        q_out[dst + QK_DIM:dst + HEAD_PAD, :] = zero_pad

    cos_t = cost_ref[...]
    sin_t = sint_ref[...]
    z_nope = jnp.zeros((QK_NOPE, tm), F32)
    z_rope = jnp.zeros((HALF_ROPE, tm), F32)
    z_pad = jnp.zeros((HEAD_PAD - QK_DIM, tm), F32)
    ck = jnp.concatenate([z_nope, cos_t, cos_t, z_pad], axis=0).T
    sk_x1 = jnp.concatenate([z_nope, -sin_t, z_rope, z_pad], axis=0).T
    sk_x2 = jnp.concatenate([z_nope, z_rope, sin_t, z_pad], axis=0).T
    kr = small[:, Q_LORA + KV_LORA:]
    kpe = (kr * ck + pltpu.roll(kr, HEAD_PAD - HALF_ROPE, axis=1) * sk_x1 + pltpu.roll(kr, HALF_ROPE, axis=1) * sk_x2)
    ckvn = _rms(small[:, Q_LORA:Q_LORA + KV_LORA], kvn_ref[...]).astype(BF16)
    kn = jnp.dot(ckvn, wuk_ref[...], preferred_element_type=F32)
    for h in range(N_HEADS):
        blk = slice(h * HEAD_PAD, (h + 1) * HEAD_PAD)
        k_out[:, blk] = (kn[:, blk] + kpe).astype(BF16)
    vt = _dot_nt(wuv_ref[...], ckvn).astype(BF16)
    sub = lax.broadcasted_iota(jnp.int32, (V_ROWS - V_HEAD, TK), 0)
    ones_rows = jnp.where(sub == 0, 1.0, 0.0).astype(BF16)
    for c in range(tm // TK):
        for h in range(N_HEADS):
            v_out[c, h * V_ROWS:h * V_ROWS + V_HEAD, :] = vt[h * V_HEAD:(h + 1) * V_HEAD, c * TK:(c + 1) * TK]
            v_out[c, h * V_ROWS + V_HEAD:(h + 1) * V_ROWS, :] = ones_rows

    @pl.when(i % (SEQ // tm) == 0)
    def _():
        cu_scr[0:8, :] = jnp.zeros((8, D_CONV), F32)

    cw = cw_ref[...]
    y_conv = None
    for c in range(D_CONV // CONV_CHUNK):
        ch = slice(c * CONV_CHUNK, (c + 1) * CONV_CHUNK)
        bcu = proj(OFF_BCU + 3 * c * CONV_CHUNK, OFF_BCU + 3 * (c + 1) * CONV_CHUNK)
        b_g = bcu[:, :CONV_CHUNK]
        cu = bcu[:, CONV_CHUNK:2 * CONV_CHUNK] * bcu[:, 2 * CONV_CHUNK:]
        cu_scr[8:8 + tm, ch] = cu
        prev1 = cu_scr[7:7 + tm, ch]
        prev2 = cu_scr[6:6 + tm, ch]
        conv = prev2 * cw[0:1, ch] + prev1 * cw[1:2, ch] + cu * cw[2:3, ch]
        cu_scr[0:8, ch] = cu[tm - 8:, :]
        part = jnp.dot((b_g * conv).astype(BF16), wco_ref[ch, :], preferred_element_type=F32)
        y_conv = part if y_conv is None else y_conv + part

    gates = proj(OFF_GATE, OFF_SMALL)
    gc_out[...] = (jax.nn.sigmoid(gates[:, :D_MODEL]) * y_conv).astype(BF16)
    sg_out[...] = jax.nn.sigmoid(gates[:, D_MODEL:]).astype(BF16)


def _front(layer, x, an, win_all, cw, wco, qn, wq, kvn, wuk, wuv, cos_t, sin_t):
    tm = TM_FRONT
    row = lambda n: pl.BlockSpec((tm, n), lambda i: (i, 0))
    col = lambda n: pl.BlockSpec((n, tm), lambda i: (0, i))
    win_spec = pl.BlockSpec((1, D_MODEL, D_IN_PAD), lambda i: (layer, 0, 0), pipeline_mode=pl.Buffered(1))
    return pl.pallas_call(
        _front_kernel,
        grid=(TOKENS // tm,),
        in_specs=[row(D_MODEL), _const_spec((1, D_MODEL)), win_spec,
                  _const_spec((CONV_WIDTH, D_CONV)), _const_spec((D_CONV, D_MODEL)),
                  _const_spec((1, Q_LORA)), _const_spec((N_HEADS * QK_DIM, Q_LORA)),
                  _const_spec((1, KV_LORA)), _const_spec((KV_LORA, N_HEADS * HEAD_PAD)),
                  _const_spec((N_HEADS * V_HEAD, KV_LORA)),
                  col(HALF_ROPE), col(HALF_ROPE)],
        out_specs=[col(N_HEADS * HEAD_PAD), row(N_HEADS * HEAD_PAD),
                   pl.BlockSpec((tm // TK, N_HEADS * V_ROWS, TK), lambda i: (i, 0, 0)),
                   row(D_MODEL), row(D_MODEL)],
        out_shape=[jax.ShapeDtypeStruct((N_HEADS * HEAD_PAD, TOKENS), BF16),
                   jax.ShapeDtypeStruct((TOKENS, N_HEADS * HEAD_PAD), BF16),
                   jax.ShapeDtypeStruct((TOKENS // TK, N_HEADS * V_ROWS, TK), BF16),
                   jax.ShapeDtypeStruct((TOKENS, D_MODEL), BF16),
                   jax.ShapeDtypeStruct((TOKENS, D_MODEL), BF16)],
        scratch_shapes=[pltpu.VMEM((tm + 8, D_CONV), F32)],
        compiler_params=pltpu.CompilerParams(dimension_semantics=("arbitrary",), vmem_limit_bytes=VMEM_LIMIT),
        name="front",
    )(x, an, win_all, cw, wco, qn, wq, kvn, wuk, wuv, cos_t, sin_t)


def _attn_items():
    items = []
    for pair in range(ATTN_PAIRS):
        for i in range(SEQ // TQ):
            for ks in range(0, (i + 1) * TQ, TK):
                items.append((pair, i, ks, TK, ks + TK == (i + 1) * TQ))
    return items


def _attn_kernel(qt_ref, k_ref, vt_ref, o_ref):
    row = lax.broadcasted_iota(jnp.int32, (TK, TQ), 0)
    col = lax.broadcasted_iota(jnp.int32, (TK, TQ), 1)

    def scores(item):
        pair, i, ks, n, _ = item
        sts = []
        for h in range(2 * pair, 2 * pair + 2):
            hs = slice(h * HEAD_PAD, (h + 1) * HEAD_PAD)
            st = jnp.dot(k_ref[ks:ks + n, hs], qt_ref[hs, i * TQ:(i + 1) * TQ], preferred_element_type=F32)
            if ks + n > i * TQ:
                st = jnp.where(row + (ks - i * TQ) <= col, st, NEG_BIG)
            sts.append(st)
        return sts

    def consume(item, sts, carry):
        pair, i, ks, n, _ = item
        stats = []
        for j in range(2):
            m, _ = carry[j]
            m_new = jnp.maximum(m, jnp.max(sts[j], axis=0, keepdims=True))
            stats.append((m_new, jnp.exp2(m - m_new), jnp.exp2(sts[j] - m_new).astype(BF16)))
        new = []
        for j in range(2):
            h = 2 * pair + j
            m_new, alpha, p = stats[j]
            vt = vt_ref[ks // TK, h * V_ROWS:(h + 1) * V_ROWS, ks % TK:ks % TK + n]
            new.append((m_new, alpha * carry[j][1] + jnp.dot(vt, p, preferred_element_type=F32)))
        return new

    items = _attn_items()
    init = [(jnp.full((1, TQ), NEG_BIG, F32), jnp.zeros((V_ROWS, TQ), F32)) for _ in range(2)]
    pending = [scores(it) for it in items[:ATTN_LOOKAHEAD]]
    carry = [init] * ATTN_PAIRS
    for t, item in enumerate(items):
        if t + ATTN_LOOKAHEAD < len(items):
            pending.append(scores(items[t + ATTN_LOOKAHEAD]))
        pair, i = item[0], item[1]
        carry[pair] = consume(item, pending.pop(0), carry[pair])
        if item[4]:
            out_t = jnp.concatenate([acc[0:V_HEAD] / acc[V_HEAD:V_HEAD + 1] for (_, acc) in carry[pair]], axis=0)
            o_ref[i * TQ:(i + 1) * TQ, pair * 2 * V_HEAD:(pair + 1) * 2 * V_HEAD] = out_t.T.astype(BF16)
            carry[pair] = init


def _attention(qt, k, vt):
    heads = 2 * ATTN_PAIRS
    return pl.pallas_call(
        _attn_kernel,
        grid=(BATCH, N_HEADS // heads),
        in_specs=[pl.BlockSpec((heads * HEAD_PAD, SEQ), lambda b, hp: (hp, b)),
                  pl.BlockSpec((SEQ, heads * HEAD_PAD), lambda b, hp: (b, hp)),
                  pl.BlockSpec((SEQ // TK, heads * V_ROWS, TK), lambda b, hp: (b, hp, 0))],
        out_specs=pl.BlockSpec((SEQ, heads * V_HEAD), lambda b, hp: (b, hp)),
        out_shape=jax.ShapeDtypeStruct((TOKENS, N_HEADS * V_HEAD), BF16),
        compiler_params=pltpu.CompilerParams(dimension_semantics=("arbitrary",) * 2, vmem_limit_bytes=VMEM_LIMIT),
        name="attention",
    )(qt, k, vt)


def _merge_core(attn_ref, gc_ref, sg_ref, x_ref, wmo_ref, wo_ref, fn_ref):
    y_mla = jnp.dot(attn_ref[...], wmo_ref[...], preferred_element_type=F32)
    merged = gc_ref[...].astype(F32) + sg_ref[...].astype(F32) * y_mla
    x2 = x_ref[...] + jnp.dot(merged.astype(BF16), wo_ref[...], preferred_element_type=F32)
    return x2, _rms(x2, fn_ref[...])


def _merge_ffn_kernel(attn_ref, gc_ref, sg_ref, x_ref, wmo_ref, wo_ref, fn_ref, wg_ref, wu_ref, wd_ref, o_ref):
    x2, hn = _merge_core(attn_ref, gc_ref, sg_ref, x_ref, wmo_ref, wo_ref, fn_ref)
    hn = hn.astype(BF16)
    acc = x2
    for c in range(D_FF // FF_CHUNK):
        cs = slice(c * FF_CHUNK, (c + 1) * FF_CHUNK)
        g = jnp.dot(hn, wg_ref[:, cs], preferred_element_type=F32)
        u = jnp.dot(hn, wu_ref[:, cs], preferred_element_type=F32)
        a = (jax.nn.silu(g) * u).astype(BF16)
        acc = acc + jnp.dot(a, wd_ref[cs, :], preferred_element_type=F32)
    o_ref[...] = acc


def _merge_route_kernel(attn_ref, gc_ref, sg_ref, x_ref, wmo_ref, wo_ref, fn_ref, router_ref,
                        x2_out, hn_out, route_t_out):
    x2, hn = _merge_core(attn_ref, gc_ref, sg_ref, x_ref, wmo_ref, wo_ref, fn_ref)
    x2_out[...] = x2
    hb = hn.astype(BF16)
    hn_out[...] = hb
    hl = (hn - hb.astype(F32)).astype(BF16)
    r = router_ref[...]
    rh = r.astype(BF16)
    rl = (r - rh.astype(F32)).astype(BF16)
    logits = _dot_nt(rh, hb) + _dot_nt(rh, hl) + _dot_nt(rl, hb)
    erow = lax.broadcasted_iota(jnp.int32, logits.shape, 0)
    logits = jnp.where(erow < N_EXPERTS, logits, -jnp.inf)
    m1 = jnp.max(logits, axis=0, keepdims=True)
    i1 = jnp.min(jnp.where(logits == m1, erow, E_PAD), axis=0, keepdims=True)
    rest = jnp.where(erow == i1, -jnp.inf, logits)
    m2 = jnp.max(rest, axis=0, keepdims=True)
    i2 = jnp.min(jnp.where(rest == m2, erow, E_PAD), axis=0, keepdims=True)
    t = jnp.exp(m2 - m1)
    w1 = 1.0 / (1.0 + t)
    w2 = t * w1
    route_t = jnp.where(erow == 0, i1.astype(F32),
                        jnp.where(erow == 1, i2.astype(F32), jnp.where(erow == 2, w1, jnp.where(erow == 3, w2, 0.0))))
    route_t_out[...] = route_t[0:SUBLANES, :]


def _merge(attn, gc, sg, x, wmo, wo, fn, router=None, ffn=None):
    tm = TM_MERGE
    row = lambda n: pl.BlockSpec((tm, n), lambda i: (i, 0))
    in_specs = [row(N_HEADS * V_HEAD), row(D_MODEL), row(D_MODEL), row(D_MODEL),
                _const_spec((N_HEADS * V_HEAD, D_MODEL)), _const_spec((D_MODEL, D_MODEL)), _const_spec((1, D_MODEL))]
    args = [attn, gc, sg, x, wmo, wo, fn]
    if router is None:
        kern = _merge_ffn_kernel
        in_specs += [_const_spec((D_MODEL, D_FF)), _const_spec((D_MODEL, D_FF)), _const_spec((D_FF, D_MODEL))]
        args += list(ffn)
        out_specs = row(D_MODEL)
        out_shape = jax.ShapeDtypeStruct((TOKENS, D_MODEL), F32)
        name = "merge_ffn"
    else:
        kern = _merge_route_kernel
        in_specs.append(_const_spec((E_PAD, D_MODEL)))
        args.append(router)
        out_specs = [row(D_MODEL), row(D_MODEL), pl.BlockSpec((SUBLANES, tm), lambda i: (0, i))]
        out_shape = [jax.ShapeDtypeStruct((TOKENS, D_MODEL), F32), jax.ShapeDtypeStruct((TOKENS, D_MODEL), BF16),
                     jax.ShapeDtypeStruct((SUBLANES, TOKENS), F32)]
        name = "merge_route"
    return pl.pallas_call(
        kern, grid=(TOKENS // tm,), in_specs=in_specs, out_specs=out_specs, out_shape=out_shape,
        compiler_params=pltpu.CompilerParams(dimension_semantics=("arbitrary",), vmem_limit_bytes=VMEM_LIMIT),
        name=name,
    )(*args)


def _pieces(n, sizes):
    return [((n & z) != 0, n & ~(2 * z - 1), z) for z in sizes]


def _aligned(rows, z):
    return pl.ds(rows if isinstance(rows, int) else pl.multiple_of(rows, SUBLANES), z)


def _start_slab_copies(tile, base_ref, offp_ref, np_ref, make_copy):
    for e in range(N_EXPERTS):
        n = np_ref[tile * N_EXPERTS + e]
        src = offp_ref[tile * N_EXPERTS + e]
        dst = base_ref[tile * N_EXPERTS + e]
        for pred, off, z in _pieces(n, SLAB_PIECES):
            @pl.when(pred)
            def _(off=off, z=z):
                make_copy(src + off, dst + off, z).start()


def _wait_rows(total, make_copy):
    for pred, _, z in _pieces(total, WAIT_PIECES):
        @pl.when(pred)
        def _(z=z):
            make_copy(0, 0, z).wait()


def _tile_placement(rt_ref, offp_ref, i):
    e0 = rt_ref[0:1, :].astype(jnp.int32)
    e1 = rt_ref[1:2, :].astype(jnp.int32)
    eid = lax.broadcasted_iota(jnp.int32, (E_PAD, TT), 0)
    sel = jnp.where(e0 == eid, 1.0, jnp.where(e1 == eid, 1.0, 0.0)).astype(BF16)
    r = lax.broadcasted_iota(jnp.int32, (TT, TT), 0)
    c = lax.broadcasted_iota(jnp.int32, (TT, TT), 1)
    earlier = jnp.where(r < c, 1.0, 0.0).astype(BF16)
    rank = jnp.dot(sel, earlier, preferred_element_type=F32)
    ecol = lax.broadcasted_iota(jnp.int32, (E_PAD, 1), 0)
    offp = jnp.zeros((E_PAD, 1), F32)
    for e in range(N_EXPERTS):
        offp = jnp.where(ecol == e, offp_ref[i * N_EXPERTS + e].astype(F32), offp)
    slot_all = rank + offp
    slot0 = jnp.sum(jnp.where(e0 == eid, slot_all, 0.0), axis=0, keepdims=True)
    slot1 = jnp.sum(jnp.where(e1 == eid, slot_all, 0.0), axis=0, keepdims=True)
    sidx = lax.broadcasted_iota(jnp.int32, (SLOTS, TT), 0).astype(F32)
    place = jnp.where(sidx == slot0, 1.0, jnp.where(sidx == slot1, 1.0, 0.0)).astype(BF16)
    return sidx, slot0, slot1, place


def _dispatch_kernel(base_ref, offp_ref, np_ref, tot_ref, tail_ref, hn_ref, rt_ref, xs_hbm, xs_scr, zbuf, sem, zsem):
    i = pl.program_id(0)
    slot = i % 2

    def copy(buf_slot):
        return lambda s, d, z: pltpu.make_async_copy(xs_scr.at[buf_slot, _aligned(s, z), :],
                                                     xs_hbm.at[_aligned(d, z), :], sem.at[buf_slot])

    def zero_copy(d, z):
        return pltpu.make_async_copy(zbuf.at[pl.ds(0, z), :], xs_hbm.at[_aligned(d, z), :], zsem)

    @pl.when(i == 0)
    def _():
        zbuf[...] = jnp.zeros_like(zbuf)
        for phase in ("start", "wait"):
            for e in range(N_EXPERTS):
                t0 = tail_ref[e]
                n = tail_ref[N_EXPERTS + e]
                full = lax.shift_right_logical(n, jnp.int32(TM_MOE.bit_length() - 1))

                def chunk(c, carry, t0=t0, phase=phase):
                    cp = zero_copy(t0 + c * TM_MOE, TM_MOE)
                    cp.start() if phase == "start" else cp.wait()
                    return carry

                lax.fori_loop(0, full, chunk, 0)
                rest = n & (TM_MOE - 1)
                for pred, off, z in _pieces(rest, TAIL_PIECES):
                    @pl.when(pred)
                    def _(off=off, z=z, t0=t0, full=full, phase=phase):
                        cp = zero_copy(t0 + full * TM_MOE + off, z)
                        cp.start() if phase == "start" else cp.wait()

    sidx, slot0, slot1, place = _tile_placement(rt_ref, offp_ref, i)
    xs_scr[slot, :, 0:D_MODEL] = jnp.dot(place, hn_ref[...], preferred_element_type=F32)
    w_slot = jnp.sum(jnp.where(sidx == slot0, rt_ref[2:3, :], jnp.where(sidx == slot1, rt_ref[3:4, :], 0.0)),
                     axis=1, keepdims=True)
    xs_scr[slot, :, D_MODEL:] = jnp.broadcast_to(w_slot, (SLOTS, LANES))

    _start_slab_copies(i, base_ref, offp_ref, np_ref, copy(slot))

    @pl.when(i > 0)
    def _():
        _wait_rows(tot_ref[i - 1], copy(1 - slot))

    @pl.when(i == pl.num_programs(0) - 1)
    def _():
        _wait_rows(tot_ref[i], copy(slot))


def _dispatch(tables, hn, route_t):
    grid_spec = pltpu.PrefetchScalarGridSpec(
        num_scalar_prefetch=5,
        grid=(N_TT,),
        in_specs=[pl.BlockSpec((TT, D_MODEL), lambda i, *_: (i, 0)),
                  pl.BlockSpec((SUBLANES, TT), lambda i, *_: (0, i))],
        out_specs=pl.BlockSpec(memory_space=pl.ANY),
        scratch_shapes=[pltpu.VMEM((2, SLOTS, XS_WIDTH), F32), pltpu.VMEM((TM_MOE, XS_WIDTH), F32),
                        pltpu.SemaphoreType.DMA((2,)), pltpu.SemaphoreType.DMA],
    )
    return pl.pallas_call(
        _dispatch_kernel,
        grid_spec=grid_spec,
        out_shape=jax.ShapeDtypeStruct((P_PAD, XS_WIDTH), F32),
        compiler_params=pltpu.CompilerParams(dimension_semantics=("arbitrary",), vmem_limit_bytes=VMEM_LIMIT),
        name="moe_dispatch",
    )(tables["base"], tables["offp"], tables["np"], tables["total"], tables["tail"], hn, route_t)


def _moe_kernel(te_ref, nv_ref, nxt_ref, par_ref, first_ref, last_ref, x_ref, wg_hbm, wu_hbm, wd_hbm, y_ref,
                stage_g, stage_u, stage_d, wgu_scr, wd_scr, sem):
    i = pl.program_id(0)
    valid = i < nv_ref[0]
    nxt = nxt_ref[i]

    def fetch(e):
        return [pltpu.make_async_copy(wg_hbm.at[e], stage_g, sem.at[0]),
                pltpu.make_async_copy(wu_hbm.at[e], stage_u, sem.at[1]),
                pltpu.make_async_copy(wd_hbm.at[e], stage_d, sem.at[2])]

    def round_into(s):
        wgu_scr[s, :, 0:D_FF_EXPERT] = stage_g[...].astype(BF16)
        wgu_scr[s, :, D_FF_EXPERT:] = stage_u[...].astype(BF16)
        wd_scr[s] = stage_d[...].astype(BF16)

    @pl.when(i == 0)
    def _():
        for cp in fetch(te_ref[0]):
            cp.start()
        for cp in fetch(te_ref[0]):
            cp.wait()
        round_into(par_ref[0])

    @pl.when(jnp.logical_and(valid, jnp.logical_and(first_ref[i] == 1, nxt >= 0)))
    def _():
        for cp in fetch(nxt):
            cp.start()

    @pl.when(valid)
    def _():
        s = par_ref[i]
        x = x_ref[:, 0:D_MODEL].astype(BF16)
        h = jnp.dot(x, wgu_scr[s], preferred_element_type=F32)
        a = (jax.nn.silu(h[:, :D_FF_EXPERT]) * h[:, D_FF_EXPERT:]).astype(BF16)
        y_ref[...] = jnp.dot(a, wd_scr[s], preferred_element_type=F32) * x_ref[:, D_MODEL:D_MODEL + 1]

    @pl.when(jnp.logical_and(valid, jnp.logical_and(last_ref[i] == 1, nxt >= 0)))
    def _():
        for cp in fetch(nxt):
            cp.wait()
        round_into(1 - par_ref[i])

    @pl.when(jnp.logical_not(valid))
    def _():
        y_ref[...] = jnp.zeros_like(y_ref)


def _moe(sched, xs, wg, wu, wd):
    grid_spec = pltpu.PrefetchScalarGridSpec(
        num_scalar_prefetch=6,
        grid=(N_MOE_TILES,),
        in_specs=[pl.BlockSpec((TM_MOE, XS_WIDTH), lambda i, *_: (i, 0)),
                  pl.BlockSpec(memory_space=pl.ANY), pl.BlockSpec(memory_space=pl.ANY),
                  pl.BlockSpec(memory_space=pl.ANY)],
        out_specs=pl.BlockSpec((TM_MOE, D_MODEL), lambda i, *_: (i, 0)),
        scratch_shapes=[pltpu.VMEM((D_MODEL, D_FF_EXPERT), F32), pltpu.VMEM((D_MODEL, D_FF_EXPERT), F32),
                        pltpu.VMEM((D_FF_EXPERT, D_MODEL), F32),
                        pltpu.VMEM((2, D_MODEL, 2 * D_FF_EXPERT), BF16), pltpu.VMEM((2, D_FF_EXPERT, D_MODEL), BF16),
                        pltpu.SemaphoreType.DMA((3,))],
    )
    return pl.pallas_call(
        _moe_kernel,
        grid_spec=grid_spec,
        out_shape=jax.ShapeDtypeStruct((P_PAD, D_MODEL), F32),
        compiler_params=pltpu.CompilerParams(dimension_semantics=("arbitrary",), vmem_limit_bytes=VMEM_LIMIT),
        name="moe_experts",
    )(sched["expert"], sched["n_valid"], sched["next"], sched["parity"], sched["first"], sched["last"],
      xs, wg, wu, wd)


def _combine_kernel(base_ref, offp_ref, np_ref, tot_ref, x_ref, route_ref, y_hbm, fn_ref, o_ref, ybuf, sem):
    i = pl.program_id(0)
    slot = i % 2

    def copy(buf_slot):
        return lambda s, d, z: pltpu.make_async_copy(y_hbm.at[_aligned(d, z), :],
                                                     ybuf.at[buf_slot, _aligned(s, z), :], sem.at[buf_slot])

    @pl.when(i == 0)
    def _():
        ybuf[...] = jnp.zeros_like(ybuf)
        _start_slab_copies(0, base_ref, offp_ref, np_ref, copy(0))

    @pl.when(i + 1 < pl.num_programs(0))
    def _():
        _start_slab_copies(i + 1, base_ref, offp_ref, np_ref, copy(1 - slot))

    _, _, _, place = _tile_placement(route_ref, offp_ref, i)

    _wait_rows(tot_ref[i], copy(slot))
    y = ybuf[slot]
    yh = y.astype(BF16)
    yl = (y - yh.astype(F32)).astype(BF16)
    tn = (((0,), (0,)), ((), ()))
    moe = (lax.dot_general(place, yh, tn, preferred_element_type=F32)
           + lax.dot_general(place, yl, tn, preferred_element_type=F32))
    o_ref[...] = _rms(x_ref[...] + moe, fn_ref[...])


def _combine(tables, x2, route_t, y_sorted, fn):
    grid_spec = pltpu.PrefetchScalarGridSpec(
        num_scalar_prefetch=4,
        grid=(N_TT,),
        in_specs=[pl.BlockSpec((TT, D_MODEL), lambda i, *_: (i, 0)),
                  pl.BlockSpec((SUBLANES, TT), lambda i, *_: (0, i)),
                  pl.BlockSpec(memory_space=pl.ANY),
                  pl.BlockSpec((1, D_MODEL), lambda i, *_: (0, 0))],
        out_specs=pl.BlockSpec((TT, D_MODEL), lambda i, *_: (i, 0)),
        scratch_shapes=[pltpu.VMEM((2, SLOTS, D_MODEL), F32), pltpu.SemaphoreType.DMA((2,))],
    )
    return pl.pallas_call(
        _combine_kernel,
        grid_spec=grid_spec,
        out_shape=jax.ShapeDtypeStruct((TOKENS, D_MODEL), F32),
        compiler_params=pltpu.CompilerParams(dimension_semantics=("arbitrary",), vmem_limit_bytes=VMEM_LIMIT),
        name="moe_combine",
    )(tables["base"], tables["offp"], tables["np"], tables["total"], x2, route_t, y_sorted, fn)


def _routing_tables(route_t):
    experts = route_t[:TOP_K, :].astype(jnp.int32)
    eid = jnp.arange(N_EXPERTS, dtype=jnp.int32)[:, None, None]
    chosen = jnp.sum((experts[None, :, :] == eid).astype(jnp.int32), axis=1)
    cnt = jnp.sum(chosen.reshape(N_EXPERTS, N_TT, TT), axis=2).T
    npad = (cnt + SUBLANES - 1) // SUBLANES * SUBLANES
    offp = jnp.cumsum(npad, axis=1) - npad
    total = jnp.sum(npad, axis=1)
    group = jnp.sum(npad, axis=0)
    group_pad = (group + TM_MOE - 1) // TM_MOE * TM_MOE
    group_end = jnp.cumsum(group_pad)
    group_start = group_end - group_pad
    base = group_start[None, :] + jnp.cumsum(npad, axis=0) - npad
    tail_start = group_start + group
    is_last = jnp.arange(N_EXPERTS) == N_EXPERTS - 1
    tail_len = jnp.where(is_last, P_PAD - tail_start, group_pad - group)
    tile_start = jnp.arange(N_MOE_TILES, dtype=jnp.int32) * TM_MOE
    tile_expert = jnp.minimum(jnp.sum((tile_start[:, None] >= group_end[None, :]).astype(jnp.int32), axis=1),
                              N_EXPERTS - 1)
    ids = jnp.arange(N_EXPERTS, dtype=jnp.int32)
    nonempty = group_pad > 0
    ordinal = jnp.cumsum(nonempty.astype(jnp.int32)) - 1
    later = jnp.logical_and(nonempty[None, :], ids[None, :] > ids[:, None])
    next_expert = jnp.min(jnp.where(later, ids[None, :], N_EXPERTS), axis=1)
    next_expert = jnp.where(next_expert == N_EXPERTS, -1, next_expert)
    is_tile_expert = tile_expert[:, None] == ids[None, :]
    per_tile = lambda table: jnp.sum(jnp.where(is_tile_expert, table[None, :], 0), axis=1)
    sched = {"expert": tile_expert, "n_valid": (group_end[-1] // TM_MOE).reshape(1),
             "next": per_tile(next_expert), "parity": per_tile(ordinal) % 2,
             "first": tile_start == per_tile(group_start),
             "last": tile_start + TM_MOE == per_tile(group_end)}
    tables = {"base": base.reshape(-1), "offp": offp.reshape(-1), "np": npad.reshape(-1), "total": total,
              "tail": jnp.concatenate([tail_start, tail_len])}
    as_i32 = lambda d: {k: v.astype(jnp.int32) for k, v in d.items()}
    return as_i32(tables), as_i32(sched)


WIN_ROWS = 256


def _win_layout_kernel(wt_ref, o_ref):
    wt = wt_ref[0]
    cols = wt.shape[1]
    kr_tile = jnp.concatenate([jnp.zeros((QK_NOPE, cols), F32), wt[REF_KR:REF_GATE, :],
                               jnp.zeros((HEAD_PAD - QK_DIM, cols), F32)], axis=0)
    conv_rows = [wt[part * D_CONV + c * CONV_CHUNK:part * D_CONV + (c + 1) * CONV_CHUNK, :]
                 for c in range(D_CONV // CONV_CHUNK) for part in range(3)]
    grouped = jnp.concatenate(conv_rows + [wt[REF_GATE:, :], wt[REF_CQ:REF_KR, :], kr_tile], axis=0)
    o_ref[0] = grouped.T.astype(BF16)


def _win_layout(w_in):
    wt = jnp.swapaxes(w_in, 1, 2)
    d_in = wt.shape[1]
    return pl.pallas_call(
        _win_layout_kernel,
        grid=(DEPTH, D_MODEL // WIN_ROWS),
        in_specs=[pl.BlockSpec((1, d_in, WIN_ROWS), lambda l, r: (l, 0, r))],
        out_specs=pl.BlockSpec((1, WIN_ROWS, D_IN_PAD), lambda l, r: (l, r, 0)),
        out_shape=jax.ShapeDtypeStruct((DEPTH, D_MODEL, D_IN_PAD), BF16),
        compiler_params=pltpu.CompilerParams(dimension_semantics=("arbitrary",) * 2, vmem_limit_bytes=VMEM_LIMIT),
        name="win_layout",
    )(wt)


def _layer_weights(w_uq, w_ukv):
    wq = w_uq.T.astype(BF16)

    wkv = w_ukv.reshape(KV_LORA, N_HEADS, QK_NOPE + V_HEAD)
    wuk = jnp.concatenate([wkv[:, :, :QK_NOPE], jnp.zeros((KV_LORA, N_HEADS, HEAD_PAD - QK_NOPE), F32)], axis=2)
    wuk = wuk.reshape(KV_LORA, N_HEADS * HEAD_PAD).astype(BF16)
    wuv = wkv[:, :, QK_NOPE:].reshape(KV_LORA, N_HEADS * V_HEAD).T.astype(BF16)
    return wq, wuk, wuv


def kernel(x, positions, attn_norm, w_in, conv_w, w_conv_out, q_norm, w_uq, kv_norm, w_ukv, w_mla_out, w_o, ffn_norm,
           w_gate, w_up, w_down, router, w_gate_e, w_up_e, w_down_e, final_norm):
    assert x.shape == (BATCH, SEQ, D_MODEL) and positions.shape == (BATCH, SEQ)
    cos_t, sin_t = _rope_tables(positions)
    xt = x.reshape(TOKENS, D_MODEL)
    out = None
    win_all = _win_layout(w_in)
    for l in range(DEPTH):
        wq, wuk, wuv = _layer_weights(w_uq[l], w_ukv[l])
        q, k, v, gc, sg = _front(l, xt, attn_norm[l].reshape(1, D_MODEL), win_all, conv_w[l],
                                 w_conv_out[l].astype(BF16),
                                 q_norm[l].reshape(1, Q_LORA), wq, kv_norm[l].reshape(1, KV_LORA), wuk, wuv,
                                 cos_t, sin_t)
        attn = _attention(q, k, v)
        wmo = w_mla_out[l].astype(BF16)
        wo = w_o[l].astype(BF16)
        fn = ffn_norm[l].reshape(1, D_MODEL)
        if l % 2 == 0:
            j = l // 2
            xt = _merge(attn, gc, sg, xt, wmo, wo, fn,
                        ffn=(w_gate[j].astype(BF16), w_up[j].astype(BF16), w_down[j].astype(BF16)))
        else:
            j = l // 2
            router_pad = jnp.pad(router[j].T, ((0, E_PAD - N_EXPERTS), (0, 0)))
            x2, hn, route_t = _merge(attn, gc, sg, xt, wmo, wo, fn, router_pad)
            tables, sched = _routing_tables(route_t)
            xs = _dispatch(tables, hn, route_t)
            y_sorted = _moe(sched, xs, w_gate_e[j], w_up_e[j], w_down_e[j])
            out = _combine(tables, x2, route_t, y_sorted, final_norm.reshape(1, D_MODEL))
    return out.reshape(BATCH, SEQ, D_MODEL)
```

```python
import math

import jax
import jax.numpy as jnp
from jax import lax
from jax.experimental import pallas as pl
from jax.experimental.pallas import tpu as pltpu

F32 = jnp.float32
BF16 = jnp.bfloat16

D_MODEL = 1024
BATCH = 8
SEQ = 2048
TOKENS = BATCH * SEQ
DEPTH = 2
D_CONV = 512
CONV_WIDTH = 3
N_HEADS = 8
QK_NOPE = 64
QK_ROPE = 32
HALF_ROPE = QK_ROPE // 2
V_HEAD = 64
Q_LORA = 384
KV_LORA = 256
ROPE_THETA = 10000.0
D_FF = 2816
N_EXPERTS = 8
TOP_K = 2
D_FF_EXPERT = 1408
EPS = 1e-6

LANES = 128
SUBLANES = 8
BF16_SUBLANES = 16
MXU_TILE = 256
VMEM_LIMIT = 56 * 1024 * 1024

HEAD_PAD = LANES
QK_DIM = QK_NOPE + QK_ROPE
Q_SCALE = (1.0 / math.sqrt(QK_DIM)) * math.log2(math.e)
NEG_BIG = -1e30

OFF_BCU = 0
OFF_GATE = 3 * D_CONV
OFF_SMALL = OFF_GATE + 2 * D_MODEL
D_IN_PAD = OFF_SMALL + Q_LORA + KV_LORA + LANES
REF_CQ = 3 * D_CONV
REF_KR = REF_CQ + Q_LORA + KV_LORA
REF_GATE = REF_KR + QK_ROPE

CONV_CHUNK = MXU_TILE
CONV_HIST = SUBLANES
assert CONV_WIDTH - 1 <= CONV_HIST
TM_FRONT = 512
TM_MERGE = 1024
TQ = MXU_TILE
TK = 128
ATTN_PAIRS = 2
ATTN_LOOKAHEAD = 3
V_ROWS = V_HEAD + BF16_SUBLANES
FF_CHUNK = MXU_TILE

TM_MOE = 512
TT = 256
N_TT = TOKENS // TT
E_PAD = BF16_SUBLANES
XS_WIDTH = D_MODEL + LANES
N_PAIRS = TOKENS * TOP_K
SLOTS = TT * TOP_K + N_EXPERTS * SUBLANES
_MAX_ROWS = N_PAIRS + N_TT * N_EXPERTS * (SUBLANES - 1) + N_EXPERTS * (TM_MOE - SUBLANES)
N_MOE_TILES = -(-_MAX_ROWS // TM_MOE)
P_PAD = N_MOE_TILES * TM_MOE


def _halvings(n):
    return tuple(n >> s for s in range((n // SUBLANES).bit_length()))


SLAB_PIECES = _halvings(TT)
WAIT_PIECES = _halvings(2 * TT)
TAIL_PIECES = _halvings(TM_MOE // 2)
assert SLOTS < 2 * WAIT_PIECES[0]


def _rms(x, g):
    return x * lax.rsqrt(jnp.mean(x * x, axis=-1, keepdims=True) + EPS) * g


def _const_spec(shape):
    nd = len(shape)
    return pl.BlockSpec(shape, lambda *_: (0,) * nd, pipeline_mode=pl.Buffered(1))


def _trig_kernel(pos_ref, invf_ref, cos_ref, sin_ref):
    ang = invf_ref[...] * pos_ref[...].astype(F32)
    cos_ref[...] = jnp.cos(ang)
    sin_ref[...] = jnp.sin(ang)


def _rope_tables(positions):
    inv_freq = ROPE_THETA ** (-jnp.arange(0, QK_ROPE, 2, dtype=F32) / QK_ROPE)
    return pl.pallas_call(
        _trig_kernel,
        out_shape=(jax.ShapeDtypeStruct((HALF_ROPE, TOKENS), F32),) * 2,
        name="rope_trig",
    )(positions.reshape(1, TOKENS), inv_freq.reshape(HALF_ROPE, 1))


def _dot_nt(a, b):
    return lax.dot_general(a, b, (((1,), (1,)), ((), ())), preferred_element_type=F32)


def _front_kernel(x_ref, an_ref, win_ref, cw_ref, wco_ref, qn_ref, wq_ref, kvn_ref, wuk_ref, wuv_ref,
                  cost_ref, sint_ref, q_out, k_out, v_out, gc_out, sg_out, cu_scr):
    tm = x_ref.shape[0]
    i = pl.program_id(0)
    xn = _rms(x_ref[...], an_ref[...]).astype(BF16)

    def proj(a, b):
        return jnp.dot(xn, win_ref[0, :, a:b], preferred_element_type=F32)

    small = proj(OFF_SMALL, D_IN_PAD)

    cq = cost_ref[...] * Q_SCALE
    sq = sint_ref[...] * Q_SCALE
    cqn = _rms(small[:, :Q_LORA], qn_ref[...]).astype(BF16)
    qt = _dot_nt(wq_ref[...], cqn)
    zero_pad = jnp.zeros((HEAD_PAD - QK_DIM, tm), BF16)
    for h in range(N_HEADS):
        src = h * QK_DIM
        dst = h * HEAD_PAD
        x1 = qt[src + QK_NOPE:src + QK_NOPE + HALF_ROPE, :]
        x2 = qt[src + QK_NOPE + HALF_ROPE:src + QK_DIM, :]
        q_out[dst:dst + QK_NOPE, :] = (qt[src:src + QK_NOPE, :] * Q_SCALE).astype(BF16)
        q_out[dst + QK_NOPE:dst + QK_NOPE + HALF_ROPE, :] = (x1 * cq - x2 * sq).astype(BF16)
        q_out[dst + QK_NOPE + HALF_ROPE:dst + QK_DIM, :] = (x2 * cq + x1 * sq).astype(BF16)
        q_out[dst + QK_DIM:dst + HEAD_PAD, :] = zero_pad

    cos_t = cost_ref[...]
    sin_t = sint_ref[...]
    z_nope = jnp.zeros((QK_NOPE, tm), F32)
    z_rope = jnp.zeros((HALF_ROPE, tm), F32)
    z_pad = jnp.zeros((HEAD_PAD - QK_DIM, tm), F32)
    ck = jnp.concatenate([z_nope, cos_t, cos_t, z_pad], axis=0).T
    sk_x1 = jnp.concatenate([z_nope, -sin_t, z_rope, z_pad], axis=0).T
    sk_x2 = jnp.concatenate([z_nope, z_rope, sin_t, z_pad], axis=0).T
    kr = small[:, Q_LORA + KV_LORA:]
    kpe = (kr * ck + pltpu.roll(kr, HEAD_PAD - HALF_ROPE, axis=1) * sk_x1 + pltpu.roll(kr, HALF_ROPE, axis=1) * sk_x2)
    ckvn = _rms(small[:, Q_LORA:Q_LORA + KV_LORA], kvn_ref[...]).astype(BF16)
    kn = jnp.dot(ckvn, wuk_ref[...], preferred_element_type=F32)
    for h in range(N_HEADS):
        blk = slice(h * HEAD_PAD, (h + 1) * HEAD_PAD)
        k_out[:, blk] = (kn[:, blk] + kpe).astype(BF16)
    vt = _dot_nt(wuv_ref[...], ckvn).astype(BF16)
    sub = lax.broadcasted_iota(jnp.int32, (V_ROWS - V_HEAD, TK), 0)
    ones_rows = jnp.where(sub == 0, 1.0, 0.0).astype(BF16)
    for c in range(tm // TK):
        for h in range(N_HEADS):
            v_out[c, h * V_ROWS:h * V_ROWS + V_HEAD, :] = vt[h * V_HEAD:(h + 1) * V_HEAD, c * TK:(c + 1) * TK]
            v_out[c, h * V_ROWS + V_HEAD:(h + 1) * V_ROWS, :] = ones_rows

    @pl.when(i % (SEQ // tm) == 0)
    def _():
        cu_scr[0:CONV_HIST, :] = jnp.zeros((CONV_HIST, D_CONV), F32)

    cw = cw_ref[...]
    y_conv = None
    for c in range(D_CONV // CONV_CHUNK):
        ch = slice(c * CONV_CHUNK, (c + 1) * CONV_CHUNK)
        bcu = proj(OFF_BCU + 3 * c * CONV_CHUNK, OFF_BCU + 3 * (c + 1) * CONV_CHUNK)
        b_g = bcu[:, :CONV_CHUNK]
        cu = bcu[:, CONV_CHUNK:2 * CONV_CHUNK] * bcu[:, 2 * CONV_CHUNK:]
        cu_scr[CONV_HIST:CONV_HIST + tm, ch] = cu
        conv = cu * cw[CONV_WIDTH - 1:CONV_WIDTH, ch]
        for tap in range(CONV_WIDTH - 1):
            back = CONV_WIDTH - 1 - tap
            conv = conv + cu_scr[CONV_HIST - back:CONV_HIST - back + tm, ch] * cw[tap:tap + 1, ch]
        cu_scr[0:CONV_HIST, ch] = cu[tm - CONV_HIST:, :]
        part = jnp.dot((b_g * conv).astype(BF16), wco_ref[ch, :], preferred_element_type=F32)
        y_conv = part if y_conv is None else y_conv + part

    gates = proj(OFF_GATE, OFF_SMALL)
    gc_out[...] = (jax.nn.sigmoid(gates[:, :D_MODEL]) * y_conv).astype(BF16)
    sg_out[...] = jax.nn.sigmoid(gates[:, D_MODEL:]).astype(BF16)


def _front(layer, x, an, win_all, cw, wco, qn, wq, kvn, wuk, wuv, cos_t, sin_t):
    tm = TM_FRONT
    row = lambda n: pl.BlockSpec((tm, n), lambda i: (i, 0))
    col = lambda n: pl.BlockSpec((n, tm), lambda i: (0, i))
    win_spec = pl.BlockSpec((1, D_MODEL, D_IN_PAD), lambda i: (layer, 0, 0), pipeline_mode=pl.Buffered(1))
    return pl.pallas_call(
        _front_kernel,
        grid=(TOKENS // tm,),
        in_specs=[row(D_MODEL), _const_spec((1, D_MODEL)), win_spec,
                  _const_spec((CONV_WIDTH, D_CONV)), _const_spec((D_CONV, D_MODEL)),
                  _const_spec((1, Q_LORA)), _const_spec((N_HEADS * QK_DIM, Q_LORA)),
                  _const_spec((1, KV_LORA)), _const_spec((KV_LORA, N_HEADS * HEAD_PAD)),
                  _const_spec((N_HEADS * V_HEAD, KV_LORA)),
                  col(HALF_ROPE), col(HALF_ROPE)],
        out_specs=[col(N_HEADS * HEAD_PAD), row(N_HEADS * HEAD_PAD),
                   pl.BlockSpec((tm // TK, N_HEADS * V_ROWS, TK), lambda i: (i, 0, 0)),
                   row(D_MODEL), row(D_MODEL)],
        out_shape=[jax.ShapeDtypeStruct((N_HEADS * HEAD_PAD, TOKENS), BF16),
                   jax.ShapeDtypeStruct((TOKENS, N_HEADS * HEAD_PAD), BF16),
                   jax.ShapeDtypeStruct((TOKENS // TK, N_HEADS * V_ROWS, TK), BF16),
                   jax.ShapeDtypeStruct((TOKENS, D_MODEL), BF16),
                   jax.ShapeDtypeStruct((TOKENS, D_MODEL), BF16)],
        scratch_shapes=[pltpu.VMEM((tm + CONV_HIST, D_CONV), F32)],
        compiler_params=pltpu.CompilerParams(dimension_semantics=("arbitrary",), vmem_limit_bytes=VMEM_LIMIT),
        name="front",
    )(x, an, win_all, cw, wco, qn, wq, kvn, wuk, wuv, cos_t, sin_t)


def _attn_items():
    items = []
    for pair in range(ATTN_PAIRS):
        for i in range(SEQ // TQ):
            for ks in range(0, (i + 1) * TQ, TK):
                items.append((pair, i, ks, TK, ks + TK == (i + 1) * TQ))
    return items


def _attn_kernel(qt_ref, k_ref, vt_ref, o_ref):
    row = lax.broadcasted_iota(jnp.int32, (TK, TQ), 0)
    col = lax.broadcasted_iota(jnp.int32, (TK, TQ), 1)

    def scores(item):
        pair, i, ks, n, _ = item
        sts = []
        for h in range(2 * pair, 2 * pair + 2):
            hs = slice(h * HEAD_PAD, (h + 1) * HEAD_PAD)
            st = jnp.dot(k_ref[ks:ks + n, hs], qt_ref[hs, i * TQ:(i + 1) * TQ], preferred_element_type=F32)
            if ks + n > i * TQ:
                st = jnp.where(row + (ks - i * TQ) <= col, st, NEG_BIG)
            sts.append(st)
        return sts

    def consume(item, sts, carry):
        pair, i, ks, n, _ = item
        stats = []
        for j in range(2):
            m, _ = carry[j]
            m_new = jnp.maximum(m, jnp.max(sts[j], axis=0, keepdims=True))
            stats.append((m_new, jnp.exp2(m - m_new), jnp.exp2(sts[j] - m_new).astype(BF16)))
        new = []
        for j in range(2):
            h = 2 * pair + j
            m_new, alpha, p = stats[j]
            vt = vt_ref[ks // TK, h * V_ROWS:(h + 1) * V_ROWS, ks % TK:ks % TK + n]
            new.append((m_new, alpha * carry[j][1] + jnp.dot(vt, p, preferred_element_type=F32)))
        return new

    items = _attn_items()
    init = [(jnp.full((1, TQ), NEG_BIG, F32), jnp.zeros((V_ROWS, TQ), F32)) for _ in range(2)]
    pending = [scores(it) for it in items[:ATTN_LOOKAHEAD]]
    carry = [init] * ATTN_PAIRS
    for t, item in enumerate(items):
        if t + ATTN_LOOKAHEAD < len(items):
            pending.append(scores(items[t + ATTN_LOOKAHEAD]))
        pair, i = item[0], item[1]
        carry[pair] = consume(item, pending.pop(0), carry[pair])
        if item[4]:
            out_t = jnp.concatenate([acc[0:V_HEAD] / acc[V_HEAD:V_HEAD + 1] for (_, acc) in carry[pair]], axis=0)
            o_ref[i * TQ:(i + 1) * TQ, pair * 2 * V_HEAD:(pair + 1) * 2 * V_HEAD] = out_t.T.astype(BF16)
            carry[pair] = init


def _attention(qt, k, vt):
    heads = 2 * ATTN_PAIRS
    return pl.pallas_call(
        _attn_kernel,
        grid=(BATCH, N_HEADS // heads),
        in_specs=[pl.BlockSpec((heads * HEAD_PAD, SEQ), lambda b, hp: (hp, b)),
                  pl.BlockSpec((SEQ, heads * HEAD_PAD), lambda b, hp: (b, hp)),
                  pl.BlockSpec((SEQ // TK, heads * V_ROWS, TK), lambda b, hp: (b, hp, 0))],
        out_specs=pl.BlockSpec((SEQ, heads * V_HEAD), lambda b, hp: (b, hp)),
        out_shape=jax.ShapeDtypeStruct((TOKENS, N_HEADS * V_HEAD), BF16),
        compiler_params=pltpu.CompilerParams(dimension_semantics=("arbitrary",) * 2, vmem_limit_bytes=VMEM_LIMIT),
        name="attention",
    )(qt, k, vt)


def _merge_core(attn_ref, gc_ref, sg_ref, x_ref, wmo_ref, wo_ref, fn_ref):
    y_mla = jnp.dot(attn_ref[...], wmo_ref[...], preferred_element_type=F32)
    merged = gc_ref[...].astype(F32) + sg_ref[...].astype(F32) * y_mla
    x2 = x_ref[...] + jnp.dot(merged.astype(BF16), wo_ref[...], preferred_element_type=F32)
    return x2, _rms(x2, fn_ref[...])


def _merge_ffn_kernel(attn_ref, gc_ref, sg_ref, x_ref, wmo_ref, wo_ref, fn_ref, wg_ref, wu_ref, wd_ref, o_ref):
    x2, hn = _merge_core(attn_ref, gc_ref, sg_ref, x_ref, wmo_ref, wo_ref, fn_ref)
    hn = hn.astype(BF16)
    acc = x2
    for c in range(D_FF // FF_CHUNK):
        cs = slice(c * FF_CHUNK, (c + 1) * FF_CHUNK)
        g = jnp.dot(hn, wg_ref[:, cs], preferred_element_type=F32)
        u = jnp.dot(hn, wu_ref[:, cs], preferred_element_type=F32)
        a = (jax.nn.silu(g) * u).astype(BF16)
        acc = acc + jnp.dot(a, wd_ref[cs, :], preferred_element_type=F32)
    o_ref[...] = acc


def _merge_route_kernel(attn_ref, gc_ref, sg_ref, x_ref, wmo_ref, wo_ref, fn_ref, router_ref,
                        x2_out, hn_out, route_t_out):
    x2, hn = _merge_core(attn_ref, gc_ref, sg_ref, x_ref, wmo_ref, wo_ref, fn_ref)
    x2_out[...] = x2
    hb = hn.astype(BF16)
    hn_out[...] = hb
    hl = (hn - hb.astype(F32)).astype(BF16)
    r = router_ref[...]
    rh = r.astype(BF16)
    rl = (r - rh.astype(F32)).astype(BF16)
    logits = _dot_nt(rh, hb) + _dot_nt(rh, hl) + _dot_nt(rl, hb)
    erow = lax.broadcasted_iota(jnp.int32, logits.shape, 0)
    logits = jnp.where(erow < N_EXPERTS, logits, -jnp.inf)
    m1 = jnp.max(logits, axis=0, keepdims=True)
    i1 = jnp.min(jnp.where(logits == m1, erow, E_PAD), axis=0, keepdims=True)
    rest = jnp.where(erow == i1, -jnp.inf, logits)
    m2 = jnp.max(rest, axis=0, keepdims=True)
    i2 = jnp.min(jnp.where(rest == m2, erow, E_PAD), axis=0, keepdims=True)
    t = jnp.exp(m2 - m1)
    w1 = 1.0 / (1.0 + t)
    w2 = t * w1
    route_t = jnp.where(erow == 0, i1.astype(F32),
                        jnp.where(erow == 1, i2.astype(F32), jnp.where(erow == 2, w1, jnp.where(erow == 3, w2, 0.0))))
    route_t_out[...] = route_t[0:SUBLANES, :]


def _merge(attn, gc, sg, x, wmo, wo, fn, router=None, ffn=None):
    tm = TM_MERGE
    row = lambda n: pl.BlockSpec((tm, n), lambda i: (i, 0))
    in_specs = [row(N_HEADS * V_HEAD), row(D_MODEL), row(D_MODEL), row(D_MODEL),
                _const_spec((N_HEADS * V_HEAD, D_MODEL)), _const_spec((D_MODEL, D_MODEL)), _const_spec((1, D_MODEL))]
    args = [attn, gc, sg, x, wmo, wo, fn]
    if router is None:
        kern = _merge_ffn_kernel
        in_specs += [_const_spec((D_MODEL, D_FF)), _const_spec((D_MODEL, D_FF)), _const_spec((D_FF, D_MODEL))]
        args += list(ffn)
        out_specs = row(D_MODEL)
        out_shape = jax.ShapeDtypeStruct((TOKENS, D_MODEL), F32)
        name = "merge_ffn"
    else:
        kern = _merge_route_kernel
        in_specs.append(_const_spec((E_PAD, D_MODEL)))
        args.append(router)
        out_specs = [row(D_MODEL), row(D_MODEL), pl.BlockSpec((SUBLANES, tm), lambda i: (0, i))]
        out_shape = [jax.ShapeDtypeStruct((TOKENS, D_MODEL), F32), jax.ShapeDtypeStruct((TOKENS, D_MODEL), BF16),
                     jax.ShapeDtypeStruct((SUBLANES, TOKENS), F32)]
        name = "merge_route"
    return pl.pallas_call(
        kern, grid=(TOKENS // tm,), in_specs=in_specs, out_specs=out_specs, out_shape=out_shape,
        compiler_params=pltpu.CompilerParams(dimension_semantics=("arbitrary",), vmem_limit_bytes=VMEM_LIMIT),
        name=name,
    )(*args)


def _pieces(n, sizes):
    return [((n & z) != 0, n & ~(2 * z - 1), z) for z in sizes]


def _aligned(rows, z):
    return pl.ds(rows if isinstance(rows, int) else pl.multiple_of(rows, SUBLANES), z)


def _start_slab_copies(tile, base_ref, offp_ref, np_ref, make_copy):
    for e in range(N_EXPERTS):
        n = np_ref[tile * N_EXPERTS + e]
        src = offp_ref[tile * N_EXPERTS + e]
        dst = base_ref[tile * N_EXPERTS + e]
        for pred, off, z in _pieces(n, SLAB_PIECES):
            @pl.when(pred)
            def _(off=off, z=z):
                make_copy(src + off, dst + off, z).start()


def _wait_rows(total, make_copy):
    for pred, _, z in _pieces(total, WAIT_PIECES):
        @pl.when(pred)
        def _(z=z):
            make_copy(0, 0, z).wait()


def _tile_placement(rt_ref, offp_ref, i):
    e0 = rt_ref[0:1, :].astype(jnp.int32)
    e1 = rt_ref[1:2, :].astype(jnp.int32)
    eid = lax.broadcasted_iota(jnp.int32, (E_PAD, TT), 0)
    sel = jnp.where(e0 == eid, 1.0, jnp.where(e1 == eid, 1.0, 0.0)).astype(BF16)
    r = lax.broadcasted_iota(jnp.int32, (TT, TT), 0)
    c = lax.broadcasted_iota(jnp.int32, (TT, TT), 1)
    earlier = jnp.where(r < c, 1.0, 0.0).astype(BF16)
    rank = jnp.dot(sel, earlier, preferred_element_type=F32)
    ecol = lax.broadcasted_iota(jnp.int32, (E_PAD, 1), 0)
    offp = jnp.zeros((E_PAD, 1), F32)
    for e in range(N_EXPERTS):
        offp = jnp.where(ecol == e, offp_ref[i * N_EXPERTS + e].astype(F32), offp)
    slot_all = rank + offp
    slot0 = jnp.sum(jnp.where(e0 == eid, slot_all, 0.0), axis=0, keepdims=True)
    slot1 = jnp.sum(jnp.where(e1 == eid, slot_all, 0.0), axis=0, keepdims=True)
    sidx = lax.broadcasted_iota(jnp.int32, (SLOTS, TT), 0).astype(F32)
    place = jnp.where(sidx == slot0, 1.0, jnp.where(sidx == slot1, 1.0, 0.0)).astype(BF16)
    return sidx, slot0, slot1, place


def _dispatch_kernel(base_ref, offp_ref, np_ref, tot_ref, tail_ref, hn_ref, rt_ref, xs_hbm, xs_scr, zbuf, sem, zsem):
    i = pl.program_id(0)
    slot = i % 2

    def copy(buf_slot):
        return lambda s, d, z: pltpu.make_async_copy(xs_scr.at[buf_slot, _aligned(s, z), :],
                                                     xs_hbm.at[_aligned(d, z), :], sem.at[buf_slot])

    def zero_copy(d, z):
        return pltpu.make_async_copy(zbuf.at[pl.ds(0, z), :], xs_hbm.at[_aligned(d, z), :], zsem)

    @pl.when(i == 0)
    def _():
        zbuf[...] = jnp.zeros_like(zbuf)
        for phase in ("start", "wait"):
            for e in range(N_EXPERTS):
                t0 = tail_ref[e]
                n = tail_ref[N_EXPERTS + e]
                full = lax.shift_right_logical(n, jnp.int32(TM_MOE.bit_length() - 1))

                def chunk(c, carry, t0=t0, phase=phase):
                    cp = zero_copy(t0 + c * TM_MOE, TM_MOE)
                    cp.start() if phase == "start" else cp.wait()
                    return carry

                lax.fori_loop(0, full, chunk, 0)
                rest = n & (TM_MOE - 1)
                for pred, off, z in _pieces(rest, TAIL_PIECES):
                    @pl.when(pred)
                    def _(off=off, z=z, t0=t0, full=full, phase=phase):
                        cp = zero_copy(t0 + full * TM_MOE + off, z)
                        cp.start() if phase == "start" else cp.wait()

    sidx, slot0, slot1, place = _tile_placement(rt_ref, offp_ref, i)
    xs_scr[slot, :, 0:D_MODEL] = jnp.dot(place, hn_ref[...], preferred_element_type=F32)
    w_slot = jnp.sum(jnp.where(sidx == slot0, rt_ref[2:3, :], jnp.where(sidx == slot1, rt_ref[3:4, :], 0.0)),
                     axis=1, keepdims=True)
    xs_scr[slot, :, D_MODEL:] = jnp.broadcast_to(w_slot, (SLOTS, LANES))

    _start_slab_copies(i, base_ref, offp_ref, np_ref, copy(slot))

    @pl.when(i > 0)
    def _():
        _wait_rows(tot_ref[i - 1], copy(1 - slot))

    @pl.when(i == pl.num_programs(0) - 1)
    def _():
        _wait_rows(tot_ref[i], copy(slot))


def _dispatch(tables, hn, route_t):
    grid_spec = pltpu.PrefetchScalarGridSpec(
        num_scalar_prefetch=5,
        grid=(N_TT,),
        in_specs=[pl.BlockSpec((TT, D_MODEL), lambda i, *_: (i, 0)),
                  pl.BlockSpec((SUBLANES, TT), lambda i, *_: (0, i))],
        out_specs=pl.BlockSpec(memory_space=pl.ANY),
        scratch_shapes=[pltpu.VMEM((2, SLOTS, XS_WIDTH), F32), pltpu.VMEM((TM_MOE, XS_WIDTH), F32),
                        pltpu.SemaphoreType.DMA((2,)), pltpu.SemaphoreType.DMA],
    )
    return pl.pallas_call(
        _dispatch_kernel,
        grid_spec=grid_spec,
        out_shape=jax.ShapeDtypeStruct((P_PAD, XS_WIDTH), F32),
        compiler_params=pltpu.CompilerParams(dimension_semantics=("arbitrary",), vmem_limit_bytes=VMEM_LIMIT),
        name="moe_dispatch",
    )(tables["base"], tables["offp"], tables["np"], tables["total"], tables["tail"], hn, route_t)


def _moe_kernel(te_ref, nv_ref, nxt_ref, par_ref, first_ref, last_ref, x_ref, wg_hbm, wu_hbm, wd_hbm, y_ref,
                stage_g, stage_u, stage_d, wgu_scr, wd_scr, sem):
    i = pl.program_id(0)
    valid = i < nv_ref[0]
    nxt = nxt_ref[i]

    def fetch(e):
        return [pltpu.make_async_copy(wg_hbm.at[e], stage_g, sem.at[0]),
                pltpu.make_async_copy(wu_hbm.at[e], stage_u, sem.at[1]),
                pltpu.make_async_copy(wd_hbm.at[e], stage_d, sem.at[2])]

    def round_into(s):
        wgu_scr[s, :, 0:D_FF_EXPERT] = stage_g[...].astype(BF16)
        wgu_scr[s, :, D_FF_EXPERT:] = stage_u[...].astype(BF16)
        wd_scr[s] = stage_d[...].astype(BF16)

    @pl.when(i == 0)
    def _():
        for cp in fetch(te_ref[0]):
            cp.start()
        for cp in fetch(te_ref[0]):
            cp.wait()
        round_into(par_ref[0])

    @pl.when(jnp.logical_and(valid, jnp.logical_and(first_ref[i] == 1, nxt >= 0)))
    def _():
        for cp in fetch(nxt):
            cp.start()

    @pl.when(valid)
    def _():
        s = par_ref[i]
        x = x_ref[:, 0:D_MODEL].astype(BF16)
        h = jnp.dot(x, wgu_scr[s], preferred_element_type=F32)
        a = (jax.nn.silu(h[:, :D_FF_EXPERT]) * h[:, D_FF_EXPERT:]).astype(BF16)
        y_ref[...] = jnp.dot(a, wd_scr[s], preferred_element_type=F32) * x_ref[:, D_MODEL:D_MODEL + 1]

    @pl.when(jnp.logical_and(valid, jnp.logical_and(last_ref[i] == 1, nxt >= 0)))
    def _():
        for cp in fetch(nxt):
            cp.wait()
        round_into(1 - par_ref[i])

    @pl.when(jnp.logical_not(valid))
    def _():
        y_ref[...] = jnp.zeros_like(y_ref)


def _moe(sched, xs, wg, wu, wd):
    grid_spec = pltpu.PrefetchScalarGridSpec(
        num_scalar_prefetch=6,
        grid=(N_MOE_TILES,),
        in_specs=[pl.BlockSpec((TM_MOE, XS_WIDTH), lambda i, *_: (i, 0)),
                  pl.BlockSpec(memory_space=pl.ANY), pl.BlockSpec(memory_space=pl.ANY),
                  pl.BlockSpec(memory_space=pl.ANY)],
        out_specs=pl.BlockSpec((TM_MOE, D_MODEL), lambda i, *_: (i, 0)),
        scratch_shapes=[pltpu.VMEM((D_MODEL, D_FF_EXPERT), F32), pltpu.VMEM((D_MODEL, D_FF_EXPERT), F32),
                        pltpu.VMEM((D_FF_EXPERT, D_MODEL), F32),
                        pltpu.VMEM((2, D_MODEL, 2 * D_FF_EXPERT), BF16), pltpu.VMEM((2, D_FF_EXPERT, D_MODEL), BF16),
                        pltpu.SemaphoreType.DMA((3,))],
    )
    return pl.pallas_call(
        _moe_kernel,
        grid_spec=grid_spec,
        out_shape=jax.ShapeDtypeStruct((P_PAD, D_MODEL), F32),
        compiler_params=pltpu.CompilerParams(dimension_semantics=("arbitrary",), vmem_limit_bytes=VMEM_LIMIT),
        name="moe_experts",
    )(sched["expert"], sched["n_valid"], sched["next"], sched["parity"], sched["first"], sched["last"],
      xs, wg, wu, wd)


def _combine_kernel(base_ref, offp_ref, np_ref, tot_ref, x_ref, route_ref, y_hbm, fn_ref, o_ref, ybuf, sem):
    i = pl.program_id(0)
    slot = i % 2

    def copy(buf_slot):
        return lambda s, d, z: pltpu.make_async_copy(y_hbm.at[_aligned(d, z), :],
                                                     ybuf.at[buf_slot, _aligned(s, z), :], sem.at[buf_slot])

    @pl.when(i == 0)
    def _():
        ybuf[...] = jnp.zeros_like(ybuf)
        _start_slab_copies(0, base_ref, offp_ref, np_ref, copy(0))

    @pl.when(i + 1 < pl.num_programs(0))
    def _():
        _start_slab_copies(i + 1, base_ref, offp_ref, np_ref, copy(1 - slot))

    _, _, _, place = _tile_placement(route_ref, offp_ref, i)

    _wait_rows(tot_ref[i], copy(slot))
    y = ybuf[slot]
    yh = y.astype(BF16)
    yl = (y - yh.astype(F32)).astype(BF16)
    tn = (((0,), (0,)), ((), ()))
    moe = (lax.dot_general(place, yh, tn, preferred_element_type=F32)
           + lax.dot_general(place, yl, tn, preferred_element_type=F32))
    o_ref[...] = _rms(x_ref[...] + moe, fn_ref[...])


def _combine(tables, x2, route_t, y_sorted, fn):
    grid_spec = pltpu.PrefetchScalarGridSpec(
        num_scalar_prefetch=4,
        grid=(N_TT,),
        in_specs=[pl.BlockSpec((TT, D_MODEL), lambda i, *_: (i, 0)),
                  pl.BlockSpec((SUBLANES, TT), lambda i, *_: (0, i)),
                  pl.BlockSpec(memory_space=pl.ANY),
                  pl.BlockSpec((1, D_MODEL), lambda i, *_: (0, 0))],
        out_specs=pl.BlockSpec((TT, D_MODEL), lambda i, *_: (i, 0)),
        scratch_shapes=[pltpu.VMEM((2, SLOTS, D_MODEL), F32), pltpu.SemaphoreType.DMA((2,))],
    )
    return pl.pallas_call(
        _combine_kernel,
        grid_spec=grid_spec,
        out_shape=jax.ShapeDtypeStruct((TOKENS, D_MODEL), F32),
        compiler_params=pltpu.CompilerParams(dimension_semantics=("arbitrary",), vmem_limit_bytes=VMEM_LIMIT),
        name="moe_combine",
    )(tables["base"], tables["offp"], tables["np"], tables["total"], x2, route_t, y_sorted, fn)


def _routing_tables(route_t):
    experts = route_t[:TOP_K, :].astype(jnp.int32)
    eid = jnp.arange(N_EXPERTS, dtype=jnp.int32)[:, None, None]
    chosen = jnp.sum((experts[None, :, :] == eid).astype(jnp.int32), axis=1)
    cnt = jnp.sum(chosen.reshape(N_EXPERTS, N_TT, TT), axis=2).T
    npad = (cnt + SUBLANES - 1) // SUBLANES * SUBLANES
    offp = jnp.cumsum(npad, axis=1) - npad
    total = jnp.sum(npad, axis=1)
    group = jnp.sum(npad, axis=0)
    group_pad = (group + TM_MOE - 1) // TM_MOE * TM_MOE
    group_end = jnp.cumsum(group_pad)
    group_start = group_end - group_pad
    base = group_start[None, :] + jnp.cumsum(npad, axis=0) - npad
    tail_start = group_start + group
    is_last = jnp.arange(N_EXPERTS) == N_EXPERTS - 1
    tail_len = jnp.where(is_last, P_PAD - tail_start, group_pad - group)
    tile_start = jnp.arange(N_MOE_TILES, dtype=jnp.int32) * TM_MOE
    tile_expert = jnp.minimum(jnp.sum((tile_start[:, None] >= group_end[None, :]).astype(jnp.int32), axis=1),
                              N_EXPERTS - 1)
    ids = jnp.arange(N_EXPERTS, dtype=jnp.int32)
    nonempty = group_pad > 0
    ordinal = jnp.cumsum(nonempty.astype(jnp.int32)) - 1
    later = jnp.logical_and(nonempty[None, :], ids[None, :] > ids[:, None])
    next_expert = jnp.min(jnp.where(later, ids[None, :], N_EXPERTS), axis=1)
    next_expert = jnp.where(next_expert == N_EXPERTS, -1, next_expert)
    is_tile_expert = tile_expert[:, None] == ids[None, :]
    per_tile = lambda table: jnp.sum(jnp.where(is_tile_expert, table[None, :], 0), axis=1)
    sched = {"expert": tile_expert, "n_valid": (group_end[-1] // TM_MOE).reshape(1),
             "next": per_tile(next_expert), "parity": per_tile(ordinal) % 2,
             "first": tile_start == per_tile(group_start),
             "last": tile_start + TM_MOE == per_tile(group_end)}
    tables = {"base": base.reshape(-1), "offp": offp.reshape(-1), "np": npad.reshape(-1), "total": total,
              "tail": jnp.concatenate([tail_start, tail_len])}
    as_i32 = lambda d: {k: v.astype(jnp.int32) for k, v in d.items()}
    return as_i32(tables), as_i32(sched)


WIN_ROWS = 256


def _win_layout_kernel(wt_ref, o_ref):
    wt = wt_ref[0]
    cols = wt.shape[1]
    kr_tile = jnp.concatenate([jnp.zeros((QK_NOPE, cols), F32), wt[REF_KR:REF_GATE, :],
                               jnp.zeros((HEAD_PAD - QK_DIM, cols), F32)], axis=0)
    conv_rows = [wt[part * D_CONV + c * CONV_CHUNK:part * D_CONV + (c + 1) * CONV_CHUNK, :]
                 for c in range(D_CONV // CONV_CHUNK) for part in range(3)]
    grouped = jnp.concatenate(conv_rows + [wt[REF_GATE:, :], wt[REF_CQ:REF_KR, :], kr_tile], axis=0)
    o_ref[0] = grouped.T.astype(BF16)


def _win_layout(w_in):
    wt = jnp.swapaxes(w_in, 1, 2)
    d_in = wt.shape[1]
    return pl.pallas_call(
        _win_layout_kernel,
        grid=(DEPTH, D_MODEL // WIN_ROWS),
        in_specs=[pl.BlockSpec((1, d_in, WIN_ROWS), lambda l, r: (l, 0, r))],
        out_specs=pl.BlockSpec((1, WIN_ROWS, D_IN_PAD), lambda l, r: (l, r, 0)),
        out_shape=jax.ShapeDtypeStruct((DEPTH, D_MODEL, D_IN_PAD), BF16),
        compiler_params=pltpu.CompilerParams(dimension_semantics=("arbitrary",) * 2, vmem_limit_bytes=VMEM_LIMIT),
        name="win_layout",
    )(wt)


def _layer_weights(w_uq, w_ukv):
    wq = w_uq.T.astype(BF16)

    wkv = w_ukv.reshape(KV_LORA, N_HEADS, QK_NOPE + V_HEAD)
    wuk = jnp.concatenate([wkv[:, :, :QK_NOPE], jnp.zeros((KV_LORA, N_HEADS, HEAD_PAD - QK_NOPE), F32)], axis=2)
    wuk = wuk.reshape(KV_LORA, N_HEADS * HEAD_PAD).astype(BF16)
    wuv = wkv[:, :, QK_NOPE:].reshape(KV_LORA, N_HEADS * V_HEAD).T.astype(BF16)
    return wq, wuk, wuv


def kernel(x, positions, attn_norm, w_in, conv_w, w_conv_out, q_norm, w_uq, kv_norm, w_ukv, w_mla_out, w_o, ffn_norm,
           w_gate, w_up, w_down, router, w_gate_e, w_up_e, w_down_e, final_norm):
    assert x.shape == (BATCH, SEQ, D_MODEL) and positions.shape == (BATCH, SEQ)
    cos_t, sin_t = _rope_tables(positions)
    xt = x.reshape(TOKENS, D_MODEL)
    out = None
    win_all = _win_layout(w_in)
    for l in range(DEPTH):
        wq, wuk, wuv = _layer_weights(w_uq[l], w_ukv[l])
        q, k, v, gc, sg = _front(l, xt, attn_norm[l].reshape(1, D_MODEL), win_all, conv_w[l],
                                 w_conv_out[l].astype(BF16),
                                 q_norm[l].reshape(1, Q_LORA), wq, kv_norm[l].reshape(1, KV_LORA), wuk, wuv,
                                 cos_t, sin_t)
        attn = _attention(q, k, v)
        wmo = w_mla_out[l].astype(BF16)
        wo = w_o[l].astype(BF16)
        fn = ffn_norm[l].reshape(1, D_MODEL)
        if l % 2 == 0:
            j = l // 2
            xt = _merge(attn, gc, sg, xt, wmo, wo, fn,
                        ffn=(w_gate[j].astype(BF16), w_up[j].astype(BF16), w_down[j].astype(BF16)))
        else:
            j = l // 2
            router_pad = jnp.pad(router[j].T, ((0, E_PAD - N_EXPERTS), (0, 0)))
            x2, hn, route_t = _merge(attn, gc, sg, xt, wmo, wo, fn, router_pad)
            tables, sched = _routing_tables(route_t)
            xs = _dispatch(tables, hn, route_t)
            y_sorted = _moe(sched, xs, w_gate_e[j], w_up_e[j], w_down_e[j])
            out = _combine(tables, x2, route_t, y_sorted, final_norm.reshape(1, D_MODEL))
    return out.reshape(BATCH, SEQ, D_MODEL)
```

```python
import math

import jax
import jax.numpy as jnp
from jax import lax
from jax.experimental import pallas as pl
from jax.experimental.pallas import tpu as pltpu

F32 = jnp.float32
BF16 = jnp.bfloat16

D_MODEL = 1024
BATCH = 8
SEQ = 2048
TOKENS = BATCH * SEQ
DEPTH = 2
D_CONV = 512
CONV_WIDTH = 3
N_HEADS = 8
QK_NOPE = 64
QK_ROPE = 32
HALF_ROPE = QK_ROPE // 2
V_HEAD = 64
Q_LORA = 384
KV_LORA = 256
ROPE_THETA = 10000.0
D_FF = 2816
N_EXPERTS = 8
TOP_K = 2
D_FF_EXPERT = 1408
EPS = 1e-6

LANES = 128
SUBLANES = 8
BF16_SUBLANES = 16
MXU_TILE = 256
VMEM_LIMIT = 56 * 1024 * 1024

HEAD_PAD = LANES
QK_DIM = QK_NOPE + QK_ROPE
Q_SCALE = (1.0 / math.sqrt(QK_DIM)) * math.log2(math.e)
NEG_BIG = -1e30

OFF_BCU = 0
OFF_GATE = 3 * D_CONV
OFF_SMALL = OFF_GATE + 2 * D_MODEL
D_IN_PAD = OFF_SMALL + Q_LORA + KV_LORA + LANES
REF_CQ = 3 * D_CONV
REF_KR = REF_CQ + Q_LORA + KV_LORA
REF_GATE = REF_KR + QK_ROPE

CONV_CHUNK = MXU_TILE
CONV_HIST = SUBLANES
assert CONV_WIDTH - 1 <= CONV_HIST
TM_FRONT = 512
TM_MERGE = 1024
TQ = MXU_TILE
TK = 256
TS = 128
ATTN_PAIRS = 2
ATTN_LOOKAHEAD = 3
V_ROWS = V_HEAD + BF16_SUBLANES
FF_CHUNK = MXU_TILE
assert D_FF % FF_CHUNK == 0 and D_CONV % CONV_CHUNK == 0 and TQ % TK == 0 and TK % TS == 0

TM_MOE = 512
TT = 256
N_TT = TOKENS // TT
E_PAD = BF16_SUBLANES
XS_WIDTH = D_MODEL + LANES
N_PAIRS = TOKENS * TOP_K
SLOTS = TT * TOP_K + N_EXPERTS * SUBLANES
_MAX_ROWS = N_PAIRS + N_TT * N_EXPERTS * (SUBLANES - 1) + N_EXPERTS * (TM_MOE - SUBLANES)
N_MOE_TILES = -(-_MAX_ROWS // TM_MOE)
P_PAD = N_MOE_TILES * TM_MOE


def _halvings(n):
    return tuple(n >> s for s in range((n // SUBLANES).bit_length()))


SLAB_PIECES = _halvings(TT)
WAIT_PIECES = _halvings(2 * TT)
TAIL_PIECES = _halvings(TM_MOE // 2)
assert SLOTS < 2 * WAIT_PIECES[0]


def _rms(x, g):
    return x * lax.rsqrt(jnp.mean(x * x, axis=-1, keepdims=True) + EPS) * g


def _const_spec(shape):
    nd = len(shape)
    return pl.BlockSpec(shape, lambda *_: (0,) * nd, pipeline_mode=pl.Buffered(1))


def _trig_kernel(pos_ref, invf_ref, cos_ref, sin_ref):
    ang = invf_ref[...] * pos_ref[...].astype(F32)
    cos_ref[...] = jnp.cos(ang)
    sin_ref[...] = jnp.sin(ang)


def _rope_tables(positions):
    inv_freq = ROPE_THETA ** (-jnp.arange(0, QK_ROPE, 2, dtype=F32) / QK_ROPE)
    return pl.pallas_call(
        _trig_kernel,
        out_shape=(jax.ShapeDtypeStruct((HALF_ROPE, TOKENS), F32),) * 2,
        name="rope_trig",
    )(positions.reshape(1, TOKENS), inv_freq.reshape(HALF_ROPE, 1))


def _dot_nt(a, b):
    return lax.dot_general(a, b, (((1,), (1,)), ((), ())), preferred_element_type=F32)


def _front_kernel(x_ref, an_ref, win_ref, cw_ref, wco_ref, qn_ref, wq_ref, kvn_ref, wuk_ref, wuv_ref,
                  cost_ref, sint_ref, q_out, k_out, v_out, gc_out, sg_out, cu_scr):
    tm = x_ref.shape[0]
    i = pl.program_id(0)
    xn = _rms(x_ref[...], an_ref[...]).astype(BF16)

    def proj(a, b):
        return jnp.dot(xn, win_ref[0, :, a:b], preferred_element_type=F32)

    small = proj(OFF_SMALL, D_IN_PAD)

    cq = cost_ref[...] * Q_SCALE
    sq = sint_ref[...] * Q_SCALE
    cqn = _rms(small[:, :Q_LORA], qn_ref[...]).astype(BF16)
    qt = _dot_nt(wq_ref[...], cqn)
    zero_pad = jnp.zeros((HEAD_PAD - QK_DIM, tm), BF16)
    for h in range(N_HEADS):
        src = h * QK_DIM
        dst = h * HEAD_PAD
        x1 = qt[src + QK_NOPE:src + QK_NOPE + HALF_ROPE, :]
        x2 = qt[src + QK_NOPE + HALF_ROPE:src + QK_DIM, :]
        q_out[dst:dst + QK_NOPE, :] = (qt[src:src + QK_NOPE, :] * Q_SCALE).astype(BF16)
        q_out[dst + QK_NOPE:dst + QK_NOPE + HALF_ROPE, :] = (x1 * cq - x2 * sq).astype(BF16)
        q_out[dst + QK_NOPE + HALF_ROPE:dst + QK_DIM, :] = (x2 * cq + x1 * sq).astype(BF16)
        q_out[dst + QK_DIM:dst + HEAD_PAD, :] = zero_pad

    cos_t = cost_ref[...]
    sin_t = sint_ref[...]
    z_nope = jnp.zeros((QK_NOPE, tm), F32)
    z_rope = jnp.zeros((HALF_ROPE, tm), F32)
    z_pad = jnp.zeros((HEAD_PAD - QK_DIM, tm), F32)
    ck = jnp.concatenate([z_nope, cos_t, cos_t, z_pad], axis=0).T
    sk_x1 = jnp.concatenate([z_nope, -sin_t, z_rope, z_pad], axis=0).T
    sk_x2 = jnp.concatenate([z_nope, z_rope, sin_t, z_pad], axis=0).T
    kr = small[:, Q_LORA + KV_LORA:]
    kpe = (kr * ck + pltpu.roll(kr, HEAD_PAD - HALF_ROPE, axis=1) * sk_x1 + pltpu.roll(kr, HALF_ROPE, axis=1) * sk_x2)
    ckvn = _rms(small[:, Q_LORA:Q_LORA + KV_LORA], kvn_ref[...]).astype(BF16)
    kn = jnp.dot(ckvn, wuk_ref[...], preferred_element_type=F32)
    for h in range(N_HEADS):
        blk = slice(h * HEAD_PAD, (h + 1) * HEAD_PAD)
        k_out[:, blk] = (kn[:, blk] + kpe).astype(BF16)
    vt = _dot_nt(wuv_ref[...], ckvn).astype(BF16)
    sub = lax.broadcasted_iota(jnp.int32, (V_ROWS - V_HEAD, TK), 0)
    ones_rows = jnp.where(sub == 0, 1.0, 0.0).astype(BF16)
    for c in range(tm // TK):
        for h in range(N_HEADS):
            v_out[c, h * V_ROWS:h * V_ROWS + V_HEAD, :] = vt[h * V_HEAD:(h + 1) * V_HEAD, c * TK:(c + 1) * TK]
            v_out[c, h * V_ROWS + V_HEAD:(h + 1) * V_ROWS, :] = ones_rows

    @pl.when(i % (SEQ // tm) == 0)
    def _():
        cu_scr[0:CONV_HIST, :] = jnp.zeros((CONV_HIST, D_CONV), F32)

    cw = cw_ref[...]
    y_conv = None
    for c in range(D_CONV // CONV_CHUNK):
        ch = slice(c * CONV_CHUNK, (c + 1) * CONV_CHUNK)
        bcu = proj(OFF_BCU + 3 * c * CONV_CHUNK, OFF_BCU + 3 * (c + 1) * CONV_CHUNK)
        b_g = bcu[:, :CONV_CHUNK]
        cu = bcu[:, CONV_CHUNK:2 * CONV_CHUNK] * bcu[:, 2 * CONV_CHUNK:]
        cu_scr[CONV_HIST:CONV_HIST + tm, ch] = cu
        conv = cu * cw[CONV_WIDTH - 1:CONV_WIDTH, ch]
        for tap in range(CONV_WIDTH - 1):
            back = CONV_WIDTH - 1 - tap
            conv = conv + cu_scr[CONV_HIST - back:CONV_HIST - back + tm, ch] * cw[tap:tap + 1, ch]
        cu_scr[0:CONV_HIST, ch] = cu[tm - CONV_HIST:, :]
        part = jnp.dot((b_g * conv).astype(BF16), wco_ref[ch, :], preferred_element_type=F32)
        y_conv = part if y_conv is None else y_conv + part

    gates = proj(OFF_GATE, OFF_SMALL)
    gc_out[...] = (jax.nn.sigmoid(gates[:, :D_MODEL]) * y_conv).astype(BF16)
    sg_out[...] = jax.nn.sigmoid(gates[:, D_MODEL:]).astype(BF16)


def _front(layer, x, an, win_all, cw, wco, qn, wq, kvn, wuk, wuv, cos_t, sin_t):
    tm = TM_FRONT
    row = lambda n: pl.BlockSpec((tm, n), lambda i: (i, 0))
    col = lambda n: pl.BlockSpec((n, tm), lambda i: (0, i))
    win_spec = pl.BlockSpec((1, D_MODEL, D_IN_PAD), lambda i: (layer, 0, 0), pipeline_mode=pl.Buffered(1))
    return pl.pallas_call(
        _front_kernel,
        grid=(TOKENS // tm,),
        in_specs=[row(D_MODEL), _const_spec((1, D_MODEL)), win_spec,
                  _const_spec((CONV_WIDTH, D_CONV)), _const_spec((D_CONV, D_MODEL)),
                  _const_spec((1, Q_LORA)), _const_spec((N_HEADS * QK_DIM, Q_LORA)),
                  _const_spec((1, KV_LORA)), _const_spec((KV_LORA, N_HEADS * HEAD_PAD)),
                  _const_spec((N_HEADS * V_HEAD, KV_LORA)),
                  col(HALF_ROPE), col(HALF_ROPE)],
        out_specs=[col(N_HEADS * HEAD_PAD), row(N_HEADS * HEAD_PAD),
                   pl.BlockSpec((tm // TK, N_HEADS * V_ROWS, TK), lambda i: (i, 0, 0)),
                   row(D_MODEL), row(D_MODEL)],
        out_shape=[jax.ShapeDtypeStruct((N_HEADS * HEAD_PAD, TOKENS), BF16),
                   jax.ShapeDtypeStruct((TOKENS, N_HEADS * HEAD_PAD), BF16),
                   jax.ShapeDtypeStruct((TOKENS // TK, N_HEADS * V_ROWS, TK), BF16),
                   jax.ShapeDtypeStruct((TOKENS, D_MODEL), BF16),
                   jax.ShapeDtypeStruct((TOKENS, D_MODEL), BF16)],
        scratch_shapes=[pltpu.VMEM((tm + CONV_HIST, D_CONV), F32)],
        compiler_params=pltpu.CompilerParams(dimension_semantics=("arbitrary",), vmem_limit_bytes=VMEM_LIMIT),
        name="front",
    )(x, an, win_all, cw, wco, qn, wq, kvn, wuk, wuv, cos_t, sin_t)


def _attn_items():
    items = []
    for pair in range(ATTN_PAIRS):
        for i in range(SEQ // TQ):
            for ks in range(0, (i + 1) * TQ, TK):
                items.append((pair, i, ks, TK, ks + TK == (i + 1) * TQ))
    return items


def _attn_kernel(qt_ref, k_ref, vt_ref, o_ref):
    row = lax.broadcasted_iota(jnp.int32, (TS, TQ), 0)
    col = lax.broadcasted_iota(jnp.int32, (TS, TQ), 1)

    def scores(item):
        pair, i, ks, n, _ = item
        sts = []
        for h in range(2 * pair, 2 * pair + 2):
            hs = slice(h * HEAD_PAD, (h + 1) * HEAD_PAD)
            tiles = []
            for k0 in range(ks, ks + n, TS):
                st = jnp.dot(k_ref[k0:k0 + TS, hs], qt_ref[hs, i * TQ:(i + 1) * TQ], preferred_element_type=F32)
                if k0 + TS > i * TQ:
                    st = jnp.where(row + (k0 - i * TQ) <= col, st, NEG_BIG)
                tiles.append(st)
            sts.append(tiles)
        return sts

    def consume(item, sts, carry):
        pair, i, ks, n, _ = item
        stats = []
        for j in range(2):
            m, _ = carry[j]
            m_new = m
            for st in sts[j]:
                m_new = jnp.maximum(m_new, jnp.max(st, axis=0, keepdims=True))
            p = jnp.concatenate([jnp.exp2(st - m_new).astype(BF16) for st in sts[j]], axis=0)
            stats.append((m_new, jnp.exp2(m - m_new), p))
        new = []
        for j in range(2):
            h = 2 * pair + j
            m_new, alpha, p = stats[j]
            vt = vt_ref[ks // TK, h * V_ROWS:(h + 1) * V_ROWS, ks % TK:ks % TK + n]
            new.append((m_new, alpha * carry[j][1] + jnp.dot(vt, p, preferred_element_type=F32)))
        return new

    items = _attn_items()
    init = [(jnp.full((1, TQ), NEG_BIG, F32), jnp.zeros((V_ROWS, TQ), F32)) for _ in range(2)]
    pending = [scores(it) for it in items[:ATTN_LOOKAHEAD]]
    carry = [init] * ATTN_PAIRS
    for t, item in enumerate(items):
        if t + ATTN_LOOKAHEAD < len(items):
            pending.append(scores(items[t + ATTN_LOOKAHEAD]))
        pair, i = item[0], item[1]
        carry[pair] = consume(item, pending.pop(0), carry[pair])
        if item[4]:
            out_t = jnp.concatenate([acc[0:V_HEAD] / acc[V_HEAD:V_HEAD + 1] for (_, acc) in carry[pair]], axis=0)
            o_ref[i * TQ:(i + 1) * TQ, pair * 2 * V_HEAD:(pair + 1) * 2 * V_HEAD] = out_t.T.astype(BF16)
            carry[pair] = init


def _attention(qt, k, vt):
    heads = 2 * ATTN_PAIRS
    return pl.pallas_call(
        _attn_kernel,
        grid=(BATCH, N_HEADS // heads),
        in_specs=[pl.BlockSpec((heads * HEAD_PAD, SEQ), lambda b, hp: (hp, b)),
                  pl.BlockSpec((SEQ, heads * HEAD_PAD), lambda b, hp: (b, hp)),
                  pl.BlockSpec((SEQ // TK, heads * V_ROWS, TK), lambda b, hp: (b, hp, 0))],
        out_specs=pl.BlockSpec((SEQ, heads * V_HEAD), lambda b, hp: (b, hp)),
        out_shape=jax.ShapeDtypeStruct((TOKENS, N_HEADS * V_HEAD), BF16),
        compiler_params=pltpu.CompilerParams(dimension_semantics=("arbitrary",) * 2, vmem_limit_bytes=VMEM_LIMIT),
        name="attention",
    )(qt, k, vt)


def _merge_core(attn_ref, gc_ref, sg_ref, x_ref, wmo_ref, wo_ref, fn_ref):
    y_mla = jnp.dot(attn_ref[...], wmo_ref[...], preferred_element_type=F32)
    merged = gc_ref[...].astype(F32) + sg_ref[...].astype(F32) * y_mla
    x2 = x_ref[...] + jnp.dot(merged.astype(BF16), wo_ref[...], preferred_element_type=F32)
    return x2, _rms(x2, fn_ref[...])


def _merge_ffn_kernel(attn_ref, gc_ref, sg_ref, x_ref, wmo_ref, wo_ref, fn_ref, wg_ref, wu_ref, wd_ref, o_ref):
    x2, hn = _merge_core(attn_ref, gc_ref, sg_ref, x_ref, wmo_ref, wo_ref, fn_ref)
    hn = hn.astype(BF16)
    acc = x2
    for c in range(D_FF // FF_CHUNK):
        cs = slice(c * FF_CHUNK, (c + 1) * FF_CHUNK)
        g = jnp.dot(hn, wg_ref[:, cs], preferred_element_type=F32)
        u = jnp.dot(hn, wu_ref[:, cs], preferred_element_type=F32)
        a = (jax.nn.silu(g) * u).astype(BF16)
        acc = acc + jnp.dot(a, wd_ref[cs, :], preferred_element_type=F32)
    o_ref[...] = acc


def _merge_route_kernel(attn_ref, gc_ref, sg_ref, x_ref, wmo_ref, wo_ref, fn_ref, router_ref,
                        x2_out, hn_out, route_t_out):
    x2, hn = _merge_core(attn_ref, gc_ref, sg_ref, x_ref, wmo_ref, wo_ref, fn_ref)
    x2_out[...] = x2
    hb = hn.astype(BF16)
    hn_out[...] = hb
    hl = (hn - hb.astype(F32)).astype(BF16)
    r = router_ref[...]
    rh = r.astype(BF16)
    rl = (r - rh.astype(F32)).astype(BF16)
    logits = _dot_nt(rh, hb) + _dot_nt(rh, hl) + _dot_nt(rl, hb)
    erow = lax.broadcasted_iota(jnp.int32, logits.shape, 0)
    logits = jnp.where(erow < N_EXPERTS, logits, -jnp.inf)
    m1 = jnp.max(logits, axis=0, keepdims=True)
    i1 = jnp.min(jnp.where(logits == m1, erow, E_PAD), axis=0, keepdims=True)
    rest = jnp.where(erow == i1, -jnp.inf, logits)
    m2 = jnp.max(rest, axis=0, keepdims=True)
    i2 = jnp.min(jnp.where(rest == m2, erow, E_PAD), axis=0, keepdims=True)
    t = jnp.exp(m2 - m1)
    w1 = 1.0 / (1.0 + t)
    w2 = t * w1
    route_t = jnp.where(erow == 0, i1.astype(F32),
                        jnp.where(erow == 1, i2.astype(F32), jnp.where(erow == 2, w1, jnp.where(erow == 3, w2, 0.0))))
    route_t_out[...] = route_t[0:SUBLANES, :]


def _merge(attn, gc, sg, x, wmo, wo, fn, router=None, ffn=None):
    tm = TM_MERGE
    row = lambda n: pl.BlockSpec((tm, n), lambda i: (i, 0))
    in_specs = [row(N_HEADS * V_HEAD), row(D_MODEL), row(D_MODEL), row(D_MODEL),
                _const_spec((N_HEADS * V_HEAD, D_MODEL)), _const_spec((D_MODEL, D_MODEL)), _const_spec((1, D_MODEL))]
    args = [attn, gc, sg, x, wmo, wo, fn]
    if router is None:
        kern = _merge_ffn_kernel
        in_specs += [_const_spec((D_MODEL, D_FF)), _const_spec((D_MODEL, D_FF)), _const_spec((D_FF, D_MODEL))]
        args += list(ffn)
        out_specs = row(D_MODEL)
        out_shape = jax.ShapeDtypeStruct((TOKENS, D_MODEL), F32)
        name = "merge_ffn"
    else:
        kern = _merge_route_kernel
        in_specs.append(_const_spec((E_PAD, D_MODEL)))
        args.append(router)
        out_specs = [row(D_MODEL), row(D_MODEL), pl.BlockSpec((SUBLANES, tm), lambda i: (0, i))]
        out_shape = [jax.ShapeDtypeStruct((TOKENS, D_MODEL), F32), jax.ShapeDtypeStruct((TOKENS, D_MODEL), BF16),
                     jax.ShapeDtypeStruct((SUBLANES, TOKENS), F32)]
        name = "merge_route"
    return pl.pallas_call(
        kern, grid=(TOKENS // tm,), in_specs=in_specs, out_specs=out_specs, out_shape=out_shape,
        compiler_params=pltpu.CompilerParams(dimension_semantics=("arbitrary",), vmem_limit_bytes=VMEM_LIMIT),
        name=name,
    )(*args)


def _pieces(n, sizes):
    return [((n & z) != 0, n & ~(2 * z - 1), z) for z in sizes]


def _aligned(rows, z):
    return pl.ds(rows if isinstance(rows, int) else pl.multiple_of(rows, SUBLANES), z)


def _start_slab_copies(tile, base_ref, offp_ref, np_ref, make_copy):
    for e in range(N_EXPERTS):
        n = np_ref[tile * N_EXPERTS + e]
        src = offp_ref[tile * N_EXPERTS + e]
        dst = base_ref[tile * N_EXPERTS + e]
        for pred, off, z in _pieces(n, SLAB_PIECES):
            @pl.when(pred)
            def _(off=off, z=z):
                make_copy(src + off, dst + off, z).start()


def _wait_rows(total, make_copy):
    for pred, _, z in _pieces(total, WAIT_PIECES):
        @pl.when(pred)
        def _(z=z):
            make_copy(0, 0, z).wait()


def _tile_placement(rt_ref, offp_ref, i):
    e0 = rt_ref[0:1, :].astype(jnp.int32)
    e1 = rt_ref[1:2, :].astype(jnp.int32)
    eid = lax.broadcasted_iota(jnp.int32, (E_PAD, TT), 0)
    sel = jnp.where(e0 == eid, 1.0, jnp.where(e1 == eid, 1.0, 0.0)).astype(BF16)
    r = lax.broadcasted_iota(jnp.int32, (TT, TT), 0)
    c = lax.broadcasted_iota(jnp.int32, (TT, TT), 1)
    earlier = jnp.where(r < c, 1.0, 0.0).astype(BF16)
    rank = jnp.dot(sel, earlier, preferred_element_type=F32)
    ecol = lax.broadcasted_iota(jnp.int32, (E_PAD, 1), 0)
    offp = jnp.zeros((E_PAD, 1), F32)
    for e in range(N_EXPERTS):
        offp = jnp.where(ecol == e, offp_ref[i * N_EXPERTS + e].astype(F32), offp)
    slot_all = rank + offp
    slot0 = jnp.sum(jnp.where(e0 == eid, slot_all, 0.0), axis=0, keepdims=True)
    slot1 = jnp.sum(jnp.where(e1 == eid, slot_all, 0.0), axis=0, keepdims=True)
    sidx = lax.broadcasted_iota(jnp.int32, (SLOTS, TT), 0).astype(F32)
    place = jnp.where(sidx == slot0, 1.0, jnp.where(sidx == slot1, 1.0, 0.0)).astype(BF16)
    return sidx, slot0, slot1, place


def _dispatch_kernel(base_ref, offp_ref, np_ref, tot_ref, tail_ref, hn_ref, rt_ref, xs_hbm, xs_scr, zbuf, sem, zsem):
    i = pl.program_id(0)
    slot = i % 2

    def copy(buf_slot):
        return lambda s, d, z: pltpu.make_async_copy(xs_scr.at[buf_slot, _aligned(s, z), :],
                                                     xs_hbm.at[_aligned(d, z), :], sem.at[buf_slot])

    def zero_copy(d, z):
        return pltpu.make_async_copy(zbuf.at[pl.ds(0, z), :], xs_hbm.at[_aligned(d, z), :], zsem)

    @pl.when(i == 0)
    def _():
        zbuf[...] = jnp.zeros_like(zbuf)
        for phase in ("start", "wait"):
            for e in range(N_EXPERTS):
                t0 = tail_ref[e]
                n = tail_ref[N_EXPERTS + e]
                full = lax.shift_right_logical(n, jnp.int32(TM_MOE.bit_length() - 1))

                def chunk(c, carry, t0=t0, phase=phase):
                    cp = zero_copy(t0 + c * TM_MOE, TM_MOE)
                    cp.start() if phase == "start" else cp.wait()
                    return carry

                lax.fori_loop(0, full, chunk, 0)
                rest = n & (TM_MOE - 1)
                for pred, off, z in _pieces(rest, TAIL_PIECES):
                    @pl.when(pred)
                    def _(off=off, z=z, t0=t0, full=full, phase=phase):
                        cp = zero_copy(t0 + full * TM_MOE + off, z)
                        cp.start() if phase == "start" else cp.wait()

    sidx, slot0, slot1, place = _tile_placement(rt_ref, offp_ref, i)
    xs_scr[slot, :, 0:D_MODEL] = jnp.dot(place, hn_ref[...], preferred_element_type=F32)
    w_slot = jnp.sum(jnp.where(sidx == slot0, rt_ref[2:3, :], jnp.where(sidx == slot1, rt_ref[3:4, :], 0.0)),
                     axis=1, keepdims=True)
    xs_scr[slot, :, D_MODEL:] = jnp.broadcast_to(w_slot, (SLOTS, LANES))

    _start_slab_copies(i, base_ref, offp_ref, np_ref, copy(slot))

    @pl.when(i > 0)
    def _():
        _wait_rows(tot_ref[i - 1], copy(1 - slot))

    @pl.when(i == pl.num_programs(0) - 1)
    def _():
        _wait_rows(tot_ref[i], copy(slot))


def _dispatch(tables, hn, route_t):
    grid_spec = pltpu.PrefetchScalarGridSpec(
        num_scalar_prefetch=5,
        grid=(N_TT,),
        in_specs=[pl.BlockSpec((TT, D_MODEL), lambda i, *_: (i, 0)),
                  pl.BlockSpec((SUBLANES, TT), lambda i, *_: (0, i))],
        out_specs=pl.BlockSpec(memory_space=pl.ANY),
        scratch_shapes=[pltpu.VMEM((2, SLOTS, XS_WIDTH), F32), pltpu.VMEM((TM_MOE, XS_WIDTH), F32),
                        pltpu.SemaphoreType.DMA((2,)), pltpu.SemaphoreType.DMA],
    )
    return pl.pallas_call(
        _dispatch_kernel,
        grid_spec=grid_spec,
        out_shape=jax.ShapeDtypeStruct((P_PAD, XS_WIDTH), F32),
        compiler_params=pltpu.CompilerParams(dimension_semantics=("arbitrary",), vmem_limit_bytes=VMEM_LIMIT),
        name="moe_dispatch",
    )(tables["base"], tables["offp"], tables["np"], tables["total"], tables["tail"], hn, route_t)


def _moe_kernel(te_ref, nv_ref, nxt_ref, par_ref, first_ref, last_ref, x_ref, wg_hbm, wu_hbm, wd_hbm, y_ref,
                stage_g, stage_u, stage_d, wgu_scr, wd_scr, sem):
    i = pl.program_id(0)
    valid = i < nv_ref[0]
    nxt = nxt_ref[i]

    def fetch(e):
        return [pltpu.make_async_copy(wg_hbm.at[e], stage_g, sem.at[0]),
                pltpu.make_async_copy(wu_hbm.at[e], stage_u, sem.at[1]),
                pltpu.make_async_copy(wd_hbm.at[e], stage_d, sem.at[2])]

    def round_into(s):
        wgu_scr[s, :, 0:D_FF_EXPERT] = stage_g[...].astype(BF16)
        wgu_scr[s, :, D_FF_EXPERT:] = stage_u[...].astype(BF16)
        wd_scr[s] = stage_d[...].astype(BF16)

    @pl.when(i == 0)
    def _():
        for cp in fetch(te_ref[0]):
            cp.start()
        for cp in fetch(te_ref[0]):
            cp.wait()
        round_into(par_ref[0])

    @pl.when(jnp.logical_and(valid, jnp.logical_and(first_ref[i] == 1, nxt >= 0)))
    def _():
        for cp in fetch(nxt):
            cp.start()

    @pl.when(valid)
    def _():
        s = par_ref[i]
        x = x_ref[:, 0:D_MODEL].astype(BF16)
        h = jnp.dot(x, wgu_scr[s], preferred_element_type=F32)
        a = (jax.nn.silu(h[:, :D_FF_EXPERT]) * h[:, D_FF_EXPERT:]).astype(BF16)
        y_ref[...] = jnp.dot(a, wd_scr[s], preferred_element_type=F32) * x_ref[:, D_MODEL:D_MODEL + 1]

    @pl.when(jnp.logical_and(valid, jnp.logical_and(last_ref[i] == 1, nxt >= 0)))
    def _():
        for cp in fetch(nxt):
            cp.wait()
        round_into(1 - par_ref[i])

    @pl.when(jnp.logical_not(valid))
    def _():
        y_ref[...] = jnp.zeros_like(y_ref)


def _moe(sched, xs, wg, wu, wd):
    grid_spec = pltpu.PrefetchScalarGridSpec(
        num_scalar_prefetch=6,
        grid=(N_MOE_TILES,),
        in_specs=[pl.BlockSpec((TM_MOE, XS_WIDTH), lambda i, *_: (i, 0)),
                  pl.BlockSpec(memory_space=pl.ANY), pl.BlockSpec(memory_space=pl.ANY),
                  pl.BlockSpec(memory_space=pl.ANY)],
        out_specs=pl.BlockSpec((TM_MOE, D_MODEL), lambda i, *_: (i, 0)),
        scratch_shapes=[pltpu.VMEM((D_MODEL, D_FF_EXPERT), F32), pltpu.VMEM((D_MODEL, D_FF_EXPERT), F32),
                        pltpu.VMEM((D_FF_EXPERT, D_MODEL), F32),
                        pltpu.VMEM((2, D_MODEL, 2 * D_FF_EXPERT), BF16), pltpu.VMEM((2, D_FF_EXPERT, D_MODEL), BF16),
                        pltpu.SemaphoreType.DMA((3,))],
    )
    return pl.pallas_call(
        _moe_kernel,
        grid_spec=grid_spec,
        out_shape=jax.ShapeDtypeStruct((P_PAD, D_MODEL), F32),
        compiler_params=pltpu.CompilerParams(dimension_semantics=("arbitrary",), vmem_limit_bytes=VMEM_LIMIT),
        name="moe_experts",
    )(sched["expert"], sched["n_valid"], sched["next"], sched["parity"], sched["first"], sched["last"],
      xs, wg, wu, wd)


def _combine_kernel(base_ref, offp_ref, np_ref, tot_ref, x_ref, route_ref, y_hbm, fn_ref, o_ref, ybuf, sem):
    i = pl.program_id(0)
    slot = i % 2

    def copy(buf_slot):
        return lambda s, d, z: pltpu.make_async_copy(y_hbm.at[_aligned(d, z), :],
                                                     ybuf.at[buf_slot, _aligned(s, z), :], sem.at[buf_slot])

    @pl.when(i == 0)
    def _():
        ybuf[...] = jnp.zeros_like(ybuf)
        _start_slab_copies(0, base_ref, offp_ref, np_ref, copy(0))

    @pl.when(i + 1 < pl.num_programs(0))
    def _():
        _start_slab_copies(i + 1, base_ref, offp_ref, np_ref, copy(1 - slot))

    _, _, _, place = _tile_placement(route_ref, offp_ref, i)

    _wait_rows(tot_ref[i], copy(slot))
    y = ybuf[slot]
    yh = y.astype(BF16)
    yl = (y - yh.astype(F32)).astype(BF16)
    tn = (((0,), (0,)), ((), ()))
    moe = (lax.dot_general(place, yh, tn, preferred_element_type=F32)
           + lax.dot_general(place, yl, tn, preferred_element_type=F32))
    o_ref[...] = _rms(x_ref[...] + moe, fn_ref[...])


def _combine(tables, x2, route_t, y_sorted, fn):
    grid_spec = pltpu.PrefetchScalarGridSpec(
        num_scalar_prefetch=4,
        grid=(N_TT,),
        in_specs=[pl.BlockSpec((TT, D_MODEL), lambda i, *_: (i, 0)),
                  pl.BlockSpec((SUBLANES, TT), lambda i, *_: (0, i)),
                  pl.BlockSpec(memory_space=pl.ANY),
                  pl.BlockSpec((1, D_MODEL), lambda i, *_: (0, 0))],
        out_specs=pl.BlockSpec((TT, D_MODEL), lambda i, *_: (i, 0)),
        scratch_shapes=[pltpu.VMEM((2, SLOTS, D_MODEL), F32), pltpu.SemaphoreType.DMA((2,))],
    )
    return pl.pallas_call(
        _combine_kernel,
        grid_spec=grid_spec,
        out_shape=jax.ShapeDtypeStruct((TOKENS, D_MODEL), F32),
        compiler_params=pltpu.CompilerParams(dimension_semantics=("arbitrary",), vmem_limit_bytes=VMEM_LIMIT),
        name="moe_combine",
    )(tables["base"], tables["offp"], tables["np"], tables["total"], x2, route_t, y_sorted, fn)


def _routing_tables(route_t):
    experts = route_t[:TOP_K, :].astype(jnp.int32)
    eid = jnp.arange(N_EXPERTS, dtype=jnp.int32)[:, None, None]
    chosen = jnp.sum((experts[None, :, :] == eid).astype(jnp.int32), axis=1)
    cnt = jnp.sum(chosen.reshape(N_EXPERTS, N_TT, TT), axis=2).T
    npad = (cnt + SUBLANES - 1) // SUBLANES * SUBLANES
    offp = jnp.cumsum(npad, axis=1) - npad
    total = jnp.sum(npad, axis=1)
    group = jnp.sum(npad, axis=0)
    group_pad = (group + TM_MOE - 1) // TM_MOE * TM_MOE
    group_end = jnp.cumsum(group_pad)
    group_start = group_end - group_pad
    base = group_start[None, :] + jnp.cumsum(npad, axis=0) - npad
    tail_start = group_start + group
    is_last = jnp.arange(N_EXPERTS) == N_EXPERTS - 1
    tail_len = jnp.where(is_last, P_PAD - tail_start, group_pad - group)
    tile_start = jnp.arange(N_MOE_TILES, dtype=jnp.int32) * TM_MOE
    tile_expert = jnp.minimum(jnp.sum((tile_start[:, None] >= group_end[None, :]).astype(jnp.int32), axis=1),
                              N_EXPERTS - 1)
    ids = jnp.arange(N_EXPERTS, dtype=jnp.int32)
    nonempty = group_pad > 0
    ordinal = jnp.cumsum(nonempty.astype(jnp.int32)) - 1
    later = jnp.logical_and(nonempty[None, :], ids[None, :] > ids[:, None])
    next_expert = jnp.min(jnp.where(later, ids[None, :], N_EXPERTS), axis=1)
    next_expert = jnp.where(next_expert == N_EXPERTS, -1, next_expert)
    is_tile_expert = tile_expert[:, None] == ids[None, :]
    per_tile = lambda table: jnp.sum(jnp.where(is_tile_expert, table[None, :], 0), axis=1)
    sched = {"expert": tile_expert, "n_valid": (group_end[-1] // TM_MOE).reshape(1),
             "next": per_tile(next_expert), "parity": per_tile(ordinal) % 2,
             "first": tile_start == per_tile(group_start),
             "last": tile_start + TM_MOE == per_tile(group_end)}
    tables = {"base": base.reshape(-1), "offp": offp.reshape(-1), "np": npad.reshape(-1), "total": total,
              "tail": jnp.concatenate([tail_start, tail_len])}
    as_i32 = lambda d: {k: v.astype(jnp.int32) for k, v in d.items()}
    return as_i32(tables), as_i32(sched)


WIN_ROWS = 256


def _win_layout_kernel(wt_ref, o_ref):
    wt = wt_ref[0]
    cols = wt.shape[1]
    kr_tile = jnp.concatenate([jnp.zeros((QK_NOPE, cols), F32), wt[REF_KR:REF_GATE, :],
                               jnp.zeros((HEAD_PAD - QK_DIM, cols), F32)], axis=0)
    conv_rows = [wt[part * D_CONV + c * CONV_CHUNK:part * D_CONV + (c + 1) * CONV_CHUNK, :]
                 for c in range(D_CONV // CONV_CHUNK) for part in range(3)]
    grouped = jnp.concatenate(conv_rows + [wt[REF_GATE:, :], wt[REF_CQ:REF_KR, :], kr_tile], axis=0)
    o_ref[0] = grouped.T.astype(BF16)


def _win_layout(w_in):
    wt = jnp.swapaxes(w_in, 1, 2)
    d_in = wt.shape[1]
    return pl.pallas_call(
        _win_layout_kernel,
        grid=(DEPTH, D_MODEL // WIN_ROWS),
        in_specs=[pl.BlockSpec((1, d_in, WIN_ROWS), lambda l, r: (l, 0, r))],
        out_specs=pl.BlockSpec((1, WIN_ROWS, D_IN_PAD), lambda l, r: (l, r, 0)),
        out_shape=jax.ShapeDtypeStruct((DEPTH, D_MODEL, D_IN_PAD), BF16),
        compiler_params=pltpu.CompilerParams(dimension_semantics=("arbitrary",) * 2, vmem_limit_bytes=VMEM_LIMIT),
        name="win_layout",
    )(wt)


def _layer_weights(w_uq, w_ukv):
    wq = w_uq.T.astype(BF16)

    wkv = w_ukv.reshape(KV_LORA, N_HEADS, QK_NOPE + V_HEAD)
    wuk = jnp.concatenate([wkv[:, :, :QK_NOPE], jnp.zeros((KV_LORA, N_HEADS, HEAD_PAD - QK_NOPE), F32)], axis=2)
    wuk = wuk.reshape(KV_LORA, N_HEADS * HEAD_PAD).astype(BF16)
    wuv = wkv[:, :, QK_NOPE:].reshape(KV_LORA, N_HEADS * V_HEAD).T.astype(BF16)
    return wq, wuk, wuv


def kernel(x, positions, attn_norm, w_in, conv_w, w_conv_out, q_norm, w_uq, kv_norm, w_ukv, w_mla_out, w_o, ffn_norm,
           w_gate, w_up, w_down, router, w_gate_e, w_up_e, w_down_e, final_norm):
    assert x.shape == (BATCH, SEQ, D_MODEL) and positions.shape == (BATCH, SEQ)
    cos_t, sin_t = _rope_tables(positions)
    xt = x.reshape(TOKENS, D_MODEL)
    out = None
    win_all = _win_layout(w_in)
    for l in range(DEPTH):
        wq, wuk, wuv = _layer_weights(w_uq[l], w_ukv[l])
        q, k, v, gc, sg = _front(l, xt, attn_norm[l].reshape(1, D_MODEL), win_all, conv_w[l],
                                 w_conv_out[l].astype(BF16),
                                 q_norm[l].reshape(1, Q_LORA), wq, kv_norm[l].reshape(1, KV_LORA), wuk, wuv,
                                 cos_t, sin_t)
        attn = _attention(q, k, v)
        wmo = w_mla_out[l].astype(BF16)
        wo = w_o[l].astype(BF16)
        fn = ffn_norm[l].reshape(1, D_MODEL)
        if l % 2 == 0:
            j = l // 2
            xt = _merge(attn, gc, sg, xt, wmo, wo, fn,
                        ffn=(w_gate[j].astype(BF16), w_up[j].astype(BF16), w_down[j].astype(BF16)))
        else:
            j = l // 2
            router_pad = jnp.pad(router[j].T, ((0, E_PAD - N_EXPERTS), (0, 0)))
            x2, hn, route_t = _merge(attn, gc, sg, xt, wmo, wo, fn, router_pad)
            tables, sched = _routing_tables(route_t)
            xs = _dispatch(tables, hn, route_t)
            y_sorted = _moe(sched, xs, w_gate_e[j], w_up_e[j], w_down_e[j])
            out = _combine(tables, x2, route_t, y_sorted, final_norm.reshape(1, D_MODEL))
    return out.reshape(BATCH, SEQ, D_MODEL)
```

```python
import math

import jax
import jax.numpy as jnp
from jax import lax
from jax.experimental import pallas as pl
from jax.experimental.pallas import tpu as pltpu

F32 = jnp.float32
BF16 = jnp.bfloat16

D_MODEL = 1024
BATCH = 8
SEQ = 2048
TOKENS = BATCH * SEQ
DEPTH = 2
D_CONV = 512
CONV_WIDTH = 3
N_HEADS = 8
QK_NOPE = 64
QK_ROPE = 32
HALF_ROPE = QK_ROPE // 2
V_HEAD = 64
Q_LORA = 384
KV_LORA = 256
ROPE_THETA = 10000.0
D_FF = 2816
N_EXPERTS = 8
TOP_K = 2
D_FF_EXPERT = 1408
EPS = 1e-6

LANES = 128
SUBLANES = 8
BF16_SUBLANES = 16
MXU_TILE = 256
VMEM_LIMIT = 56 * 1024 * 1024

HEAD_PAD = LANES
QK_DIM = QK_NOPE + QK_ROPE
Q_SCALE = (1.0 / math.sqrt(QK_DIM)) * math.log2(math.e)
NEG_BIG = -1e30

OFF_BCU = 0
OFF_GATE = 3 * D_CONV
OFF_SMALL = OFF_GATE + 2 * D_MODEL
D_IN_PAD = OFF_SMALL + Q_LORA + KV_LORA + LANES
REF_CQ = 3 * D_CONV
REF_KR = REF_CQ + Q_LORA + KV_LORA
REF_GATE = REF_KR + QK_ROPE

CONV_CHUNK = MXU_TILE
CONV_HIST = SUBLANES
assert CONV_WIDTH - 1 <= CONV_HIST
TM_FRONT = 512
TM_MERGE = 1024
TQ = MXU_TILE
TK = 128
ATTN_PAIRS = 2
ATTN_LOOKAHEAD = 3
V_ROWS = V_HEAD + BF16_SUBLANES
FF_CHUNK = MXU_TILE
assert D_FF % FF_CHUNK == 0 and D_CONV % CONV_CHUNK == 0 and TQ % TK == 0

TM_MOE = 512
TT = 256
N_TT = TOKENS // TT
E_PAD = BF16_SUBLANES
XS_WIDTH = D_MODEL + LANES
N_PAIRS = TOKENS * TOP_K
SLOTS = TT * TOP_K + N_EXPERTS * SUBLANES
_MAX_ROWS = N_PAIRS + N_TT * N_EXPERTS * (SUBLANES - 1) + N_EXPERTS * (TM_MOE - SUBLANES)
N_MOE_TILES = -(-_MAX_ROWS // TM_MOE)
P_PAD = N_MOE_TILES * TM_MOE


def _halvings(n):
    return tuple(n >> s for s in range((n // SUBLANES).bit_length()))


SLAB_PIECES = _halvings(TT)
WAIT_PIECES = _halvings(2 * TT)
TAIL_PIECES = _halvings(TM_MOE // 2)
assert SLOTS < 2 * WAIT_PIECES[0]


def _rms(x, g):
    return x * lax.rsqrt(jnp.mean(x * x, axis=-1, keepdims=True) + EPS) * g


def _const_spec(shape):
    nd = len(shape)
    return pl.BlockSpec(shape, lambda *_: (0,) * nd, pipeline_mode=pl.Buffered(1))


def _layer_spec(layer, shape):
    return pl.BlockSpec((1,) + tuple(shape), lambda *_: (layer,) + (0,) * len(shape), pipeline_mode=pl.Buffered(1))


def _trig_kernel(pos_ref, invf_ref, cos_ref, sin_ref):
    ang = invf_ref[...] * pos_ref[...].astype(F32)
    cos_ref[...] = jnp.cos(ang)
    sin_ref[...] = jnp.sin(ang)


def _rope_tables(positions):
    inv_freq = ROPE_THETA ** (-jnp.arange(0, QK_ROPE, 2, dtype=F32) / QK_ROPE)
    return pl.pallas_call(
        _trig_kernel,
        out_shape=(jax.ShapeDtypeStruct((HALF_ROPE, TOKENS), F32),) * 2,
        name="rope_trig",
    )(positions.reshape(1, TOKENS), inv_freq.reshape(HALF_ROPE, 1))


def _dot_nt(a, b):
    return lax.dot_general(a, b, (((1,), (1,)), ((), ())), preferred_element_type=F32)


def _front_kernel(x_ref, an_ref, win_ref, cw_ref, wco_ref, qn_ref, wq_ref, kvn_ref, wuk_ref, wuv_ref,
                  cost_ref, sint_ref, q_out, k_out, v_out, gc_out, sg_out, cu_scr):
    tm = x_ref.shape[0]
    i = pl.program_id(0)
    xn = _rms(x_ref[...], an_ref[...]).astype(BF16)

    def proj(a, b):
        return jnp.dot(xn, win_ref[0, :, a:b], preferred_element_type=F32)

    small = proj(OFF_SMALL, D_IN_PAD)

    cq = cost_ref[...] * Q_SCALE
    sq = sint_ref[...] * Q_SCALE
    cqn = _rms(small[:, :Q_LORA], qn_ref[...]).astype(BF16)
    qt = _dot_nt(wq_ref[0], cqn)
    zero_pad = jnp.zeros((HEAD_PAD - QK_DIM, tm), BF16)
    for h in range(N_HEADS):
        src = h * QK_DIM
        dst = h * HEAD_PAD
        x1 = qt[src + QK_NOPE:src + QK_NOPE + HALF_ROPE, :]
        x2 = qt[src + QK_NOPE + HALF_ROPE:src + QK_DIM, :]
        q_out[dst:dst + QK_NOPE, :] = (qt[src:src + QK_NOPE, :] * Q_SCALE).astype(BF16)
        q_out[dst + QK_NOPE:dst + QK_NOPE + HALF_ROPE, :] = (x1 * cq - x2 * sq).astype(BF16)
        q_out[dst + QK_NOPE + HALF_ROPE:dst + QK_DIM, :] = (x2 * cq + x1 * sq).astype(BF16)
        q_out[dst + QK_DIM:dst + HEAD_PAD, :] = zero_pad

    cos_t = cost_ref[...]
    sin_t = sint_ref[...]
    z_nope = jnp.zeros((QK_NOPE, tm), F32)
    z_rope = jnp.zeros((HALF_ROPE, tm), F32)
    z_pad = jnp.zeros((HEAD_PAD - QK_DIM, tm), F32)
    ck = jnp.concatenate([z_nope, cos_t, cos_t, z_pad], axis=0).T
    sk_x1 = jnp.concatenate([z_nope, -sin_t, z_rope, z_pad], axis=0).T
    sk_x2 = jnp.concatenate([z_nope, z_rope, sin_t, z_pad], axis=0).T
    kr = small[:, Q_LORA + KV_LORA:]
    kpe = (kr * ck + pltpu.roll(kr, HEAD_PAD - HALF_ROPE, axis=1) * sk_x1 + pltpu.roll(kr, HALF_ROPE, axis=1) * sk_x2)
    ckvn = _rms(small[:, Q_LORA:Q_LORA + KV_LORA], kvn_ref[...]).astype(BF16)
    kn = jnp.dot(ckvn, wuk_ref[0], preferred_element_type=F32)
    for h in range(N_HEADS):
        blk = slice(h * HEAD_PAD, (h + 1) * HEAD_PAD)
        k_out[:, blk] = (kn[:, blk] + kpe).astype(BF16)
    vt = _dot_nt(wuv_ref[0], ckvn).astype(BF16)
    sub = lax.broadcasted_iota(jnp.int32, (V_ROWS - V_HEAD, TK), 0)
    ones_rows = jnp.where(sub == 0, 1.0, 0.0).astype(BF16)
    for c in range(tm // TK):
        for h in range(N_HEADS):
            v_out[c, h * V_ROWS:h * V_ROWS + V_HEAD, :] = vt[h * V_HEAD:(h + 1) * V_HEAD, c * TK:(c + 1) * TK]
            v_out[c, h * V_ROWS + V_HEAD:(h + 1) * V_ROWS, :] = ones_rows

    @pl.when(i % (SEQ // tm) == 0)
    def _():
        cu_scr[0:CONV_HIST, :] = jnp.zeros((CONV_HIST, D_CONV), F32)

    cw = cw_ref[...]
    y_conv = None
    for c in range(D_CONV // CONV_CHUNK):
        ch = slice(c * CONV_CHUNK, (c + 1) * CONV_CHUNK)
        bcu = proj(OFF_BCU + 3 * c * CONV_CHUNK, OFF_BCU + 3 * (c + 1) * CONV_CHUNK)
        b_g = bcu[:, :CONV_CHUNK]
        cu = bcu[:, CONV_CHUNK:2 * CONV_CHUNK] * bcu[:, 2 * CONV_CHUNK:]
        cu_scr[CONV_HIST:CONV_HIST + tm, ch] = cu
        conv = cu * cw[CONV_WIDTH - 1:CONV_WIDTH, ch]
        for tap in range(CONV_WIDTH - 1):
            back = CONV_WIDTH - 1 - tap
            conv = conv + cu_scr[CONV_HIST - back:CONV_HIST - back + tm, ch] * cw[tap:tap + 1, ch]
        cu_scr[0:CONV_HIST, ch] = cu[tm - CONV_HIST:, :]
        part = jnp.dot((b_g * conv).astype(BF16), wco_ref[0, ch, :], preferred_element_type=F32)
        y_conv = part if y_conv is None else y_conv + part

    gates = proj(OFF_GATE, OFF_SMALL)
    gc_out[...] = (jax.nn.sigmoid(gates[:, :D_MODEL]) * y_conv).astype(BF16)
    sg_out[...] = jax.nn.sigmoid(gates[:, D_MODEL:]).astype(BF16)


def _front(layer, x, an, win_all, cw, wco, qn, wq, kvn, wuk, wuv, cos_t, sin_t):
    tm = TM_FRONT
    row = lambda n: pl.BlockSpec((tm, n), lambda i: (i, 0))
    col = lambda n: pl.BlockSpec((n, tm), lambda i: (0, i))
    return pl.pallas_call(
        _front_kernel,
        grid=(TOKENS // tm,),
        in_specs=[row(D_MODEL), _const_spec((1, D_MODEL)), _layer_spec(layer, (D_MODEL, D_IN_PAD)),
                  _const_spec((CONV_WIDTH, D_CONV)), _layer_spec(layer, (D_CONV, D_MODEL)),
                  _const_spec((1, Q_LORA)), _layer_spec(layer, (N_HEADS * QK_DIM, Q_LORA)),
                  _const_spec((1, KV_LORA)), _layer_spec(layer, (KV_LORA, N_HEADS * HEAD_PAD)),
                  _layer_spec(layer, (N_HEADS * V_HEAD, KV_LORA)),
                  col(HALF_ROPE), col(HALF_ROPE)],
        out_specs=[col(N_HEADS * HEAD_PAD), row(N_HEADS * HEAD_PAD),
                   pl.BlockSpec((tm // TK, N_HEADS * V_ROWS, TK), lambda i: (i, 0, 0)),
                   row(D_MODEL), row(D_MODEL)],
        out_shape=[jax.ShapeDtypeStruct((N_HEADS * HEAD_PAD, TOKENS), BF16),
                   jax.ShapeDtypeStruct((TOKENS, N_HEADS * HEAD_PAD), BF16),
                   jax.ShapeDtypeStruct((TOKENS // TK, N_HEADS * V_ROWS, TK), BF16),
                   jax.ShapeDtypeStruct((TOKENS, D_MODEL), BF16),
                   jax.ShapeDtypeStruct((TOKENS, D_MODEL), BF16)],
        scratch_shapes=[pltpu.VMEM((tm + CONV_HIST, D_CONV), F32)],
        compiler_params=pltpu.CompilerParams(dimension_semantics=("arbitrary",), vmem_limit_bytes=VMEM_LIMIT),
        name="front",
    )(x, an, win_all, cw, wco, qn, wq, kvn, wuk, wuv, cos_t, sin_t)


def _attn_items():
    items = []
    for pair in range(ATTN_PAIRS):
        for i in range(SEQ // TQ):
            for ks in range(0, (i + 1) * TQ, TK):
                items.append((pair, i, ks, TK, ks + TK == (i + 1) * TQ))
    return items


def _attn_kernel(qt_ref, k_ref, vt_ref, o_ref):
    row = lax.broadcasted_iota(jnp.int32, (TK, TQ), 0)
    col = lax.broadcasted_iota(jnp.int32, (TK, TQ), 1)

    def scores(item):
        pair, i, ks, n, _ = item
        sts = []
        for h in range(2 * pair, 2 * pair + 2):
            hs = slice(h * HEAD_PAD, (h + 1) * HEAD_PAD)
            st = jnp.dot(k_ref[ks:ks + n, hs], qt_ref[hs, i * TQ:(i + 1) * TQ], preferred_element_type=F32)
            if ks + n > i * TQ:
                st = jnp.where(row + (ks - i * TQ) <= col, st, NEG_BIG)
            sts.append(st)
        return sts

    def consume(item, sts, carry):
        pair, i, ks, n, _ = item
        stats = []
        for j in range(2):
            m, _ = carry[j]
            m_new = jnp.maximum(m, jnp.max(sts[j], axis=0, keepdims=True))
            stats.append((m_new, jnp.exp2(m - m_new), jnp.exp2(sts[j] - m_new).astype(BF16)))
        new = []
        for j in range(2):
            h = 2 * pair + j
            m_new, alpha, p = stats[j]
            vt = vt_ref[ks // TK, h * V_ROWS:(h + 1) * V_ROWS, ks % TK:ks % TK + n]
            new.append((m_new, alpha * carry[j][1] + jnp.dot(vt, p, preferred_element_type=F32)))
        return new

    items = _attn_items()
    init = [(jnp.full((1, TQ), NEG_BIG, F32), jnp.zeros((V_ROWS, TQ), F32)) for _ in range(2)]
    pending = [scores(it) for it in items[:ATTN_LOOKAHEAD]]
    carry = [init] * ATTN_PAIRS
    for t, item in enumerate(items):
        if t + ATTN_LOOKAHEAD < len(items):
            pending.append(scores(items[t + ATTN_LOOKAHEAD]))
        pair, i = item[0], item[1]
        carry[pair] = consume(item, pending.pop(0), carry[pair])
        if item[4]:
            out_t = jnp.concatenate([acc[0:V_HEAD] / acc[V_HEAD:V_HEAD + 1] for (_, acc) in carry[pair]], axis=0)
            o_ref[i * TQ:(i + 1) * TQ, pair * 2 * V_HEAD:(pair + 1) * 2 * V_HEAD] = out_t.T.astype(BF16)
            carry[pair] = init


def _attention(qt, k, vt):
    heads = 2 * ATTN_PAIRS
    return pl.pallas_call(
        _attn_kernel,
        grid=(BATCH, N_HEADS // heads),
        in_specs=[pl.BlockSpec((heads * HEAD_PAD, SEQ), lambda b, hp: (hp, b)),
                  pl.BlockSpec((SEQ, heads * HEAD_PAD), lambda b, hp: (b, hp)),
                  pl.BlockSpec((SEQ // TK, heads * V_ROWS, TK), lambda b, hp: (b, hp, 0))],
        out_specs=pl.BlockSpec((SEQ, heads * V_HEAD), lambda b, hp: (b, hp)),
        out_shape=jax.ShapeDtypeStruct((TOKENS, N_HEADS * V_HEAD), BF16),
        compiler_params=pltpu.CompilerParams(dimension_semantics=("arbitrary",) * 2, vmem_limit_bytes=VMEM_LIMIT),
        name="attention",
    )(qt, k, vt)


def _merge_core(attn_ref, gc_ref, sg_ref, x_ref, wmo_ref, wo_ref, fn_ref):
    y_mla = jnp.dot(attn_ref[...], wmo_ref[0], preferred_element_type=F32)
    merged = gc_ref[...].astype(F32) + sg_ref[...].astype(F32) * y_mla
    x2 = x_ref[...] + jnp.dot(merged.astype(BF16), wo_ref[0], preferred_element_type=F32)
    return x2, _rms(x2, fn_ref[...])


def _merge_ffn_kernel(attn_ref, gc_ref, sg_ref, x_ref, wmo_ref, wo_ref, fn_ref, wg_ref, wu_ref, wd_ref, o_ref):
    x2, hn = _merge_core(attn_ref, gc_ref, sg_ref, x_ref, wmo_ref, wo_ref, fn_ref)
    hn = hn.astype(BF16)
    acc = x2
    for c in range(D_FF // FF_CHUNK):
        cs = slice(c * FF_CHUNK, (c + 1) * FF_CHUNK)
        g = jnp.dot(hn, wg_ref[:, cs], preferred_element_type=F32)
        u = jnp.dot(hn, wu_ref[:, cs], preferred_element_type=F32)
        a = (jax.nn.silu(g) * u).astype(BF16)
        acc = acc + jnp.dot(a, wd_ref[cs, :], preferred_element_type=F32)
    o_ref[...] = acc


def _merge_route_kernel(attn_ref, gc_ref, sg_ref, x_ref, wmo_ref, wo_ref, fn_ref, router_ref,
                        x2_out, hn_out, route_t_out):
    x2, hn = _merge_core(attn_ref, gc_ref, sg_ref, x_ref, wmo_ref, wo_ref, fn_ref)
    x2_out[...] = x2
    hb = hn.astype(BF16)
    hn_out[...] = hb
    hl = (hn - hb.astype(F32)).astype(BF16)
    r = router_ref[...]
    rh = r.astype(BF16)
    rl = (r - rh.astype(F32)).astype(BF16)
    logits = _dot_nt(rh, hb) + _dot_nt(rh, hl) + _dot_nt(rl, hb)
    erow = lax.broadcasted_iota(jnp.int32, logits.shape, 0)
    logits = jnp.where(erow < N_EXPERTS, logits, -jnp.inf)
    m1 = jnp.max(logits, axis=0, keepdims=True)
    i1 = jnp.min(jnp.where(logits == m1, erow, E_PAD), axis=0, keepdims=True)
    rest = jnp.where(erow == i1, -jnp.inf, logits)
    m2 = jnp.max(rest, axis=0, keepdims=True)
    i2 = jnp.min(jnp.where(rest == m2, erow, E_PAD), axis=0, keepdims=True)
    t = jnp.exp(m2 - m1)
    w1 = 1.0 / (1.0 + t)
    w2 = t * w1
    route_t = jnp.where(erow == 0, i1.astype(F32),
                        jnp.where(erow == 1, i2.astype(F32), jnp.where(erow == 2, w1, jnp.where(erow == 3, w2, 0.0))))
    route_t_out[...] = route_t[0:SUBLANES, :]


def _merge(layer, attn, gc, sg, x, wmo, wo, fn, router=None, ffn=None):
    tm = TM_MERGE
    row = lambda n: pl.BlockSpec((tm, n), lambda i: (i, 0))
    in_specs = [row(N_HEADS * V_HEAD), row(D_MODEL), row(D_MODEL), row(D_MODEL),
                _layer_spec(layer, (N_HEADS * V_HEAD, D_MODEL)), _layer_spec(layer, (D_MODEL, D_MODEL)),
                _const_spec((1, D_MODEL))]
    args = [attn, gc, sg, x, wmo, wo, fn]
    if router is None:
        kern = _merge_ffn_kernel
        in_specs += [_const_spec((D_MODEL, D_FF)), _const_spec((D_MODEL, D_FF)), _const_spec((D_FF, D_MODEL))]
        args += list(ffn)
        out_specs = row(D_MODEL)
        out_shape = jax.ShapeDtypeStruct((TOKENS, D_MODEL), F32)
        name = "merge_ffn"
    else:
        kern = _merge_route_kernel
        in_specs.append(_const_spec((E_PAD, D_MODEL)))
        args.append(router)
        out_specs = [row(D_MODEL), row(D_MODEL), pl.BlockSpec((SUBLANES, tm), lambda i: (0, i))]
        out_shape = [jax.ShapeDtypeStruct((TOKENS, D_MODEL), F32), jax.ShapeDtypeStruct((TOKENS, D_MODEL), BF16),
                     jax.ShapeDtypeStruct((SUBLANES, TOKENS), F32)]
        name = "merge_route"
    return pl.pallas_call(
        kern, grid=(TOKENS // tm,), in_specs=in_specs, out_specs=out_specs, out_shape=out_shape,
        compiler_params=pltpu.CompilerParams(dimension_semantics=("arbitrary",), vmem_limit_bytes=VMEM_LIMIT),
        name=name,
    )(*args)


def _pieces(n, sizes):
    return [((n & z) != 0, n & ~(2 * z - 1), z) for z in sizes]


def _aligned(rows, z):
    return pl.ds(rows if isinstance(rows, int) else pl.multiple_of(rows, SUBLANES), z)


def _start_slab_copies(tile, base_ref, offp_ref, np_ref, make_copy):
    for e in range(N_EXPERTS):
        n = np_ref[tile * N_EXPERTS + e]
        src = offp_ref[tile * N_EXPERTS + e]
        dst = base_ref[tile * N_EXPERTS + e]
        for pred, off, z in _pieces(n, SLAB_PIECES):
            @pl.when(pred)
            def _(off=off, z=z):
                make_copy(src + off, dst + off, z).start()


def _wait_rows(total, make_copy):
    for pred, _, z in _pieces(total, WAIT_PIECES):
        @pl.when(pred)
        def _(z=z):
            make_copy(0, 0, z).wait()


def _tile_placement(rt_ref, offp_ref, i):
    e0 = rt_ref[0:1, :].astype(jnp.int32)
    e1 = rt_ref[1:2, :].astype(jnp.int32)
    eid = lax.broadcasted_iota(jnp.int32, (E_PAD, TT), 0)
    sel = jnp.where(e0 == eid, 1.0, jnp.where(e1 == eid, 1.0, 0.0)).astype(BF16)
    r = lax.broadcasted_iota(jnp.int32, (TT, TT), 0)
    c = lax.broadcasted_iota(jnp.int32, (TT, TT), 1)
    earlier = jnp.where(r < c, 1.0, 0.0).astype(BF16)
    rank = jnp.dot(sel, earlier, preferred_element_type=F32)
    ecol = lax.broadcasted_iota(jnp.int32, (E_PAD, 1), 0)
    offp = jnp.zeros((E_PAD, 1), F32)
    for e in range(N_EXPERTS):
        offp = jnp.where(ecol == e, offp_ref[i * N_EXPERTS + e].astype(F32), offp)
    slot_all = rank + offp
    slot0 = jnp.sum(jnp.where(e0 == eid, slot_all, 0.0), axis=0, keepdims=True)
    slot1 = jnp.sum(jnp.where(e1 == eid, slot_all, 0.0), axis=0, keepdims=True)
    sidx = lax.broadcasted_iota(jnp.int32, (SLOTS, TT), 0).astype(F32)
    place = jnp.where(sidx == slot0, 1.0, jnp.where(sidx == slot1, 1.0, 0.0)).astype(BF16)
    return sidx, slot0, slot1, place


def _dispatch_kernel(base_ref, offp_ref, np_ref, tot_ref, tail_ref, hn_ref, rt_ref, xs_hbm, xs_scr, zbuf, sem, zsem):
    i = pl.program_id(0)
    slot = i % 2

    def copy(buf_slot):
        return lambda s, d, z: pltpu.make_async_copy(xs_scr.at[buf_slot, _aligned(s, z), :],
                                                     xs_hbm.at[_aligned(d, z), :], sem.at[buf_slot])

    def zero_copy(d, z):
        return pltpu.make_async_copy(zbuf.at[pl.ds(0, z), :], xs_hbm.at[_aligned(d, z), :], zsem)

    @pl.when(i == 0)
    def _():
        zbuf[...] = jnp.zeros_like(zbuf)
        for phase in ("start", "wait"):
            for e in range(N_EXPERTS):
                t0 = tail_ref[e]
                n = tail_ref[N_EXPERTS + e]
                full = lax.shift_right_logical(n, jnp.int32(TM_MOE.bit_length() - 1))

                def chunk(c, carry, t0=t0, phase=phase):
                    cp = zero_copy(t0 + c * TM_MOE, TM_MOE)
                    cp.start() if phase == "start" else cp.wait()
                    return carry

                lax.fori_loop(0, full, chunk, 0)
                rest = n & (TM_MOE - 1)
                for pred, off, z in _pieces(rest, TAIL_PIECES):
                    @pl.when(pred)
                    def _(off=off, z=z, t0=t0, full=full, phase=phase):
                        cp = zero_copy(t0 + full * TM_MOE + off, z)
                        cp.start() if phase == "start" else cp.wait()

    sidx, slot0, slot1, place = _tile_placement(rt_ref, offp_ref, i)
    xs_scr[slot, :, 0:D_MODEL] = jnp.dot(place, hn_ref[...], preferred_element_type=F32)
    w_slot = jnp.sum(jnp.where(sidx == slot0, rt_ref[2:3, :], jnp.where(sidx == slot1, rt_ref[3:4, :], 0.0)),
                     axis=1, keepdims=True)
    xs_scr[slot, :, D_MODEL:] = jnp.broadcast_to(w_slot, (SLOTS, LANES))

    _start_slab_copies(i, base_ref, offp_ref, np_ref, copy(slot))

    @pl.when(i > 0)
    def _():
        _wait_rows(tot_ref[i - 1], copy(1 - slot))

    @pl.when(i == pl.num_programs(0) - 1)
    def _():
        _wait_rows(tot_ref[i], copy(slot))


def _dispatch(tables, hn, route_t):
    grid_spec = pltpu.PrefetchScalarGridSpec(
        num_scalar_prefetch=5,
        grid=(N_TT,),
        in_specs=[pl.BlockSpec((TT, D_MODEL), lambda i, *_: (i, 0)),
                  pl.BlockSpec((SUBLANES, TT), lambda i, *_: (0, i))],
        out_specs=pl.BlockSpec(memory_space=pl.ANY),
        scratch_shapes=[pltpu.VMEM((2, SLOTS, XS_WIDTH), F32), pltpu.VMEM((TM_MOE, XS_WIDTH), F32),
                        pltpu.SemaphoreType.DMA((2,)), pltpu.SemaphoreType.DMA],
    )
    return pl.pallas_call(
        _dispatch_kernel,
        grid_spec=grid_spec,
        out_shape=jax.ShapeDtypeStruct((P_PAD, XS_WIDTH), F32),
        compiler_params=pltpu.CompilerParams(dimension_semantics=("arbitrary",), vmem_limit_bytes=VMEM_LIMIT),
        name="moe_dispatch",
    )(tables["base"], tables["offp"], tables["np"], tables["total"], tables["tail"], hn, route_t)


def _moe_kernel(te_ref, nv_ref, nxt_ref, par_ref, first_ref, last_ref, x_ref, wg_hbm, wu_hbm, wd_hbm, y_ref,
                stage_g, stage_u, stage_d, wgu_scr, wd_scr, sem):
    i = pl.program_id(0)
    valid = i < nv_ref[0]
    nxt = nxt_ref[i]

    def fetch(e):
        return [pltpu.make_async_copy(wg_hbm.at[e], stage_g, sem.at[0]),
                pltpu.make_async_copy(wu_hbm.at[e], stage_u, sem.at[1]),
                pltpu.make_async_copy(wd_hbm.at[e], stage_d, sem.at[2])]

    def round_into(s):
        wgu_scr[s, :, 0:D_FF_EXPERT] = stage_g[...].astype(BF16)
        wgu_scr[s, :, D_FF_EXPERT:] = stage_u[...].astype(BF16)
        wd_scr[s] = stage_d[...].astype(BF16)

    @pl.when(i == 0)
    def _():
        for cp in fetch(te_ref[0]):
            cp.start()
        for cp in fetch(te_ref[0]):
            cp.wait()
        round_into(par_ref[0])

    @pl.when(jnp.logical_and(valid, jnp.logical_and(first_ref[i] == 1, nxt >= 0)))
    def _():
        for cp in fetch(nxt):
            cp.start()

    @pl.when(valid)
    def _():
        s = par_ref[i]
        x = x_ref[:, 0:D_MODEL].astype(BF16)
        h = jnp.dot(x, wgu_scr[s], preferred_element_type=F32)
        a = (jax.nn.silu(h[:, :D_FF_EXPERT]) * h[:, D_FF_EXPERT:]).astype(BF16)
        y_ref[...] = jnp.dot(a, wd_scr[s], preferred_element_type=F32) * x_ref[:, D_MODEL:D_MODEL + 1]

    @pl.when(jnp.logical_and(valid, jnp.logical_and(last_ref[i] == 1, nxt >= 0)))
    def _():
        for cp in fetch(nxt):
            cp.wait()
        round_into(1 - par_ref[i])

    @pl.when(jnp.logical_not(valid))
    def _():
        y_ref[...] = jnp.zeros_like(y_ref)


def _moe(sched, xs, wg, wu, wd):
    grid_spec = pltpu.PrefetchScalarGridSpec(
        num_scalar_prefetch=6,
        grid=(N_MOE_TILES,),
        in_specs=[pl.BlockSpec((TM_MOE, XS_WIDTH), lambda i, *_: (i, 0)),
                  pl.BlockSpec(memory_space=pl.ANY), pl.BlockSpec(memory_space=pl.ANY),
                  pl.BlockSpec(memory_space=pl.ANY)],
        out_specs=pl.BlockSpec((TM_MOE, D_MODEL), lambda i, *_: (i, 0)),
        scratch_shapes=[pltpu.VMEM((D_MODEL, D_FF_EXPERT), F32), pltpu.VMEM((D_MODEL, D_FF_EXPERT), F32),
                        pltpu.VMEM((D_FF_EXPERT, D_MODEL), F32),
                        pltpu.VMEM((2, D_MODEL, 2 * D_FF_EXPERT), BF16), pltpu.VMEM((2, D_FF_EXPERT, D_MODEL), BF16),
                        pltpu.SemaphoreType.DMA((3,))],
    )
    return pl.pallas_call(
        _moe_kernel,
        grid_spec=grid_spec,
        out_shape=jax.ShapeDtypeStruct((P_PAD, D_MODEL), F32),
        compiler_params=pltpu.CompilerParams(dimension_semantics=("arbitrary",), vmem_limit_bytes=VMEM_LIMIT),
        name="moe_experts",
    )(sched["expert"], sched["n_valid"], sched["next"], sched["parity"], sched["first"], sched["last"],
      xs, wg, wu, wd)


def _combine_kernel(base_ref, offp_ref, np_ref, tot_ref, x_ref, route_ref, y_hbm, fn_ref, o_ref, ybuf, sem):
    i = pl.program_id(0)
    slot = i % 2

    def copy(buf_slot):
        return lambda s, d, z: pltpu.make_async_copy(y_hbm.at[_aligned(d, z), :],
                                                     ybuf.at[buf_slot, _aligned(s, z), :], sem.at[buf_slot])

    @pl.when(i == 0)
    def _():
        ybuf[...] = jnp.zeros_like(ybuf)
        _start_slab_copies(0, base_ref, offp_ref, np_ref, copy(0))

    @pl.when(i + 1 < pl.num_programs(0))
    def _():
        _start_slab_copies(i + 1, base_ref, offp_ref, np_ref, copy(1 - slot))

    _, _, _, place = _tile_placement(route_ref, offp_ref, i)

    _wait_rows(tot_ref[i], copy(slot))
    y = ybuf[slot]
    yh = y.astype(BF16)
    yl = (y - yh.astype(F32)).astype(BF16)
    tn = (((0,), (0,)), ((), ()))
    moe = (lax.dot_general(place, yh, tn, preferred_element_type=F32)
           + lax.dot_general(place, yl, tn, preferred_element_type=F32))
    o_ref[...] = _rms(x_ref[...] + moe, fn_ref[...])


def _combine(tables, x2, route_t, y_sorted, fn):
    grid_spec = pltpu.PrefetchScalarGridSpec(
        num_scalar_prefetch=4,
        grid=(N_TT,),
        in_specs=[pl.BlockSpec((TT, D_MODEL), lambda i, *_: (i, 0)),
                  pl.BlockSpec((SUBLANES, TT), lambda i, *_: (0, i)),
                  pl.BlockSpec(memory_space=pl.ANY),
                  pl.BlockSpec((1, D_MODEL), lambda i, *_: (0, 0))],
        out_specs=pl.BlockSpec((TT, D_MODEL), lambda i, *_: (i, 0)),
        scratch_shapes=[pltpu.VMEM((2, SLOTS, D_MODEL), F32), pltpu.SemaphoreType.DMA((2,))],
    )
    return pl.pallas_call(
        _combine_kernel,
        grid_spec=grid_spec,
        out_shape=jax.ShapeDtypeStruct((TOKENS, D_MODEL), F32),
        compiler_params=pltpu.CompilerParams(dimension_semantics=("arbitrary",), vmem_limit_bytes=VMEM_LIMIT),
        name="moe_combine",
    )(tables["base"], tables["offp"], tables["np"], tables["total"], x2, route_t, y_sorted, fn)


def _routing_tables(route_t):
    experts = route_t[:TOP_K, :].astype(jnp.int32)
    eid = jnp.arange(N_EXPERTS, dtype=jnp.int32)[:, None, None]
    chosen = jnp.sum((experts[None, :, :] == eid).astype(jnp.int32), axis=1)
    cnt = jnp.sum(chosen.reshape(N_EXPERTS, N_TT, TT), axis=2).T
    npad = (cnt + SUBLANES - 1) // SUBLANES * SUBLANES
    offp = jnp.cumsum(npad, axis=1) - npad
    total = jnp.sum(npad, axis=1)
    group = jnp.sum(npad, axis=0)
    group_pad = (group + TM_MOE - 1) // TM_MOE * TM_MOE
    group_end = jnp.cumsum(group_pad)
    group_start = group_end - group_pad
    base = group_start[None, :] + jnp.cumsum(npad, axis=0) - npad
    tail_start = group_start + group
    is_last = jnp.arange(N_EXPERTS) == N_EXPERTS - 1
    tail_len = jnp.where(is_last, P_PAD - tail_start, group_pad - group)
    tile_start = jnp.arange(N_MOE_TILES, dtype=jnp.int32) * TM_MOE
    tile_expert = jnp.minimum(jnp.sum((tile_start[:, None] >= group_end[None, :]).astype(jnp.int32), axis=1),
                              N_EXPERTS - 1)
    ids = jnp.arange(N_EXPERTS, dtype=jnp.int32)
    nonempty = group_pad > 0
    ordinal = jnp.cumsum(nonempty.astype(jnp.int32)) - 1
    later = jnp.logical_and(nonempty[None, :], ids[None, :] > ids[:, None])
    next_expert = jnp.min(jnp.where(later, ids[None, :], N_EXPERTS), axis=1)
    next_expert = jnp.where(next_expert == N_EXPERTS, -1, next_expert)
    is_tile_expert = tile_expert[:, None] == ids[None, :]
    per_tile = lambda table: jnp.sum(jnp.where(is_tile_expert, table[None, :], 0), axis=1)
    sched = {"expert": tile_expert, "n_valid": (group_end[-1] // TM_MOE).reshape(1),
             "next": per_tile(next_expert), "parity": per_tile(ordinal) % 2,
             "first": tile_start == per_tile(group_start),
             "last": tile_start + TM_MOE == per_tile(group_end)}
    tables = {"base": base.reshape(-1), "offp": offp.reshape(-1), "np": npad.reshape(-1), "total": total,
              "tail": jnp.concatenate([tail_start, tail_len])}
    as_i32 = lambda d: {k: v.astype(jnp.int32) for k, v in d.items()}
    return as_i32(tables), as_i32(sched)


WIN_ROWS = 256


def _win_layout_kernel(wt_ref, o_ref):
    wt = wt_ref[0]
    cols = wt.shape[1]
    kr_tile = jnp.concatenate([jnp.zeros((QK_NOPE, cols), F32), wt[REF_KR:REF_GATE, :],
                               jnp.zeros((HEAD_PAD - QK_DIM, cols), F32)], axis=0)
    conv_rows = [wt[part * D_CONV + c * CONV_CHUNK:part * D_CONV + (c + 1) * CONV_CHUNK, :]
                 for c in range(D_CONV // CONV_CHUNK) for part in range(3)]
    grouped = jnp.concatenate(conv_rows + [wt[REF_GATE:, :], wt[REF_CQ:REF_KR, :], kr_tile], axis=0)
    o_ref[0] = grouped.T.astype(BF16)


def _win_layout(w_in):
    wt = jnp.swapaxes(w_in, 1, 2)
    d_in = wt.shape[1]
    return pl.pallas_call(
        _win_layout_kernel,
        grid=(DEPTH, D_MODEL // WIN_ROWS),
        in_specs=[pl.BlockSpec((1, d_in, WIN_ROWS), lambda l, r: (l, 0, r))],
        out_specs=pl.BlockSpec((1, WIN_ROWS, D_IN_PAD), lambda l, r: (l, r, 0)),
        out_shape=jax.ShapeDtypeStruct((DEPTH, D_MODEL, D_IN_PAD), BF16),
        compiler_params=pltpu.CompilerParams(dimension_semantics=("arbitrary",) * 2, vmem_limit_bytes=VMEM_LIMIT),
        name="win_layout",
    )(wt)


def _small_layout_kernel(wuq_ref, wukv_ref, wco_ref, wmo_ref, wo_ref,
                         wq_out, wuk_out, wuv_out, wco_out, wmo_out, wo_out):
    wq_out[0] = wuq_ref[0].T.astype(BF16)
    wkv = wukv_ref[0]
    zeros = jnp.zeros((KV_LORA, HEAD_PAD - QK_NOPE), BF16)
    values = []
    for h in range(N_HEADS):
        c0 = h * (QK_NOPE + V_HEAD)
        wuk_out[0, :, h * HEAD_PAD:h * HEAD_PAD + QK_NOPE] = wkv[:, c0:c0 + QK_NOPE].astype(BF16)
        wuk_out[0, :, h * HEAD_PAD + QK_NOPE:(h + 1) * HEAD_PAD] = zeros
        values.append(wkv[:, c0 + QK_NOPE:c0 + QK_NOPE + V_HEAD])
    wuv_out[0] = jnp.concatenate(values, axis=1).T.astype(BF16)
    wco_out[0] = wco_ref[0].astype(BF16)
    wmo_out[0] = wmo_ref[0].astype(BF16)
    wo_out[0] = wo_ref[0].astype(BF16)


def _small_layout(w_uq, w_ukv, w_conv_out, w_mla_out, w_o):
    layer = lambda *shape: pl.BlockSpec((1,) + shape, lambda l: (l, 0, 0))
    shapes = [(N_HEADS * QK_DIM, Q_LORA), (KV_LORA, N_HEADS * HEAD_PAD), (N_HEADS * V_HEAD, KV_LORA),
              (D_CONV, D_MODEL), (N_HEADS * V_HEAD, D_MODEL), (D_MODEL, D_MODEL)]
    return pl.pallas_call(
        _small_layout_kernel,
        grid=(DEPTH,),
        in_specs=[layer(*w.shape[1:]) for w in (w_uq, w_ukv, w_conv_out, w_mla_out, w_o)],
        out_specs=[layer(*s) for s in shapes],
        out_shape=[jax.ShapeDtypeStruct((DEPTH,) + s, BF16) for s in shapes],
        compiler_params=pltpu.CompilerParams(dimension_semantics=("arbitrary",), vmem_limit_bytes=VMEM_LIMIT),
        name="small_layout",
    )(w_uq, w_ukv, w_conv_out, w_mla_out, w_o)


def kernel(x, positions, attn_norm, w_in, conv_w, w_conv_out, q_norm, w_uq, kv_norm, w_ukv, w_mla_out, w_o, ffn_norm,
           w_gate, w_up, w_down, router, w_gate_e, w_up_e, w_down_e, final_norm):
    assert x.shape == (BATCH, SEQ, D_MODEL) and positions.shape == (BATCH, SEQ)
    cos_t, sin_t = _rope_tables(positions)
    xt = x.reshape(TOKENS, D_MODEL)
    out = None
    win = _win_layout(w_in)
    wq, wuk, wuv, wco, wmo, wo = _small_layout(w_uq, w_ukv, w_conv_out, w_mla_out, w_o)
    for l in range(DEPTH):
        q, k, v, gc, sg = _front(l, xt, attn_norm[l].reshape(1, D_MODEL), win, conv_w[l], wco,
                                 q_norm[l].reshape(1, Q_LORA), wq, kv_norm[l].reshape(1, KV_LORA), wuk, wuv,
                                 cos_t, sin_t)
        attn = _attention(q, k, v)
        fn = ffn_norm[l].reshape(1, D_MODEL)
        if l % 2 == 0:
            j = l // 2
            xt = _merge(l, attn, gc, sg, xt, wmo, wo, fn,
                        ffn=(w_gate[j].astype(BF16), w_up[j].astype(BF16), w_down[j].astype(BF16)))
        else:
            j = l // 2
            router_pad = jnp.pad(router[j].T, ((0, E_PAD - N_EXPERTS), (0, 0)))
            x2, hn, route_t = _merge(l, attn, gc, sg, xt, wmo, wo, fn, router_pad)
            tables, sched = _routing_tables(route_t)
            xs = _dispatch(tables, hn, route_t)
            y_sorted = _moe(sched, xs, w_gate_e[j], w_up_e[j], w_down_e[j])
            out = _combine(tables, x2, route_t, y_sorted, final_norm.reshape(1, D_MODEL))
    return out.reshape(BATCH, SEQ, D_MODEL)
```

```python
import math

import jax
import jax.numpy as jnp
from jax import lax
from jax.experimental import pallas as pl
from jax.experimental.pallas import tpu as pltpu

F32 = jnp.float32
BF16 = jnp.bfloat16

D_MODEL = 1024
BATCH = 8
SEQ = 2048
TOKENS = BATCH * SEQ
DEPTH = 2
D_CONV = 512
CONV_WIDTH = 3
N_HEADS = 8
QK_NOPE = 64
QK_ROPE = 32
HALF_ROPE = QK_ROPE // 2
V_HEAD = 64
Q_LORA = 384
KV_LORA = 256
ROPE_THETA = 10000.0
D_FF = 2816
N_EXPERTS = 8
TOP_K = 2
D_FF_EXPERT = 1408
EPS = 1e-6

LANES = 128
SUBLANES = 8
BF16_SUBLANES = 16
MXU_TILE = 256
VMEM_LIMIT = 56 * 1024 * 1024

HEAD_PAD = LANES
QK_DIM = QK_NOPE + QK_ROPE
Q_SCALE = (1.0 / math.sqrt(QK_DIM)) * math.log2(math.e)
NEG_BIG = -1e30

OFF_BCU = 0
OFF_GATE = 3 * D_CONV
OFF_SMALL = OFF_GATE + 2 * D_MODEL
D_IN_PAD = OFF_SMALL + Q_LORA + KV_LORA + LANES
REF_CQ = 3 * D_CONV
REF_KR = REF_CQ + Q_LORA + KV_LORA
REF_GATE = REF_KR + QK_ROPE

CONV_CHUNK = MXU_TILE
CONV_HIST = SUBLANES
assert CONV_WIDTH - 1 <= CONV_HIST
TM_FRONT = 512
TM_MERGE = 1024
TQ = MXU_TILE
TK = 128
ATTN_PAIRS = 2
ATTN_LOOKAHEAD = 3
V_ROWS = V_HEAD + BF16_SUBLANES
FF_CHUNK = MXU_TILE
assert D_FF % FF_CHUNK == 0 and D_CONV % CONV_CHUNK == 0 and TQ % TK == 0

TM_MOE = 512
TT = 512
N_TT = TOKENS // TT
E_PAD = BF16_SUBLANES
XS_WIDTH = D_MODEL + LANES
N_PAIRS = TOKENS * TOP_K
SLOTS = TT * TOP_K + N_EXPERTS * SUBLANES
_MAX_ROWS = N_PAIRS + N_TT * N_EXPERTS * (SUBLANES - 1) + N_EXPERTS * (TM_MOE - SUBLANES)
N_MOE_TILES = -(-_MAX_ROWS // TM_MOE)
P_PAD = N_MOE_TILES * TM_MOE


def _halvings(n):
    return tuple(n >> s for s in range((n // SUBLANES).bit_length()))


SLAB_PIECES = _halvings(TT)
WAIT_PIECES = _halvings(2 * TT)
TAIL_PIECES = _halvings(TM_MOE // 2)
assert SLOTS < 2 * WAIT_PIECES[0]


def _rms(x, g):
    return x * lax.rsqrt(jnp.mean(x * x, axis=-1, keepdims=True) + EPS) * g


def _const_spec(shape):
    nd = len(shape)
    return pl.BlockSpec(shape, lambda *_: (0,) * nd, pipeline_mode=pl.Buffered(1))


def _layer_spec(layer, shape):
    return pl.BlockSpec((1,) + tuple(shape), lambda *_: (layer,) + (0,) * len(shape), pipeline_mode=pl.Buffered(1))


def _trig_kernel(pos_ref, invf_ref, cos_ref, sin_ref):
    ang = invf_ref[...] * pos_ref[...].astype(F32)
    cos_ref[...] = jnp.cos(ang)
    sin_ref[...] = jnp.sin(ang)


def _rope_tables(positions):
    inv_freq = ROPE_THETA ** (-jnp.arange(0, QK_ROPE, 2, dtype=F32) / QK_ROPE)
    return pl.pallas_call(
        _trig_kernel,
        out_shape=(jax.ShapeDtypeStruct((HALF_ROPE, TOKENS), F32),) * 2,
        name="rope_trig",
    )(positions.reshape(1, TOKENS), inv_freq.reshape(HALF_ROPE, 1))


def _dot_nt(a, b):
    return lax.dot_general(a, b, (((1,), (1,)), ((), ())), preferred_element_type=F32)


def _front_kernel(x_ref, an_ref, win_ref, cw_ref, wco_ref, qn_ref, wq_ref, kvn_ref, wuk_ref, wuv_ref,
                  cost_ref, sint_ref, q_out, k_out, v_out, gc_out, sg_out, cu_scr):
    tm = x_ref.shape[0]
    i = pl.program_id(0)
    xn = _rms(x_ref[...], an_ref[...]).astype(BF16)

    def proj(a, b):
        return jnp.dot(xn, win_ref[0, :, a:b], preferred_element_type=F32)

    small = proj(OFF_SMALL, D_IN_PAD)

    cq = cost_ref[...] * Q_SCALE
    sq = sint_ref[...] * Q_SCALE
    cqn = _rms(small[:, :Q_LORA], qn_ref[...]).astype(BF16)
    qt = _dot_nt(wq_ref[0], cqn)
    zero_pad = jnp.zeros((HEAD_PAD - QK_DIM, tm), BF16)
    for h in range(N_HEADS):
        src = h * QK_DIM
        dst = h * HEAD_PAD
        x1 = qt[src + QK_NOPE:src + QK_NOPE + HALF_ROPE, :]
        x2 = qt[src + QK_NOPE + HALF_ROPE:src + QK_DIM, :]
        q_out[dst:dst + QK_NOPE, :] = (qt[src:src + QK_NOPE, :] * Q_SCALE).astype(BF16)
        q_out[dst + QK_NOPE:dst + QK_NOPE + HALF_ROPE, :] = (x1 * cq - x2 * sq).astype(BF16)
        q_out[dst + QK_NOPE + HALF_ROPE:dst + QK_DIM, :] = (x2 * cq + x1 * sq).astype(BF16)
        q_out[dst + QK_DIM:dst + HEAD_PAD, :] = zero_pad

    cos_t = cost_ref[...]
    sin_t = sint_ref[...]
    z_nope = jnp.zeros((QK_NOPE, tm), F32)
    z_rope = jnp.zeros((HALF_ROPE, tm), F32)
    z_pad = jnp.zeros((HEAD_PAD - QK_DIM, tm), F32)
    ck = jnp.concatenate([z_nope, cos_t, cos_t, z_pad], axis=0).T
    sk_x1 = jnp.concatenate([z_nope, -sin_t, z_rope, z_pad], axis=0).T
    sk_x2 = jnp.concatenate([z_nope, z_rope, sin_t, z_pad], axis=0).T
    kr = small[:, Q_LORA + KV_LORA:]
    kpe = (kr * ck + pltpu.roll(kr, HEAD_PAD - HALF_ROPE, axis=1) * sk_x1 + pltpu.roll(kr, HALF_ROPE, axis=1) * sk_x2)
    ckvn = _rms(small[:, Q_LORA:Q_LORA + KV_LORA], kvn_ref[...]).astype(BF16)
    kn = jnp.dot(ckvn, wuk_ref[0], preferred_element_type=F32)
    for h in range(N_HEADS):
        blk = slice(h * HEAD_PAD, (h + 1) * HEAD_PAD)
        k_out[:, blk] = (kn[:, blk] + kpe).astype(BF16)
    vt = _dot_nt(wuv_ref[0], ckvn).astype(BF16)
    sub = lax.broadcasted_iota(jnp.int32, (V_ROWS - V_HEAD, TK), 0)
    ones_rows = jnp.where(sub == 0, 1.0, 0.0).astype(BF16)
    for c in range(tm // TK):
        for h in range(N_HEADS):
            v_out[c, h * V_ROWS:h * V_ROWS + V_HEAD, :] = vt[h * V_HEAD:(h + 1) * V_HEAD, c * TK:(c + 1) * TK]
            v_out[c, h * V_ROWS + V_HEAD:(h + 1) * V_ROWS, :] = ones_rows

    @pl.when(i % (SEQ // tm) == 0)
    def _():
        cu_scr[0:CONV_HIST, :] = jnp.zeros((CONV_HIST, D_CONV), F32)

    cw = cw_ref[...]
    y_conv = None
    for c in range(D_CONV // CONV_CHUNK):
        ch = slice(c * CONV_CHUNK, (c + 1) * CONV_CHUNK)
        bcu = proj(OFF_BCU + 3 * c * CONV_CHUNK, OFF_BCU + 3 * (c + 1) * CONV_CHUNK)
        b_g = bcu[:, :CONV_CHUNK]
        cu = bcu[:, CONV_CHUNK:2 * CONV_CHUNK] * bcu[:, 2 * CONV_CHUNK:]
        cu_scr[CONV_HIST:CONV_HIST + tm, ch] = cu
        conv = cu * cw[CONV_WIDTH - 1:CONV_WIDTH, ch]
        for tap in range(CONV_WIDTH - 1):
            back = CONV_WIDTH - 1 - tap
            conv = conv + cu_scr[CONV_HIST - back:CONV_HIST - back + tm, ch] * cw[tap:tap + 1, ch]
        cu_scr[0:CONV_HIST, ch] = cu[tm - CONV_HIST:, :]
        part = jnp.dot((b_g * conv).astype(BF16), wco_ref[0, ch, :], preferred_element_type=F32)
        y_conv = part if y_conv is None else y_conv + part

    gates = proj(OFF_GATE, OFF_SMALL)
    gc_out[...] = (jax.nn.sigmoid(gates[:, :D_MODEL]) * y_conv).astype(BF16)
    sg_out[...] = jax.nn.sigmoid(gates[:, D_MODEL:]).astype(BF16)


def _front(layer, x, an, win_all, cw, wco, qn, wq, kvn, wuk, wuv, cos_t, sin_t):
    tm = TM_FRONT
    row = lambda n: pl.BlockSpec((tm, n), lambda i: (i, 0))
    col = lambda n: pl.BlockSpec((n, tm), lambda i: (0, i))
    return pl.pallas_call(
        _front_kernel,
        grid=(TOKENS // tm,),
        in_specs=[row(D_MODEL), _const_spec((1, D_MODEL)), _layer_spec(layer, (D_MODEL, D_IN_PAD)),
                  _const_spec((CONV_WIDTH, D_CONV)), _layer_spec(layer, (D_CONV, D_MODEL)),
                  _const_spec((1, Q_LORA)), _layer_spec(layer, (N_HEADS * QK_DIM, Q_LORA)),
                  _const_spec((1, KV_LORA)), _layer_spec(layer, (KV_LORA, N_HEADS * HEAD_PAD)),
                  _layer_spec(layer, (N_HEADS * V_HEAD, KV_LORA)),
                  col(HALF_ROPE), col(HALF_ROPE)],
        out_specs=[col(N_HEADS * HEAD_PAD), row(N_HEADS * HEAD_PAD),
                   pl.BlockSpec((tm // TK, N_HEADS * V_ROWS, TK), lambda i: (i, 0, 0)),
                   row(D_MODEL), row(D_MODEL)],
        out_shape=[jax.ShapeDtypeStruct((N_HEADS * HEAD_PAD, TOKENS), BF16),
                   jax.ShapeDtypeStruct((TOKENS, N_HEADS * HEAD_PAD), BF16),
                   jax.ShapeDtypeStruct((TOKENS // TK, N_HEADS * V_ROWS, TK), BF16),
                   jax.ShapeDtypeStruct((TOKENS, D_MODEL), BF16),
                   jax.ShapeDtypeStruct((TOKENS, D_MODEL), BF16)],
        scratch_shapes=[pltpu.VMEM((tm + CONV_HIST, D_CONV), F32)],
        compiler_params=pltpu.CompilerParams(dimension_semantics=("arbitrary",), vmem_limit_bytes=VMEM_LIMIT),
        name="front",
    )(x, an, win_all, cw, wco, qn, wq, kvn, wuk, wuv, cos_t, sin_t)


def _attn_items():
    items = []
    for pair in range(ATTN_PAIRS):
        for i in range(SEQ // TQ):
            for ks in range(0, (i + 1) * TQ, TK):
                items.append((pair, i, ks, TK, ks + TK == (i + 1) * TQ))
    return items


def _attn_kernel(qt_ref, k_ref, vt_ref, o_ref):
    row = lax.broadcasted_iota(jnp.int32, (TK, TQ), 0)
    col = lax.broadcasted_iota(jnp.int32, (TK, TQ), 1)

    def scores(item):
        pair, i, ks, n, _ = item
        sts = []
        for h in range(2 * pair, 2 * pair + 2):
            hs = slice(h * HEAD_PAD, (h + 1) * HEAD_PAD)
            st = jnp.dot(k_ref[ks:ks + n, hs], qt_ref[hs, i * TQ:(i + 1) * TQ], preferred_element_type=F32)
            if ks + n > i * TQ:
                st = jnp.where(row + (ks - i * TQ) <= col, st, NEG_BIG)
            sts.append(st)
        return sts

    def consume(item, sts, carry):
        pair, i, ks, n, _ = item
        stats = []
        for j in range(2):
            m, _ = carry[j]
            m_new = jnp.maximum(m, jnp.max(sts[j], axis=0, keepdims=True))
            stats.append((m_new, jnp.exp2(m - m_new), jnp.exp2(sts[j] - m_new).astype(BF16)))
        new = []
        for j in range(2):
            h = 2 * pair + j
            m_new, alpha, p = stats[j]
            vt = vt_ref[ks // TK, h * V_ROWS:(h + 1) * V_ROWS, ks % TK:ks % TK + n]
            new.append((m_new, alpha * carry[j][1] + jnp.dot(vt, p, preferred_element_type=F32)))
        return new

    items = _attn_items()
    init = [(jnp.full((1, TQ), NEG_BIG, F32), jnp.zeros((V_ROWS, TQ), F32)) for _ in range(2)]
    pending = [scores(it) for it in items[:ATTN_LOOKAHEAD]]
    carry = [init] * ATTN_PAIRS
    for t, item in enumerate(items):
        if t + ATTN_LOOKAHEAD < len(items):
            pending.append(scores(items[t + ATTN_LOOKAHEAD]))
        pair, i = item[0], item[1]
        carry[pair] = consume(item, pending.pop(0), carry[pair])
        if item[4]:
            out_t = jnp.concatenate([acc[0:V_HEAD] / acc[V_HEAD:V_HEAD + 1] for (_, acc) in carry[pair]], axis=0)
            o_ref[i * TQ:(i + 1) * TQ, pair * 2 * V_HEAD:(pair + 1) * 2 * V_HEAD] = out_t.T.astype(BF16)
            carry[pair] = init


def _attention(qt, k, vt):
    heads = 2 * ATTN_PAIRS
    return pl.pallas_call(
        _attn_kernel,
        grid=(BATCH, N_HEADS // heads),
        in_specs=[pl.BlockSpec((heads * HEAD_PAD, SEQ), lambda b, hp: (hp, b)),
                  pl.BlockSpec((SEQ, heads * HEAD_PAD), lambda b, hp: (b, hp)),
                  pl.BlockSpec((SEQ // TK, heads * V_ROWS, TK), lambda b, hp: (b, hp, 0))],
        out_specs=pl.BlockSpec((SEQ, heads * V_HEAD), lambda b, hp: (b, hp)),
        out_shape=jax.ShapeDtypeStruct((TOKENS, N_HEADS * V_HEAD), BF16),
        compiler_params=pltpu.CompilerParams(dimension_semantics=("arbitrary",) * 2, vmem_limit_bytes=VMEM_LIMIT),
        name="attention",
    )(qt, k, vt)


def _merge_core(attn_ref, gc_ref, sg_ref, x_ref, wmo_ref, wo_ref, fn_ref):
    y_mla = jnp.dot(attn_ref[...], wmo_ref[0], preferred_element_type=F32)
    merged = gc_ref[...].astype(F32) + sg_ref[...].astype(F32) * y_mla
    x2 = x_ref[...] + jnp.dot(merged.astype(BF16), wo_ref[0], preferred_element_type=F32)
    return x2, _rms(x2, fn_ref[...])


def _merge_ffn_kernel(attn_ref, gc_ref, sg_ref, x_ref, wmo_ref, wo_ref, fn_ref, wg_ref, wu_ref, wd_ref, o_ref):
    x2, hn = _merge_core(attn_ref, gc_ref, sg_ref, x_ref, wmo_ref, wo_ref, fn_ref)
    hn = hn.astype(BF16)
    acc = x2
    for c in range(D_FF // FF_CHUNK):
        cs = slice(c * FF_CHUNK, (c + 1) * FF_CHUNK)
        g = jnp.dot(hn, wg_ref[:, cs], preferred_element_type=F32)
        u = jnp.dot(hn, wu_ref[:, cs], preferred_element_type=F32)
        a = (jax.nn.silu(g) * u).astype(BF16)
        acc = acc + jnp.dot(a, wd_ref[cs, :], preferred_element_type=F32)
    o_ref[...] = acc


def _merge_route_kernel(attn_ref, gc_ref, sg_ref, x_ref, wmo_ref, wo_ref, fn_ref, router_ref,
                        x2_out, hn_out, route_t_out):
    x2, hn = _merge_core(attn_ref, gc_ref, sg_ref, x_ref, wmo_ref, wo_ref, fn_ref)
    x2_out[...] = x2
    hb = hn.astype(BF16)
    hn_out[...] = hb
    hl = (hn - hb.astype(F32)).astype(BF16)
    r = router_ref[...]
    rh = r.astype(BF16)
    rl = (r - rh.astype(F32)).astype(BF16)
    logits = _dot_nt(rh, hb) + _dot_nt(rh, hl) + _dot_nt(rl, hb)
    erow = lax.broadcasted_iota(jnp.int32, logits.shape, 0)
    logits = jnp.where(erow < N_EXPERTS, logits, -jnp.inf)
    m1 = jnp.max(logits, axis=0, keepdims=True)
    i1 = jnp.min(jnp.where(logits == m1, erow, E_PAD), axis=0, keepdims=True)
    rest = jnp.where(erow == i1, -jnp.inf, logits)
    m2 = jnp.max(rest, axis=0, keepdims=True)
    i2 = jnp.min(jnp.where(rest == m2, erow, E_PAD), axis=0, keepdims=True)
    t = jnp.exp(m2 - m1)
    w1 = 1.0 / (1.0 + t)
    w2 = t * w1
    route_t = jnp.where(erow == 0, i1.astype(F32),
                        jnp.where(erow == 1, i2.astype(F32), jnp.where(erow == 2, w1, jnp.where(erow == 3, w2, 0.0))))
    route_t_out[...] = route_t[0:SUBLANES, :]


def _merge(layer, attn, gc, sg, x, wmo, wo, fn, router=None, ffn=None):
    tm = TM_MERGE
    row = lambda n: pl.BlockSpec((tm, n), lambda i: (i, 0))
    in_specs = [row(N_HEADS * V_HEAD), row(D_MODEL), row(D_MODEL), row(D_MODEL),
                _layer_spec(layer, (N_HEADS * V_HEAD, D_MODEL)), _layer_spec(layer, (D_MODEL, D_MODEL)),
                _const_spec((1, D_MODEL))]
    args = [attn, gc, sg, x, wmo, wo, fn]
    if router is None:
        kern = _merge_ffn_kernel
        in_specs += [_const_spec((D_MODEL, D_FF)), _const_spec((D_MODEL, D_FF)), _const_spec((D_FF, D_MODEL))]
        args += list(ffn)
        out_specs = row(D_MODEL)
        out_shape = jax.ShapeDtypeStruct((TOKENS, D_MODEL), F32)
        name = "merge_ffn"
    else:
        kern = _merge_route_kernel
        in_specs.append(_const_spec((E_PAD, D_MODEL)))
        args.append(router)
        out_specs = [row(D_MODEL), row(D_MODEL), pl.BlockSpec((SUBLANES, tm), lambda i: (0, i))]
        out_shape = [jax.ShapeDtypeStruct((TOKENS, D_MODEL), F32), jax.ShapeDtypeStruct((TOKENS, D_MODEL), BF16),
                     jax.ShapeDtypeStruct((SUBLANES, TOKENS), F32)]
        name = "merge_route"
    return pl.pallas_call(
        kern, grid=(TOKENS // tm,), in_specs=in_specs, out_specs=out_specs, out_shape=out_shape,
        compiler_params=pltpu.CompilerParams(dimension_semantics=("arbitrary",), vmem_limit_bytes=VMEM_LIMIT),
        name=name,
    )(*args)


def _pieces(n, sizes):
    return [((n & z) != 0, n & ~(2 * z - 1), z) for z in sizes]


def _aligned(rows, z):
    return pl.ds(rows if isinstance(rows, int) else pl.multiple_of(rows, SUBLANES), z)


def _start_slab_copies(tile, base_ref, offp_ref, np_ref, make_copy):
    for e in range(N_EXPERTS):
        n = np_ref[tile * N_EXPERTS + e]
        src = offp_ref[tile * N_EXPERTS + e]
        dst = base_ref[tile * N_EXPERTS + e]
        for pred, off, z in _pieces(n, SLAB_PIECES):
            @pl.when(pred)
            def _(off=off, z=z):
                make_copy(src + off, dst + off, z).start()


def _wait_rows(total, make_copy):
    for pred, _, z in _pieces(total, WAIT_PIECES):
        @pl.when(pred)
        def _(z=z):
            make_copy(0, 0, z).wait()


def _tile_placement(rt_ref, offp_ref, i):
    e0 = rt_ref[0:1, :].astype(jnp.int32)
    e1 = rt_ref[1:2, :].astype(jnp.int32)
    eid = lax.broadcasted_iota(jnp.int32, (E_PAD, TT), 0)
    sel = jnp.where(e0 == eid, 1.0, jnp.where(e1 == eid, 1.0, 0.0)).astype(BF16)
    r = lax.broadcasted_iota(jnp.int32, (TT, TT), 0)
    c = lax.broadcasted_iota(jnp.int32, (TT, TT), 1)
    earlier = jnp.where(r < c, 1.0, 0.0).astype(BF16)
    rank = jnp.dot(sel, earlier, preferred_element_type=F32)
    ecol = lax.broadcasted_iota(jnp.int32, (E_PAD, 1), 0)
    offp = jnp.zeros((E_PAD, 1), F32)
    for e in range(N_EXPERTS):
        offp = jnp.where(ecol == e, offp_ref[i * N_EXPERTS + e].astype(F32), offp)
    slot_all = rank + offp
    slot0 = jnp.sum(jnp.where(e0 == eid, slot_all, 0.0), axis=0, keepdims=True)
    slot1 = jnp.sum(jnp.where(e1 == eid, slot_all, 0.0), axis=0, keepdims=True)
    sidx = lax.broadcasted_iota(jnp.int32, (SLOTS, TT), 0).astype(F32)
    place = jnp.where(sidx == slot0, 1.0, jnp.where(sidx == slot1, 1.0, 0.0)).astype(BF16)
    return sidx, slot0, slot1, place


def _dispatch_kernel(base_ref, offp_ref, np_ref, tot_ref, tail_ref, hn_ref, rt_ref, xs_hbm, xs_scr, zbuf, sem, zsem):
    i = pl.program_id(0)
    slot = i % 2

    def copy(buf_slot):
        return lambda s, d, z: pltpu.make_async_copy(xs_scr.at[buf_slot, _aligned(s, z), :],
                                                     xs_hbm.at[_aligned(d, z), :], sem.at[buf_slot])

    def zero_copy(d, z):
        return pltpu.make_async_copy(zbuf.at[pl.ds(0, z), :], xs_hbm.at[_aligned(d, z), :], zsem)

    @pl.when(i == 0)
    def _():
        zbuf[...] = jnp.zeros_like(zbuf)
        for phase in ("start", "wait"):
            for e in range(N_EXPERTS):
                t0 = tail_ref[e]
                n = tail_ref[N_EXPERTS + e]
                full = lax.shift_right_logical(n, jnp.int32(TM_MOE.bit_length() - 1))

                def chunk(c, carry, t0=t0, phase=phase):
                    cp = zero_copy(t0 + c * TM_MOE, TM_MOE)
                    cp.start() if phase == "start" else cp.wait()
                    return carry

                lax.fori_loop(0, full, chunk, 0)
                rest = n & (TM_MOE - 1)
                for pred, off, z in _pieces(rest, TAIL_PIECES):
                    @pl.when(pred)
                    def _(off=off, z=z, t0=t0, full=full, phase=phase):
                        cp = zero_copy(t0 + full * TM_MOE + off, z)
                        cp.start() if phase == "start" else cp.wait()

    sidx, slot0, slot1, place = _tile_placement(rt_ref, offp_ref, i)
    xs_scr[slot, :, 0:D_MODEL] = jnp.dot(place, hn_ref[...], preferred_element_type=F32)
    w_slot = jnp.sum(jnp.where(sidx == slot0, rt_ref[2:3, :], jnp.where(sidx == slot1, rt_ref[3:4, :], 0.0)),
                     axis=1, keepdims=True)
    xs_scr[slot, :, D_MODEL:] = jnp.broadcast_to(w_slot, (SLOTS, LANES))

    _start_slab_copies(i, base_ref, offp_ref, np_ref, copy(slot))

    @pl.when(i > 0)
    def _():
        _wait_rows(tot_ref[i - 1], copy(1 - slot))

    @pl.when(i == pl.num_programs(0) - 1)
    def _():
        _wait_rows(tot_ref[i], copy(slot))


def _dispatch(tables, hn, route_t):
    grid_spec = pltpu.PrefetchScalarGridSpec(
        num_scalar_prefetch=5,
        grid=(N_TT,),
        in_specs=[pl.BlockSpec((TT, D_MODEL), lambda i, *_: (i, 0)),
                  pl.BlockSpec((SUBLANES, TT), lambda i, *_: (0, i))],
        out_specs=pl.BlockSpec(memory_space=pl.ANY),
        scratch_shapes=[pltpu.VMEM((2, SLOTS, XS_WIDTH), F32), pltpu.VMEM((TM_MOE, XS_WIDTH), F32),
                        pltpu.SemaphoreType.DMA((2,)), pltpu.SemaphoreType.DMA],
    )
    return pl.pallas_call(
        _dispatch_kernel,
        grid_spec=grid_spec,
        out_shape=jax.ShapeDtypeStruct((P_PAD, XS_WIDTH), F32),
        compiler_params=pltpu.CompilerParams(dimension_semantics=("arbitrary",), vmem_limit_bytes=VMEM_LIMIT),
        name="moe_dispatch",
    )(tables["base"], tables["offp"], tables["np"], tables["total"], tables["tail"], hn, route_t)


def _moe_kernel(te_ref, nv_ref, nxt_ref, par_ref, first_ref, last_ref, x_ref, wg_hbm, wu_hbm, wd_hbm, y_ref,
                stage_g, stage_u, stage_d, wgu_scr, wd_scr, sem):
    i = pl.program_id(0)
    valid = i < nv_ref[0]
    nxt = nxt_ref[i]

    def fetch(e):
        return [pltpu.make_async_copy(wg_hbm.at[e], stage_g, sem.at[0]),
                pltpu.make_async_copy(wu_hbm.at[e], stage_u, sem.at[1]),
                pltpu.make_async_copy(wd_hbm.at[e], stage_d, sem.at[2])]

    def round_into(s):
        wgu_scr[s, :, 0:D_FF_EXPERT] = stage_g[...].astype(BF16)
        wgu_scr[s, :, D_FF_EXPERT:] = stage_u[...].astype(BF16)
        wd_scr[s] = stage_d[...].astype(BF16)

    @pl.when(i == 0)
    def _():
        for cp in fetch(te_ref[0]):
            cp.start()
        for cp in fetch(te_ref[0]):
            cp.wait()
        round_into(par_ref[0])

    @pl.when(jnp.logical_and(valid, jnp.logical_and(first_ref[i] == 1, nxt >= 0)))
    def _():
        for cp in fetch(nxt):
            cp.start()

    @pl.when(valid)
    def _():
        s = par_ref[i]
        x = x_ref[:, 0:D_MODEL].astype(BF16)
        h = jnp.dot(x, wgu_scr[s], preferred_element_type=F32)
        a = (jax.nn.silu(h[:, :D_FF_EXPERT]) * h[:, D_FF_EXPERT:]).astype(BF16)
        y_ref[...] = jnp.dot(a, wd_scr[s], preferred_element_type=F32) * x_ref[:, D_MODEL:D_MODEL + 1]

    @pl.when(jnp.logical_and(valid, jnp.logical_and(last_ref[i] == 1, nxt >= 0)))
    def _():
        for cp in fetch(nxt):
            cp.wait()
        round_into(1 - par_ref[i])

    @pl.when(jnp.logical_not(valid))
    def _():
        y_ref[...] = jnp.zeros_like(y_ref)


def _moe(sched, xs, wg, wu, wd):
    grid_spec = pltpu.PrefetchScalarGridSpec(
        num_scalar_prefetch=6,
        grid=(N_MOE_TILES,),
        in_specs=[pl.BlockSpec((TM_MOE, XS_WIDTH), lambda i, *_: (i, 0)),
                  pl.BlockSpec(memory_space=pl.ANY), pl.BlockSpec(memory_space=pl.ANY),
                  pl.BlockSpec(memory_space=pl.ANY)],
        out_specs=pl.BlockSpec((TM_MOE, D_MODEL), lambda i, *_: (i, 0)),
        scratch_shapes=[pltpu.VMEM((D_MODEL, D_FF_EXPERT), F32), pltpu.VMEM((D_MODEL, D_FF_EXPERT), F32),
                        pltpu.VMEM((D_FF_EXPERT, D_MODEL), F32),
                        pltpu.VMEM((2, D_MODEL, 2 * D_FF_EXPERT), BF16), pltpu.VMEM((2, D_FF_EXPERT, D_MODEL), BF16),
                        pltpu.SemaphoreType.DMA((3,))],
    )
    return pl.pallas_call(
        _moe_kernel,
        grid_spec=grid_spec,
        out_shape=jax.ShapeDtypeStruct((P_PAD, D_MODEL), F32),
        compiler_params=pltpu.CompilerParams(dimension_semantics=("arbitrary",), vmem_limit_bytes=VMEM_LIMIT),
        name="moe_experts",
    )(sched["expert"], sched["n_valid"], sched["next"], sched["parity"], sched["first"], sched["last"],
      xs, wg, wu, wd)


def _combine_kernel(base_ref, offp_ref, np_ref, tot_ref, x_ref, route_ref, y_hbm, fn_ref, o_ref, ybuf, sem):
    i = pl.program_id(0)
    slot = i % 2

    def copy(buf_slot):
        return lambda s, d, z: pltpu.make_async_copy(y_hbm.at[_aligned(d, z), :],
                                                     ybuf.at[buf_slot, _aligned(s, z), :], sem.at[buf_slot])

    @pl.when(i == 0)
    def _():
        ybuf[...] = jnp.zeros_like(ybuf)
        _start_slab_copies(0, base_ref, offp_ref, np_ref, copy(0))

    @pl.when(i + 1 < pl.num_programs(0))
    def _():
        _start_slab_copies(i + 1, base_ref, offp_ref, np_ref, copy(1 - slot))

    _, _, _, place = _tile_placement(route_ref, offp_ref, i)

    _wait_rows(tot_ref[i], copy(slot))
    y = ybuf[slot]
    yh = y.astype(BF16)
    yl = (y - yh.astype(F32)).astype(BF16)
    tn = (((0,), (0,)), ((), ()))
    moe = (lax.dot_general(place, yh, tn, preferred_element_type=F32)
           + lax.dot_general(place, yl, tn, preferred_element_type=F32))
    o_ref[...] = _rms(x_ref[...] + moe, fn_ref[...])


def _combine(tables, x2, route_t, y_sorted, fn):
    grid_spec = pltpu.PrefetchScalarGridSpec(
        num_scalar_prefetch=4,
        grid=(N_TT,),
        in_specs=[pl.BlockSpec((TT, D_MODEL), lambda i, *_: (i, 0)),
                  pl.BlockSpec((SUBLANES, TT), lambda i, *_: (0, i)),
                  pl.BlockSpec(memory_space=pl.ANY),
                  pl.BlockSpec((1, D_MODEL), lambda i, *_: (0, 0))],
        out_specs=pl.BlockSpec((TT, D_MODEL), lambda i, *_: (i, 0)),
        scratch_shapes=[pltpu.VMEM((2, SLOTS, D_MODEL), F32), pltpu.SemaphoreType.DMA((2,))],
    )
    return pl.pallas_call(
        _combine_kernel,
        grid_spec=grid_spec,
        out_shape=jax.ShapeDtypeStruct((TOKENS, D_MODEL), F32),
        compiler_params=pltpu.CompilerParams(dimension_semantics=("arbitrary",), vmem_limit_bytes=VMEM_LIMIT),
        name="moe_combine",
    )(tables["base"], tables["offp"], tables["np"], tables["total"], x2, route_t, y_sorted, fn)


def _routing_tables(route_t):
    experts = route_t[:TOP_K, :].astype(jnp.int32)
    eid = jnp.arange(N_EXPERTS, dtype=jnp.int32)[:, None, None]
    chosen = jnp.sum((experts[None, :, :] == eid).astype(jnp.int32), axis=1)
    cnt = jnp.sum(chosen.reshape(N_EXPERTS, N_TT, TT), axis=2).T
    npad = (cnt + SUBLANES - 1) // SUBLANES * SUBLANES
    offp = jnp.cumsum(npad, axis=1) - npad
    total = jnp.sum(npad, axis=1)
    group = jnp.sum(npad, axis=0)
    group_pad = (group + TM_MOE - 1) // TM_MOE * TM_MOE
    group_end = jnp.cumsum(group_pad)
    group_start = group_end - group_pad
    base = group_start[None, :] + jnp.cumsum(npad, axis=0) - npad
    tail_start = group_start + group
    is_last = jnp.arange(N_EXPERTS) == N_EXPERTS - 1
    tail_len = jnp.where(is_last, P_PAD - tail_start, group_pad - group)
    tile_start = jnp.arange(N_MOE_TILES, dtype=jnp.int32) * TM_MOE
    tile_expert = jnp.minimum(jnp.sum((tile_start[:, None] >= group_end[None, :]).astype(jnp.int32), axis=1),
                              N_EXPERTS - 1)
    ids = jnp.arange(N_EXPERTS, dtype=jnp.int32)
    nonempty = group_pad > 0
    ordinal = jnp.cumsum(nonempty.astype(jnp.int32)) - 1
    later = jnp.logical_and(nonempty[None, :], ids[None, :] > ids[:, None])
    next_expert = jnp.min(jnp.where(later, ids[None, :], N_EXPERTS), axis=1)
    next_expert = jnp.where(next_expert == N_EXPERTS, -1, next_expert)
    is_tile_expert = tile_expert[:, None] == ids[None, :]
    per_tile = lambda table: jnp.sum(jnp.where(is_tile_expert, table[None, :], 0), axis=1)
    sched = {"expert": tile_expert, "n_valid": (group_end[-1] // TM_MOE).reshape(1),
             "next": per_tile(next_expert), "parity": per_tile(ordinal) % 2,
             "first": tile_start == per_tile(group_start),
             "last": tile_start + TM_MOE == per_tile(group_end)}
    tables = {"base": base.reshape(-1), "offp": offp.reshape(-1), "np": npad.reshape(-1), "total": total,
              "tail": jnp.concatenate([tail_start, tail_len])}
    as_i32 = lambda d: {k: v.astype(jnp.int32) for k, v in d.items()}
    return as_i32(tables), as_i32(sched)


WIN_ROWS = 256


def _win_layout_kernel(wt_ref, o_ref):
    wt = wt_ref[0]
    cols = wt.shape[1]
    kr_tile = jnp.concatenate([jnp.zeros((QK_NOPE, cols), F32), wt[REF_KR:REF_GATE, :],
                               jnp.zeros((HEAD_PAD - QK_DIM, cols), F32)], axis=0)
    conv_rows = [wt[part * D_CONV + c * CONV_CHUNK:part * D_CONV + (c + 1) * CONV_CHUNK, :]
                 for c in range(D_CONV // CONV_CHUNK) for part in range(3)]
    grouped = jnp.concatenate(conv_rows + [wt[REF_GATE:, :], wt[REF_CQ:REF_KR, :], kr_tile], axis=0)
    o_ref[0] = grouped.T.astype(BF16)


def _win_layout(w_in):
    wt = jnp.swapaxes(w_in, 1, 2)
    d_in = wt.shape[1]
    return pl.pallas_call(
        _win_layout_kernel,
        grid=(DEPTH, D_MODEL // WIN_ROWS),
        in_specs=[pl.BlockSpec((1, d_in, WIN_ROWS), lambda l, r: (l, 0, r))],
        out_specs=pl.BlockSpec((1, WIN_ROWS, D_IN_PAD), lambda l, r: (l, r, 0)),
        out_shape=jax.ShapeDtypeStruct((DEPTH, D_MODEL, D_IN_PAD), BF16),
        compiler_params=pltpu.CompilerParams(dimension_semantics=("arbitrary",) * 2, vmem_limit_bytes=VMEM_LIMIT),
        name="win_layout",
    )(wt)


def _small_layout_kernel(wuq_ref, wukv_ref, wco_ref, wmo_ref, wo_ref,
                         wq_out, wuk_out, wuv_out, wco_out, wmo_out, wo_out):
    wq_out[0] = wuq_ref[0].T.astype(BF16)
    wkv = wukv_ref[0]
    zeros = jnp.zeros((KV_LORA, HEAD_PAD - QK_NOPE), BF16)
    values = []
    for h in range(N_HEADS):
        c0 = h * (QK_NOPE + V_HEAD)
        wuk_out[0, :, h * HEAD_PAD:h * HEAD_PAD + QK_NOPE] = wkv[:, c0:c0 + QK_NOPE].astype(BF16)
        wuk_out[0, :, h * HEAD_PAD + QK_NOPE:(h + 1) * HEAD_PAD] = zeros
        values.append(wkv[:, c0 + QK_NOPE:c0 + QK_NOPE + V_HEAD])
    wuv_out[0] = jnp.concatenate(values, axis=1).T.astype(BF16)
    wco_out[0] = wco_ref[0].astype(BF16)
    wmo_out[0] = wmo_ref[0].astype(BF16)
    wo_out[0] = wo_ref[0].astype(BF16)


def _small_layout(w_uq, w_ukv, w_conv_out, w_mla_out, w_o):
    layer = lambda *shape: pl.BlockSpec((1,) + shape, lambda l: (l, 0, 0))
    shapes = [(N_HEADS * QK_DIM, Q_LORA), (KV_LORA, N_HEADS * HEAD_PAD), (N_HEADS * V_HEAD, KV_LORA),
              (D_CONV, D_MODEL), (N_HEADS * V_HEAD, D_MODEL), (D_MODEL, D_MODEL)]
    return pl.pallas_call(
        _small_layout_kernel,
        grid=(DEPTH,),
        in_specs=[layer(*w.shape[1:]) for w in (w_uq, w_ukv, w_conv_out, w_mla_out, w_o)],
        out_specs=[layer(*s) for s in shapes],
        out_shape=[jax.ShapeDtypeStruct((DEPTH,) + s, BF16) for s in shapes],
        compiler_params=pltpu.CompilerParams(dimension_semantics=("arbitrary",), vmem_limit_bytes=VMEM_LIMIT),
        name="small_layout",
    )(w_uq, w_ukv, w_conv_out, w_mla_out, w_o)


def kernel(x, positions, attn_norm, w_in, conv_w, w_conv_out, q_norm, w_uq, kv_norm, w_ukv, w_mla_out, w_o, ffn_norm,
           w_gate, w_up, w_down, router, w_gate_e, w_up_e, w_down_e, final_norm):
    assert x.shape == (BATCH, SEQ, D_MODEL) and positions.shape == (BATCH, SEQ)
    cos_t, sin_t = _rope_tables(positions)
    xt = x.reshape(TOKENS, D_MODEL)
    out = None
    win = _win_layout(w_in)
    wq, wuk, wuv, wco, wmo, wo = _small_layout(w_uq, w_ukv, w_conv_out, w_mla_out, w_o)
    for l in range(DEPTH):
        q, k, v, gc, sg = _front(l, xt, attn_norm[l].reshape(1, D_MODEL), win, conv_w[l], wco,
                                 q_norm[l].reshape(1, Q_LORA), wq, kv_norm[l].reshape(1, KV_LORA), wuk, wuv,
                                 cos_t, sin_t)
        attn = _attention(q, k, v)
        fn = ffn_norm[l].reshape(1, D_MODEL)
        if l % 2 == 0:
            j = l // 2
            xt = _merge(l, attn, gc, sg, xt, wmo, wo, fn,
                        ffn=(w_gate[j].astype(BF16), w_up[j].astype(BF16), w_down[j].astype(BF16)))
        else:
            j = l // 2
            router_pad = jnp.pad(router[j].T, ((0, E_PAD - N_EXPERTS), (0, 0)))
            x2, hn, route_t = _merge(l, attn, gc, sg, xt, wmo, wo, fn, router_pad)
            tables, sched = _routing_tables(route_t)
            xs = _dispatch(tables, hn, route_t)
            y_sorted = _moe(sched, xs, w_gate_e[j], w_up_e[j], w_down_e[j])
            out = _combine(tables, x2, route_t, y_sorted, final_norm.reshape(1, D_MODEL))
    return out.reshape(BATCH, SEQ, D_MODEL)
```

```python
import math

import jax
import jax.numpy as jnp
from jax import lax
from jax.experimental import pallas as pl
from jax.experimental.pallas import tpu as pltpu

F32 = jnp.float32
BF16 = jnp.bfloat16

D_MODEL = 1024
BATCH = 8
SEQ = 2048
TOKENS = BATCH * SEQ
DEPTH = 2
D_CONV = 512
CONV_WIDTH = 3
N_HEADS = 8
QK_NOPE = 64
QK_ROPE = 32
HALF_ROPE = QK_ROPE // 2
V_HEAD = 64
Q_LORA = 384
KV_LORA = 256
ROPE_THETA = 10000.0
D_FF = 2816
N_EXPERTS = 8
TOP_K = 2
D_FF_EXPERT = 1408
EPS = 1e-6

LANES = 128
SUBLANES = 8
BF16_SUBLANES = 16
MXU_TILE = 256
VMEM_LIMIT = 56 * 1024 * 1024

HEAD_PAD = LANES
QK_DIM = QK_NOPE + QK_ROPE
Q_SCALE = (1.0 / math.sqrt(QK_DIM)) * math.log2(math.e)
NEG_BIG = -1e30

OFF_BCU = 0
OFF_GATE = 3 * D_CONV
OFF_SMALL = OFF_GATE + 2 * D_MODEL
D_IN_PAD = OFF_SMALL + Q_LORA + KV_LORA + LANES
REF_CQ = 3 * D_CONV
REF_KR = REF_CQ + Q_LORA + KV_LORA
REF_GATE = REF_KR + QK_ROPE

CONV_CHUNK = MXU_TILE
CONV_HIST = SUBLANES
assert CONV_WIDTH - 1 <= CONV_HIST
TM_FRONT = 512
TM_MERGE = 1024
TQ = MXU_TILE
TK = 128
ATTN_PAIRS = 2
ATTN_LOOKAHEAD = 3
V_ROWS = V_HEAD + BF16_SUBLANES
FF_CHUNK = MXU_TILE
assert D_FF % FF_CHUNK == 0 and D_CONV % CONV_CHUNK == 0 and TQ % TK == 0

TM_MOE = 512
TT = 512
N_TT = TOKENS // TT
E_PAD = BF16_SUBLANES
XS_WIDTH = D_MODEL + LANES
N_PAIRS = TOKENS * TOP_K
SLOTS = TT * TOP_K + N_EXPERTS * SUBLANES
_MAX_ROWS = N_PAIRS + N_TT * N_EXPERTS * (SUBLANES - 1) + N_EXPERTS * (TM_MOE - SUBLANES)
N_MOE_TILES = -(-_MAX_ROWS // TM_MOE)
P_PAD = N_MOE_TILES * TM_MOE


def _halvings(n):
    return tuple(n >> s for s in range((n // SUBLANES).bit_length()))


SLAB_PIECES = _halvings(TT)
WAIT_PIECES = _halvings(2 * TT)
TAIL_PIECES = _halvings(TM_MOE // 2)
assert SLOTS < 2 * WAIT_PIECES[0]


def _rms(x, g):
    return x * lax.rsqrt(jnp.mean(x * x, axis=-1, keepdims=True) + EPS) * g


def _const_spec(shape):
    nd = len(shape)
    return pl.BlockSpec(shape, lambda *_: (0,) * nd, pipeline_mode=pl.Buffered(1))


def _layer_spec(layer, shape):
    return pl.BlockSpec((1,) + tuple(shape), lambda *_: (layer,) + (0,) * len(shape), pipeline_mode=pl.Buffered(1))


def _trig_kernel(pos_ref, invf_ref, cos_ref, sin_ref):
    ang = invf_ref[...] * pos_ref[...].astype(F32)
    cos_ref[...] = jnp.cos(ang)
    sin_ref[...] = jnp.sin(ang)


def _rope_tables(positions):
    inv_freq = ROPE_THETA ** (-jnp.arange(0, QK_ROPE, 2, dtype=F32) / QK_ROPE)
    return pl.pallas_call(
        _trig_kernel,
        out_shape=(jax.ShapeDtypeStruct((HALF_ROPE, TOKENS), F32),) * 2,
        name="rope_trig",
    )(positions.reshape(1, TOKENS), inv_freq.reshape(HALF_ROPE, 1))


def _dot_nt(a, b):
    return lax.dot_general(a, b, (((1,), (1,)), ((), ())), preferred_element_type=F32)


def _front_kernel(x_ref, an_ref, win_ref, cw_ref, wco_ref, qn_ref, wq_ref, kvn_ref, wuk_ref, wuv_ref,
                  cost_ref, sint_ref, q_out, k_out, v_out, gc_out, sg_out, cu_scr):
    tm = x_ref.shape[0]
    i = pl.program_id(0)
    xn = _rms(x_ref[...], an_ref[...]).astype(BF16)

    def proj(a, b):
        return jnp.dot(xn, win_ref[0, :, a:b], preferred_element_type=F32)

    small = proj(OFF_SMALL, D_IN_PAD)

    cq = cost_ref[...] * Q_SCALE
    sq = sint_ref[...] * Q_SCALE
    cqn = _rms(small[:, :Q_LORA], qn_ref[...]).astype(BF16)
    qt = _dot_nt(wq_ref[0], cqn)
    zero_pad = jnp.zeros((HEAD_PAD - QK_DIM, tm), BF16)
    for h in range(N_HEADS):
        src = h * QK_DIM
        dst = h * HEAD_PAD
        x1 = qt[src + QK_NOPE:src + QK_NOPE + HALF_ROPE, :]
        x2 = qt[src + QK_NOPE + HALF_ROPE:src + QK_DIM, :]
        q_out[dst:dst + QK_NOPE, :] = (qt[src:src + QK_NOPE, :] * Q_SCALE).astype(BF16)
        q_out[dst + QK_NOPE:dst + QK_NOPE + HALF_ROPE, :] = (x1 * cq - x2 * sq).astype(BF16)
        q_out[dst + QK_NOPE + HALF_ROPE:dst + QK_DIM, :] = (x2 * cq + x1 * sq).astype(BF16)
        q_out[dst + QK_DIM:dst + HEAD_PAD, :] = zero_pad

    cos_t = cost_ref[...]
    sin_t = sint_ref[...]
    z_nope = jnp.zeros((QK_NOPE, tm), F32)
    z_rope = jnp.zeros((HALF_ROPE, tm), F32)
    z_pad = jnp.zeros((HEAD_PAD - QK_DIM, tm), F32)
    ck = jnp.concatenate([z_nope, cos_t, cos_t, z_pad], axis=0).T
    sk_x1 = jnp.concatenate([z_nope, -sin_t, z_rope, z_pad], axis=0).T
    sk_x2 = jnp.concatenate([z_nope, z_rope, sin_t, z_pad], axis=0).T
    kr = small[:, Q_LORA + KV_LORA:]
    kpe = (kr * ck + pltpu.roll(kr, HEAD_PAD - HALF_ROPE, axis=1) * sk_x1 + pltpu.roll(kr, HALF_ROPE, axis=1) * sk_x2)
    ckvn = _rms(small[:, Q_LORA:Q_LORA + KV_LORA], kvn_ref[...]).astype(BF16)
    kn = jnp.dot(ckvn, wuk_ref[0], preferred_element_type=F32)
    low = lax.broadcasted_iota(jnp.int32, (1, HEAD_PAD), 1) < QK_NOPE
    for h in range(0, N_HEADS, 2):
        two = kn[:, h * QK_NOPE:(h + 2) * QK_NOPE]
        k_out[:, h * HEAD_PAD:(h + 1) * HEAD_PAD] = (jnp.where(low, two, 0.0) + kpe).astype(BF16)
        swapped = pltpu.roll(two, QK_NOPE, axis=1)
        k_out[:, (h + 1) * HEAD_PAD:(h + 2) * HEAD_PAD] = (jnp.where(low, swapped, 0.0) + kpe).astype(BF16)
    vt = _dot_nt(wuv_ref[0], ckvn).astype(BF16)
    sub = lax.broadcasted_iota(jnp.int32, (V_ROWS - V_HEAD, TK), 0)
    ones_rows = jnp.where(sub == 0, 1.0, 0.0).astype(BF16)
    for c in range(tm // TK):
        for h in range(N_HEADS):
            v_out[c, h * V_ROWS:h * V_ROWS + V_HEAD, :] = vt[h * V_HEAD:(h + 1) * V_HEAD, c * TK:(c + 1) * TK]
            v_out[c, h * V_ROWS + V_HEAD:(h + 1) * V_ROWS, :] = ones_rows

    @pl.when(i % (SEQ // tm) == 0)
    def _():
        cu_scr[0:CONV_HIST, :] = jnp.zeros((CONV_HIST, D_CONV), F32)

    cw = cw_ref[...]
    y_conv = None
    for c in range(D_CONV // CONV_CHUNK):
        ch = slice(c * CONV_CHUNK, (c + 1) * CONV_CHUNK)
        bcu = proj(OFF_BCU + 3 * c * CONV_CHUNK, OFF_BCU + 3 * (c + 1) * CONV_CHUNK)
        b_g = bcu[:, :CONV_CHUNK]
        cu = bcu[:, CONV_CHUNK:2 * CONV_CHUNK] * bcu[:, 2 * CONV_CHUNK:]
        cu_scr[CONV_HIST:CONV_HIST + tm, ch] = cu
        conv = cu * cw[CONV_WIDTH - 1:CONV_WIDTH, ch]
        for tap in range(CONV_WIDTH - 1):
            back = CONV_WIDTH - 1 - tap
            conv = conv + cu_scr[CONV_HIST - back:CONV_HIST - back + tm, ch] * cw[tap:tap + 1, ch]
        cu_scr[0:CONV_HIST, ch] = cu[tm - CONV_HIST:, :]
        part = jnp.dot((b_g * conv).astype(BF16), wco_ref[0, ch, :], preferred_element_type=F32)
        y_conv = part if y_conv is None else y_conv + part

    gates = proj(OFF_GATE, OFF_SMALL)
    gc_out[...] = (jax.nn.sigmoid(gates[:, :D_MODEL]) * y_conv).astype(BF16)
    sg_out[...] = jax.nn.sigmoid(gates[:, D_MODEL:]).astype(BF16)


def _front(layer, x, an, win_all, cw, wco, qn, wq, kvn, wuk, wuv, cos_t, sin_t):
    tm = TM_FRONT
    row = lambda n: pl.BlockSpec((tm, n), lambda i: (i, 0))
    col = lambda n: pl.BlockSpec((n, tm), lambda i: (0, i))
    return pl.pallas_call(
        _front_kernel,
        grid=(TOKENS // tm,),
        in_specs=[row(D_MODEL), _const_spec((1, D_MODEL)), _layer_spec(layer, (D_MODEL, D_IN_PAD)),
                  _const_spec((CONV_WIDTH, D_CONV)), _layer_spec(layer, (D_CONV, D_MODEL)),
                  _const_spec((1, Q_LORA)), _layer_spec(layer, (N_HEADS * QK_DIM, Q_LORA)),
                  _const_spec((1, KV_LORA)), _layer_spec(layer, (KV_LORA, N_HEADS * QK_NOPE)),
                  _layer_spec(layer, (N_HEADS * V_HEAD, KV_LORA)),
                  col(HALF_ROPE), col(HALF_ROPE)],
        out_specs=[col(N_HEADS * HEAD_PAD), row(N_HEADS * HEAD_PAD),
                   pl.BlockSpec((tm // TK, N_HEADS * V_ROWS, TK), lambda i: (i, 0, 0)),
                   row(D_MODEL), row(D_MODEL)],
        out_shape=[jax.ShapeDtypeStruct((N_HEADS * HEAD_PAD, TOKENS), BF16),
                   jax.ShapeDtypeStruct((TOKENS, N_HEADS * HEAD_PAD), BF16),
                   jax.ShapeDtypeStruct((TOKENS // TK, N_HEADS * V_ROWS, TK), BF16),
                   jax.ShapeDtypeStruct((TOKENS, D_MODEL), BF16),
                   jax.ShapeDtypeStruct((TOKENS, D_MODEL), BF16)],
        scratch_shapes=[pltpu.VMEM((tm + CONV_HIST, D_CONV), F32)],
        compiler_params=pltpu.CompilerParams(dimension_semantics=("arbitrary",), vmem_limit_bytes=VMEM_LIMIT),
        name="front",
    )(x, an, win_all, cw, wco, qn, wq, kvn, wuk, wuv, cos_t, sin_t)


def _attn_items():
    items = []
    for pair in range(ATTN_PAIRS):
        for i in range(SEQ // TQ):
            for ks in range(0, (i + 1) * TQ, TK):
                items.append((pair, i, ks, TK, ks + TK == (i + 1) * TQ))
    return items


def _attn_kernel(qt_ref, k_ref, vt_ref, o_ref):
    row = lax.broadcasted_iota(jnp.int32, (TK, TQ), 0)
    col = lax.broadcasted_iota(jnp.int32, (TK, TQ), 1)

    def scores(item):
        pair, i, ks, n, _ = item
        sts = []
        for h in range(2 * pair, 2 * pair + 2):
            hs = slice(h * HEAD_PAD, (h + 1) * HEAD_PAD)
            st = jnp.dot(k_ref[ks:ks + n, hs], qt_ref[hs, i * TQ:(i + 1) * TQ], preferred_element_type=F32)
            if ks + n > i * TQ:
                st = jnp.where(row + (ks - i * TQ) <= col, st, NEG_BIG)
            sts.append(st)
        return sts

    def consume(item, sts, carry):
        pair, i, ks, n, _ = item
        stats = []
        for j in range(2):
            m, _ = carry[j]
            m_new = jnp.maximum(m, jnp.max(sts[j], axis=0, keepdims=True))
            stats.append((m_new, jnp.exp2(m - m_new), jnp.exp2(sts[j] - m_new).astype(BF16)))
        new = []
        for j in range(2):
            h = 2 * pair + j
            m_new, alpha, p = stats[j]
            vt = vt_ref[ks // TK, h * V_ROWS:(h + 1) * V_ROWS, ks % TK:ks % TK + n]
            new.append((m_new, alpha * carry[j][1] + jnp.dot(vt, p, preferred_element_type=F32)))
        return new

    items = _attn_items()
    init = [(jnp.full((1, TQ), NEG_BIG, F32), jnp.zeros((V_ROWS, TQ), F32)) for _ in range(2)]
    pending = [scores(it) for it in items[:ATTN_LOOKAHEAD]]
    carry = [init] * ATTN_PAIRS
    for t, item in enumerate(items):
        if t + ATTN_LOOKAHEAD < len(items):
            pending.append(scores(items[t + ATTN_LOOKAHEAD]))
        pair, i = item[0], item[1]
        carry[pair] = consume(item, pending.pop(0), carry[pair])
        if item[4]:
            out_t = jnp.concatenate([acc[0:V_HEAD] / acc[V_HEAD:V_HEAD + 1] for (_, acc) in carry[pair]], axis=0)
            o_ref[i * TQ:(i + 1) * TQ, pair * 2 * V_HEAD:(pair + 1) * 2 * V_HEAD] = out_t.T.astype(BF16)
            carry[pair] = init


def _attention(qt, k, vt):
    heads = 2 * ATTN_PAIRS
    return pl.pallas_call(
        _attn_kernel,
        grid=(BATCH, N_HEADS // heads),
        in_specs=[pl.BlockSpec((heads * HEAD_PAD, SEQ), lambda b, hp: (hp, b)),
                  pl.BlockSpec((SEQ, heads * HEAD_PAD), lambda b, hp: (b, hp)),
                  pl.BlockSpec((SEQ // TK, heads * V_ROWS, TK), lambda b, hp: (b, hp, 0))],
        out_specs=pl.BlockSpec((SEQ, heads * V_HEAD), lambda b, hp: (b, hp)),
        out_shape=jax.ShapeDtypeStruct((TOKENS, N_HEADS * V_HEAD), BF16),
        compiler_params=pltpu.CompilerParams(dimension_semantics=("arbitrary",) * 2, vmem_limit_bytes=VMEM_LIMIT),
        name="attention",
    )(qt, k, vt)


def _merge_core(attn_ref, gc_ref, sg_ref, x_ref, wmo_ref, wo_ref, fn_ref):
    y_mla = jnp.dot(attn_ref[...], wmo_ref[0], preferred_element_type=F32)
    merged = gc_ref[...].astype(F32) + sg_ref[...].astype(F32) * y_mla
    x2 = x_ref[...] + jnp.dot(merged.astype(BF16), wo_ref[0], preferred_element_type=F32)
    return x2, _rms(x2, fn_ref[...])


def _merge_ffn_kernel(attn_ref, gc_ref, sg_ref, x_ref, wmo_ref, wo_ref, fn_ref, wg_ref, wu_ref, wd_ref, o_ref):
    x2, hn = _merge_core(attn_ref, gc_ref, sg_ref, x_ref, wmo_ref, wo_ref, fn_ref)
    hn = hn.astype(BF16)
    acc = x2
    for c in range(D_FF // FF_CHUNK):
        cs = slice(c * FF_CHUNK, (c + 1) * FF_CHUNK)
        g = jnp.dot(hn, wg_ref[:, cs], preferred_element_type=F32)
        u = jnp.dot(hn, wu_ref[:, cs], preferred_element_type=F32)
        a = (jax.nn.silu(g) * u).astype(BF16)
        acc = acc + jnp.dot(a, wd_ref[cs, :], preferred_element_type=F32)
    o_ref[...] = acc


def _merge_route_kernel(attn_ref, gc_ref, sg_ref, x_ref, wmo_ref, wo_ref, fn_ref, router_ref,
                        x2_out, hn_out, route_t_out):
    x2, hn = _merge_core(attn_ref, gc_ref, sg_ref, x_ref, wmo_ref, wo_ref, fn_ref)
    x2_out[...] = x2
    hb = hn.astype(BF16)
    hn_out[...] = hb
    hl = (hn - hb.astype(F32)).astype(BF16)
    r = router_ref[...]
    rh = r.astype(BF16)
    rl = (r - rh.astype(F32)).astype(BF16)
    logits = _dot_nt(rh, hb) + _dot_nt(rh, hl) + _dot_nt(rl, hb)
    erow = lax.broadcasted_iota(jnp.int32, logits.shape, 0)
    logits = jnp.where(erow < N_EXPERTS, logits, -jnp.inf)
    m1 = jnp.max(logits, axis=0, keepdims=True)
    i1 = jnp.min(jnp.where(logits == m1, erow, E_PAD), axis=0, keepdims=True)
    rest = jnp.where(erow == i1, -jnp.inf, logits)
    m2 = jnp.max(rest, axis=0, keepdims=True)
    i2 = jnp.min(jnp.where(rest == m2, erow, E_PAD), axis=0, keepdims=True)
    t = jnp.exp(m2 - m1)
    w1 = 1.0 / (1.0 + t)
    w2 = t * w1
    route_t = jnp.where(erow == 0, i1.astype(F32),
                        jnp.where(erow == 1, i2.astype(F32), jnp.where(erow == 2, w1, jnp.where(erow == 3, w2, 0.0))))
    route_t_out[...] = route_t[0:SUBLANES, :]


def _merge(layer, attn, gc, sg, x, wmo, wo, fn, router=None, ffn=None):
    tm = TM_MERGE
    row = lambda n: pl.BlockSpec((tm, n), lambda i: (i, 0))
    in_specs = [row(N_HEADS * V_HEAD), row(D_MODEL), row(D_MODEL), row(D_MODEL),
                _layer_spec(layer, (N_HEADS * V_HEAD, D_MODEL)), _layer_spec(layer, (D_MODEL, D_MODEL)),
                _const_spec((1, D_MODEL))]
    args = [attn, gc, sg, x, wmo, wo, fn]
    if router is None:
        kern = _merge_ffn_kernel
        in_specs += [_const_spec((D_MODEL, D_FF)), _const_spec((D_MODEL, D_FF)), _const_spec((D_FF, D_MODEL))]
        args += list(ffn)
        out_specs = row(D_MODEL)
        out_shape = jax.ShapeDtypeStruct((TOKENS, D_MODEL), F32)
        name = "merge_ffn"
    else:
        kern = _merge_route_kernel
        in_specs.append(_const_spec((E_PAD, D_MODEL)))
        args.append(router)
        out_specs = [row(D_MODEL), row(D_MODEL), pl.BlockSpec((SUBLANES, tm), lambda i: (0, i))]
        out_shape = [jax.ShapeDtypeStruct((TOKENS, D_MODEL), F32), jax.ShapeDtypeStruct((TOKENS, D_MODEL), BF16),
                     jax.ShapeDtypeStruct((SUBLANES, TOKENS), F32)]
        name = "merge_route"
    return pl.pallas_call(
        kern, grid=(TOKENS // tm,), in_specs=in_specs, out_specs=out_specs, out_shape=out_shape,
        compiler_params=pltpu.CompilerParams(dimension_semantics=("arbitrary",), vmem_limit_bytes=VMEM_LIMIT),
        name=name,
    )(*args)


def _pieces(n, sizes):
    return [((n & z) != 0, n & ~(2 * z - 1), z) for z in sizes]


def _aligned(rows, z):
    return pl.ds(rows if isinstance(rows, int) else pl.multiple_of(rows, SUBLANES), z)


def _start_slab_copies(tile, base_ref, offp_ref, np_ref, make_copy):
    for e in range(N_EXPERTS):
        n = np_ref[tile * N_EXPERTS + e]
        src = offp_ref[tile * N_EXPERTS + e]
        dst = base_ref[tile * N_EXPERTS + e]
        for pred, off, z in _pieces(n, SLAB_PIECES):
            @pl.when(pred)
            def _(off=off, z=z):
                make_copy(src + off, dst + off, z).start()


def _wait_rows(total, make_copy):
    for pred, _, z in _pieces(total, WAIT_PIECES):
        @pl.when(pred)
        def _(z=z):
            make_copy(0, 0, z).wait()


def _tile_placement(rt_ref, offp_ref, i):
    e0 = rt_ref[0:1, :].astype(jnp.int32)
    e1 = rt_ref[1:2, :].astype(jnp.int32)
    eid = lax.broadcasted_iota(jnp.int32, (E_PAD, TT), 0)
    sel = jnp.where(e0 == eid, 1.0, jnp.where(e1 == eid, 1.0, 0.0)).astype(BF16)
    r = lax.broadcasted_iota(jnp.int32, (TT, TT), 0)
    c = lax.broadcasted_iota(jnp.int32, (TT, TT), 1)
    earlier = jnp.where(r < c, 1.0, 0.0).astype(BF16)
    rank = jnp.dot(sel, earlier, preferred_element_type=F32)
    ecol = lax.broadcasted_iota(jnp.int32, (E_PAD, 1), 0)
    offp = jnp.zeros((E_PAD, 1), F32)
    for e in range(N_EXPERTS):
        offp = jnp.where(ecol == e, offp_ref[i * N_EXPERTS + e].astype(F32), offp)
    slot_all = rank + offp
    slot0 = jnp.sum(jnp.where(e0 == eid, slot_all, 0.0), axis=0, keepdims=True)
    slot1 = jnp.sum(jnp.where(e1 == eid, slot_all, 0.0), axis=0, keepdims=True)
    sidx = lax.broadcasted_iota(jnp.int32, (SLOTS, TT), 0).astype(F32)
    place = jnp.where(sidx == slot0, 1.0, jnp.where(sidx == slot1, 1.0, 0.0)).astype(BF16)
    return sidx, slot0, slot1, place


def _dispatch_kernel(base_ref, offp_ref, np_ref, tot_ref, tail_ref, hn_ref, rt_ref, xs_hbm, xs_scr, zbuf, sem, zsem):
    i = pl.program_id(0)
    slot = i % 2

    def copy(buf_slot):
        return lambda s, d, z: pltpu.make_async_copy(xs_scr.at[buf_slot, _aligned(s, z), :],
                                                     xs_hbm.at[_aligned(d, z), :], sem.at[buf_slot])

    def zero_copy(d, z):
        return pltpu.make_async_copy(zbuf.at[pl.ds(0, z), :], xs_hbm.at[_aligned(d, z), :], zsem)

    @pl.when(i == 0)
    def _():
        zbuf[...] = jnp.zeros_like(zbuf)
        for phase in ("start", "wait"):
            for e in range(N_EXPERTS):
                t0 = tail_ref[e]
                n = tail_ref[N_EXPERTS + e]
                full = lax.shift_right_logical(n, jnp.int32(TM_MOE.bit_length() - 1))

                def chunk(c, carry, t0=t0, phase=phase):
                    cp = zero_copy(t0 + c * TM_MOE, TM_MOE)
                    cp.start() if phase == "start" else cp.wait()
                    return carry

                lax.fori_loop(0, full, chunk, 0)
                rest = n & (TM_MOE - 1)
                for pred, off, z in _pieces(rest, TAIL_PIECES):
                    @pl.when(pred)
                    def _(off=off, z=z, t0=t0, full=full, phase=phase):
                        cp = zero_copy(t0 + full * TM_MOE + off, z)
                        cp.start() if phase == "start" else cp.wait()

    sidx, slot0, slot1, place = _tile_placement(rt_ref, offp_ref, i)
    xs_scr[slot, :, 0:D_MODEL] = jnp.dot(place, hn_ref[...], preferred_element_type=F32)
    w_slot = jnp.sum(jnp.where(sidx == slot0, rt_ref[2:3, :], jnp.where(sidx == slot1, rt_ref[3:4, :], 0.0)),
                     axis=1, keepdims=True)
    xs_scr[slot, :, D_MODEL:] = jnp.broadcast_to(w_slot, (SLOTS, LANES))

    _start_slab_copies(i, base_ref, offp_ref, np_ref, copy(slot))

    @pl.when(i > 0)
    def _():
        _wait_rows(tot_ref[i - 1], copy(1 - slot))

    @pl.when(i == pl.num_programs(0) - 1)
    def _():
        _wait_rows(tot_ref[i], copy(slot))


def _dispatch(tables, hn, route_t):
    grid_spec = pltpu.PrefetchScalarGridSpec(
        num_scalar_prefetch=5,
        grid=(N_TT,),
        in_specs=[pl.BlockSpec((TT, D_MODEL), lambda i, *_: (i, 0)),
                  pl.BlockSpec((SUBLANES, TT), lambda i, *_: (0, i))],
        out_specs=pl.BlockSpec(memory_space=pl.ANY),
        scratch_shapes=[pltpu.VMEM((2, SLOTS, XS_WIDTH), F32), pltpu.VMEM((TM_MOE, XS_WIDTH), F32),
                        pltpu.SemaphoreType.DMA((2,)), pltpu.SemaphoreType.DMA],
    )
    return pl.pallas_call(
        _dispatch_kernel,
        grid_spec=grid_spec,
        out_shape=jax.ShapeDtypeStruct((P_PAD, XS_WIDTH), F32),
        compiler_params=pltpu.CompilerParams(dimension_semantics=("arbitrary",), vmem_limit_bytes=VMEM_LIMIT),
        name="moe_dispatch",
    )(tables["base"], tables["offp"], tables["np"], tables["total"], tables["tail"], hn, route_t)


def _moe_kernel(te_ref, nv_ref, nxt_ref, par_ref, first_ref, last_ref, x_ref, wg_hbm, wu_hbm, wd_hbm, y_ref,
                stage_g, stage_u, stage_d, wgu_scr, wd_scr, sem):
    i = pl.program_id(0)
    valid = i < nv_ref[0]
    nxt = nxt_ref[i]

    def fetch(e):
        return [pltpu.make_async_copy(wg_hbm.at[e], stage_g, sem.at[0]),
                pltpu.make_async_copy(wu_hbm.at[e], stage_u, sem.at[1]),
                pltpu.make_async_copy(wd_hbm.at[e], stage_d, sem.at[2])]

    def round_into(s):
        wgu_scr[s, :, 0:D_FF_EXPERT] = stage_g[...].astype(BF16)
        wgu_scr[s, :, D_FF_EXPERT:] = stage_u[...].astype(BF16)
        wd_scr[s] = stage_d[...].astype(BF16)

    @pl.when(i == 0)
    def _():
        for cp in fetch(te_ref[0]):
            cp.start()
        for cp in fetch(te_ref[0]):
            cp.wait()
        round_into(par_ref[0])

    @pl.when(jnp.logical_and(valid, jnp.logical_and(first_ref[i] == 1, nxt >= 0)))
    def _():
        for cp in fetch(nxt):
            cp.start()

    @pl.when(valid)
    def _():
        s = par_ref[i]
        x = x_ref[:, 0:D_MODEL].astype(BF16)
        h = jnp.dot(x, wgu_scr[s], preferred_element_type=F32)
        a = (jax.nn.silu(h[:, :D_FF_EXPERT]) * h[:, D_FF_EXPERT:]).astype(BF16)
        y_ref[...] = jnp.dot(a, wd_scr[s], preferred_element_type=F32) * x_ref[:, D_MODEL:D_MODEL + 1]

    @pl.when(jnp.logical_and(valid, jnp.logical_and(last_ref[i] == 1, nxt >= 0)))
    def _():
        for cp in fetch(nxt):
            cp.wait()
        round_into(1 - par_ref[i])

    @pl.when(jnp.logical_not(valid))
    def _():
        y_ref[...] = jnp.zeros_like(y_ref)


def _moe(sched, xs, wg, wu, wd):
    grid_spec = pltpu.PrefetchScalarGridSpec(
        num_scalar_prefetch=6,
        grid=(N_MOE_TILES,),
        in_specs=[pl.BlockSpec((TM_MOE, XS_WIDTH), lambda i, *_: (i, 0)),
                  pl.BlockSpec(memory_space=pl.ANY), pl.BlockSpec(memory_space=pl.ANY),
                  pl.BlockSpec(memory_space=pl.ANY)],
        out_specs=pl.BlockSpec((TM_MOE, D_MODEL), lambda i, *_: (i, 0)),
        scratch_shapes=[pltpu.VMEM((D_MODEL, D_FF_EXPERT), F32), pltpu.VMEM((D_MODEL, D_FF_EXPERT), F32),
                        pltpu.VMEM((D_FF_EXPERT, D_MODEL), F32),
                        pltpu.VMEM((2, D_MODEL, 2 * D_FF_EXPERT), BF16), pltpu.VMEM((2, D_FF_EXPERT, D_MODEL), BF16),
                        pltpu.SemaphoreType.DMA((3,))],
    )
    return pl.pallas_call(
        _moe_kernel,
        grid_spec=grid_spec,
        out_shape=jax.ShapeDtypeStruct((P_PAD, D_MODEL), F32),
        compiler_params=pltpu.CompilerParams(dimension_semantics=("arbitrary",), vmem_limit_bytes=VMEM_LIMIT),
        name="moe_experts",
    )(sched["expert"], sched["n_valid"], sched["next"], sched["parity"], sched["first"], sched["last"],
      xs, wg, wu, wd)


def _combine_kernel(base_ref, offp_ref, np_ref, tot_ref, x_ref, route_ref, y_hbm, fn_ref, o_ref, ybuf, sem):
    i = pl.program_id(0)
    slot = i % 2

    def copy(buf_slot):
        return lambda s, d, z: pltpu.make_async_copy(y_hbm.at[_aligned(d, z), :],
                                                     ybuf.at[buf_slot, _aligned(s, z), :], sem.at[buf_slot])

    @pl.when(i == 0)
    def _():
        ybuf[...] = jnp.zeros_like(ybuf)
        _start_slab_copies(0, base_ref, offp_ref, np_ref, copy(0))

    @pl.when(i + 1 < pl.num_programs(0))
    def _():
        _start_slab_copies(i + 1, base_ref, offp_ref, np_ref, copy(1 - slot))

    _, _, _, place = _tile_placement(route_ref, offp_ref, i)

    _wait_rows(tot_ref[i], copy(slot))
    y = ybuf[slot]
    yh = y.astype(BF16)
    yl = (y - yh.astype(F32)).astype(BF16)
    tn = (((0,), (0,)), ((), ()))
    moe = (lax.dot_general(place, yh, tn, preferred_element_type=F32)
           + lax.dot_general(place, yl, tn, preferred_element_type=F32))
    o_ref[...] = _rms(x_ref[...] + moe, fn_ref[...])


def _combine(tables, x2, route_t, y_sorted, fn):
    grid_spec = pltpu.PrefetchScalarGridSpec(
        num_scalar_prefetch=4,
        grid=(N_TT,),
        in_specs=[pl.BlockSpec((TT, D_MODEL), lambda i, *_: (i, 0)),
                  pl.BlockSpec((SUBLANES, TT), lambda i, *_: (0, i)),
                  pl.BlockSpec(memory_space=pl.ANY),
                  pl.BlockSpec((1, D_MODEL), lambda i, *_: (0, 0))],
        out_specs=pl.BlockSpec((TT, D_MODEL), lambda i, *_: (i, 0)),
        scratch_shapes=[pltpu.VMEM((2, SLOTS, D_MODEL), F32), pltpu.SemaphoreType.DMA((2,))],
    )
    return pl.pallas_call(
        _combine_kernel,
        grid_spec=grid_spec,
        out_shape=jax.ShapeDtypeStruct((TOKENS, D_MODEL), F32),
        compiler_params=pltpu.CompilerParams(dimension_semantics=("arbitrary",), vmem_limit_bytes=VMEM_LIMIT),
        name="moe_combine",
    )(tables["base"], tables["offp"], tables["np"], tables["total"], x2, route_t, y_sorted, fn)


def _routing_tables(route_t):
    experts = route_t[:TOP_K, :].astype(jnp.int32)
    eid = jnp.arange(N_EXPERTS, dtype=jnp.int32)[:, None, None]
    chosen = jnp.sum((experts[None, :, :] == eid).astype(jnp.int32), axis=1)
    cnt = jnp.sum(chosen.reshape(N_EXPERTS, N_TT, TT), axis=2).T
    npad = (cnt + SUBLANES - 1) // SUBLANES * SUBLANES
    offp = jnp.cumsum(npad, axis=1) - npad
    total = jnp.sum(npad, axis=1)
    group = jnp.sum(npad, axis=0)
    group_pad = (group + TM_MOE - 1) // TM_MOE * TM_MOE
    group_end = jnp.cumsum(group_pad)
    group_start = group_end - group_pad
    base = group_start[None, :] + jnp.cumsum(npad, axis=0) - npad
    tail_start = group_start + group
    is_last = jnp.arange(N_EXPERTS) == N_EXPERTS - 1
    tail_len = jnp.where(is_last, P_PAD - tail_start, group_pad - group)
    tile_start = jnp.arange(N_MOE_TILES, dtype=jnp.int32) * TM_MOE
    tile_expert = jnp.minimum(jnp.sum((tile_start[:, None] >= group_end[None, :]).astype(jnp.int32), axis=1),
                              N_EXPERTS - 1)
    ids = jnp.arange(N_EXPERTS, dtype=jnp.int32)
    nonempty = group_pad > 0
    ordinal = jnp.cumsum(nonempty.astype(jnp.int32)) - 1
    later = jnp.logical_and(nonempty[None, :], ids[None, :] > ids[:, None])
    next_expert = jnp.min(jnp.where(later, ids[None, :], N_EXPERTS), axis=1)
    next_expert = jnp.where(next_expert == N_EXPERTS, -1, next_expert)
    is_tile_expert = tile_expert[:, None] == ids[None, :]
    per_tile = lambda table: jnp.sum(jnp.where(is_tile_expert, table[None, :], 0), axis=1)
    sched = {"expert": tile_expert, "n_valid": (group_end[-1] // TM_MOE).reshape(1),
             "next": per_tile(next_expert), "parity": per_tile(ordinal) % 2,
             "first": tile_start == per_tile(group_start),
             "last": tile_start + TM_MOE == per_tile(group_end)}
    tables = {"base": base.reshape(-1), "offp": offp.reshape(-1), "np": npad.reshape(-1), "total": total,
              "tail": jnp.concatenate([tail_start, tail_len])}
    as_i32 = lambda d: {k: v.astype(jnp.int32) for k, v in d.items()}
    return as_i32(tables), as_i32(sched)


WIN_ROWS = 256


def _win_layout_kernel(wt_ref, o_ref):
    wt = wt_ref[0]
    cols = wt.shape[1]
    kr_tile = jnp.concatenate([jnp.zeros((QK_NOPE, cols), F32), wt[REF_KR:REF_GATE, :],
                               jnp.zeros((HEAD_PAD - QK_DIM, cols), F32)], axis=0)
    conv_rows = [wt[part * D_CONV + c * CONV_CHUNK:part * D_CONV + (c + 1) * CONV_CHUNK, :]
                 for c in range(D_CONV // CONV_CHUNK) for part in range(3)]
    grouped = jnp.concatenate(conv_rows + [wt[REF_GATE:, :], wt[REF_CQ:REF_KR, :], kr_tile], axis=0)
    o_ref[0] = grouped.T.astype(BF16)


def _win_layout(w_in):
    wt = jnp.swapaxes(w_in, 1, 2)
    d_in = wt.shape[1]
    return pl.pallas_call(
        _win_layout_kernel,
        grid=(DEPTH, D_MODEL // WIN_ROWS),
        in_specs=[pl.BlockSpec((1, d_in, WIN_ROWS), lambda l, r: (l, 0, r))],
        out_specs=pl.BlockSpec((1, WIN_ROWS, D_IN_PAD), lambda l, r: (l, r, 0)),
        out_shape=jax.ShapeDtypeStruct((DEPTH, D_MODEL, D_IN_PAD), BF16),
        compiler_params=pltpu.CompilerParams(dimension_semantics=("arbitrary",) * 2, vmem_limit_bytes=VMEM_LIMIT),
        name="win_layout",
    )(wt)


def _small_layout_kernel(wuq_ref, wukv_ref, wco_ref, wmo_ref, wo_ref,
                         wq_out, wuk_out, wuv_out, wco_out, wmo_out, wo_out):
    wq_out[0] = wuq_ref[0].T.astype(BF16)
    wkv = wukv_ref[0]
    keys, values = [], []
    for h in range(N_HEADS):
        c0 = h * (QK_NOPE + V_HEAD)
        keys.append(wkv[:, c0:c0 + QK_NOPE])
        values.append(wkv[:, c0 + QK_NOPE:c0 + QK_NOPE + V_HEAD])
    wuk_out[0] = jnp.concatenate(keys, axis=1).astype(BF16)
    wuv_out[0] = jnp.concatenate(values, axis=1).T.astype(BF16)
    wco_out[0] = wco_ref[0].astype(BF16)
    wmo_out[0] = wmo_ref[0].astype(BF16)
    wo_out[0] = wo_ref[0].astype(BF16)


def _small_layout(w_uq, w_ukv, w_conv_out, w_mla_out, w_o):
    layer = lambda *shape: pl.BlockSpec((1,) + shape, lambda l: (l, 0, 0))
    shapes = [(N_HEADS * QK_DIM, Q_LORA), (KV_LORA, N_HEADS * QK_NOPE), (N_HEADS * V_HEAD, KV_LORA),
              (D_CONV, D_MODEL), (N_HEADS * V_HEAD, D_MODEL), (D_MODEL, D_MODEL)]
    return pl.pallas_call(
        _small_layout_kernel,
        grid=(DEPTH,),
        in_specs=[layer(*w.shape[1:]) for w in (w_uq, w_ukv, w_conv_out, w_mla_out, w_o)],
        out_specs=[layer(*s) for s in shapes],
        out_shape=[jax.ShapeDtypeStruct((DEPTH,) + s, BF16) for s in shapes],
        compiler_params=pltpu.CompilerParams(dimension_semantics=("arbitrary",), vmem_limit_bytes=VMEM_LIMIT),
        name="small_layout",
    )(w_uq, w_ukv, w_conv_out, w_mla_out, w_o)


def kernel(x, positions, attn_norm, w_in, conv_w, w_conv_out, q_norm, w_uq, kv_norm, w_ukv, w_mla_out, w_o, ffn_norm,
           w_gate, w_up, w_down, router, w_gate_e, w_up_e, w_down_e, final_norm):
    assert x.shape == (BATCH, SEQ, D_MODEL) and positions.shape == (BATCH, SEQ)
    cos_t, sin_t = _rope_tables(positions)
    xt = x.reshape(TOKENS, D_MODEL)
    out = None
    win = _win_layout(w_in)
    wq, wuk, wuv, wco, wmo, wo = _small_layout(w_uq, w_ukv, w_conv_out, w_mla_out, w_o)
    for l in range(DEPTH):
        q, k, v, gc, sg = _front(l, xt, attn_norm[l].reshape(1, D_MODEL), win, conv_w[l], wco,
                                 q_norm[l].reshape(1, Q_LORA), wq, kv_norm[l].reshape(1, KV_LORA), wuk, wuv,
                                 cos_t, sin_t)
        attn = _attention(q, k, v)
        fn = ffn_norm[l].reshape(1, D_MODEL)
        if l % 2 == 0:
            j = l // 2
            xt = _merge(l, attn, gc, sg, xt, wmo, wo, fn,
                        ffn=(w_gate[j].astype(BF16), w_up[j].astype(BF16), w_down[j].astype(BF16)))
        else:
            j = l // 2
            router_pad = jnp.pad(router[j].T, ((0, E_PAD - N_EXPERTS), (0, 0)))
            x2, hn, route_t = _merge(l, attn, gc, sg, xt, wmo, wo, fn, router_pad)
            tables, sched = _routing_tables(route_t)
            xs = _dispatch(tables, hn, route_t)
            y_sorted = _moe(sched, xs, w_gate_e[j], w_up_e[j], w_down_e[j])
            out = _combine(tables, x2, route_t, y_sorted, final_norm.reshape(1, D_MODEL))
    return out.reshape(BATCH, SEQ, D_MODEL)
```

```python
import math

import jax
import jax.numpy as jnp
from jax import lax
from jax.experimental import pallas as pl
from jax.experimental.pallas import tpu as pltpu

F32 = jnp.float32
BF16 = jnp.bfloat16

D_MODEL = 1024
BATCH = 8
SEQ = 2048
TOKENS = BATCH * SEQ
DEPTH = 2
D_CONV = 512
CONV_WIDTH = 3
N_HEADS = 8
QK_NOPE = 64
QK_ROPE = 32
HALF_ROPE = QK_ROPE // 2
V_HEAD = 64
Q_LORA = 384
KV_LORA = 256
ROPE_THETA = 10000.0
D_FF = 2816
N_EXPERTS = 8
TOP_K = 2
D_FF_EXPERT = 1408
EPS = 1e-6

LANES = 128
SUBLANES = 8
BF16_SUBLANES = 16
MXU_TILE = 256
VMEM_LIMIT = 56 * 1024 * 1024

HEAD_PAD = LANES
QK_DIM = QK_NOPE + QK_ROPE
Q_SCALE = (1.0 / math.sqrt(QK_DIM)) * math.log2(math.e)
NEG_BIG = -1e30

OFF_BCU = 0
OFF_GATE = 3 * D_CONV
OFF_SMALL = OFF_GATE + 2 * D_MODEL
D_IN_PAD = OFF_SMALL + Q_LORA + KV_LORA + LANES
REF_CQ = 3 * D_CONV
REF_KR = REF_CQ + Q_LORA + KV_LORA
REF_GATE = REF_KR + QK_ROPE

CONV_CHUNK = MXU_TILE
CONV_HIST = SUBLANES
assert CONV_WIDTH - 1 <= CONV_HIST
TM_FRONT = 512
TM_MERGE = 1024
TQ = MXU_TILE
TK = 128
ATTN_PAIRS = 2
ATTN_LOOKAHEAD = 3
V_ROWS = V_HEAD + BF16_SUBLANES
FF_CHUNK = MXU_TILE
assert D_FF % FF_CHUNK == 0 and D_CONV % CONV_CHUNK == 0 and TQ % TK == 0

TM_MOE = 512
EXPERT_CHUNKS = tuple((lo, min(MXU_TILE, D_FF_EXPERT - lo)) for lo in range(0, D_FF_EXPERT, MXU_TILE))
TT = 512
N_TT = TOKENS // TT
E_PAD = BF16_SUBLANES
XS_WIDTH = D_MODEL + LANES
N_PAIRS = TOKENS * TOP_K
SLOTS = TT * TOP_K + N_EXPERTS * SUBLANES
_MAX_ROWS = N_PAIRS + N_TT * N_EXPERTS * (SUBLANES - 1) + N_EXPERTS * (TM_MOE - SUBLANES)
N_MOE_TILES = -(-_MAX_ROWS // TM_MOE)
P_PAD = N_MOE_TILES * TM_MOE


def _halvings(n):
    return tuple(n >> s for s in range((n // SUBLANES).bit_length()))


SLAB_PIECES = _halvings(TT)
WAIT_PIECES = _halvings(2 * TT)
TAIL_PIECES = _halvings(TM_MOE // 2)
assert SLOTS < 2 * WAIT_PIECES[0]


def _rms(x, g):
    return x * lax.rsqrt(jnp.mean(x * x, axis=-1, keepdims=True) + EPS) * g


def _const_spec(shape):
    nd = len(shape)
    return pl.BlockSpec(shape, lambda *_: (0,) * nd, pipeline_mode=pl.Buffered(1))


def _layer_spec(layer, shape):
    return pl.BlockSpec((1,) + tuple(shape), lambda *_: (layer,) + (0,) * len(shape), pipeline_mode=pl.Buffered(1))


def _trig_kernel(pos_ref, invf_ref, cos_ref, sin_ref):
    ang = invf_ref[...] * pos_ref[...].astype(F32)
    cos_ref[...] = jnp.cos(ang)
    sin_ref[...] = jnp.sin(ang)


def _rope_tables(positions):
    inv_freq = ROPE_THETA ** (-jnp.arange(0, QK_ROPE, 2, dtype=F32) / QK_ROPE)
    return pl.pallas_call(
        _trig_kernel,
        out_shape=(jax.ShapeDtypeStruct((HALF_ROPE, TOKENS), F32),) * 2,
        name="rope_trig",
    )(positions.reshape(1, TOKENS), inv_freq.reshape(HALF_ROPE, 1))


def _dot_nt(a, b):
    return lax.dot_general(a, b, (((1,), (1,)), ((), ())), preferred_element_type=F32)


def _front_kernel(x_ref, an_ref, win_ref, cw_ref, wco_ref, qn_ref, wq_ref, kvn_ref, wuk_ref, wuv_ref,
                  cost_ref, sint_ref, q_out, k_out, v_out, gc_out, sg_out, cu_scr):
    tm = x_ref.shape[0]
    i = pl.program_id(0)
    xn = _rms(x_ref[...], an_ref[...]).astype(BF16)

    def proj(a, b):
        return jnp.dot(xn, win_ref[0, :, a:b], preferred_element_type=F32)

    small = proj(OFF_SMALL, D_IN_PAD)

    cq = cost_ref[...] * Q_SCALE
    sq = sint_ref[...] * Q_SCALE
    cqn = _rms(small[:, :Q_LORA], qn_ref[...]).astype(BF16)
    qt = _dot_nt(wq_ref[0], cqn)
    zero_pad = jnp.zeros((HEAD_PAD - QK_DIM, tm), BF16)
    for h in range(N_HEADS):
        src = h * QK_DIM
        dst = h * HEAD_PAD
        x1 = qt[src + QK_NOPE:src + QK_NOPE + HALF_ROPE, :]
        x2 = qt[src + QK_NOPE + HALF_ROPE:src + QK_DIM, :]
        q_out[dst:dst + QK_NOPE, :] = (qt[src:src + QK_NOPE, :] * Q_SCALE).astype(BF16)
        q_out[dst + QK_NOPE:dst + QK_NOPE + HALF_ROPE, :] = (x1 * cq - x2 * sq).astype(BF16)
        q_out[dst + QK_NOPE + HALF_ROPE:dst + QK_DIM, :] = (x2 * cq + x1 * sq).astype(BF16)
        q_out[dst + QK_DIM:dst + HEAD_PAD, :] = zero_pad

    cos_t = cost_ref[...]
    sin_t = sint_ref[...]
    z_nope = jnp.zeros((QK_NOPE, tm), F32)
    z_rope = jnp.zeros((HALF_ROPE, tm), F32)
    z_pad = jnp.zeros((HEAD_PAD - QK_DIM, tm), F32)
    ck = jnp.concatenate([z_nope, cos_t, cos_t, z_pad], axis=0).T
    sk_x1 = jnp.concatenate([z_nope, -sin_t, z_rope, z_pad], axis=0).T
    sk_x2 = jnp.concatenate([z_nope, z_rope, sin_t, z_pad], axis=0).T
    kr = small[:, Q_LORA + KV_LORA:]
    kpe = (kr * ck + pltpu.roll(kr, HEAD_PAD - HALF_ROPE, axis=1) * sk_x1 + pltpu.roll(kr, HALF_ROPE, axis=1) * sk_x2)
    ckvn = _rms(small[:, Q_LORA:Q_LORA + KV_LORA], kvn_ref[...]).astype(BF16)
    kn = jnp.dot(ckvn, wuk_ref[0], preferred_element_type=F32)
    low = lax.broadcasted_iota(jnp.int32, (1, HEAD_PAD), 1) < QK_NOPE
    for h in range(0, N_HEADS, 2):
        two = kn[:, h * QK_NOPE:(h + 2) * QK_NOPE]
        k_out[:, h * HEAD_PAD:(h + 1) * HEAD_PAD] = (jnp.where(low, two, 0.0) + kpe).astype(BF16)
        swapped = pltpu.roll(two, QK_NOPE, axis=1)
        k_out[:, (h + 1) * HEAD_PAD:(h + 2) * HEAD_PAD] = (jnp.where(low, swapped, 0.0) + kpe).astype(BF16)
    vt = _dot_nt(wuv_ref[0], ckvn).astype(BF16)
    sub = lax.broadcasted_iota(jnp.int32, (V_ROWS - V_HEAD, TK), 0)
    ones_rows = jnp.where(sub == 0, 1.0, 0.0).astype(BF16)
    for c in range(tm // TK):
        for h in range(N_HEADS):
            v_out[c, h * V_ROWS:h * V_ROWS + V_HEAD, :] = vt[h * V_HEAD:(h + 1) * V_HEAD, c * TK:(c + 1) * TK]
            v_out[c, h * V_ROWS + V_HEAD:(h + 1) * V_ROWS, :] = ones_rows

    @pl.when(i % (SEQ // tm) == 0)
    def _():
        cu_scr[0:CONV_HIST, :] = jnp.zeros((CONV_HIST, D_CONV), F32)

    cw = cw_ref[...]
    y_conv = None
    for c in range(D_CONV // CONV_CHUNK):
        ch = slice(c * CONV_CHUNK, (c + 1) * CONV_CHUNK)
        bcu = proj(OFF_BCU + 3 * c * CONV_CHUNK, OFF_BCU + 3 * (c + 1) * CONV_CHUNK)
        b_g = bcu[:, :CONV_CHUNK]
        cu = bcu[:, CONV_CHUNK:2 * CONV_CHUNK] * bcu[:, 2 * CONV_CHUNK:]
        cu_scr[CONV_HIST:CONV_HIST + tm, ch] = cu
        conv = cu * cw[CONV_WIDTH - 1:CONV_WIDTH, ch]
        for tap in range(CONV_WIDTH - 1):
            back = CONV_WIDTH - 1 - tap
            conv = conv + cu_scr[CONV_HIST - back:CONV_HIST - back + tm, ch] * cw[tap:tap + 1, ch]
        cu_scr[0:CONV_HIST, ch] = cu[tm - CONV_HIST:, :]
        part = jnp.dot((b_g * conv).astype(BF16), wco_ref[0, ch, :], preferred_element_type=F32)
        y_conv = part if y_conv is None else y_conv + part

    gates = proj(OFF_GATE, OFF_SMALL)
    gc_out[...] = (jax.nn.sigmoid(gates[:, :D_MODEL]) * y_conv).astype(BF16)
    sg_out[...] = jax.nn.sigmoid(gates[:, D_MODEL:]).astype(BF16)


def _front(layer, x, an, win_all, cw, wco, qn, wq, kvn, wuk, wuv, cos_t, sin_t):
    tm = TM_FRONT
    row = lambda n: pl.BlockSpec((tm, n), lambda i: (i, 0))
    col = lambda n: pl.BlockSpec((n, tm), lambda i: (0, i))
    return pl.pallas_call(
        _front_kernel,
        grid=(TOKENS // tm,),
        in_specs=[row(D_MODEL), _const_spec((1, D_MODEL)), _layer_spec(layer, (D_MODEL, D_IN_PAD)),
                  _const_spec((CONV_WIDTH, D_CONV)), _layer_spec(layer, (D_CONV, D_MODEL)),
                  _const_spec((1, Q_LORA)), _layer_spec(layer, (N_HEADS * QK_DIM, Q_LORA)),
                  _const_spec((1, KV_LORA)), _layer_spec(layer, (KV_LORA, N_HEADS * QK_NOPE)),
                  _layer_spec(layer, (N_HEADS * V_HEAD, KV_LORA)),
                  col(HALF_ROPE), col(HALF_ROPE)],
        out_specs=[col(N_HEADS * HEAD_PAD), row(N_HEADS * HEAD_PAD),
                   pl.BlockSpec((tm // TK, N_HEADS * V_ROWS, TK), lambda i: (i, 0, 0)),
                   row(D_MODEL), row(D_MODEL)],
        out_shape=[jax.ShapeDtypeStruct((N_HEADS * HEAD_PAD, TOKENS), BF16),
                   jax.ShapeDtypeStruct((TOKENS, N_HEADS * HEAD_PAD), BF16),
                   jax.ShapeDtypeStruct((TOKENS // TK, N_HEADS * V_ROWS, TK), BF16),
                   jax.ShapeDtypeStruct((TOKENS, D_MODEL), BF16),
                   jax.ShapeDtypeStruct((TOKENS, D_MODEL), BF16)],
        scratch_shapes=[pltpu.VMEM((tm + CONV_HIST, D_CONV), F32)],
        compiler_params=pltpu.CompilerParams(dimension_semantics=("arbitrary",), vmem_limit_bytes=VMEM_LIMIT),
        name="front",
    )(x, an, win_all, cw, wco, qn, wq, kvn, wuk, wuv, cos_t, sin_t)


def _attn_items():
    items = []
    for pair in range(ATTN_PAIRS):
        for i in range(SEQ // TQ):
            for ks in range(0, (i + 1) * TQ, TK):
                items.append((pair, i, ks, TK, ks + TK == (i + 1) * TQ))
    return items


def _attn_kernel(qt_ref, k_ref, vt_ref, o_ref):
    row = lax.broadcasted_iota(jnp.int32, (TK, TQ), 0)
    col = lax.broadcasted_iota(jnp.int32, (TK, TQ), 1)

    def scores(item):
        pair, i, ks, n, _ = item
        sts = []
        for h in range(2 * pair, 2 * pair + 2):
            hs = slice(h * HEAD_PAD, (h + 1) * HEAD_PAD)
            st = jnp.dot(k_ref[ks:ks + n, hs], qt_ref[hs, i * TQ:(i + 1) * TQ], preferred_element_type=F32)
            if ks + n > i * TQ:
                st = jnp.where(row + (ks - i * TQ) <= col, st, NEG_BIG)
            sts.append(st)
        return sts

    def consume(item, sts, carry):
        pair, i, ks, n, _ = item
        stats = []
        for j in range(2):
            m, _ = carry[j]
            m_new = jnp.maximum(m, jnp.max(sts[j], axis=0, keepdims=True))
            stats.append((m_new, jnp.exp2(m - m_new), jnp.exp2(sts[j] - m_new).astype(BF16)))
        new = []
        for j in range(2):
            h = 2 * pair + j
            m_new, alpha, p = stats[j]
            vt = vt_ref[ks // TK, h * V_ROWS:(h + 1) * V_ROWS, ks % TK:ks % TK + n]
            new.append((m_new, alpha * carry[j][1] + jnp.dot(vt, p, preferred_element_type=F32)))
        return new

    items = _attn_items()
    init = [(jnp.full((1, TQ), NEG_BIG, F32), jnp.zeros((V_ROWS, TQ), F32)) for _ in range(2)]
    pending = [scores(it) for it in items[:ATTN_LOOKAHEAD]]
    carry = [init] * ATTN_PAIRS
    for t, item in enumerate(items):
        if t + ATTN_LOOKAHEAD < len(items):
            pending.append(scores(items[t + ATTN_LOOKAHEAD]))
        pair, i = item[0], item[1]
        carry[pair] = consume(item, pending.pop(0), carry[pair])
        if item[4]:
            out_t = jnp.concatenate([acc[0:V_HEAD] / acc[V_HEAD:V_HEAD + 1] for (_, acc) in carry[pair]], axis=0)
            o_ref[i * TQ:(i + 1) * TQ, pair * 2 * V_HEAD:(pair + 1) * 2 * V_HEAD] = out_t.T.astype(BF16)
            carry[pair] = init


def _attention(qt, k, vt):
    heads = 2 * ATTN_PAIRS
    return pl.pallas_call(
        _attn_kernel,
        grid=(BATCH, N_HEADS // heads),
        in_specs=[pl.BlockSpec((heads * HEAD_PAD, SEQ), lambda b, hp: (hp, b)),
                  pl.BlockSpec((SEQ, heads * HEAD_PAD), lambda b, hp: (b, hp)),
                  pl.BlockSpec((SEQ // TK, heads * V_ROWS, TK), lambda b, hp: (b, hp, 0))],
        out_specs=pl.BlockSpec((SEQ, heads * V_HEAD), lambda b, hp: (b, hp)),
        out_shape=jax.ShapeDtypeStruct((TOKENS, N_HEADS * V_HEAD), BF16),
        compiler_params=pltpu.CompilerParams(dimension_semantics=("arbitrary",) * 2, vmem_limit_bytes=VMEM_LIMIT),
        name="attention",
    )(qt, k, vt)


def _merge_core(attn_ref, gc_ref, sg_ref, x_ref, wmo_ref, wo_ref, fn_ref):
    y_mla = jnp.dot(attn_ref[...], wmo_ref[0], preferred_element_type=F32)
    merged = gc_ref[...].astype(F32) + sg_ref[...].astype(F32) * y_mla
    x2 = x_ref[...] + jnp.dot(merged.astype(BF16), wo_ref[0], preferred_element_type=F32)
    return x2, _rms(x2, fn_ref[...])


def _merge_ffn_kernel(attn_ref, gc_ref, sg_ref, x_ref, wmo_ref, wo_ref, fn_ref, wg_ref, wu_ref, wd_ref, o_ref):
    x2, hn = _merge_core(attn_ref, gc_ref, sg_ref, x_ref, wmo_ref, wo_ref, fn_ref)
    hn = hn.astype(BF16)
    acc = x2
    for c in range(D_FF // FF_CHUNK):
        cs = slice(c * FF_CHUNK, (c + 1) * FF_CHUNK)
        g = jnp.dot(hn, wg_ref[:, cs], preferred_element_type=F32)
        u = jnp.dot(hn, wu_ref[:, cs], preferred_element_type=F32)
        a = (jax.nn.silu(g) * u).astype(BF16)
        acc = acc + jnp.dot(a, wd_ref[cs, :], preferred_element_type=F32)
    o_ref[...] = acc


def _merge_route_kernel(attn_ref, gc_ref, sg_ref, x_ref, wmo_ref, wo_ref, fn_ref, router_ref,
                        x2_out, hn_out, route_t_out):
    x2, hn = _merge_core(attn_ref, gc_ref, sg_ref, x_ref, wmo_ref, wo_ref, fn_ref)
    x2_out[...] = x2
    hb = hn.astype(BF16)
    hn_out[...] = hb
    hl = (hn - hb.astype(F32)).astype(BF16)
    r = router_ref[...]
    rh = r.astype(BF16)
    rl = (r - rh.astype(F32)).astype(BF16)
    logits = _dot_nt(rh, hb) + _dot_nt(rh, hl) + _dot_nt(rl, hb)
    erow = lax.broadcasted_iota(jnp.int32, logits.shape, 0)
    logits = jnp.where(erow < N_EXPERTS, logits, -jnp.inf)
    m1 = jnp.max(logits, axis=0, keepdims=True)
    i1 = jnp.min(jnp.where(logits == m1, erow, E_PAD), axis=0, keepdims=True)
    rest = jnp.where(erow == i1, -jnp.inf, logits)
    m2 = jnp.max(rest, axis=0, keepdims=True)
    i2 = jnp.min(jnp.where(rest == m2, erow, E_PAD), axis=0, keepdims=True)
    t = jnp.exp(m2 - m1)
    w1 = 1.0 / (1.0 + t)
    w2 = t * w1
    route_t = jnp.where(erow == 0, i1.astype(F32),
                        jnp.where(erow == 1, i2.astype(F32), jnp.where(erow == 2, w1, jnp.where(erow == 3, w2, 0.0))))
    route_t_out[...] = route_t[0:SUBLANES, :]


def _merge(layer, attn, gc, sg, x, wmo, wo, fn, router=None, ffn=None):
    tm = TM_MERGE
    row = lambda n: pl.BlockSpec((tm, n), lambda i: (i, 0))
    in_specs = [row(N_HEADS * V_HEAD), row(D_MODEL), row(D_MODEL), row(D_MODEL),
                _layer_spec(layer, (N_HEADS * V_HEAD, D_MODEL)), _layer_spec(layer, (D_MODEL, D_MODEL)),
                _const_spec((1, D_MODEL))]
    args = [attn, gc, sg, x, wmo, wo, fn]
    if router is None:
        kern = _merge_ffn_kernel
        in_specs += [_const_spec((D_MODEL, D_FF)), _const_spec((D_MODEL, D_FF)), _const_spec((D_FF, D_MODEL))]
        args += list(ffn)
        out_specs = row(D_MODEL)
        out_shape = jax.ShapeDtypeStruct((TOKENS, D_MODEL), F32)
        name = "merge_ffn"
    else:
        kern = _merge_route_kernel
        in_specs.append(_const_spec((E_PAD, D_MODEL)))
        args.append(router)
        out_specs = [row(D_MODEL), row(D_MODEL), pl.BlockSpec((SUBLANES, tm), lambda i: (0, i))]
        out_shape = [jax.ShapeDtypeStruct((TOKENS, D_MODEL), F32), jax.ShapeDtypeStruct((TOKENS, D_MODEL), BF16),
                     jax.ShapeDtypeStruct((SUBLANES, TOKENS), F32)]
        name = "merge_route"
    return pl.pallas_call(
        kern, grid=(TOKENS // tm,), in_specs=in_specs, out_specs=out_specs, out_shape=out_shape,
        compiler_params=pltpu.CompilerParams(dimension_semantics=("arbitrary",), vmem_limit_bytes=VMEM_LIMIT),
        name=name,
    )(*args)


def _pieces(n, sizes):
    return [((n & z) != 0, n & ~(2 * z - 1), z) for z in sizes]


def _aligned(rows, z):
    return pl.ds(rows if isinstance(rows, int) else pl.multiple_of(rows, SUBLANES), z)


def _start_slab_copies(tile, base_ref, offp_ref, np_ref, make_copy):
    for e in range(N_EXPERTS):
        n = np_ref[tile * N_EXPERTS + e]
        src = offp_ref[tile * N_EXPERTS + e]
        dst = base_ref[tile * N_EXPERTS + e]
        for pred, off, z in _pieces(n, SLAB_PIECES):
            @pl.when(pred)
            def _(off=off, z=z):
                make_copy(src + off, dst + off, z).start()


def _wait_rows(total, make_copy):
    for pred, _, z in _pieces(total, WAIT_PIECES):
        @pl.when(pred)
        def _(z=z):
            make_copy(0, 0, z).wait()


def _tile_placement(rt_ref, offp_ref, i):
    e0 = rt_ref[0:1, :].astype(jnp.int32)
    e1 = rt_ref[1:2, :].astype(jnp.int32)
    eid = lax.broadcasted_iota(jnp.int32, (E_PAD, TT), 0)
    sel = jnp.where(e0 == eid, 1.0, jnp.where(e1 == eid, 1.0, 0.0)).astype(BF16)
    r = lax.broadcasted_iota(jnp.int32, (TT, TT), 0)
    c = lax.broadcasted_iota(jnp.int32, (TT, TT), 1)
    earlier = jnp.where(r < c, 1.0, 0.0).astype(BF16)
    rank = jnp.dot(sel, earlier, preferred_element_type=F32)
    ecol = lax.broadcasted_iota(jnp.int32, (E_PAD, 1), 0)
    offp = jnp.zeros((E_PAD, 1), F32)
    for e in range(N_EXPERTS):
        offp = jnp.where(ecol == e, offp_ref[i * N_EXPERTS + e].astype(F32), offp)
    slot_all = rank + offp
    slot0 = jnp.sum(jnp.where(e0 == eid, slot_all, 0.0), axis=0, keepdims=True)
    slot1 = jnp.sum(jnp.where(e1 == eid, slot_all, 0.0), axis=0, keepdims=True)
    sidx = lax.broadcasted_iota(jnp.int32, (SLOTS, TT), 0).astype(F32)
    place = jnp.where(sidx == slot0, 1.0, jnp.where(sidx == slot1, 1.0, 0.0)).astype(BF16)
    return sidx, slot0, slot1, place


def _dispatch_kernel(base_ref, offp_ref, np_ref, tot_ref, tail_ref, hn_ref, rt_ref, xs_hbm, xs_scr, zbuf, sem, zsem):
    i = pl.program_id(0)
    slot = i % 2

    def copy(buf_slot):
        return lambda s, d, z: pltpu.make_async_copy(xs_scr.at[buf_slot, _aligned(s, z), :],
                                                     xs_hbm.at[_aligned(d, z), :], sem.at[buf_slot])

    def zero_copy(d, z):
        return pltpu.make_async_copy(zbuf.at[pl.ds(0, z), :], xs_hbm.at[_aligned(d, z), :], zsem)

    @pl.when(i == 0)
    def _():
        zbuf[...] = jnp.zeros_like(zbuf)
        for phase in ("start", "wait"):
            for e in range(N_EXPERTS):
                t0 = tail_ref[e]
                n = tail_ref[N_EXPERTS + e]
                full = lax.shift_right_logical(n, jnp.int32(TM_MOE.bit_length() - 1))

                def chunk(c, carry, t0=t0, phase=phase):
                    cp = zero_copy(t0 + c * TM_MOE, TM_MOE)
                    cp.start() if phase == "start" else cp.wait()
                    return carry

                lax.fori_loop(0, full, chunk, 0)
                rest = n & (TM_MOE - 1)
                for pred, off, z in _pieces(rest, TAIL_PIECES):
                    @pl.when(pred)
                    def _(off=off, z=z, t0=t0, full=full, phase=phase):
                        cp = zero_copy(t0 + full * TM_MOE + off, z)
                        cp.start() if phase == "start" else cp.wait()

    sidx, slot0, slot1, place = _tile_placement(rt_ref, offp_ref, i)
    xs_scr[slot, :, 0:D_MODEL] = jnp.dot(place, hn_ref[...], preferred_element_type=F32)
    w_slot = jnp.sum(jnp.where(sidx == slot0, rt_ref[2:3, :], jnp.where(sidx == slot1, rt_ref[3:4, :], 0.0)),
                     axis=1, keepdims=True)
    xs_scr[slot, :, D_MODEL:] = jnp.broadcast_to(w_slot, (SLOTS, LANES))

    _start_slab_copies(i, base_ref, offp_ref, np_ref, copy(slot))

    @pl.when(i > 0)
    def _():
        _wait_rows(tot_ref[i - 1], copy(1 - slot))

    @pl.when(i == pl.num_programs(0) - 1)
    def _():
        _wait_rows(tot_ref[i], copy(slot))


def _dispatch(tables, hn, route_t):
    grid_spec = pltpu.PrefetchScalarGridSpec(
        num_scalar_prefetch=5,
        grid=(N_TT,),
        in_specs=[pl.BlockSpec((TT, D_MODEL), lambda i, *_: (i, 0)),
                  pl.BlockSpec((SUBLANES, TT), lambda i, *_: (0, i))],
        out_specs=pl.BlockSpec(memory_space=pl.ANY),
        scratch_shapes=[pltpu.VMEM((2, SLOTS, XS_WIDTH), F32), pltpu.VMEM((TM_MOE, XS_WIDTH), F32),
                        pltpu.SemaphoreType.DMA((2,)), pltpu.SemaphoreType.DMA],
    )
    return pl.pallas_call(
        _dispatch_kernel,
        grid_spec=grid_spec,
        out_shape=jax.ShapeDtypeStruct((P_PAD, XS_WIDTH), F32),
        compiler_params=pltpu.CompilerParams(dimension_semantics=("arbitrary",), vmem_limit_bytes=VMEM_LIMIT),
        name="moe_dispatch",
    )(tables["base"], tables["offp"], tables["np"], tables["total"], tables["tail"], hn, route_t)


def _moe_kernel(te_ref, nv_ref, nxt_ref, par_ref, first_ref, last_ref, x_ref, wg_hbm, wu_hbm, wd_hbm, y_ref,
                stage_g, stage_u, stage_d, wgu_scr, wd_scr, sem):
    i = pl.program_id(0)
    valid = i < nv_ref[0]
    nxt = nxt_ref[i]

    def fetch(e):
        return [pltpu.make_async_copy(wg_hbm.at[e], stage_g, sem.at[0]),
                pltpu.make_async_copy(wu_hbm.at[e], stage_u, sem.at[1]),
                pltpu.make_async_copy(wd_hbm.at[e], stage_d, sem.at[2])]

    def round_into(s):
        for lo, w in EXPERT_CHUNKS:
            wgu_scr[s, :, 2 * lo:2 * lo + w] = stage_g[:, lo:lo + w].astype(BF16)
            wgu_scr[s, :, 2 * lo + w:2 * lo + 2 * w] = stage_u[:, lo:lo + w].astype(BF16)
        wd_scr[s] = stage_d[...].astype(BF16)

    @pl.when(i == 0)
    def _():
        for cp in fetch(te_ref[0]):
            cp.start()
        for cp in fetch(te_ref[0]):
            cp.wait()
        round_into(par_ref[0])

    @pl.when(jnp.logical_and(valid, jnp.logical_and(first_ref[i] == 1, nxt >= 0)))
    def _():
        for cp in fetch(nxt):
            cp.start()

    @pl.when(valid)
    def _():
        s = par_ref[i]
        x = x_ref[:, 0:D_MODEL].astype(BF16)
        y = None
        for lo, w in EXPERT_CHUNKS:
            h = jnp.dot(x, wgu_scr[s, :, 2 * lo:2 * lo + 2 * w], preferred_element_type=F32)
            a = (jax.nn.silu(h[:, :w]) * h[:, w:]).astype(BF16)
            part = jnp.dot(a, wd_scr[s, lo:lo + w, :], preferred_element_type=F32)
            y = part if y is None else y + part
        y_ref[...] = y * x_ref[:, D_MODEL:D_MODEL + 1]

    @pl.when(jnp.logical_and(valid, jnp.logical_and(last_ref[i] == 1, nxt >= 0)))
    def _():
        for cp in fetch(nxt):
            cp.wait()
        round_into(1 - par_ref[i])

    @pl.when(jnp.logical_not(valid))
    def _():
        y_ref[...] = jnp.zeros_like(y_ref)


def _moe(sched, xs, wg, wu, wd):
    grid_spec = pltpu.PrefetchScalarGridSpec(
        num_scalar_prefetch=6,
        grid=(N_MOE_TILES,),
        in_specs=[pl.BlockSpec((TM_MOE, XS_WIDTH), lambda i, *_: (i, 0)),
                  pl.BlockSpec(memory_space=pl.ANY), pl.BlockSpec(memory_space=pl.ANY),
                  pl.BlockSpec(memory_space=pl.ANY)],
        out_specs=pl.BlockSpec((TM_MOE, D_MODEL), lambda i, *_: (i, 0)),
        scratch_shapes=[pltpu.VMEM((D_MODEL, D_FF_EXPERT), F32), pltpu.VMEM((D_MODEL, D_FF_EXPERT), F32),
                        pltpu.VMEM((D_FF_EXPERT, D_MODEL), F32),
                        pltpu.VMEM((2, D_MODEL, 2 * D_FF_EXPERT), BF16), pltpu.VMEM((2, D_FF_EXPERT, D_MODEL), BF16),
                        pltpu.SemaphoreType.DMA((3,))],
    )
    return pl.pallas_call(
        _moe_kernel,
        grid_spec=grid_spec,
        out_shape=jax.ShapeDtypeStruct((P_PAD, D_MODEL), F32),
        compiler_params=pltpu.CompilerParams(dimension_semantics=("arbitrary",), vmem_limit_bytes=VMEM_LIMIT),
        name="moe_experts",
    )(sched["expert"], sched["n_valid"], sched["next"], sched["parity"], sched["first"], sched["last"],
      xs, wg, wu, wd)


def _combine_kernel(base_ref, offp_ref, np_ref, tot_ref, x_ref, route_ref, y_hbm, fn_ref, o_ref, ybuf, sem):
    i = pl.program_id(0)
    slot = i % 2

    def copy(buf_slot):
        return lambda s, d, z: pltpu.make_async_copy(y_hbm.at[_aligned(d, z), :],
                                                     ybuf.at[buf_slot, _aligned(s, z), :], sem.at[buf_slot])

    @pl.when(i == 0)
    def _():
        ybuf[...] = jnp.zeros_like(ybuf)
        _start_slab_copies(0, base_ref, offp_ref, np_ref, copy(0))

    @pl.when(i + 1 < pl.num_programs(0))
    def _():
        _start_slab_copies(i + 1, base_ref, offp_ref, np_ref, copy(1 - slot))

    _, _, _, place = _tile_placement(route_ref, offp_ref, i)

    _wait_rows(tot_ref[i], copy(slot))
    y = ybuf[slot]
    yh = y.astype(BF16)
    yl = (y - yh.astype(F32)).astype(BF16)
    tn = (((0,), (0,)), ((), ()))
    moe = (lax.dot_general(place, yh, tn, preferred_element_type=F32)
           + lax.dot_general(place, yl, tn, preferred_element_type=F32))
    o_ref[...] = _rms(x_ref[...] + moe, fn_ref[...])


def _combine(tables, x2, route_t, y_sorted, fn):
    grid_spec = pltpu.PrefetchScalarGridSpec(
        num_scalar_prefetch=4,
        grid=(N_TT,),
        in_specs=[pl.BlockSpec((TT, D_MODEL), lambda i, *_: (i, 0)),
                  pl.BlockSpec((SUBLANES, TT), lambda i, *_: (0, i)),
                  pl.BlockSpec(memory_space=pl.ANY),
                  pl.BlockSpec((1, D_MODEL), lambda i, *_: (0, 0))],
        out_specs=pl.BlockSpec((TT, D_MODEL), lambda i, *_: (i, 0)),
        scratch_shapes=[pltpu.VMEM((2, SLOTS, D_MODEL), F32), pltpu.SemaphoreType.DMA((2,))],
    )
    return pl.pallas_call(
        _combine_kernel,
        grid_spec=grid_spec,
        out_shape=jax.ShapeDtypeStruct((TOKENS, D_MODEL), F32),
        compiler_params=pltpu.CompilerParams(dimension_semantics=("arbitrary",), vmem_limit_bytes=VMEM_LIMIT),
        name="moe_combine",
    )(tables["base"], tables["offp"], tables["np"], tables["total"], x2, route_t, y_sorted, fn)


def _routing_tables(route_t):
    experts = route_t[:TOP_K, :].astype(jnp.int32)
    eid = jnp.arange(N_EXPERTS, dtype=jnp.int32)[:, None, None]
    chosen = jnp.sum((experts[None, :, :] == eid).astype(jnp.int32), axis=1)
    cnt = jnp.sum(chosen.reshape(N_EXPERTS, N_TT, TT), axis=2).T
    npad = (cnt + SUBLANES - 1) // SUBLANES * SUBLANES
    offp = jnp.cumsum(npad, axis=1) - npad
    total = jnp.sum(npad, axis=1)
    group = jnp.sum(npad, axis=0)
    group_pad = (group + TM_MOE - 1) // TM_MOE * TM_MOE
    group_end = jnp.cumsum(group_pad)
    group_start = group_end - group_pad
    base = group_start[None, :] + jnp.cumsum(npad, axis=0) - npad
    tail_start = group_start + group
    is_last = jnp.arange(N_EXPERTS) == N_EXPERTS - 1
    tail_len = jnp.where(is_last, P_PAD - tail_start, group_pad - group)
    tile_start = jnp.arange(N_MOE_TILES, dtype=jnp.int32) * TM_MOE
    tile_expert = jnp.minimum(jnp.sum((tile_start[:, None] >= group_end[None, :]).astype(jnp.int32), axis=1),
                              N_EXPERTS - 1)
    ids = jnp.arange(N_EXPERTS, dtype=jnp.int32)
    nonempty = group_pad > 0
    ordinal = jnp.cumsum(nonempty.astype(jnp.int32)) - 1
    later = jnp.logical_and(nonempty[None, :], ids[None, :] > ids[:, None])
    next_expert = jnp.min(jnp.where(later, ids[None, :], N_EXPERTS), axis=1)
    next_expert = jnp.where(next_expert == N_EXPERTS, -1, next_expert)
    is_tile_expert = tile_expert[:, None] == ids[None, :]
    per_tile = lambda table: jnp.sum(jnp.where(is_tile_expert, table[None, :], 0), axis=1)
    sched = {"expert": tile_expert, "n_valid": (group_end[-1] // TM_MOE).reshape(1),
             "next": per_tile(next_expert), "parity": per_tile(ordinal) % 2,
             "first": tile_start == per_tile(group_start),
             "last": tile_start + TM_MOE == per_tile(group_end)}
    tables = {"base": base.reshape(-1), "offp": offp.reshape(-1), "np": npad.reshape(-1), "total": total,
              "tail": jnp.concatenate([tail_start, tail_len])}
    as_i32 = lambda d: {k: v.astype(jnp.int32) for k, v in d.items()}
    return as_i32(tables), as_i32(sched)


WIN_ROWS = 256


def _win_layout_kernel(wt_ref, o_ref):
    wt = wt_ref[0]
    cols = wt.shape[1]
    kr_tile = jnp.concatenate([jnp.zeros((QK_NOPE, cols), F32), wt[REF_KR:REF_GATE, :],
                               jnp.zeros((HEAD_PAD - QK_DIM, cols), F32)], axis=0)
    conv_rows = [wt[part * D_CONV + c * CONV_CHUNK:part * D_CONV + (c + 1) * CONV_CHUNK, :]
                 for c in range(D_CONV // CONV_CHUNK) for part in range(3)]
    grouped = jnp.concatenate(conv_rows + [wt[REF_GATE:, :], wt[REF_CQ:REF_KR, :], kr_tile], axis=0)
    o_ref[0] = grouped.T.astype(BF16)


def _win_layout(w_in):
    wt = jnp.swapaxes(w_in, 1, 2)
    d_in = wt.shape[1]
    return pl.pallas_call(
        _win_layout_kernel,
        grid=(DEPTH, D_MODEL // WIN_ROWS),
        in_specs=[pl.BlockSpec((1, d_in, WIN_ROWS), lambda l, r: (l, 0, r))],
        out_specs=pl.BlockSpec((1, WIN_ROWS, D_IN_PAD), lambda l, r: (l, r, 0)),
        out_shape=jax.ShapeDtypeStruct((DEPTH, D_MODEL, D_IN_PAD), BF16),
        compiler_params=pltpu.CompilerParams(dimension_semantics=("arbitrary",) * 2, vmem_limit_bytes=VMEM_LIMIT),
        name="win_layout",
    )(wt)


def _small_layout_kernel(wuq_ref, wukv_ref, wco_ref, wmo_ref, wo_ref,
                         wq_out, wuk_out, wuv_out, wco_out, wmo_out, wo_out):
    wq_out[0] = wuq_ref[0].T.astype(BF16)
    wkv = wukv_ref[0]
    keys, values = [], []
    for h in range(N_HEADS):
        c0 = h * (QK_NOPE + V_HEAD)
        keys.append(wkv[:, c0:c0 + QK_NOPE])
        values.append(wkv[:, c0 + QK_NOPE:c0 + QK_NOPE + V_HEAD])
    wuk_out[0] = jnp.concatenate(keys, axis=1).astype(BF16)
    wuv_out[0] = jnp.concatenate(values, axis=1).T.astype(BF16)
    wco_out[0] = wco_ref[0].astype(BF16)
    wmo_out[0] = wmo_ref[0].astype(BF16)
    wo_out[0] = wo_ref[0].astype(BF16)


def _small_layout(w_uq, w_ukv, w_conv_out, w_mla_out, w_o):
    layer = lambda *shape: pl.BlockSpec((1,) + shape, lambda l: (l, 0, 0))
    shapes = [(N_HEADS * QK_DIM, Q_LORA), (KV_LORA, N_HEADS * QK_NOPE), (N_HEADS * V_HEAD, KV_LORA),
              (D_CONV, D_MODEL), (N_HEADS * V_HEAD, D_MODEL), (D_MODEL, D_MODEL)]
    return pl.pallas_call(
        _small_layout_kernel,
        grid=(DEPTH,),
        in_specs=[layer(*w.shape[1:]) for w in (w_uq, w_ukv, w_conv_out, w_mla_out, w_o)],
        out_specs=[layer(*s) for s in shapes],
        out_shape=[jax.ShapeDtypeStruct((DEPTH,) + s, BF16) for s in shapes],
        compiler_params=pltpu.CompilerParams(dimension_semantics=("arbitrary",), vmem_limit_bytes=VMEM_LIMIT),
        name="small_layout",
    )(w_uq, w_ukv, w_conv_out, w_mla_out, w_o)


def kernel(x, positions, attn_norm, w_in, conv_w, w_conv_out, q_norm, w_uq, kv_norm, w_ukv, w_mla_out, w_o, ffn_norm,
           w_gate, w_up, w_down, router, w_gate_e, w_up_e, w_down_e, final_norm):
    assert x.shape == (BATCH, SEQ, D_MODEL) and positions.shape == (BATCH, SEQ)
    cos_t, sin_t = _rope_tables(positions)
    xt = x.reshape(TOKENS, D_MODEL)
    out = None
    win = _win_layout(w_in)
    wq, wuk, wuv, wco, wmo, wo = _small_layout(w_uq, w_ukv, w_conv_out, w_mla_out, w_o)
    for l in range(DEPTH):
        q, k, v, gc, sg = _front(l, xt, attn_norm[l].reshape(1, D_MODEL), win, conv_w[l], wco,
                                 q_norm[l].reshape(1, Q_LORA), wq, kv_norm[l].reshape(1, KV_LORA), wuk, wuv,
                                 cos_t, sin_t)
        attn = _attention(q, k, v)
        fn = ffn_norm[l].reshape(1, D_MODEL)
        if l % 2 == 0:
            j = l // 2
            xt = _merge(l, attn, gc, sg, xt, wmo, wo, fn,
                        ffn=(w_gate[j].astype(BF16), w_up[j].astype(BF16), w_down[j].astype(BF16)))
        else:
            j = l // 2
            router_pad = jnp.pad(router[j].T, ((0, E_PAD - N_EXPERTS), (0, 0)))
            x2, hn, route_t = _merge(l, attn, gc, sg, xt, wmo, wo, fn, router_pad)
            tables, sched = _routing_tables(route_t)
            xs = _dispatch(tables, hn, route_t)
            y_sorted = _moe(sched, xs, w_gate_e[j], w_up_e[j], w_down_e[j])
            out = _combine(tables, x2, route_t, y_sorted, final_norm.reshape(1, D_MODEL))
    return out.reshape(BATCH, SEQ, D_MODEL)
```
